```python
import math
import jax, jax.numpy as jnp
from jax import lax
import numpy as np

D_MODEL = 1024
BATCH = 8
SEQ = 2048
DEPTH = 1
DEC_BATCH = 128
DEC_SEQ = 8
PAST_LEN = 16384
PAGE_SIZE = 128

M_HEADS = 4
M_HEAD_DIM = D_MODEL // M_HEADS
M_WIDTH = M_HEADS * M_HEAD_DIM
CONV_W = 4
G_HEADS = 4
G_DK = D_MODEL // (2 * G_HEADS)
G_DV = D_MODEL // G_HEADS
G_KW = G_HEADS * G_DK
G_VW = G_HEADS * G_DV
G_RANK = 16
G_TAU = 16.0
D_FF = 2816
CHUNK = 64
EPS = 1e-6

kernel_name = "hybrid_mlstm_gla_macaron_step"

IN_SPLITS = (M_WIDTH, M_WIDTH, M_WIDTH, 2 * M_HEADS, G_KW, G_KW, G_VW, G_VW, G_RANK, D_MODEL, D_MODEL)
IN_WIDTH = 3 * M_WIDTH + 2 * M_HEADS + 2 * G_KW + 2 * G_VW + G_RANK + 2 * D_MODEL


def _rmsnorm(x, g):
    xf = x.astype(jnp.float32)
    y = xf * lax.rsqrt(jnp.mean(xf * xf, axis=-1, keepdims=True) + EPS)
    return (y * g.astype(jnp.float32)).astype(x.dtype)


def _head_norm(h, g, dtype):
    y = h * lax.rsqrt(jnp.mean(h * h, axis=-1, keepdims=True) + EPS)
    B, H, T, d = h.shape
    y = y.transpose(0, 2, 1, 3).reshape(B, T, H * d)
    return (y * g.astype(jnp.float32)).astype(dtype)


def _swiglu(h, w_up, w_down):
    a, g = jnp.split(h @ w_up, 2, axis=-1)
    return (jax.nn.silu(g) * a) @ w_down


def _causal_conv(u, buf, w, b):
    full = jnp.concatenate([buf.astype(u.dtype), u], axis=1)
    T = u.shape[1]
    y = b
    for j in range(CONV_W):
        y = y + full[:, j:j + T] * w[j]
    return y, full[:, full.shape[1] - (CONV_W - 1):]


def _chunk_len(T):
    return CHUNK if T % CHUNK == 0 else T


def _to_chunks(a, L):
    B, H, T = a.shape[:3]
    a = a.reshape(B, H, T // L, L, *a.shape[3:])
    return jnp.moveaxis(a, 2, 0)


def _mlstm_scan(q, k, v, ig, lf, C0, n0, m0):
    B, H, T, d = q.shape
    L = _chunk_len(T)
    xs = tuple(_to_chunks(a, L) for a in (q, k, v, ig, lf))
    causal = jnp.tril(jnp.ones((L, L), dtype=bool))

    def step(carry, inp):
        C, n, m = carry
        qb, kb, vb, ib, fb = inp
        b = jnp.cumsum(fb, axis=-1)
        m_t = b + jnp.maximum(m[..., None], lax.cummax(ib - b, axis=2))
        inter = jnp.exp(b + m[..., None] - m_t)
        logD = b[..., :, None] - b[..., None, :] + ib[..., None, :] - m_t[..., :, None]
        Dm = jnp.exp(jnp.where(causal, logD, -jnp.inf))
        s = jnp.einsum('bhtd,bhsd->bhts', qb, kb) * Dm
        num = inter[..., None] * jnp.einsum('bhtd,bhed->bhte', qb, C) + jnp.einsum('bhts,bhse->bhte', s, vb)
        den = inter * jnp.einsum('bhtd,bhd->bht', qb, n) + jnp.sum(s, axis=-1)
        h = num / jnp.maximum(jnp.abs(den), jnp.exp(-m_t))[..., None]
        m_L = m_t[..., -1]
        w = jnp.exp(b[..., -1:] - b + ib - m_L[..., None])
        decay = jnp.exp(b[..., -1] + m - m_L)
        C_new = decay[..., None, None] * C + jnp.einsum('bhs,bhse,bhsd->bhed', w, vb, kb)
        n_new = decay[..., None] * n + jnp.einsum('bhs,bhsd->bhd', w, kb)
        return (C_new, n_new, m_L), h

    (C, n, m), hc = lax.scan(step, (C0, n0, m0), xs)
    h = jnp.moveaxis(hc, 0, 2).reshape(B, H, T, d)
    return h, C, n, m


def _gla_scan(q, k, v, la, S0):
    B, H, T, _ = q.shape
    dv = v.shape[-1]
    L = _chunk_len(T)
    xs = tuple(_to_chunks(a, L) for a in (q, k, v, la))
    causal = jnp.tril(jnp.ones((L, L), dtype=bool))[:, :, None]

    def step(S, inp):
        qb, kb, vb, ab = inp
        A = jnp.cumsum(ab, axis=2)
        inter = jnp.einsum('bhtd,bhde->bhte', qb * jnp.exp(A), S)
        diff = jnp.where(causal, A[:, :, :, None, :] - A[:, :, None, :, :], -jnp.inf)
        att = jnp.einsum('bhtd,bhtsd,bhsd->bhts', qb, jnp.exp(diff), kb)
        o = inter + jnp.einsum('bhts,bhse->bhte', att, vb)
        A_L = A[:, :, -1]
        S_new = jnp.exp(A_L)[..., None] * S + jnp.einsum('bhsd,bhse->bhde', kb * jnp.exp(A_L[:, :, None] - A), vb)
        return S_new, o

    S, oc = lax.scan(step, S0, xs)
    o = jnp.moveaxis(oc, 0, 2).reshape(B, H, T, dv)
    return o, S


def _mixer(h, conv_buf, C0, n0, m0, S0, W, l):
    B, T, _ = h.shape
    dt = h.dtype
    f32 = jnp.float32
    proj = h @ W['w_in'][l]
    idx, acc = [], 0
    for wdt in IN_SPLITS[:-1]:
        acc += wdt
        idx.append(acc)
    u_m, v_m, o_m, if_m, q_g, k_g, v_g, r_g, a_g, g_a, g_b = jnp.split(proj, idx, axis=-1)

    c, conv_new = _causal_conv(u_m, conv_buf, W['conv_w'][l], W['conv_b'][l])
    ch = jax.nn.silu(c).reshape(B, T, M_HEADS, M_HEAD_DIM)
    q = jnp.einsum('bthd,hde->bhte', ch, W['w_mq'][l]).astype(f32)
    k = (jnp.einsum('bthd,hde->bhte', ch, W['w_mk'][l]) * (M_HEAD_DIM ** -0.5)).astype(f32)
    v = v_m.reshape(B, T, M_HEADS, M_HEAD_DIM).transpose(0, 2, 1, 3).astype(f32)
    gates = (if_m.reshape(B, T, 2, M_HEADS) + W['b_if'][l]).astype(f32)
    ig = gates[:, :, 0].transpose(0, 2, 1)
    lf = jax.nn.log_sigmoid(gates[:, :, 1]).transpose(0, 2, 1)
    hm, C, n, m = _mlstm_scan(q, k, v, ig, lf, C0.astype(f32), n0.astype(f32), m0.astype(f32))
    hm = jax.nn.sigmoid(o_m) * _head_norm(hm, W['g_mhead'][l], dt)

    qg = (q_g.reshape(B, T, G_HEADS, G_DK).transpose(0, 2, 1, 3) * (G_DK ** -0.5)).astype(f32)
    kg = k_g.reshape(B, T, G_HEADS, G_DK).transpose(0, 2, 1, 3).astype(f32)
    vg = v_g.reshape(B, T, G_HEADS, G_DV).transpose(0, 2, 1, 3).astype(f32)
    la = jax.nn.log_sigmoid((a_g @ W['w_a2'][l] + W['b_a'][l]).astype(f32)) / G_TAU
    la = la.reshape(B, T, G_HEADS, G_DK).transpose(0, 2, 1, 3)
    og, S = _gla_scan(qg, kg, vg, la, S0.astype(f32))
    og = jax.nn.silu(r_g) * _head_norm(og, W['g_ghead'][l], dt)

    y = jax.nn.sigmoid(g_a) * (hm @ W['w_pa'][l]) + jax.nn.sigmoid(g_b) * (og @ W['w_pb'][l])
    out = y @ W['w_o'][l]
    return out, conv_new.astype(dt), C.astype(dt), n.astype(dt), m.astype(dt), S.astype(dt)


def _trunk(x, st_conv, st_C, st_n, st_m, st_S, W):
    convs, Cs, ns, ms, Ss = [], [], [], [], []
    for l in range(DEPTH):
        x = x + 0.5 * _swiglu(_rmsnorm(x, W['g_ffn1'][l]), W['w_ffn1_up'][l], W['w_ffn1_down'][l])
        mix, cv, C, n, m, S = _mixer(_rmsnorm(x, W['g_mix'][l]), st_conv[l], st_C[l], st_n[l], st_m[l], st_S[l], W, l)
        x = x + mix
        x = x + 0.5 * _swiglu(_rmsnorm(x, W['g_ffn2'][l]), W['w_ffn2_up'][l], W['w_ffn2_down'][l])
        convs.append(cv); Cs.append(C); ns.append(n); ms.append(m); Ss.append(S)
    y = _rmsnorm(x, W['g_final'])
    return y, jnp.stack(convs), jnp.stack(Cs), jnp.stack(ns), jnp.stack(ms), jnp.stack(Ss)


def setup_inputs(seed: int = 0) -> dict:
    key = jax.random.key(seed)
    ks = iter(jax.random.split(key, 40))
    nrm = lambda shape, s: jax.random.normal(next(ks), shape, jnp.float32) * s
    gain = lambda shape: 1.0 + nrm(shape, 0.05)
    f_bias = jnp.linspace(3.0, 6.0, M_HEADS, dtype=jnp.float32)
    b_if = jnp.stack([nrm((DEPTH, M_HEADS), 0.1), f_bias + nrm((DEPTH, M_HEADS), 0.1)], axis=1)
    return {
        "x_prompt": nrm((BATCH, SEQ, D_MODEL), 1.0),
        "x_sample": nrm((DEC_BATCH, DEC_SEQ, D_MODEL), 1.0),
        "state_conv": nrm((DEPTH, DEC_BATCH, CONV_W - 1, M_WIDTH), 1.0),
        "state_mlstm_C": nrm((DEPTH, DEC_BATCH, M_HEADS, M_HEAD_DIM, M_HEAD_DIM), 0.1),
        "state_mlstm_n": nrm((DEPTH, DEC_BATCH, M_HEADS, M_HEAD_DIM), 0.1),
        "state_mlstm_m": nrm((DEPTH, DEC_BATCH, M_HEADS), 1.0),
        "state_gla_S": nrm((DEPTH, DEC_BATCH, G_HEADS, G_DK, G_DV), 0.1),
        "g_ffn1": gain((DEPTH, D_MODEL)),
        "w_ffn1_up": nrm((DEPTH, D_MODEL, 2 * D_FF), D_MODEL ** -0.5),
        "w_ffn1_down": nrm((DEPTH, D_FF, D_MODEL), D_FF ** -0.5),
        "g_mix": gain((DEPTH, D_MODEL)),
        "w_in": nrm((DEPTH, D_MODEL, IN_WIDTH), D_MODEL ** -0.5),
        "conv_w": nrm((DEPTH, CONV_W, M_WIDTH), CONV_W ** -0.5),
        "conv_b": nrm((DEPTH, M_WIDTH), 0.02),
        "w_mq": nrm((DEPTH, M_HEADS, M_HEAD_DIM, M_HEAD_DIM), M_HEAD_DIM ** -0.5),
        "w_mk": nrm((DEPTH, M_HEADS, M_HEAD_DIM, M_HEAD_DIM), M_HEAD_DIM ** -0.5),
        "b_if": b_if,
        "g_mhead": gain((DEPTH, M_WIDTH)),
        "w_a2": nrm((DEPTH, G_RANK, G_KW), G_RANK ** -0.5),
        "b_a": nrm((DEPTH, G_KW), 0.1),
        "g_ghead": gain((DEPTH, G_VW)),
        "w_pa": nrm((DEPTH, M_WIDTH, D_MODEL), M_WIDTH ** -0.5),
        "w_pb": nrm((DEPTH, G_VW, D_MODEL), G_VW ** -0.5),
        "w_o": nrm((DEPTH, D_MODEL, D_MODEL), D_MODEL ** -0.5),
        "g_ffn2": gain((DEPTH, D_MODEL)),
        "w_ffn2_up": nrm((DEPTH, D_MODEL, 2 * D_FF), D_MODEL ** -0.5),
        "w_ffn2_down": nrm((DEPTH, D_FF, D_MODEL), D_FF ** -0.5),
        "g_final": gain((D_MODEL,)),
    }


def reference(x_prompt, x_sample, state_conv, state_mlstm_C, state_mlstm_n, state_mlstm_m, state_gla_S,
              g_ffn1, w_ffn1_up, w_ffn1_down, g_mix, w_in, conv_w, conv_b, w_mq, w_mk, b_if, g_mhead,
              w_a2, b_a, g_ghead, w_pa, w_pb, w_o, g_ffn2, w_ffn2_up, w_ffn2_down, g_final):
    W = {
        'g_ffn1': g_ffn1, 'w_ffn1_up': w_ffn1_up, 'w_ffn1_down': w_ffn1_down, 'g_mix': g_mix,
        'w_in': w_in, 'conv_w': conv_w, 'conv_b': conv_b, 'w_mq': w_mq, 'w_mk': w_mk, 'b_if': b_if,
        'g_mhead': g_mhead, 'w_a2': w_a2, 'b_a': b_a, 'g_ghead': g_ghead, 'w_pa': w_pa, 'w_pb': w_pb,
        'w_o': w_o, 'g_ffn2': g_ffn2, 'w_ffn2_up': w_ffn2_up, 'w_ffn2_down': w_ffn2_down, 'g_final': g_final,
    }
    B = x_prompt.shape[0]
    dt = x_prompt.dtype
    z_conv = jnp.zeros((DEPTH, B, CONV_W - 1, M_WIDTH), dt)
    z_C = jnp.zeros((DEPTH, B, M_HEADS, M_HEAD_DIM, M_HEAD_DIM), dt)
    z_n = jnp.zeros((DEPTH, B, M_HEADS, M_HEAD_DIM), dt)
    z_m = jnp.zeros((DEPTH, B, M_HEADS), dt)
    z_S = jnp.zeros((DEPTH, B, G_HEADS, G_DK, G_DV), dt)
    y_prompt, conv_p, C_p, n_p, m_p, S_p = _trunk(x_prompt, z_conv, z_C, z_n, z_m, z_S, W)
    y_sample, conv_s, C_s, n_s, m_s, S_s = _trunk(x_sample, state_conv, state_mlstm_C, state_mlstm_n,
                                                  state_mlstm_m, state_gla_S, W)
    return (y_prompt, y_sample, conv_p, C_p, n_p, m_p, S_p, conv_s, C_s, n_s, m_s, S_s)
```

```python
import functools
import math

import jax
import jax.numpy as jnp
from jax import lax
from jax.experimental import pallas as pl
from jax.experimental.pallas import tpu as pltpu

D_MODEL = 1024
M_HEADS = 4
M_HEAD_DIM = 256
M_WIDTH = 1024
CONV_W = 4
G_HEADS = 4
G_DK = 128
G_DV = 256
G_KW = 512
G_VW = 1024
G_RANK = 16
G_TAU = 16.0
D_FF = 2816
EPS = 1e-6

LANES = 128
SUBLANES = 8
VMEM_LIMIT = 56 * 1024 * 1024

F32 = jnp.float32
BF16 = jnp.bfloat16
NEG_INF = float("-inf")


def _sigmoid(x):
    return 1.0 / (1.0 + jnp.exp(-x))


def _silu(x):
    return x * _sigmoid(x)


def _log_sigmoid(x):
    return jnp.minimum(x, 0.0) - jnp.log(1.0 + jnp.exp(-jnp.abs(x)))


def _rms(x, g):
    return x * lax.rsqrt(jnp.mean(x * x, axis=-1, keepdims=True) + EPS) * g


def _dot(a, b):
    return jnp.dot(a, b, preferred_element_type=F32)


def _dot_nt(a, b):
    return lax.dot_general(a, b, (((1,), (1,)), ((), ())), preferred_element_type=F32)


def _dot_tn(a, b):
    return lax.dot_general(a, b, (((0,), (0,)), ((), ())), preferred_element_type=F32)


def _const_spec(shape):
    nd = len(shape)
    return pl.BlockSpec(shape, lambda *_: (0,) * nd, pipeline_mode=pl.Buffered(1))


FF_CHUNK = 256


def _ffn_kernel(x_ref, g_ref, wup_ref, wdown_ref, gfin_ref, o_ref, h_ref, *, final_norm):
    x = x_ref[...]
    xn = _rms(x, g_ref[...]).astype(BF16)
    for c in range(D_FF // FF_CHUNK):
        a = _dot(xn, wup_ref[:, c * FF_CHUNK:(c + 1) * FF_CHUNK])
        g = _dot(xn, wup_ref[:, D_FF + c * FF_CHUNK:D_FF + (c + 1) * FF_CHUNK])
        h_ref[:, c * FF_CHUNK:(c + 1) * FF_CHUNK] = (_silu(g) * a).astype(BF16)
    y = x + 0.5 * _dot(h_ref[...], wdown_ref[...])
    if final_norm:
        y = _rms(y, gfin_ref[...])
    o_ref[...] = y


def _ffn(x2d, g, wup, wdown, gfin, *, final_norm, tm):
    n = x2d.shape[0]
    assert n % tm == 0
    return pl.pallas_call(
        functools.partial(_ffn_kernel, final_norm=final_norm),
        grid=(n // tm,),
        in_specs=[
            pl.BlockSpec((tm, D_MODEL), lambda i: (i, 0)),
            _const_spec((1, D_MODEL)),
            _const_spec((D_MODEL, 2 * D_FF)),
            _const_spec((D_FF, D_MODEL)),
            _const_spec((1, D_MODEL)),
        ],
        out_specs=pl.BlockSpec((tm, D_MODEL), lambda i: (i, 0)),
        out_shape=jax.ShapeDtypeStruct((n, D_MODEL), F32),
        scratch_shapes=[pltpu.VMEM((tm, D_FF), BF16)],
        compiler_params=pltpu.CompilerParams(
            dimension_semantics=("parallel",), vmem_limit_bytes=VMEM_LIMIT),
        name="ffn_final" if final_norm else "ffn",
    )(x2d, g, wup, wdown, gfin)


PRE_U, PRE_V, PRE_QG, PRE_KG, PRE_VG = 0, 1024, 2048, 2560, 3072
PRE_WIDTH = 4096
A_LANE0 = 8


def _pre_kernel(x_ref, cst_ref, g_ref, w_ref, ws_ref, cw_ref, cb_ref, wq_ref, wk_ref, bif_ref,
                wa2_ref, ba_ref,
                qm_ref, km_ref, vm_ref, gif_ref, qg_ref, kg_ref, la_ref, vg_ref, cout_ref,
                cbuf_ref, *, bt, tb):
    t = pl.program_id(1)
    m = bt * tb
    xn = _rms(x_ref[...].reshape(m, D_MODEL), g_ref[...]).astype(BF16)

    @pl.when(t == 0)
    def _():
        cbuf_ref[:, 0:SUBLANES, :] = cst_ref[...]

    @pl.when(t > 0)
    def _():
        cbuf_ref[:, 0:SUBLANES, :] = cbuf_ref[:, tb:tb + SUBLANES, :]

    u = _dot(xn, w_ref[:, PRE_U:PRE_U + M_WIDTH])
    cbuf_ref[:, SUBLANES:SUBLANES + tb, :] = u.reshape(bt, tb, M_WIDTH)
    c = cb_ref[...].reshape(1, 1, M_WIDTH)
    for j in range(CONV_W):
        off = SUBLANES - (CONV_W - 1) + j
        c = c + cbuf_ref[:, off:off + tb, :] * cw_ref[j:j + 1, :].reshape(1, 1, M_WIDTH)
    cout_ref[...] = cbuf_ref[:, tb:tb + SUBLANES, :]
    ch = _silu(c).reshape(m, M_WIDTH).astype(BF16)
    for h in range(M_HEADS):
        sl = slice(h * M_HEAD_DIM, (h + 1) * M_HEAD_DIM)
        qm_ref[:, :, sl] = _dot(ch[:, sl], wq_ref[h]).astype(BF16).reshape(bt, tb, M_HEAD_DIM)
        km_ref[:, :, sl] = (_dot(ch[:, sl], wk_ref[h]) * (M_HEAD_DIM ** -0.5)).astype(BF16).reshape(
            bt, tb, M_HEAD_DIM)

    vm_ref[...] = _dot(xn, w_ref[:, PRE_V:PRE_V + M_WIDTH]).astype(BF16).reshape(bt, tb, M_WIDTH)
    qg_ref[...] = (_dot(xn, w_ref[:, PRE_QG:PRE_QG + G_KW]) * (G_DK ** -0.5)).astype(BF16).reshape(
        bt, tb, G_KW)
    kg_ref[...] = _dot(xn, w_ref[:, PRE_KG:PRE_KG + G_KW]).astype(BF16).reshape(bt, tb, G_KW)
    vg_ref[...] = _dot(xn, w_ref[:, PRE_VG:PRE_VG + G_VW]).astype(BF16).reshape(bt, tb, G_VW)

    small = _dot(xn, ws_ref[...])
    gates = small + bif_ref[...]
    lane = lax.broadcasted_iota(jnp.int32, gates.shape, 1)
    gif = jnp.where(lane < M_HEADS, gates, jnp.where(lane < 2 * M_HEADS, _log_sigmoid(gates), 0.0))
    gif_ref[...] = gif.reshape(bt, tb, LANES)
    a2 = _dot(small.astype(BF16), wa2_ref[...]) + ba_ref[...]
    la_ref[...] = (_log_sigmoid(a2) * (1.0 / G_TAU)).reshape(bt, tb, G_KW)


def _pre(x, cst, g, w_pre, w_small, conv_w, conv_b, wq, wk, bif, wa2, ba, *, bt, tb):
    b, t, _ = x.shape
    assert b % bt == 0 and t % tb == 0 and tb % SUBLANES == 0
    tok = lambda w: pl.BlockSpec((bt, tb, w), lambda i, j: (i, j, 0))
    per_seq = pl.BlockSpec((bt, SUBLANES, M_WIDTH), lambda i, j: (i, 0, 0))
    sds = lambda w, dt: jax.ShapeDtypeStruct((b, t, w), dt)
    return pl.pallas_call(
        functools.partial(_pre_kernel, bt=bt, tb=tb),
        grid=(b // bt, t // tb),
        in_specs=[
            tok(D_MODEL), per_seq,
            _const_spec((1, D_MODEL)),
            _const_spec((D_MODEL, PRE_WIDTH)),
            _const_spec((D_MODEL, LANES)),
            _const_spec((CONV_W, M_WIDTH)),
            _const_spec((1, M_WIDTH)),
            _const_spec((M_HEADS, M_HEAD_DIM, M_HEAD_DIM)),
            _const_spec((M_HEADS, M_HEAD_DIM, M_HEAD_DIM)),
            _const_spec((1, LANES)),
            _const_spec((LANES, G_KW)),
            _const_spec((1, G_KW)),
        ],
        out_specs=[tok(M_WIDTH), tok(M_WIDTH), tok(M_WIDTH), tok(LANES), tok(G_KW), tok(G_KW),
                   tok(G_KW), tok(G_VW), per_seq],
        out_shape=[sds(M_WIDTH, BF16), sds(M_WIDTH, BF16), sds(M_WIDTH, BF16), sds(LANES, F32),
                   sds(G_KW, BF16), sds(G_KW, BF16), sds(G_KW, F32), sds(G_VW, BF16),
                   jax.ShapeDtypeStruct((b, SUBLANES, M_WIDTH), F32)],
        scratch_shapes=[pltpu.VMEM((bt, tb + SUBLANES, M_WIDTH), F32)],
        compiler_params=pltpu.CompilerParams(
            dimension_semantics=("parallel", "arbitrary"), vmem_limit_bytes=VMEM_LIMIT),
        name="mixer_pre",
    )(x, cst, g, w_pre, w_small, conv_w, conv_b, wq, wk, bif, wa2, ba)


def _scan_rows(x, op, fill):
    n = x.shape[0]
    row = lax.broadcasted_iota(jnp.int32, x.shape, 0)
    s = 1
    while s < n:
        x = op(x, jnp.where(row >= s, pltpu.roll(x, s, axis=0), fill))
        s *= 2
    return x


def _col_of_row(row):
    return jnp.transpose(jnp.broadcast_to(row, (SUBLANES, LANES)))[:, 0:1]


def _mlstm_chunk(q, k, v, gif, c_ref, n_ref, m_ref, gh_ref, hm_store, *, lm):
    cs = _scan_rows(gif, jnp.add, 0.0)
    b = pltpu.roll(cs, LANES - M_HEADS, axis=1)
    g = gif - b
    m_prev = m_ref[...]
    mx = jnp.maximum(m_prev, _scan_rows(g, jnp.maximum, NEG_INF))
    inter = jnp.exp(m_prev - mx)
    nfloor = jnp.exp(-(b + mx))
    g_t = jnp.transpose(g)
    mx_last = mx[lm - 1:lm, :]
    w_in = jnp.exp(g - mx_last)
    decay = jnp.exp(m_prev - mx_last)
    m_ref[...] = b[lm - 1:lm, :] + mx_last

    row = lax.broadcasted_iota(jnp.int32, (lm, lm), 0)
    col = lax.broadcasted_iota(jnp.int32, (lm, lm), 1)
    causal = col <= row
    for h in range(M_HEADS):
        sl = slice(h * M_HEAD_DIM, (h + 1) * M_HEAD_DIM)
        qh, kh, vh = q[:, sl], k[:, sl], v[:, sl]
        c_old = c_ref[h]
        n_old = n_ref[h]
        log_d = g_t[h:h + 1, :] - mx[:, h:h + 1]
        dm = jnp.exp(jnp.where(causal, log_d, NEG_INF))
        s = _dot_nt(qh, kh) * dm
        inter_h = inter[:, h:h + 1]
        num = inter_h * _dot_nt(qh, c_old.astype(BF16)) + _dot(s.astype(BF16), vh)
        qn = jnp.sum(qh.astype(F32) * n_old, axis=-1, keepdims=True)
        den = inter_h * qn + jnp.sum(s, axis=-1, keepdims=True)
        hh = num * (1.0 / jnp.maximum(jnp.abs(den), nfloor[:, h:h + 1]))
        hm_store(h, _rms(hh, gh_ref[:, sl]))
        w_h = w_in[:, h:h + 1]
        dec_h = decay[:, h:h + 1]
        wv = (w_h * vh.astype(F32)).astype(BF16)
        c_ref[h] = dec_h * c_old + _dot_tn(wv, kh)
        n_ref[h] = dec_h * n_old + jnp.sum(w_h * kh.astype(F32), axis=0, keepdims=True)


def _gla_chunk(q, k, v, la, s_ref, gh_ref, og_store, *, lg, sub):
    row = lax.broadcasted_iota(jnp.int32, (lg, lg), 0)
    col = lax.broadcasted_iota(jnp.int32, (lg, lg), 1)
    for h in range(G_HEADS):
        ks = slice(h * G_DK, (h + 1) * G_DK)
        vs = slice(h * G_DV, (h + 1) * G_DV)
        qh = q[:, ks].astype(F32)
        kh = k[:, ks].astype(F32)
        vh = v[:, vs]
        a = _scan_rows(la[:, ks], jnp.add, 0.0)
        s_old = s_ref[h]
        o = _dot((qh * jnp.exp(a)).astype(BF16), s_old.astype(BF16))

        att = jnp.zeros((lg, lg), F32)
        c = sub
        while c < lg:
            nblk = lg // (2 * c)
            a_ref = jnp.concatenate(
                [jnp.broadcast_to(a[2 * c * i + c - 1:2 * c * i + c, :], (2 * c, G_DK))
                 for i in range(nblk)], axis=0)
            qt = (qh * jnp.exp(jnp.minimum(a - a_ref, 0.0))).astype(BF16)
            kt = (kh * jnp.exp(jnp.minimum(a_ref - a, 0.0))).astype(BF16)
            valid = (row // (2 * c) == col // (2 * c)) & (row % (2 * c) >= c) & (col % (2 * c) < c)
            att = att + jnp.where(valid, _dot_nt(qt, kt), 0.0)
            c *= 2
        for blk in range(lg // sub):
            r0 = blk * sub
            a_b = a[r0:r0 + sub, :]
            q_b = qh[r0:r0 + sub, :]
            rows = lax.broadcasted_iota(jnp.int32, (sub, lg), 0)
            cols = lax.broadcasted_iota(jnp.int32, (sub, lg), 1)
            acc = jnp.zeros((sub, lg), F32)
            for s in range(sub):
                e = jnp.exp(jnp.minimum(a_b - a[r0 + s:r0 + s + 1, :], 0.0))
                p = jnp.sum(q_b * e * kh[r0 + s:r0 + s + 1, :], axis=-1, keepdims=True)
                acc = jnp.where((cols == r0 + s) & (rows >= s), p, acc)
            att = att + jnp.concatenate(
                [acc if i == blk else jnp.zeros((sub, lg), F32) for i in range(lg // sub)], axis=0)
        o = o + _dot(att.astype(BF16), vh)
        og_store(h, _rms(o, gh_ref[:, vs]))

        a_last = a[lg - 1:lg, :]
        kt = (kh * jnp.exp(a_last - a)).astype(BF16)
        s_ref[h] = jnp.exp(_col_of_row(a_last)) * s_old + _dot_tn(kt, vh)


def _scan_kernel(*refs, bt, tb, lm, lg, sub, has_state):
    if has_state:
        (qm_ref, km_ref, vm_ref, gif_ref, qg_ref, kg_ref, la_ref, vg_ref, gmh_ref, ggh_ref,
         c0_ref, n0_ref, m0_ref, s0_ref,
         hm_ref, og_ref, c_out, n_out, m_out, s_out, c_ref, n_ref, m_ref, s_ref) = refs
    else:
        (qm_ref, km_ref, vm_ref, gif_ref, qg_ref, kg_ref, la_ref, vg_ref, gmh_ref, ggh_ref,
         hm_ref, og_ref, c_out, n_out, m_out, s_out, c_ref, n_ref, m_ref, s_ref) = refs
    t = pl.program_id(1)
    nt = pl.num_programs(1)

    for bi in range(bt):
        @pl.when(t == 0)
        def _():
            if has_state:
                c_ref[bi] = c0_ref[bi]
                n_ref[bi] = n0_ref[bi]
                m_ref[bi] = m0_ref[bi]
                s_ref[bi] = s0_ref[bi]
            else:
                c_ref[bi] = jnp.zeros(c_ref.shape[1:], F32)
                n_ref[bi] = jnp.zeros(n_ref.shape[1:], F32)
                m_ref[bi] = jnp.zeros(m_ref.shape[1:], F32)
                s_ref[bi] = jnp.zeros(s_ref.shape[1:], F32)

        def m_body(ci, carry):
            r0 = pl.multiple_of(ci * lm, lm)
            rows = pl.ds(r0, lm)

            def store(h, val):
                hm_ref[bi, rows, h * M_HEAD_DIM:(h + 1) * M_HEAD_DIM] = val

            _mlstm_chunk(qm_ref[bi, rows, :], km_ref[bi, rows, :], vm_ref[bi, rows, :],
                         gif_ref[bi, rows, :], c_ref.at[bi], n_ref.at[bi], m_ref.at[bi],
                         gmh_ref, store, lm=lm)
            return carry

        def g_body(ci, carry):
            r0 = pl.multiple_of(ci * lg, lg)
            rows = pl.ds(r0, lg)

            def store(h, val):
                og_ref[bi, rows, h * G_DV:(h + 1) * G_DV] = val

            _gla_chunk(qg_ref[bi, rows, :], kg_ref[bi, rows, :], vg_ref[bi, rows, :],
                       la_ref[bi, rows, :], s_ref.at[bi], ggh_ref, store, lg=lg, sub=sub)
            return carry

        if tb == lm:
            m_body(0, 0)
        else:
            lax.fori_loop(0, tb // lm, m_body, 0)
        if tb == lg:
            g_body(0, 0)
        else:
            lax.fori_loop(0, tb // lg, g_body, 0)

        @pl.when(t == nt - 1)
        def _():
            c_out[bi] = c_ref[bi]
            n_out[bi] = n_ref[bi]
            m_out[bi] = m_ref[bi]
            s_out[bi] = s_ref[bi]


def _scan(qm, km, vm, gif, qg, kg, la, vg, gmh, ggh, state, *, bt, tb, lm, lg, sub):
    b, t, _ = qm.shape
    assert b % bt == 0 and t % tb == 0 and tb % lm == 0 and tb % lg == 0 and lg % sub == 0
    has_state = state is not None
    tok = lambda w: pl.BlockSpec((bt, tb, w), lambda i, j: (i, j, 0))
    c_shape = (M_HEADS, M_HEAD_DIM, M_HEAD_DIM)
    n_shape = (M_HEADS, 1, M_HEAD_DIM)
    m_shape = (1, LANES)
    s_shape = (G_HEADS, G_DK, G_DV)
    st_spec = lambda shp: pl.BlockSpec((bt,) + shp, lambda i, j: (i,) + (0,) * len(shp))
    st_specs = [st_spec(c_shape), st_spec(n_shape), st_spec(m_shape), st_spec(s_shape)]
    in_specs = [tok(M_WIDTH), tok(M_WIDTH), tok(M_WIDTH), tok(LANES), tok(G_KW), tok(G_KW),
                tok(G_KW), tok(G_VW), _const_spec((1, M_WIDTH)), _const_spec((1, G_VW))]
    args = [qm, km, vm, gif, qg, kg, la, vg, gmh, ggh]
    if has_state:
        in_specs += st_specs
        args += list(state)
    return pl.pallas_call(
        functools.partial(_scan_kernel, bt=bt, tb=tb, lm=lm, lg=lg, sub=sub, has_state=has_state),
        grid=(b // bt, t // tb),
        in_specs=in_specs,
        out_specs=[tok(M_WIDTH), tok(G_VW)] + st_specs,
        out_shape=[jax.ShapeDtypeStruct((b, t, M_WIDTH), F32),
                   jax.ShapeDtypeStruct((b, t, G_VW), F32),
                   jax.ShapeDtypeStruct((b,) + c_shape, F32),
                   jax.ShapeDtypeStruct((b,) + n_shape, F32),
                   jax.ShapeDtypeStruct((b,) + m_shape, F32),
                   jax.ShapeDtypeStruct((b,) + s_shape, F32)],
        scratch_shapes=[pltpu.VMEM((bt,) + c_shape, F32), pltpu.VMEM((bt,) + n_shape, F32),
                        pltpu.VMEM((bt,) + m_shape, F32), pltpu.VMEM((bt,) + s_shape, F32)],
        compiler_params=pltpu.CompilerParams(
            dimension_semantics=("parallel", "arbitrary"), vmem_limit_bytes=VMEM_LIMIT),
        name="mixer_scan_state" if has_state else "mixer_scan",
    )(*args)


POST_O, POST_R, POST_A, POST_B = 0, 1024, 2048, 3072


def _post_kernel(x_ref, hm_ref, og_ref, g_ref, w_ref, wpa_ref, wpb_ref, wo_ref, o_ref):
    x = x_ref[...]
    xn = _rms(x, g_ref[...]).astype(BF16)
    gate = lambda off: _dot(xn, w_ref[:, off:off + D_MODEL])
    hm = (_sigmoid(gate(POST_O)) * hm_ref[...]).astype(BF16)
    og = (_silu(gate(POST_R)) * og_ref[...]).astype(BF16)
    y = (_sigmoid(gate(POST_A)) * _dot(hm, wpa_ref[...])
         + _sigmoid(gate(POST_B)) * _dot(og, wpb_ref[...]))
    o_ref[...] = x + _dot(y.astype(BF16), wo_ref[...])


def _post(x2d, hm2d, og2d, g, w_post, wpa, wpb, wo, *, tm):
    n = x2d.shape[0]
    assert n % tm == 0
    tok = pl.BlockSpec((tm, D_MODEL), lambda i: (i, 0))
    return pl.pallas_call(
        _post_kernel,
        grid=(n // tm,),
        in_specs=[tok, tok, tok, _const_spec((1, D_MODEL)), _const_spec((D_MODEL, 4 * D_MODEL)),
                  _const_spec((M_WIDTH, D_MODEL)), _const_spec((G_VW, D_MODEL)),
                  _const_spec((D_MODEL, D_MODEL))],
        out_specs=tok,
        out_shape=jax.ShapeDtypeStruct((n, D_MODEL), F32),
        compiler_params=pltpu.CompilerParams(
            dimension_semantics=("parallel",), vmem_limit_bytes=VMEM_LIMIT),
        name="mixer_post",
    )(x2d, hm2d, og2d, g, w_post, wpa, wpb, wo)


def _pack_weights(g_ffn1, w_ffn1_up, w_ffn1_down, g_mix, w_in, conv_w, conv_b, w_mq, w_mk, b_if,
                  g_mhead, w_a2, b_a, g_ghead, w_pa, w_pb, w_o, g_ffn2, w_ffn2_up, w_ffn2_down,
                  g_final):
    l = 0
    row = lambda v: v.reshape(1, -1).astype(F32)
    edges = [0]
    for wd in (M_WIDTH, M_WIDTH, M_WIDTH, 2 * M_HEADS, G_KW, G_KW, G_VW, G_VW, G_RANK, D_MODEL, D_MODEL):
        edges.append(edges[-1] + wd)
    seg = lambda i: w_in[l][:, edges[i]:edges[i + 1]]
    u_m, v_m, o_m, if_m, q_g, k_g, v_g, r_g, a_g, g_a, g_b = (seg(i) for i in range(11))
    w_pre = jnp.concatenate([u_m, v_m, q_g, k_g, v_g], axis=1).astype(BF16)
    w_small = jnp.concatenate(
        [if_m, a_g, jnp.zeros((D_MODEL, LANES - 2 * M_HEADS - G_RANK), F32)], axis=1).astype(BF16)
    w_post = jnp.concatenate([o_m, r_g, g_a, g_b], axis=1).astype(BF16)
    bif = jnp.concatenate([b_if[l].reshape(-1), jnp.zeros((LANES - 2 * M_HEADS,), F32)]).reshape(1, LANES)
    wa2 = jnp.zeros((LANES, G_KW), F32).at[A_LANE0:A_LANE0 + G_RANK].set(w_a2[l]).astype(BF16)
    return dict(
        g_ffn1=row(g_ffn1[l]), w_ffn1_up=w_ffn1_up[l].astype(BF16), w_ffn1_down=w_ffn1_down[l].astype(BF16),
        g_mix=row(g_mix[l]), w_pre=w_pre, w_small=w_small, w_post=w_post,
        conv_w=conv_w[l].astype(F32), conv_b=row(conv_b[l]),
        w_mq=w_mq[l].astype(BF16), w_mk=w_mk[l].astype(BF16), bif=bif,
        g_mhead=row(g_mhead[l]), wa2=wa2, b_a=row(b_a[l]), g_ghead=row(g_ghead[l]),
        w_pa=w_pa[l].astype(BF16), w_pb=w_pb[l].astype(BF16), w_o=w_o[l].astype(BF16),
        g_ffn2=row(g_ffn2[l]), w_ffn2_up=w_ffn2_up[l].astype(BF16), w_ffn2_down=w_ffn2_down[l].astype(BF16),
        g_final=row(g_final))


def _trunk(x, state, w, *, tm, pre_tile, scan_tile, lm, lg, sub):
    b, t, _ = x.shape
    n = b * t
    x2d = x.reshape(n, D_MODEL)
    x1 = _ffn(x2d, w["g_ffn1"], w["w_ffn1_up"], w["w_ffn1_down"], w["g_final"], final_norm=False, tm=tm)
    if state is None:
        cst = jnp.zeros((b, SUBLANES, M_WIDTH), F32)
        scan_state = None
    else:
        conv0, c0, n0, m0, s0 = state
        cst = jnp.concatenate(
            [jnp.zeros((b, SUBLANES - (CONV_W - 1), M_WIDTH), F32), conv0.astype(F32)], axis=1)
        m0p = jnp.concatenate([m0, jnp.zeros((b, LANES - M_HEADS), F32)], axis=1).reshape(b, 1, LANES)
        scan_state = (c0, n0.reshape(b, M_HEADS, 1, M_HEAD_DIM), m0p, s0)
    qm, km, vm, gif, qg, kg, la, vg, cout = _pre(
        x1.reshape(b, t, D_MODEL), cst, w["g_mix"], w["w_pre"], w["w_small"], w["conv_w"], w["conv_b"],
        w["w_mq"], w["w_mk"], w["bif"], w["wa2"], w["b_a"], bt=pre_tile[0], tb=pre_tile[1])
    hm, og, c_new, n_new, m_new, s_new = _scan(
        qm, km, vm, gif, qg, kg, la, vg, w["g_mhead"], w["g_ghead"], scan_state,
        bt=scan_tile[0], tb=scan_tile[1], lm=lm, lg=lg, sub=sub)
    x2 = _post(x1, hm.reshape(n, M_WIDTH), og.reshape(n, G_VW), w["g_mix"], w["w_post"],
               w["w_pa"], w["w_pb"], w["w_o"], tm=tm)
    y = _ffn(x2, w["g_ffn2"], w["w_ffn2_up"], w["w_ffn2_down"], w["g_final"], final_norm=True, tm=tm)
    return (y.reshape(b, t, D_MODEL),
            cout[:, SUBLANES - (CONV_W - 1):, :][None],
            c_new[None],
            n_new.reshape(b, M_HEADS, M_HEAD_DIM)[None],
            m_new[:, 0, :M_HEADS][None],
            s_new[None])


def _tiles(b, t):
    n = b * t
    tm = math.gcd(n, 512)
    if t >= 256:
        tb = math.gcd(t, 256)
        lm = math.gcd(t, 128)
        lg = math.gcd(t, 64)
        return dict(tm=tm, pre_tile=(1, tb), scan_tile=(1, tb), lm=lm, lg=lg, sub=min(16, lg))
    bt = max(1, min(b, 512 // t))
    while b % bt:
        bt -= 1
    return dict(tm=tm, pre_tile=(bt, t), scan_tile=(1, t), lm=t, lg=t, sub=min(16, t))


def kernel(x_prompt, x_sample, state_conv, state_mlstm_C, state_mlstm_n, state_mlstm_m, state_gla_S,
           g_ffn1, w_ffn1_up, w_ffn1_down, g_mix, w_in, conv_w, conv_b, w_mq, w_mk, b_if, g_mhead,
           w_a2, b_a, g_ghead, w_pa, w_pb, w_o, g_ffn2, w_ffn2_up, w_ffn2_down, g_final):
    w = _pack_weights(g_ffn1, w_ffn1_up, w_ffn1_down, g_mix, w_in, conv_w, conv_b, w_mq, w_mk, b_if,
                      g_mhead, w_a2, b_a, g_ghead, w_pa, w_pb, w_o, g_ffn2, w_ffn2_up, w_ffn2_down,
                      g_final)
    bp, tp, _ = x_prompt.shape
    bs, ts, _ = x_sample.shape
    out_p = _trunk(x_prompt, None, w, **_tiles(bp, tp))
    state = (state_conv[0], state_mlstm_C[0], state_mlstm_n[0], state_mlstm_m[0], state_gla_S[0])
    out_s = _trunk(x_sample, state, w, **_tiles(bs, ts))
    return (out_p[0], out_s[0]) + out_p[1:] + out_s[1:]
```

```python
import functools
import math

import jax
import jax.numpy as jnp
from jax import lax
from jax.experimental import pallas as pl
from jax.experimental.pallas import tpu as pltpu

D_MODEL = 1024
M_HEADS = 4
M_HEAD_DIM = 256
M_WIDTH = 1024
CONV_W = 4
G_HEADS = 4
G_DK = 128
G_DV = 256
G_KW = 512
G_VW = 1024
G_RANK = 16
G_TAU = 16.0
D_FF = 2816
EPS = 1e-6

LANES = 128
SUBLANES = 8
BF16_ROWS = 16
VMEM_LIMIT = 56 * 1024 * 1024

F32 = jnp.float32
BF16 = jnp.bfloat16
NEG_INF = float("-inf")


def _sigmoid(x):
    return 1.0 / (1.0 + jnp.exp(-x))


def _silu(x):
    return x * _sigmoid(x)


def _log_sigmoid(x):
    return jnp.minimum(x, 0.0) - jnp.log(1.0 + jnp.exp(-jnp.abs(x)))


def _rms(x, g):
    return x * lax.rsqrt(jnp.mean(x * x, axis=-1, keepdims=True) + EPS) * g


def _dot(a, b):
    return jnp.dot(a, b, preferred_element_type=F32)


def _dot_nt(a, b):
    return lax.dot_general(a, b, (((1,), (1,)), ((), ())), preferred_element_type=F32)


def _dot_tn(a, b):
    return lax.dot_general(a, b, (((0,), (0,)), ((), ())), preferred_element_type=F32)


def _const_spec(shape):
    nd = len(shape)
    return pl.BlockSpec(shape, lambda *_: (0,) * nd, pipeline_mode=pl.Buffered(1))


FF_CHUNK = 256


def _ffn_kernel(x_ref, g_ref, wup_ref, wdown_ref, gfin_ref, o_ref, h_ref, *, final_norm):
    x = x_ref[...]
    xn = _rms(x, g_ref[...]).astype(BF16)
    for c in range(D_FF // FF_CHUNK):
        a = _dot(xn, wup_ref[:, c * FF_CHUNK:(c + 1) * FF_CHUNK])
        g = _dot(xn, wup_ref[:, D_FF + c * FF_CHUNK:D_FF + (c + 1) * FF_CHUNK])
        h_ref[:, c * FF_CHUNK:(c + 1) * FF_CHUNK] = (_silu(g) * a).astype(BF16)
    y = x + 0.5 * _dot(h_ref[...], wdown_ref[...])
    if final_norm:
        y = _rms(y, gfin_ref[...])
    o_ref[...] = y


def _ffn(x2d, g, wup, wdown, gfin, *, final_norm, tm):
    n = x2d.shape[0]
    assert n % tm == 0
    return pl.pallas_call(
        functools.partial(_ffn_kernel, final_norm=final_norm),
        grid=(n // tm,),
        in_specs=[
            pl.BlockSpec((tm, D_MODEL), lambda i: (i, 0)),
            _const_spec((1, D_MODEL)),
            _const_spec((D_MODEL, 2 * D_FF)),
            _const_spec((D_FF, D_MODEL)),
            _const_spec((1, D_MODEL)),
        ],
        out_specs=pl.BlockSpec((tm, D_MODEL), lambda i: (i, 0)),
        out_shape=jax.ShapeDtypeStruct((n, D_MODEL), F32),
        scratch_shapes=[pltpu.VMEM((tm, D_FF), BF16)],
        compiler_params=pltpu.CompilerParams(
            dimension_semantics=("parallel",), vmem_limit_bytes=VMEM_LIMIT),
        name="ffn_final" if final_norm else "ffn",
    )(x2d, g, wup, wdown, gfin)


PRE_U, PRE_V, PRE_QG, PRE_KG, PRE_VG = 0, 1024, 2048, 2560, 3072
PRE_WIDTH = 4096
A_LANE0 = 8


def _pre_kernel(x_ref, cst_ref, g_ref, w_ref, ws_ref, cw_ref, cb_ref, wq_ref, wk_ref, bif_ref,
                wa2_ref, ba_ref,
                qm_ref, km_ref, vm_ref, gif_ref, qg_ref, kg_ref, la_ref, vg_ref, cout_ref,
                cbuf_ref, *, bt, tb):
    t = pl.program_id(1)
    m = bt * tb
    xn = _rms(x_ref[...].reshape(m, D_MODEL), g_ref[...]).astype(BF16)

    @pl.when(t == 0)
    def _():
        cbuf_ref[:, 0:SUBLANES, :] = cst_ref[...]

    @pl.when(t > 0)
    def _():
        cbuf_ref[:, 0:SUBLANES, :] = cbuf_ref[:, tb:tb + SUBLANES, :]

    u = _dot(xn, w_ref[:, PRE_U:PRE_U + M_WIDTH])
    cbuf_ref[:, SUBLANES:SUBLANES + tb, :] = u.reshape(bt, tb, M_WIDTH)
    c = cb_ref[...].reshape(1, 1, M_WIDTH)
    for j in range(CONV_W):
        off = SUBLANES - (CONV_W - 1) + j
        c = c + cbuf_ref[:, off:off + tb, :] * cw_ref[j:j + 1, :].reshape(1, 1, M_WIDTH)
    cout_ref[...] = cbuf_ref[:, tb:tb + SUBLANES, :]
    ch = _silu(c).reshape(m, M_WIDTH).astype(BF16)
    for h in range(M_HEADS):
        sl = slice(h * M_HEAD_DIM, (h + 1) * M_HEAD_DIM)
        qm_ref[:, :, sl] = _dot(ch[:, sl], wq_ref[h]).astype(BF16).reshape(bt, tb, M_HEAD_DIM)
        km_ref[:, :, sl] = (_dot(ch[:, sl], wk_ref[h]) * (M_HEAD_DIM ** -0.5)).astype(BF16).reshape(
            bt, tb, M_HEAD_DIM)

    vm_ref[...] = _dot(xn, w_ref[:, PRE_V:PRE_V + M_WIDTH]).astype(BF16).reshape(bt, tb, M_WIDTH)
    qg_ref[...] = (_dot(xn, w_ref[:, PRE_QG:PRE_QG + G_KW]) * (G_DK ** -0.5)).astype(BF16).reshape(
        bt, tb, G_KW)
    kg_ref[...] = _dot(xn, w_ref[:, PRE_KG:PRE_KG + G_KW]).astype(BF16).reshape(bt, tb, G_KW)
    vg_ref[...] = _dot(xn, w_ref[:, PRE_VG:PRE_VG + G_VW]).astype(BF16).reshape(bt, tb, G_VW)

    small = _dot(xn, ws_ref[...])
    gates = small + bif_ref[...]
    lane = lax.broadcasted_iota(jnp.int32, gates.shape, 1)
    gif = jnp.where(lane < M_HEADS, gates, jnp.where(lane < 2 * M_HEADS, _log_sigmoid(gates), 0.0))
    gif_ref[...] = gif.reshape(bt, tb, LANES)
    a2 = _dot(small.astype(BF16), wa2_ref[...]) + ba_ref[...]
    la_ref[...] = (_log_sigmoid(a2) * (1.0 / G_TAU)).reshape(bt, tb, G_KW)


def _pre(x, cst, g, w_pre, w_small, conv_w, conv_b, wq, wk, bif, wa2, ba, *, bt, tb):
    b, t, _ = x.shape
    assert b % bt == 0 and t % tb == 0 and tb % SUBLANES == 0
    tok = lambda w: pl.BlockSpec((bt, tb, w), lambda i, j: (i, j, 0))
    per_seq = pl.BlockSpec((bt, SUBLANES, M_WIDTH), lambda i, j: (i, 0, 0))
    sds = lambda w, dt: jax.ShapeDtypeStruct((b, t, w), dt)
    return pl.pallas_call(
        functools.partial(_pre_kernel, bt=bt, tb=tb),
        grid=(b // bt, t // tb),
        in_specs=[
            tok(D_MODEL), per_seq,
            _const_spec((1, D_MODEL)),
            _const_spec((D_MODEL, PRE_WIDTH)),
            _const_spec((D_MODEL, LANES)),
            _const_spec((CONV_W, M_WIDTH)),
            _const_spec((1, M_WIDTH)),
            _const_spec((M_HEADS, M_HEAD_DIM, M_HEAD_DIM)),
            _const_spec((M_HEADS, M_HEAD_DIM, M_HEAD_DIM)),
            _const_spec((1, LANES)),
            _const_spec((LANES, G_KW)),
            _const_spec((1, G_KW)),
        ],
        out_specs=[tok(M_WIDTH), tok(M_WIDTH), tok(M_WIDTH), tok(LANES), tok(G_KW), tok(G_KW),
                   tok(G_KW), tok(G_VW), per_seq],
        out_shape=[sds(M_WIDTH, BF16), sds(M_WIDTH, BF16), sds(M_WIDTH, BF16), sds(LANES, F32),
                   sds(G_KW, BF16), sds(G_KW, BF16), sds(G_KW, F32), sds(G_VW, BF16),
                   jax.ShapeDtypeStruct((b, SUBLANES, M_WIDTH), F32)],
        scratch_shapes=[pltpu.VMEM((bt, tb + SUBLANES, M_WIDTH), F32)],
        compiler_params=pltpu.CompilerParams(
            dimension_semantics=("parallel", "arbitrary"), vmem_limit_bytes=VMEM_LIMIT),
        name="mixer_pre",
    )(x, cst, g, w_pre, w_small, conv_w, conv_b, wq, wk, bif, wa2, ba)


def _scan_rows(x, op, fill):
    n = x.shape[0]
    row = lax.broadcasted_iota(jnp.int32, x.shape, 0)
    s = 1
    while s < n:
        x = op(x, jnp.where(row >= s, pltpu.roll(x, s, axis=0), fill))
        s *= 2
    return x


def _col_of_row(row):
    return jnp.transpose(jnp.broadcast_to(row, (SUBLANES, LANES)))[:, 0:1]


def _mlstm_chunk(q, k, v, gif, c_ref, n_ref, m_ref, gh_ref, hm_store, *, lm):
    cs = _scan_rows(gif, jnp.add, 0.0)
    b = pltpu.roll(cs, LANES - M_HEADS, axis=1)
    g = gif - b
    m_prev = m_ref[...]
    mx = jnp.maximum(m_prev, _scan_rows(g, jnp.maximum, NEG_INF))
    inter = jnp.exp(m_prev - mx)
    nfloor = jnp.exp(-(b + mx))
    g_t = jnp.transpose(g)
    mx_last = mx[lm - 1:lm, :]
    w_in = jnp.exp(g - mx_last)
    decay = jnp.exp(m_prev - mx_last)
    m_ref[...] = b[lm - 1:lm, :] + mx_last

    row = lax.broadcasted_iota(jnp.int32, (lm, lm), 0)
    col = lax.broadcasted_iota(jnp.int32, (lm, lm), 1)
    causal = col <= row
    heads = range(M_HEADS)
    sl = [slice(h * M_HEAD_DIM, (h + 1) * M_HEAD_DIM) for h in heads]
    c_old = [c_ref[h] for h in heads]
    n_old = [n_ref[h] for h in heads]
    dm = [jnp.exp(jnp.where(causal, g_t[h:h + 1, :] - mx[:, h:h + 1], NEG_INF)) for h in heads]
    if lm % BF16_ROWS == 0:
        kn = [jnp.concatenate([k[:, sl[h]],
                               jnp.broadcast_to(n_old[h].astype(BF16), (BF16_ROWS, M_HEAD_DIM))], axis=0)
              for h in heads]
        qk = [_dot_nt(q[:, sl[h]], kn[h]) for h in heads]
        qn = [qk[h][:, lm:lm + 1] for h in heads]
        s = [qk[h][:, :lm] * dm[h] for h in heads]
    else:
        qn = [jnp.sum(q[:, sl[h]].astype(F32) * n_old[h], axis=-1, keepdims=True)
              for h in heads]
        s = [_dot_nt(q[:, sl[h]], k[:, sl[h]]) * dm[h] for h in heads]
    num = [inter[:, h:h + 1] * _dot_nt(q[:, sl[h]], c_old[h].astype(BF16))
           + _dot(s[h].astype(BF16), v[:, sl[h]]) for h in heads]
    for h in heads:
        den = inter[:, h:h + 1] * qn[h] + jnp.sum(s[h], axis=-1, keepdims=True)
        rcp = 1.0 / jnp.maximum(jnp.abs(den), nfloor[:, h:h + 1])
        ms = jnp.mean(num[h] * num[h], axis=-1, keepdims=True)
        scale = rcp * lax.rsqrt(rcp * rcp * ms + EPS)
        hm_store(h, num[h] * scale * gh_ref[:, sl[h]])
    for h in heads:
        w_h = w_in[:, h:h + 1]
        dec_h = decay[:, h:h + 1]
        wv = (w_h * v[:, sl[h]].astype(F32)).astype(BF16)
        c_ref[h] = dec_h * c_old[h] + _dot_tn(wv, k[:, sl[h]])
        n_ref[h] = dec_h * n_old[h] + jnp.sum(w_h * k[:, sl[h]].astype(F32), axis=0, keepdims=True)


GLA_SAFE_SPAN = 40.0


def _gla_scores_guarded(qh, kh, a, *, lg, sub):
    row = lax.broadcasted_iota(jnp.int32, (lg, lg), 0)
    col = lax.broadcasted_iota(jnp.int32, (lg, lg), 1)
    att = jnp.zeros((lg, lg), F32)
    c = sub
    while c < lg:
        nblk = lg // (2 * c)
        a_ref = jnp.concatenate(
            [jnp.broadcast_to(a[2 * c * i + c - 1:2 * c * i + c, :], (2 * c, G_DK))
             for i in range(nblk)], axis=0)
        qt = (qh * jnp.exp(jnp.minimum(a - a_ref, 0.0))).astype(BF16)
        kt = (kh * jnp.exp(jnp.minimum(a_ref - a, 0.0))).astype(BF16)
        valid = (row // (2 * c) == col // (2 * c)) & (row % (2 * c) >= c) & (col % (2 * c) < c)
        att = att + jnp.where(valid, _dot_nt(qt, kt), 0.0)
        c *= 2
    rows = lax.broadcasted_iota(jnp.int32, (sub, lg), 0)
    cols = lax.broadcasted_iota(jnp.int32, (sub, lg), 1)
    blocks = []
    for blk in range(lg // sub):
        r0 = blk * sub
        a_b = a[r0:r0 + sub, :]
        q_b = qh[r0:r0 + sub, :]
        acc = jnp.zeros((sub, lg), F32)
        for s in range(sub):
            e = jnp.exp(jnp.minimum(a_b - a[r0 + s:r0 + s + 1, :], 0.0))
            p = jnp.sum(q_b * e * kh[r0 + s:r0 + s + 1, :], axis=-1, keepdims=True)
            acc = jnp.where((cols == r0 + s) & (rows >= s), p, acc)
        blocks.append(acc)
    return att + jnp.concatenate(blocks, axis=0)


def _gla_chunk(q, k, v, la, s_ref, gh_ref, og_store, *, lg, sub, direct):
    heads = range(G_HEADS)
    ks = [slice(h * G_DK, (h + 1) * G_DK) for h in heads]
    vs = [slice(h * G_DV, (h + 1) * G_DV) for h in heads]
    a = _scan_rows(la, jnp.add, 0.0)
    a_last = a[lg - 1:lg, :]
    qf = q.astype(F32)
    kf = k.astype(F32)
    qs = (qf * jnp.exp(a)).astype(BF16)
    kl = (kf * jnp.exp(a_last - a)).astype(BF16)
    s_old = [s_ref[h] for h in heads]
    if direct:
        kt = (kf * jnp.exp(-a)).astype(BF16)
        row = lax.broadcasted_iota(jnp.int32, (lg, lg), 0)
        col = lax.broadcasted_iota(jnp.int32, (lg, lg), 1)
        att = [jnp.where(col <= row, _dot_nt(qs[:, ks[h]], kt[:, ks[h]]), 0.0) for h in heads]
    else:
        att = [_gla_scores_guarded(qf[:, ks[h]], kf[:, ks[h]], a[:, ks[h]], lg=lg, sub=sub)
               for h in heads]
    o = [_dot(qs[:, ks[h]], s_old[h].astype(BF16)) + _dot(att[h].astype(BF16), v[:, vs[h]])
         for h in heads]
    for h in heads:
        og_store(h, _rms(o[h], gh_ref[:, vs[h]]))
    for h in heads:
        s_ref[h] = (jnp.exp(_col_of_row(a_last[:, ks[h]])) * s_old[h]
                    + _dot_tn(kl[:, ks[h]], v[:, vs[h]]))


def _scan_kernel(*refs, bt, tb, lm, lg, sub, has_state):
    if has_state:
        (qm_ref, km_ref, vm_ref, gif_ref, qg_ref, kg_ref, la_ref, vg_ref, gmh_ref, ggh_ref,
         c0_ref, n0_ref, m0_ref, s0_ref,
         hm_ref, og_ref, c_out, n_out, m_out, s_out, c_ref, n_ref, m_ref, s_ref) = refs
    else:
        (qm_ref, km_ref, vm_ref, gif_ref, qg_ref, kg_ref, la_ref, vg_ref, gmh_ref, ggh_ref,
         hm_ref, og_ref, c_out, n_out, m_out, s_out, c_ref, n_ref, m_ref, s_ref) = refs
    t = pl.program_id(1)
    nt = pl.num_programs(1)

    for bi in range(bt):
        @pl.when(t == 0)
        def _():
            if has_state:
                c_ref[bi] = c0_ref[bi]
                n_ref[bi] = n0_ref[bi]
                m_ref[bi] = m0_ref[bi]
                s_ref[bi] = s0_ref[bi]
            else:
                c_ref[bi] = jnp.zeros(c_ref.shape[1:], F32)
                n_ref[bi] = jnp.zeros(n_ref.shape[1:], F32)
                m_ref[bi] = jnp.zeros(m_ref.shape[1:], F32)
                s_ref[bi] = jnp.zeros(s_ref.shape[1:], F32)

        def m_body(ci, carry):
            r0 = pl.multiple_of(ci * lm, lm)
            rows = pl.ds(r0, lm)

            def store(h, val):
                hm_ref[bi, rows, h * M_HEAD_DIM:(h + 1) * M_HEAD_DIM] = val

            _mlstm_chunk(qm_ref[bi, rows, :], km_ref[bi, rows, :], vm_ref[bi, rows, :],
                         gif_ref[bi, rows, :], c_ref.at[bi], n_ref.at[bi], m_ref.at[bi],
                         gmh_ref, store, lm=lm)
            return carry

        def g_body(ci, carry, *, direct):
            r0 = pl.multiple_of(ci * lg, lg)
            rows = pl.ds(r0, lg)

            def store(h, val):
                og_ref[bi, rows, h * G_DV:(h + 1) * G_DV] = val

            _gla_chunk(qg_ref[bi, rows, :], kg_ref[bi, rows, :], vg_ref[bi, rows, :],
                       la_ref[bi, rows, :], s_ref.at[bi], ggh_ref, store, lg=lg, sub=sub, direct=direct)
            return carry

        if tb == lm:
            m_body(0, 0)
        else:
            lax.fori_loop(0, tb // lm, m_body, 0)

        chunk_decay = jnp.sum(la_ref[bi].reshape(tb // lg, lg, G_KW), axis=1)
        span_ok = jnp.min(chunk_decay) >= -GLA_SAFE_SPAN
        for direct in (True, False):
            @pl.when(span_ok if direct else jnp.logical_not(span_ok))
            def _():
                body = functools.partial(g_body, direct=direct)
                if tb == lg:
                    body(0, 0)
                else:
                    lax.fori_loop(0, tb // lg, body, 0)

        @pl.when(t == nt - 1)
        def _():
            c_out[bi] = c_ref[bi]
            n_out[bi] = n_ref[bi]
            m_out[bi] = m_ref[bi]
            s_out[bi] = s_ref[bi]


def _scan(qm, km, vm, gif, qg, kg, la, vg, gmh, ggh, state, *, bt, tb, lm, lg, sub):
    b, t, _ = qm.shape
    assert b % bt == 0 and t % tb == 0 and tb % lm == 0 and tb % lg == 0 and lg % sub == 0
    has_state = state is not None
    tok = lambda w: pl.BlockSpec((bt, tb, w), lambda i, j: (i, j, 0))
    c_shape = (M_HEADS, M_HEAD_DIM, M_HEAD_DIM)
    n_shape = (M_HEADS, 1, M_HEAD_DIM)
    m_shape = (1, LANES)
    s_shape = (G_HEADS, G_DK, G_DV)
    st_spec = lambda shp: pl.BlockSpec((bt,) + shp, lambda i, j: (i,) + (0,) * len(shp))
    st_specs = [st_spec(c_shape), st_spec(n_shape), st_spec(m_shape), st_spec(s_shape)]
    in_specs = [tok(M_WIDTH), tok(M_WIDTH), tok(M_WIDTH), tok(LANES), tok(G_KW), tok(G_KW),
                tok(G_KW), tok(G_VW), _const_spec((1, M_WIDTH)), _const_spec((1, G_VW))]
    args = [qm, km, vm, gif, qg, kg, la, vg, gmh, ggh]
    if has_state:
        in_specs += st_specs
        args += list(state)
    return pl.pallas_call(
        functools.partial(_scan_kernel, bt=bt, tb=tb, lm=lm, lg=lg, sub=sub, has_state=has_state),
        grid=(b // bt, t // tb),
        in_specs=in_specs,
        out_specs=[tok(M_WIDTH), tok(G_VW)] + st_specs,
        out_shape=[jax.ShapeDtypeStruct((b, t, M_WIDTH), F32),
                   jax.ShapeDtypeStruct((b, t, G_VW), F32),
                   jax.ShapeDtypeStruct((b,) + c_shape, F32),
                   jax.ShapeDtypeStruct((b,) + n_shape, F32),
                   jax.ShapeDtypeStruct((b,) + m_shape, F32),
                   jax.ShapeDtypeStruct((b,) + s_shape, F32)],
        scratch_shapes=[pltpu.VMEM((bt,) + c_shape, F32), pltpu.VMEM((bt,) + n_shape, F32),
                        pltpu.VMEM((bt,) + m_shape, F32), pltpu.VMEM((bt,) + s_shape, F32)],
        compiler_params=pltpu.CompilerParams(
            dimension_semantics=("parallel", "arbitrary"), vmem_limit_bytes=VMEM_LIMIT),
        name="mixer_scan_state" if has_state else "mixer_scan",
    )(*args)


POST_O, POST_R, POST_A, POST_B = 0, 1024, 2048, 3072


def _post_kernel(x_ref, hm_ref, og_ref, g_ref, w_ref, wpa_ref, wpb_ref, wo_ref, o_ref):
    x = x_ref[...]
    xn = _rms(x, g_ref[...]).astype(BF16)
    gate = lambda off: _dot(xn, w_ref[:, off:off + D_MODEL])
    hm = (_sigmoid(gate(POST_O)) * hm_ref[...]).astype(BF16)
    og = (_silu(gate(POST_R)) * og_ref[...]).astype(BF16)
    y = (_sigmoid(gate(POST_A)) * _dot(hm, wpa_ref[...])
         + _sigmoid(gate(POST_B)) * _dot(og, wpb_ref[...]))
    o_ref[...] = x + _dot(y.astype(BF16), wo_ref[...])


def _post(x2d, hm2d, og2d, g, w_post, wpa, wpb, wo, *, tm):
    n = x2d.shape[0]
    assert n % tm == 0
    tok = pl.BlockSpec((tm, D_MODEL), lambda i: (i, 0))
    return pl.pallas_call(
        _post_kernel,
        grid=(n // tm,),
        in_specs=[tok, tok, tok, _const_spec((1, D_MODEL)), _const_spec((D_MODEL, 4 * D_MODEL)),
                  _const_spec((M_WIDTH, D_MODEL)), _const_spec((G_VW, D_MODEL)),
                  _const_spec((D_MODEL, D_MODEL))],
        out_specs=tok,
        out_shape=jax.ShapeDtypeStruct((n, D_MODEL), F32),
        compiler_params=pltpu.CompilerParams(
            dimension_semantics=("parallel",), vmem_limit_bytes=VMEM_LIMIT),
        name="mixer_post",
    )(x2d, hm2d, og2d, g, w_post, wpa, wpb, wo)


def _pack_weights(g_ffn1, w_ffn1_up, w_ffn1_down, g_mix, w_in, conv_w, conv_b, w_mq, w_mk, b_if,
                  g_mhead, w_a2, b_a, g_ghead, w_pa, w_pb, w_o, g_ffn2, w_ffn2_up, w_ffn2_down,
                  g_final):
    l = 0
    row = lambda v: v.reshape(1, -1).astype(F32)
    edges = [0]
    for wd in (M_WIDTH, M_WIDTH, M_WIDTH, 2 * M_HEADS, G_KW, G_KW, G_VW, G_VW, G_RANK, D_MODEL, D_MODEL):
        edges.append(edges[-1] + wd)
    seg = lambda i: w_in[l][:, edges[i]:edges[i + 1]]
    u_m, v_m, o_m, if_m, q_g, k_g, v_g, r_g, a_g, g_a, g_b = (seg(i) for i in range(11))
    w_pre = jnp.concatenate([u_m, v_m, q_g, k_g, v_g], axis=1).astype(BF16)
    w_small = jnp.concatenate(
        [if_m, a_g, jnp.zeros((D_MODEL, LANES - 2 * M_HEADS - G_RANK), F32)], axis=1).astype(BF16)
    w_post = jnp.concatenate([o_m, r_g, g_a, g_b], axis=1).astype(BF16)
    bif = jnp.concatenate([b_if[l].reshape(-1), jnp.zeros((LANES - 2 * M_HEADS,), F32)]).reshape(1, LANES)
    wa2 = jnp.zeros((LANES, G_KW), F32).at[A_LANE0:A_LANE0 + G_RANK].set(w_a2[l]).astype(BF16)
    return dict(
        g_ffn1=row(g_ffn1[l]), w_ffn1_up=w_ffn1_up[l].astype(BF16), w_ffn1_down=w_ffn1_down[l].astype(BF16),
        g_mix=row(g_mix[l]), w_pre=w_pre, w_small=w_small, w_post=w_post,
        conv_w=conv_w[l].astype(F32), conv_b=row(conv_b[l]),
        w_mq=w_mq[l].astype(BF16), w_mk=w_mk[l].astype(BF16), bif=bif,
        g_mhead=row(g_mhead[l]), wa2=wa2, b_a=row(b_a[l]), g_ghead=row(g_ghead[l]),
        w_pa=w_pa[l].astype(BF16), w_pb=w_pb[l].astype(BF16), w_o=w_o[l].astype(BF16),
        g_ffn2=row(g_ffn2[l]), w_ffn2_up=w_ffn2_up[l].astype(BF16), w_ffn2_down=w_ffn2_down[l].astype(BF16),
        g_final=row(g_final))


def _trunk(x, state, w, *, tm, pre_tile, scan_tile, lm, lg, sub):
    b, t, _ = x.shape
    n = b * t
    x2d = x.reshape(n, D_MODEL)
    x1 = _ffn(x2d, w["g_ffn1"], w["w_ffn1_up"], w["w_ffn1_down"], w["g_final"], final_norm=False, tm=tm)
    if state is None:
        cst = jnp.zeros((b, SUBLANES, M_WIDTH), F32)
        scan_state = None
    else:
        conv0, c0, n0, m0, s0 = state
        cst = jnp.concatenate(
            [jnp.zeros((b, SUBLANES - (CONV_W - 1), M_WIDTH), F32), conv0.astype(F32)], axis=1)
        m0p = jnp.concatenate([m0, jnp.zeros((b, LANES - M_HEADS), F32)], axis=1).reshape(b, 1, LANES)
        scan_state = (c0, n0.reshape(b, M_HEADS, 1, M_HEAD_DIM), m0p, s0)
    qm, km, vm, gif, qg, kg, la, vg, cout = _pre(
        x1.reshape(b, t, D_MODEL), cst, w["g_mix"], w["w_pre"], w["w_small"], w["conv_w"], w["conv_b"],
        w["w_mq"], w["w_mk"], w["bif"], w["wa2"], w["b_a"], bt=pre_tile[0], tb=pre_tile[1])
    hm, og, c_new, n_new, m_new, s_new = _scan(
        qm, km, vm, gif, qg, kg, la, vg, w["g_mhead"], w["g_ghead"], scan_state,
        bt=scan_tile[0], tb=scan_tile[1], lm=lm, lg=lg, sub=sub)
    x2 = _post(x1, hm.reshape(n, M_WIDTH), og.reshape(n, G_VW), w["g_mix"], w["w_post"],
               w["w_pa"], w["w_pb"], w["w_o"], tm=tm)
    y = _ffn(x2, w["g_ffn2"], w["w_ffn2_up"], w["w_ffn2_down"], w["g_final"], final_norm=True, tm=tm)
    return (y.reshape(b, t, D_MODEL),
            cout[:, SUBLANES - (CONV_W - 1):, :][None],
            c_new[None],
            n_new.reshape(b, M_HEADS, M_HEAD_DIM)[None],
            m_new[:, 0, :M_HEADS][None],
            s_new[None])


def _tiles(b, t):
    n = b * t
    tm = math.gcd(n, 512)
    if t >= 256:
        tb = math.gcd(t, 256)
        lm = math.gcd(t, 128)
        lg = math.gcd(t, 64)
        return dict(tm=tm, pre_tile=(1, tb), scan_tile=(1, tb), lm=lm, lg=lg, sub=min(16, lg))
    bt = max(1, min(b, 512 // t))
    while b % bt:
        bt -= 1
    return dict(tm=tm, pre_tile=(bt, t), scan_tile=(1, t), lm=t, lg=t, sub=min(16, t))


def kernel(x_prompt, x_sample, state_conv, state_mlstm_C, state_mlstm_n, state_mlstm_m, state_gla_S,
           g_ffn1, w_ffn1_up, w_ffn1_down, g_mix, w_in, conv_w, conv_b, w_mq, w_mk, b_if, g_mhead,
           w_a2, b_a, g_ghead, w_pa, w_pb, w_o, g_ffn2, w_ffn2_up, w_ffn2_down, g_final):
    w = _pack_weights(g_ffn1, w_ffn1_up, w_ffn1_down, g_mix, w_in, conv_w, conv_b, w_mq, w_mk, b_if,
                      g_mhead, w_a2, b_a, g_ghead, w_pa, w_pb, w_o, g_ffn2, w_ffn2_up, w_ffn2_down,
                      g_final)
    bp, tp, _ = x_prompt.shape
    bs, ts, _ = x_sample.shape
    out_p = _trunk(x_prompt, None, w, **_tiles(bp, tp))
    state = (state_conv[0], state_mlstm_C[0], state_mlstm_n[0], state_mlstm_m[0], state_gla_S[0])
    out_s = _trunk(x_sample, state, w, **_tiles(bs, ts))
    return (out_p[0], out_s[0]) + out_p[1:] + out_s[1:]
```

```python
import functools
import math

import jax
import jax.numpy as jnp
from jax import lax
from jax.experimental import pallas as pl
from jax.experimental.pallas import tpu as pltpu

D_MODEL = 1024
M_HEADS = 4
M_HEAD_DIM = 256
M_WIDTH = 1024
CONV_W = 4
G_HEADS = 4
G_DK = 128
G_DV = 256
G_KW = 512
G_VW = 1024
G_RANK = 16
G_TAU = 16.0
D_FF = 2816
EPS = 1e-6

LANES = 128
SUBLANES = 8
BF16_ROWS = 16
VMEM_LIMIT = 56 * 1024 * 1024

F32 = jnp.float32
BF16 = jnp.bfloat16
NEG_INF = float("-inf")


def _sigmoid(x):
    return 1.0 / (1.0 + jnp.exp(-x))


def _silu(x):
    return x * _sigmoid(x)


def _log_sigmoid(x):
    return jnp.minimum(x, 0.0) - jnp.log(1.0 + jnp.exp(-jnp.abs(x)))


def _rms(x, g):
    return x * lax.rsqrt(jnp.mean(x * x, axis=-1, keepdims=True) + EPS) * g


def _dot(a, b):
    return jnp.dot(a, b, preferred_element_type=F32)


def _dot_nt(a, b):
    return lax.dot_general(a, b, (((1,), (1,)), ((), ())), preferred_element_type=F32)


def _dot_tn(a, b):
    return lax.dot_general(a, b, (((0,), (0,)), ((), ())), preferred_element_type=F32)


def _const_spec(shape):
    nd = len(shape)
    return pl.BlockSpec(shape, lambda *_: (0,) * nd, pipeline_mode=pl.Buffered(1))


FF_CHUNK = 256


def _ffn_kernel(x_ref, g_ref, wup_ref, wdown_ref, gfin_ref, o_ref, h_ref, *, final_norm):
    x = x_ref[...]
    xn = _rms(x, g_ref[...]).astype(BF16)
    for c in range(D_FF // FF_CHUNK):
        a = _dot(xn, wup_ref[:, c * FF_CHUNK:(c + 1) * FF_CHUNK])
        g = _dot(xn, wup_ref[:, D_FF + c * FF_CHUNK:D_FF + (c + 1) * FF_CHUNK])
        h_ref[:, c * FF_CHUNK:(c + 1) * FF_CHUNK] = (_silu(g) * a).astype(BF16)
    y = x + 0.5 * _dot(h_ref[...], wdown_ref[...])
    if final_norm:
        y = _rms(y, gfin_ref[...])
    o_ref[...] = y


def _ffn(x2d, g, wup, wdown, gfin, *, final_norm, tm):
    n = x2d.shape[0]
    assert n % tm == 0
    return pl.pallas_call(
        functools.partial(_ffn_kernel, final_norm=final_norm),
        grid=(n // tm,),
        in_specs=[
            pl.BlockSpec((tm, D_MODEL), lambda i: (i, 0)),
            _const_spec((1, D_MODEL)),
            _const_spec((D_MODEL, 2 * D_FF)),
            _const_spec((D_FF, D_MODEL)),
            _const_spec((1, D_MODEL)),
        ],
        out_specs=pl.BlockSpec((tm, D_MODEL), lambda i: (i, 0)),
        out_shape=jax.ShapeDtypeStruct((n, D_MODEL), F32),
        scratch_shapes=[pltpu.VMEM((tm, D_FF), BF16)],
        compiler_params=pltpu.CompilerParams(
            dimension_semantics=("parallel",), vmem_limit_bytes=VMEM_LIMIT),
        name="ffn_final" if final_norm else "ffn",
    )(x2d, g, wup, wdown, gfin)


PRE_U, PRE_V, PRE_QG, PRE_KG, PRE_VG = 0, 1024, 2048, 2560, 3072
PRE_WIDTH = 4096
A_LANE0 = 8


def _pre_kernel(x_ref, cst_ref, g_ref, w_ref, ws_ref, cw_ref, cb_ref, wq_ref, wk_ref, bif_ref,
                wa2_ref, ba_ref,
                qm_ref, km_ref, vm_ref, gif_ref, qg_ref, kg_ref, la_ref, vg_ref, cout_ref,
                cbuf_ref, *, bt, tb, time_minor):
    t = pl.program_id(1)
    m = bt * tb
    xn = _rms(x_ref[...].reshape(m, D_MODEL), g_ref[...]).astype(BF16)

    @pl.when(t == 0)
    def _():
        cbuf_ref[:, 0:SUBLANES, :] = cst_ref[...]

    @pl.when(t > 0)
    def _():
        cbuf_ref[:, 0:SUBLANES, :] = cbuf_ref[:, tb:tb + SUBLANES, :]

    u = _dot(xn, w_ref[:, PRE_U:PRE_U + M_WIDTH])
    cbuf_ref[:, SUBLANES:SUBLANES + tb, :] = u.reshape(bt, tb, M_WIDTH)
    c = cb_ref[...].reshape(1, 1, M_WIDTH)
    for j in range(CONV_W):
        off = SUBLANES - (CONV_W - 1) + j
        c = c + cbuf_ref[:, off:off + tb, :] * cw_ref[j:j + 1, :].reshape(1, 1, M_WIDTH)
    cout_ref[...] = cbuf_ref[:, tb:tb + SUBLANES, :]
    ch = _silu(c).reshape(m, M_WIDTH).astype(BF16)
    vm = _dot(xn, w_ref[:, PRE_V:PRE_V + M_WIDTH])
    for h in range(M_HEADS):
        sl = slice(h * M_HEAD_DIM, (h + 1) * M_HEAD_DIM)
        qh = _dot(ch[:, sl], wq_ref[h])
        if time_minor:
            qm_ref[0, sl, :] = jnp.transpose(qh).astype(BF16)
            vm_ref[0, sl, :] = jnp.transpose(vm[:, sl]).astype(BF16)
        else:
            qm_ref[:, :, sl] = qh.astype(BF16).reshape(bt, tb, M_HEAD_DIM)
        km_ref[:, :, sl] = (_dot(ch[:, sl], wk_ref[h]) * (M_HEAD_DIM ** -0.5)).astype(BF16).reshape(
            bt, tb, M_HEAD_DIM)
    if not time_minor:
        vm_ref[...] = vm.astype(BF16).reshape(bt, tb, M_WIDTH)

    qg_ref[...] = (_dot(xn, w_ref[:, PRE_QG:PRE_QG + G_KW]) * (G_DK ** -0.5)).astype(BF16).reshape(
        bt, tb, G_KW)
    kg_ref[...] = _dot(xn, w_ref[:, PRE_KG:PRE_KG + G_KW]).astype(BF16).reshape(bt, tb, G_KW)
    vg_ref[...] = _dot(xn, w_ref[:, PRE_VG:PRE_VG + G_VW]).astype(BF16).reshape(bt, tb, G_VW)

    small = _dot(xn, ws_ref[...])
    gates = small + bif_ref[...]
    lane = lax.broadcasted_iota(jnp.int32, gates.shape, 1)
    gif = jnp.where(lane < M_HEADS, gates, jnp.where(lane < 2 * M_HEADS, _log_sigmoid(gates), 0.0))
    if time_minor:
        b = pltpu.roll(_scan_rows(gif, jnp.add, 0.0), LANES - M_HEADS, axis=1)
        g = gif - b
        gmax = _scan_rows(g, jnp.maximum, NEG_INF)
        packed = jnp.where(lane < M_HEADS, g,
                           jnp.where(lane < 2 * M_HEADS, pltpu.roll(gmax, M_HEADS, axis=1),
                                     jnp.where(lane < 3 * M_HEADS, pltpu.roll(b, 2 * M_HEADS, axis=1), 0.0)))
        gif_ref[0] = jnp.transpose(packed)[0:2 * SUBLANES, :]
    else:
        gif_ref[...] = gif.reshape(bt, tb, LANES)
    a2 = _dot(small.astype(BF16), wa2_ref[...]) + ba_ref[...]
    la_ref[...] = (_log_sigmoid(a2) * (1.0 / G_TAU)).reshape(bt, tb, G_KW)


def _pre(x, cst, g, w_pre, w_small, conv_w, conv_b, wq, wk, bif, wa2, ba, *, bt, tb, time_minor):
    b, t, _ = x.shape
    assert b % bt == 0 and t % tb == 0 and tb % SUBLANES == 0
    assert not time_minor or (bt == 1 and tb % LANES == 0)
    tok = lambda w: pl.BlockSpec((bt, tb, w), lambda i, j: (i, j, 0))
    per_seq = pl.BlockSpec((bt, SUBLANES, M_WIDTH), lambda i, j: (i, 0, 0))
    sds = lambda w, dt: jax.ShapeDtypeStruct((b, t, w), dt)
    if time_minor:
        tok_t = lambda w: pl.BlockSpec((1, w, tb), lambda i, j: (i, 0, j))
        sds_t = lambda w, dt: jax.ShapeDtypeStruct((b, w, t), dt)
        qv_spec, qv_shape = tok_t(M_WIDTH), sds_t(M_WIDTH, BF16)
        gif_spec, gif_shape = tok_t(2 * SUBLANES), sds_t(2 * SUBLANES, F32)
    else:
        qv_spec, qv_shape = tok(M_WIDTH), sds(M_WIDTH, BF16)
        gif_spec, gif_shape = tok(LANES), sds(LANES, F32)
    return pl.pallas_call(
        functools.partial(_pre_kernel, bt=bt, tb=tb, time_minor=time_minor),
        grid=(b // bt, t // tb),
        in_specs=[
            tok(D_MODEL), per_seq,
            _const_spec((1, D_MODEL)),
            _const_spec((D_MODEL, PRE_WIDTH)),
            _const_spec((D_MODEL, LANES)),
            _const_spec((CONV_W, M_WIDTH)),
            _const_spec((1, M_WIDTH)),
            _const_spec((M_HEADS, M_HEAD_DIM, M_HEAD_DIM)),
            _const_spec((M_HEADS, M_HEAD_DIM, M_HEAD_DIM)),
            _const_spec((1, LANES)),
            _const_spec((LANES, G_KW)),
            _const_spec((1, G_KW)),
        ],
        out_specs=[qv_spec, tok(M_WIDTH), qv_spec, gif_spec, tok(G_KW), tok(G_KW),
                   tok(G_KW), tok(G_VW), per_seq],
        out_shape=[qv_shape, sds(M_WIDTH, BF16), qv_shape, gif_shape,
                   sds(G_KW, BF16), sds(G_KW, BF16), sds(G_KW, F32), sds(G_VW, BF16),
                   jax.ShapeDtypeStruct((b, SUBLANES, M_WIDTH), F32)],
        scratch_shapes=[pltpu.VMEM((bt, tb + SUBLANES, M_WIDTH), F32)],
        compiler_params=pltpu.CompilerParams(
            dimension_semantics=("parallel", "arbitrary"), vmem_limit_bytes=VMEM_LIMIT),
        name="mixer_pre",
    )(x, cst, g, w_pre, w_small, conv_w, conv_b, wq, wk, bif, wa2, ba)


def _scan_rows(x, op, fill):
    n = x.shape[0]
    row = lax.broadcasted_iota(jnp.int32, x.shape, 0)
    s = 1
    while s < n:
        x = op(x, jnp.where(row >= s, pltpu.roll(x, s, axis=0), fill))
        s *= 2
    return x


def _col_of_row(row):
    return jnp.transpose(jnp.broadcast_to(row, (SUBLANES, LANES)))[:, 0:1]


def _mlstm_chunk_tm(qt, k, vt, gift, c_ref, n_ref, m_ref, hm_store, *, lm):
    reps = lm // LANES
    g = gift[0:SUBLANES]
    gmax = pltpu.roll(g, M_HEADS, axis=0)
    b = gift[SUBLANES:2 * SUBLANES]
    m_prev = jnp.concatenate([m_ref[...]] * reps, axis=1)
    mx = jnp.maximum(m_prev, gmax)
    inter = jnp.exp(m_prev - mx)
    nfloor = jnp.exp(-(b + mx))
    mx_last = jnp.broadcast_to(mx[:, lm - 1:lm], (SUBLANES, lm))
    w_in = jnp.exp(g - mx_last)
    decay = jnp.exp(m_prev - mx_last)
    head_row = lax.broadcasted_iota(jnp.int32, (SUBLANES, LANES), 0) < M_HEADS
    m_new = jnp.broadcast_to(b[:, lm - 1:lm], (SUBLANES, lm)) + mx_last
    m_ref[...] = jnp.where(head_row, m_new[:, :LANES], 0.0)
    g_col = jnp.transpose(g)

    src = lax.broadcasted_iota(jnp.int32, (lm, lm), 0)
    dst = lax.broadcasted_iota(jnp.int32, (lm, lm), 1)
    causal = src <= dst
    heads = range(M_HEADS)
    sl = [slice(h * M_HEAD_DIM, (h + 1) * M_HEAD_DIM) for h in heads]
    row = lambda x, h: x[h:h + 1, :]
    c_old = [c_ref[h] for h in heads]
    n_old = [n_ref[h] for h in heads]
    cn = [jnp.concatenate([c_old[h].astype(BF16),
                           jnp.broadcast_to(n_old[h].astype(BF16), (BF16_ROWS, M_HEAD_DIM))], axis=0)
          for h in heads]
    cq = [_dot(cn[h], qt[sl[h], :]) for h in heads]
    dm = [jnp.exp(jnp.where(causal, g_col[:, h:h + 1] - row(mx, h), NEG_INF)) for h in heads]
    st = [_dot(k[:, sl[h]], qt[sl[h], :]) * dm[h] for h in heads]
    num = [row(inter, h) * cq[h][:M_HEAD_DIM] + _dot(vt[sl[h], :], st[h].astype(BF16))
           for h in heads]
    for h in heads:
        den = (row(inter, h) * cq[h][M_HEAD_DIM:M_HEAD_DIM + 1]
               + jnp.sum(st[h], axis=0, keepdims=True))
        rcp = 1.0 / jnp.maximum(jnp.abs(den), row(nfloor, h))
        ms = jnp.mean(num[h] * num[h], axis=0, keepdims=True)
        hm_store(h, num[h] * (rcp * lax.rsqrt(rcp * rcp * ms + EPS)))
    for h in heads:
        w_h = row(w_in, h)
        dec_h = jnp.concatenate([row(decay, h)[:, :LANES]] * (M_HEAD_DIM // LANES), axis=1)
        wv = jnp.concatenate([(vt[sl[h], :].astype(F32) * w_h).astype(BF16),
                              jnp.broadcast_to(w_h.astype(BF16), (BF16_ROWS, lm))], axis=0)
        upd = _dot(wv, k[:, sl[h]])
        c_ref[h] = dec_h * c_old[h] + upd[:M_HEAD_DIM]
        n_ref[h] = dec_h * n_old[h] + upd[M_HEAD_DIM:M_HEAD_DIM + 1]


def _mlstm_chunk(q, k, v, gif, c_ref, n_ref, m_ref, hm_store, *, lm):
    cs = _scan_rows(gif, jnp.add, 0.0)
    b = pltpu.roll(cs, LANES - M_HEADS, axis=1)
    g = gif - b
    m_prev = m_ref[...]
    mx = jnp.maximum(m_prev, _scan_rows(g, jnp.maximum, NEG_INF))
    inter = jnp.exp(m_prev - mx)
    nfloor = jnp.exp(-(b + mx))
    g_t = jnp.transpose(g)
    mx_last = mx[lm - 1:lm, :]
    w_in = jnp.exp(g - mx_last)
    decay = jnp.exp(m_prev - mx_last)
    m_ref[...] = b[lm - 1:lm, :] + mx_last

    row = lax.broadcasted_iota(jnp.int32, (lm, lm), 0)
    col = lax.broadcasted_iota(jnp.int32, (lm, lm), 1)
    causal = col <= row
    heads = range(M_HEADS)
    sl = [slice(h * M_HEAD_DIM, (h + 1) * M_HEAD_DIM) for h in heads]
    c_old = [c_ref[h] for h in heads]
    n_old = [n_ref[h] for h in heads]
    dm = [jnp.exp(jnp.where(causal, g_t[h:h + 1, :] - mx[:, h:h + 1], NEG_INF)) for h in heads]
    if lm % BF16_ROWS == 0:
        kn = [jnp.concatenate([k[:, sl[h]],
                               jnp.broadcast_to(n_old[h].astype(BF16), (BF16_ROWS, M_HEAD_DIM))], axis=0)
              for h in heads]
        qk = [_dot_nt(q[:, sl[h]], kn[h]) for h in heads]
        qn = [qk[h][:, lm:lm + 1] for h in heads]
        s = [qk[h][:, :lm] * dm[h] for h in heads]
    else:
        qn = [jnp.sum(q[:, sl[h]].astype(F32) * n_old[h], axis=-1, keepdims=True)
              for h in heads]
        s = [_dot_nt(q[:, sl[h]], k[:, sl[h]]) * dm[h] for h in heads]
    num = [inter[:, h:h + 1] * _dot_nt(q[:, sl[h]], c_old[h].astype(BF16))
           + _dot(s[h].astype(BF16), v[:, sl[h]]) for h in heads]
    for h in heads:
        den = inter[:, h:h + 1] * qn[h] + jnp.sum(s[h], axis=-1, keepdims=True)
        rcp = 1.0 / jnp.maximum(jnp.abs(den), nfloor[:, h:h + 1])
        ms = jnp.mean(num[h] * num[h], axis=-1, keepdims=True)
        scale = rcp * lax.rsqrt(rcp * rcp * ms + EPS)
        hm_store(h, num[h] * scale)
    for h in heads:
        w_h = w_in[:, h:h + 1]
        dec_h = decay[:, h:h + 1]
        wv = (w_h * v[:, sl[h]].astype(F32)).astype(BF16)
        c_ref[h] = dec_h * c_old[h] + _dot_tn(wv, k[:, sl[h]])
        n_ref[h] = dec_h * n_old[h] + jnp.sum(w_h * k[:, sl[h]].astype(F32), axis=0, keepdims=True)


GLA_SAFE_SPAN = 40.0


def _gla_scores_guarded(qh, kh, a, *, lg, sub):
    row = lax.broadcasted_iota(jnp.int32, (lg, lg), 0)
    col = lax.broadcasted_iota(jnp.int32, (lg, lg), 1)
    att = jnp.zeros((lg, lg), F32)
    c = sub
    while c < lg:
        nblk = lg // (2 * c)
        a_ref = jnp.concatenate(
            [jnp.broadcast_to(a[2 * c * i + c - 1:2 * c * i + c, :], (2 * c, G_DK))
             for i in range(nblk)], axis=0)
        qt = (qh * jnp.exp(jnp.minimum(a - a_ref, 0.0))).astype(BF16)
        kt = (kh * jnp.exp(jnp.minimum(a_ref - a, 0.0))).astype(BF16)
        valid = (row // (2 * c) == col // (2 * c)) & (row % (2 * c) >= c) & (col % (2 * c) < c)
        att = att + jnp.where(valid, _dot_nt(qt, kt), 0.0)
        c *= 2
    rows = lax.broadcasted_iota(jnp.int32, (sub, lg), 0)
    cols = lax.broadcasted_iota(jnp.int32, (sub, lg), 1)
    blocks = []
    for blk in range(lg // sub):
        r0 = blk * sub
        a_b = a[r0:r0 + sub, :]
        q_b = qh[r0:r0 + sub, :]
        acc = jnp.zeros((sub, lg), F32)
        for s in range(sub):
            e = jnp.exp(jnp.minimum(a_b - a[r0 + s:r0 + s + 1, :], 0.0))
            p = jnp.sum(q_b * e * kh[r0 + s:r0 + s + 1, :], axis=-1, keepdims=True)
            acc = jnp.where((cols == r0 + s) & (rows >= s), p, acc)
        blocks.append(acc)
    return att + jnp.concatenate(blocks, axis=0)


def _gla_chunk(q, k, v, la, s_ref, gh_ref, og_store, *, lg, sub, direct):
    heads = range(G_HEADS)
    ks = [slice(h * G_DK, (h + 1) * G_DK) for h in heads]
    vs = [slice(h * G_DV, (h + 1) * G_DV) for h in heads]
    a = _scan_rows(la, jnp.add, 0.0)
    a_last = a[lg - 1:lg, :]
    qf = q.astype(F32)
    kf = k.astype(F32)
    qs = (qf * jnp.exp(a)).astype(BF16)
    kl = (kf * jnp.exp(a_last - a)).astype(BF16)
    s_old = [s_ref[h] for h in heads]
    if direct:
        kt = (kf * jnp.exp(-a)).astype(BF16)
        row = lax.broadcasted_iota(jnp.int32, (lg, lg), 0)
        col = lax.broadcasted_iota(jnp.int32, (lg, lg), 1)
        att = [jnp.where(col <= row, _dot_nt(qs[:, ks[h]], kt[:, ks[h]]), 0.0) for h in heads]
    else:
        att = [_gla_scores_guarded(qf[:, ks[h]], kf[:, ks[h]], a[:, ks[h]], lg=lg, sub=sub)
               for h in heads]
    o = [_dot(qs[:, ks[h]], s_old[h].astype(BF16)) + _dot(att[h].astype(BF16), v[:, vs[h]])
         for h in heads]
    for h in heads:
        og_store(h, _rms(o[h], gh_ref[:, vs[h]]))
    for h in heads:
        s_ref[h] = (jnp.exp(_col_of_row(a_last[:, ks[h]])) * s_old[h]
                    + _dot_tn(kl[:, ks[h]], v[:, vs[h]]))


def _scan_kernel(*refs, bt, tb, lm, lg, sub, has_state, time_minor):
    n_in = 9 + (4 if has_state else 0)
    qm_ref, km_ref, vm_ref, gif_ref, qg_ref, kg_ref, la_ref, vg_ref, ggh_ref = refs[:9]
    state_in = refs[9:n_in]
    hm_ref, og_ref, c_out, n_out, m_out, s_out, c_ref, n_ref, m_ref, s_ref = refs[n_in:]
    t = pl.program_id(1)
    nt = pl.num_programs(1)

    for bi in range(bt):
        @pl.when(t == 0)
        def _():
            for i, ref in enumerate((c_ref, n_ref, m_ref, s_ref)):
                ref[bi] = state_in[i][bi] if has_state else jnp.zeros(ref.shape[1:], F32)

        def m_body(ci, carry):
            r0 = pl.multiple_of(ci * lm, lm)
            rows = pl.ds(r0, lm)
            state = (c_ref.at[bi], n_ref.at[bi], m_ref.at[bi])
            if time_minor:
                def store(h, val):
                    hm_ref[bi, h * M_HEAD_DIM:(h + 1) * M_HEAD_DIM, rows] = val

                _mlstm_chunk_tm(qm_ref[bi, :, rows], km_ref[bi, rows, :], vm_ref[bi, :, rows],
                                gif_ref[bi, :, rows], *state, store, lm=lm)
            else:
                def store(h, val):
                    hm_ref[bi, rows, h * M_HEAD_DIM:(h + 1) * M_HEAD_DIM] = val

                _mlstm_chunk(qm_ref[bi, rows, :], km_ref[bi, rows, :], vm_ref[bi, rows, :],
                             gif_ref[bi, rows, :], *state, store, lm=lm)
            return carry

        def g_body(ci, carry, *, direct):
            r0 = pl.multiple_of(ci * lg, lg)
            rows = pl.ds(r0, lg)

            def store(h, val):
                og_ref[bi, rows, h * G_DV:(h + 1) * G_DV] = val

            _gla_chunk(qg_ref[bi, rows, :], kg_ref[bi, rows, :], vg_ref[bi, rows, :],
                       la_ref[bi, rows, :], s_ref.at[bi], ggh_ref, store, lg=lg, sub=sub, direct=direct)
            return carry

        if tb == lm:
            m_body(0, 0)
        else:
            lax.fori_loop(0, tb // lm, m_body, 0)

        chunk_decay = jnp.sum(la_ref[bi].reshape(tb // lg, lg, G_KW), axis=1)
        span_ok = jnp.min(chunk_decay) >= -GLA_SAFE_SPAN
        for direct in (True, False):
            @pl.when(span_ok if direct else jnp.logical_not(span_ok))
            def _():
                body = functools.partial(g_body, direct=direct)
                if tb == lg:
                    body(0, 0)
                else:
                    lax.fori_loop(0, tb // lg, body, 0)

        @pl.when(t == nt - 1)
        def _():
            c_out[bi] = c_ref[bi]
            n_out[bi] = n_ref[bi]
            m_out[bi] = m_ref[bi]
            s_out[bi] = s_ref[bi]


def _scan(qm, km, vm, gif, qg, kg, la, vg, ggh, state, *, bt, tb, lm, lg, sub, time_minor):
    b, t, _ = km.shape
    assert b % bt == 0 and t % tb == 0 and tb % lm == 0 and tb % lg == 0 and lg % sub == 0
    has_state = state is not None
    assert not (time_minor and has_state) and (not time_minor or lm % LANES == 0)
    tok = lambda w: pl.BlockSpec((bt, tb, w), lambda i, j: (i, j, 0))
    tok_t = lambda w: pl.BlockSpec((bt, w, tb), lambda i, j: (i, 0, j))
    c_shape = (M_HEADS, M_HEAD_DIM, M_HEAD_DIM)
    n_shape = (M_HEADS, 1, M_HEAD_DIM)
    m_shape = (SUBLANES, LANES) if time_minor else (1, LANES)
    s_shape = (G_HEADS, G_DK, G_DV)
    st_spec = lambda shp: pl.BlockSpec((bt,) + shp, lambda i, j: (i,) + (0,) * len(shp))
    st_specs = [st_spec(c_shape), st_spec(n_shape), st_spec(m_shape), st_spec(s_shape)]
    if time_minor:
        qv_spec, gif_spec = tok_t(M_WIDTH), tok_t(2 * SUBLANES)
        hm_spec, hm_shape = tok_t(M_WIDTH), (b, M_WIDTH, t)
    else:
        qv_spec, gif_spec = tok(M_WIDTH), tok(LANES)
        hm_spec, hm_shape = tok(M_WIDTH), (b, t, M_WIDTH)
    in_specs = [qv_spec, tok(M_WIDTH), qv_spec, gif_spec, tok(G_KW), tok(G_KW),
                tok(G_KW), tok(G_VW), _const_spec((1, G_VW))]
    args = [qm, km, vm, gif, qg, kg, la, vg, ggh]
    if has_state:
        in_specs += st_specs
        args += list(state)
    return pl.pallas_call(
        functools.partial(_scan_kernel, bt=bt, tb=tb, lm=lm, lg=lg, sub=sub, has_state=has_state,
                          time_minor=time_minor),
        grid=(b // bt, t // tb),
        in_specs=in_specs,
        out_specs=[hm_spec, tok(G_VW)] + st_specs,
        out_shape=[jax.ShapeDtypeStruct(hm_shape, F32),
                   jax.ShapeDtypeStruct((b, t, G_VW), F32),
                   jax.ShapeDtypeStruct((b,) + c_shape, F32),
                   jax.ShapeDtypeStruct((b,) + n_shape, F32),
                   jax.ShapeDtypeStruct((b,) + m_shape, F32),
                   jax.ShapeDtypeStruct((b,) + s_shape, F32)],
        scratch_shapes=[pltpu.VMEM((bt,) + c_shape, F32), pltpu.VMEM((bt,) + n_shape, F32),
                        pltpu.VMEM((bt,) + m_shape, F32), pltpu.VMEM((bt,) + s_shape, F32)],
        compiler_params=pltpu.CompilerParams(
            dimension_semantics=("parallel", "arbitrary"), vmem_limit_bytes=VMEM_LIMIT),
        name="mixer_scan_state" if has_state else "mixer_scan",
    )(*args)


POST_O, POST_R, POST_A, POST_B = 0, 1024, 2048, 3072


def _post_kernel(x_ref, hm_ref, og_ref, g_ref, gmh_ref, w_ref, wpa_ref, wpb_ref, wo_ref, o_ref, *,
                 time_minor):
    x = x_ref[...]
    xn = _rms(x, g_ref[...]).astype(BF16)
    gate = lambda off: _dot(xn, w_ref[:, off:off + D_MODEL])
    if time_minor:
        hm = jnp.concatenate(
            [jnp.transpose(hm_ref[0, h * M_HEAD_DIM:(h + 1) * M_HEAD_DIM, :]) for h in range(M_HEADS)],
            axis=1)
    else:
        hm = hm_ref[...]
    hm = (_sigmoid(gate(POST_O)) * (hm * gmh_ref[...])).astype(BF16)
    og = (_silu(gate(POST_R)) * og_ref[...]).astype(BF16)
    y = (_sigmoid(gate(POST_A)) * _dot(hm, wpa_ref[...])
         + _sigmoid(gate(POST_B)) * _dot(og, wpb_ref[...]))
    o_ref[...] = x + _dot(y.astype(BF16), wo_ref[...])


def _post(x2d, hm, og2d, g, gmh, w_post, wpa, wpb, wo, *, tm, time_minor):
    n = x2d.shape[0]
    assert n % tm == 0
    tok = pl.BlockSpec((tm, D_MODEL), lambda i: (i, 0))
    if time_minor:
        per_seq = hm.shape[2] // tm
        assert hm.shape[2] % tm == 0 and tm % LANES == 0
        hm_spec = pl.BlockSpec((1, M_WIDTH, tm), lambda i: (i // per_seq, 0, i % per_seq))
    else:
        hm_spec = tok
    return pl.pallas_call(
        functools.partial(_post_kernel, time_minor=time_minor),
        grid=(n // tm,),
        in_specs=[tok, hm_spec, tok, _const_spec((1, D_MODEL)), _const_spec((1, M_WIDTH)),
                  _const_spec((D_MODEL, 4 * D_MODEL)),
                  _const_spec((M_WIDTH, D_MODEL)), _const_spec((G_VW, D_MODEL)),
                  _const_spec((D_MODEL, D_MODEL))],
        out_specs=tok,
        out_shape=jax.ShapeDtypeStruct((n, D_MODEL), F32),
        compiler_params=pltpu.CompilerParams(
            dimension_semantics=("parallel",), vmem_limit_bytes=VMEM_LIMIT),
        name="mixer_post",
    )(x2d, hm, og2d, g, gmh, w_post, wpa, wpb, wo)


def _pack_weights(g_ffn1, w_ffn1_up, w_ffn1_down, g_mix, w_in, conv_w, conv_b, w_mq, w_mk, b_if,
                  g_mhead, w_a2, b_a, g_ghead, w_pa, w_pb, w_o, g_ffn2, w_ffn2_up, w_ffn2_down,
                  g_final):
    l = 0
    row = lambda v: v.reshape(1, -1).astype(F32)
    edges = [0]
    for wd in (M_WIDTH, M_WIDTH, M_WIDTH, 2 * M_HEADS, G_KW, G_KW, G_VW, G_VW, G_RANK, D_MODEL, D_MODEL):
        edges.append(edges[-1] + wd)
    seg = lambda i: w_in[l][:, edges[i]:edges[i + 1]]
    u_m, v_m, o_m, if_m, q_g, k_g, v_g, r_g, a_g, g_a, g_b = (seg(i) for i in range(11))
    w_pre = jnp.concatenate([u_m, v_m, q_g, k_g, v_g], axis=1).astype(BF16)
    w_small = jnp.concatenate(
        [if_m, a_g, jnp.zeros((D_MODEL, LANES - 2 * M_HEADS - G_RANK), F32)], axis=1).astype(BF16)
    w_post = jnp.concatenate([o_m, r_g, g_a, g_b], axis=1).astype(BF16)
    bif = jnp.concatenate([b_if[l].reshape(-1), jnp.zeros((LANES - 2 * M_HEADS,), F32)]).reshape(1, LANES)
    wa2 = jnp.zeros((LANES, G_KW), F32).at[A_LANE0:A_LANE0 + G_RANK].set(w_a2[l]).astype(BF16)
    return dict(
        g_ffn1=row(g_ffn1[l]), w_ffn1_up=w_ffn1_up[l].astype(BF16), w_ffn1_down=w_ffn1_down[l].astype(BF16),
        g_mix=row(g_mix[l]), w_pre=w_pre, w_small=w_small, w_post=w_post,
        conv_w=conv_w[l].astype(F32), conv_b=row(conv_b[l]),
        w_mq=w_mq[l].astype(BF16), w_mk=w_mk[l].astype(BF16), bif=bif,
        g_mhead=row(g_mhead[l]), wa2=wa2, b_a=row(b_a[l]), g_ghead=row(g_ghead[l]),
        w_pa=w_pa[l].astype(BF16), w_pb=w_pb[l].astype(BF16), w_o=w_o[l].astype(BF16),
        g_ffn2=row(g_ffn2[l]), w_ffn2_up=w_ffn2_up[l].astype(BF16), w_ffn2_down=w_ffn2_down[l].astype(BF16),
        g_final=row(g_final))


def _trunk(x, state, w, *, tm, pre_tile, scan_tile, lm, lg, sub, time_minor):
    b, t, _ = x.shape
    n = b * t
    x2d = x.reshape(n, D_MODEL)
    x1 = _ffn(x2d, w["g_ffn1"], w["w_ffn1_up"], w["w_ffn1_down"], w["g_final"], final_norm=False, tm=tm)
    if state is None:
        cst = jnp.zeros((b, SUBLANES, M_WIDTH), F32)
        scan_state = None
    else:
        conv0, c0, n0, m0, s0 = state
        cst = jnp.concatenate(
            [jnp.zeros((b, SUBLANES - (CONV_W - 1), M_WIDTH), F32), conv0.astype(F32)], axis=1)
        m0p = jnp.concatenate([m0, jnp.zeros((b, LANES - M_HEADS), F32)], axis=1).reshape(b, 1, LANES)
        scan_state = (c0, n0.reshape(b, M_HEADS, 1, M_HEAD_DIM), m0p, s0)
    qm, km, vm, gif, qg, kg, la, vg, cout = _pre(
        x1.reshape(b, t, D_MODEL), cst, w["g_mix"], w["w_pre"], w["w_small"], w["conv_w"], w["conv_b"],
        w["w_mq"], w["w_mk"], w["bif"], w["wa2"], w["b_a"], bt=pre_tile[0], tb=pre_tile[1],
        time_minor=time_minor)
    hm, og, c_new, n_new, m_new, s_new = _scan(
        qm, km, vm, gif, qg, kg, la, vg, w["g_ghead"], scan_state,
        bt=scan_tile[0], tb=scan_tile[1], lm=lm, lg=lg, sub=sub, time_minor=time_minor)
    x2 = _post(x1, hm if time_minor else hm.reshape(n, M_WIDTH), og.reshape(n, G_VW), w["g_mix"],
               w["g_mhead"], w["w_post"], w["w_pa"], w["w_pb"], w["w_o"], tm=tm, time_minor=time_minor)
    y = _ffn(x2, w["g_ffn2"], w["w_ffn2_up"], w["w_ffn2_down"], w["g_final"], final_norm=True, tm=tm)
    m_heads = m_new[:, :M_HEADS, 0] if time_minor else m_new[:, 0, :M_HEADS]
    return (y.reshape(b, t, D_MODEL),
            cout[:, SUBLANES - (CONV_W - 1):, :][None],
            c_new[None],
            n_new.reshape(b, M_HEADS, M_HEAD_DIM)[None],
            m_heads[None],
            s_new[None])


def _tiles(b, t):
    n = b * t
    tm = math.gcd(n, 512)
    if t % 256 == 0:
        return dict(tm=math.gcd(t, 512), pre_tile=(1, 256), scan_tile=(1, 256), lm=256, lg=64, sub=16, time_minor=True)
    assert t <= 64 and t % SUBLANES == 0
    bt = max(1, min(b, 512 // t))
    while b % bt:
        bt -= 1
    return dict(tm=tm, pre_tile=(bt, t), scan_tile=(1, t), lm=t, lg=t, sub=min(16, t), time_minor=False)


def kernel(x_prompt, x_sample, state_conv, state_mlstm_C, state_mlstm_n, state_mlstm_m, state_gla_S,
           g_ffn1, w_ffn1_up, w_ffn1_down, g_mix, w_in, conv_w, conv_b, w_mq, w_mk, b_if, g_mhead,
           w_a2, b_a, g_ghead, w_pa, w_pb, w_o, g_ffn2, w_ffn2_up, w_ffn2_down, g_final):
    w = _pack_weights(g_ffn1, w_ffn1_up, w_ffn1_down, g_mix, w_in, conv_w, conv_b, w_mq, w_mk, b_if,
                      g_mhead, w_a2, b_a, g_ghead, w_pa, w_pb, w_o, g_ffn2, w_ffn2_up, w_ffn2_down,
                      g_final)
    bp, tp, _ = x_prompt.shape
    bs, ts, _ = x_sample.shape
    out_p = _trunk(x_prompt, None, w, **_tiles(bp, tp))
    state = (state_conv[0], state_mlstm_C[0], state_mlstm_n[0], state_mlstm_m[0], state_gla_S[0])
    out_s = _trunk(x_sample, state, w, **_tiles(bs, ts))
    return (out_p[0], out_s[0]) + out_p[1:] + out_s[1:]
```

```python
import functools
import math
import types

import jax
import jax.numpy as jnp
from jax import lax
from jax.experimental import pallas as pl
from jax.experimental.pallas import tpu as pltpu

D_MODEL = 1024
M_HEADS = 4
M_HEAD_DIM = 256
M_WIDTH = 1024
CONV_W = 4
G_HEADS = 4
G_DK = 128
G_DV = 256
G_KW = 512
G_VW = 1024
G_RANK = 16
G_TAU = 16.0
D_FF = 2816
EPS = 1e-6

LANES = 128
SUBLANES = 8
BF16_ROWS = 16
VMEM_LIMIT = 56 * 1024 * 1024

F32 = jnp.float32
BF16 = jnp.bfloat16
NEG_INF = float("-inf")


def _sigmoid(x):
    return 1.0 / (1.0 + jnp.exp(-x))


def _silu(x):
    return x * _sigmoid(x)


def _log_sigmoid(x):
    return jnp.minimum(x, 0.0) - jnp.log(1.0 + jnp.exp(-jnp.abs(x)))


def _rms(x, g):
    return x * lax.rsqrt(jnp.mean(x * x, axis=-1, keepdims=True) + EPS) * g


def _dot(a, b):
    return jnp.dot(a, b, preferred_element_type=F32)


def _dot_nt(a, b):
    return lax.dot_general(a, b, (((1,), (1,)), ((), ())), preferred_element_type=F32)


def _dot_tn(a, b):
    return lax.dot_general(a, b, (((0,), (0,)), ((), ())), preferred_element_type=F32)


def _const_spec(shape):
    nd = len(shape)
    return pl.BlockSpec(shape, lambda *_: (0,) * nd, pipeline_mode=pl.Buffered(1))


FF_CHUNK = 256


def _ffn_kernel(x_ref, g_ref, wup_ref, wdown_ref, gfin_ref, o_ref, h_ref, *, final_norm):
    x = x_ref[...]
    xn = _rms(x, g_ref[...]).astype(BF16)
    for c in range(D_FF // FF_CHUNK):
        a = _dot(xn, wup_ref[:, c * FF_CHUNK:(c + 1) * FF_CHUNK])
        g = _dot(xn, wup_ref[:, D_FF + c * FF_CHUNK:D_FF + (c + 1) * FF_CHUNK])
        h_ref[:, c * FF_CHUNK:(c + 1) * FF_CHUNK] = (_silu(g) * a).astype(BF16)
    y = x + 0.5 * _dot(h_ref[...], wdown_ref[...])
    if final_norm:
        y = _rms(y, gfin_ref[...])
    o_ref[...] = y


def _ffn(x2d, g, wup, wdown, gfin, *, final_norm, tm):
    n = x2d.shape[0]
    assert n % tm == 0
    return pl.pallas_call(
        functools.partial(_ffn_kernel, final_norm=final_norm),
        grid=(n // tm,),
        in_specs=[
            pl.BlockSpec((tm, D_MODEL), lambda i: (i, 0)),
            _const_spec((1, D_MODEL)),
            _const_spec((D_MODEL, 2 * D_FF)),
            _const_spec((D_FF, D_MODEL)),
            _const_spec((1, D_MODEL)),
        ],
        out_specs=pl.BlockSpec((tm, D_MODEL), lambda i: (i, 0)),
        out_shape=jax.ShapeDtypeStruct((n, D_MODEL), F32),
        scratch_shapes=[pltpu.VMEM((tm, D_FF), BF16)],
        compiler_params=pltpu.CompilerParams(
            dimension_semantics=("parallel",), vmem_limit_bytes=VMEM_LIMIT),
        name="ffn_final" if final_norm else "ffn",
    )(x2d, g, wup, wdown, gfin)


PRE_U, PRE_V, PRE_QG, PRE_KG, PRE_VG = 0, 1024, 2048, 2560, 3072
PRE_WIDTH = 4096
A_LANE0 = 8


def _pre_kernel(x_ref, cst_ref, g_ref, w_ref, ws_ref, cw_ref, cb_ref, wq_ref, wk_ref, bif_ref,
                wa2_ref, ba_ref,
                qm_ref, km_ref, vm_ref, gif_ref, qg_ref, kg_ref, la_ref, vg_ref, cout_ref,
                cbuf_ref, *, bt, tb, time_minor):
    t = pl.program_id(1)
    m = bt * tb
    xn = _rms(x_ref[...].reshape(m, D_MODEL), g_ref[...]).astype(BF16)

    @pl.when(t == 0)
    def _():
        cbuf_ref[:, 0:SUBLANES, :] = cst_ref[...]

    @pl.when(t > 0)
    def _():
        cbuf_ref[:, 0:SUBLANES, :] = cbuf_ref[:, tb:tb + SUBLANES, :]

    u = _dot(xn, w_ref[:, PRE_U:PRE_U + M_WIDTH])
    cbuf_ref[:, SUBLANES:SUBLANES + tb, :] = u.reshape(bt, tb, M_WIDTH)
    c = cb_ref[...].reshape(1, 1, M_WIDTH)
    for j in range(CONV_W):
        off = SUBLANES - (CONV_W - 1) + j
        c = c + cbuf_ref[:, off:off + tb, :] * cw_ref[j:j + 1, :].reshape(1, 1, M_WIDTH)
    cout_ref[...] = cbuf_ref[:, tb:tb + SUBLANES, :]
    ch = _silu(c).reshape(m, M_WIDTH).astype(BF16)
    vm = _dot(xn, w_ref[:, PRE_V:PRE_V + M_WIDTH])
    for h in range(M_HEADS):
        sl = slice(h * M_HEAD_DIM, (h + 1) * M_HEAD_DIM)
        qh = _dot(ch[:, sl], wq_ref[h])
        if time_minor:
            qm_ref[0, sl, :] = jnp.transpose(qh).astype(BF16)
            vm_ref[0, sl, :] = jnp.transpose(vm[:, sl]).astype(BF16)
        else:
            qm_ref[:, :, sl] = qh.astype(BF16).reshape(bt, tb, M_HEAD_DIM)
        km_ref[:, :, sl] = (_dot(ch[:, sl], wk_ref[h]) * (M_HEAD_DIM ** -0.5)).astype(BF16).reshape(
            bt, tb, M_HEAD_DIM)
    if not time_minor:
        vm_ref[...] = vm.astype(BF16).reshape(bt, tb, M_WIDTH)

    qg_ref[...] = (_dot(xn, w_ref[:, PRE_QG:PRE_QG + G_KW]) * (G_DK ** -0.5)).astype(BF16).reshape(
        bt, tb, G_KW)
    kg_ref[...] = _dot(xn, w_ref[:, PRE_KG:PRE_KG + G_KW]).astype(BF16).reshape(bt, tb, G_KW)
    vg_ref[...] = _dot(xn, w_ref[:, PRE_VG:PRE_VG + G_VW]).astype(BF16).reshape(bt, tb, G_VW)

    small = _dot(xn, ws_ref[...])
    gates = small + bif_ref[...]
    lane = lax.broadcasted_iota(jnp.int32, gates.shape, 1)
    gif = jnp.where(lane < M_HEADS, gates, jnp.where(lane < 2 * M_HEADS, _log_sigmoid(gates), 0.0))
    if time_minor:
        b = pltpu.roll(_scan_rows(gif, jnp.add, 0.0), LANES - M_HEADS, axis=1)
        g = gif - b
        gmax = _scan_rows(g, jnp.maximum, NEG_INF)
        packed = jnp.where(lane < M_HEADS, g,
                           jnp.where(lane < 2 * M_HEADS, pltpu.roll(gmax, M_HEADS, axis=1),
                                     jnp.where(lane < 3 * M_HEADS, pltpu.roll(b, 2 * M_HEADS, axis=1), 0.0)))
        gif_ref[0] = jnp.transpose(packed)[0:2 * SUBLANES, :]
    else:
        gif_ref[...] = gif.reshape(bt, tb, LANES)
    a2 = _dot(small.astype(BF16), wa2_ref[...]) + ba_ref[...]
    la_ref[...] = (_log_sigmoid(a2) * (1.0 / G_TAU)).reshape(bt, tb, G_KW)


def _pre(x, cst, g, w_pre, w_small, conv_w, conv_b, wq, wk, bif, wa2, ba, *, bt, tb, time_minor):
    b, t, _ = x.shape
    assert b % bt == 0 and t % tb == 0 and tb % SUBLANES == 0
    assert not time_minor or (bt == 1 and tb % LANES == 0)
    tok = lambda w: pl.BlockSpec((bt, tb, w), lambda i, j: (i, j, 0))
    per_seq = pl.BlockSpec((bt, SUBLANES, M_WIDTH), lambda i, j: (i, 0, 0))
    sds = lambda w, dt: jax.ShapeDtypeStruct((b, t, w), dt)
    if time_minor:
        tok_t = lambda w: pl.BlockSpec((1, w, tb), lambda i, j: (i, 0, j))
        sds_t = lambda w, dt: jax.ShapeDtypeStruct((b, w, t), dt)
        qv_spec, qv_shape = tok_t(M_WIDTH), sds_t(M_WIDTH, BF16)
        gif_spec, gif_shape = tok_t(2 * SUBLANES), sds_t(2 * SUBLANES, F32)
    else:
        qv_spec, qv_shape = tok(M_WIDTH), sds(M_WIDTH, BF16)
        gif_spec, gif_shape = tok(LANES), sds(LANES, F32)
    return pl.pallas_call(
        functools.partial(_pre_kernel, bt=bt, tb=tb, time_minor=time_minor),
        grid=(b // bt, t // tb),
        in_specs=[
            tok(D_MODEL), per_seq,
            _const_spec((1, D_MODEL)),
            _const_spec((D_MODEL, PRE_WIDTH)),
            _const_spec((D_MODEL, LANES)),
            _const_spec((CONV_W, M_WIDTH)),
            _const_spec((1, M_WIDTH)),
            _const_spec((M_HEADS, M_HEAD_DIM, M_HEAD_DIM)),
            _const_spec((M_HEADS, M_HEAD_DIM, M_HEAD_DIM)),
            _const_spec((1, LANES)),
            _const_spec((LANES, G_KW)),
            _const_spec((1, G_KW)),
        ],
        out_specs=[qv_spec, tok(M_WIDTH), qv_spec, gif_spec, tok(G_KW), tok(G_KW),
                   tok(G_KW), tok(G_VW), per_seq],
        out_shape=[qv_shape, sds(M_WIDTH, BF16), qv_shape, gif_shape,
                   sds(G_KW, BF16), sds(G_KW, BF16), sds(G_KW, F32), sds(G_VW, BF16),
                   jax.ShapeDtypeStruct((b, SUBLANES, M_WIDTH), F32)],
        scratch_shapes=[pltpu.VMEM((bt, tb + SUBLANES, M_WIDTH), F32)],
        compiler_params=pltpu.CompilerParams(
            dimension_semantics=("parallel", "arbitrary"), vmem_limit_bytes=VMEM_LIMIT),
        name="mixer_pre",
    )(x, cst, g, w_pre, w_small, conv_w, conv_b, wq, wk, bif, wa2, ba)


def _scan_rows(x, op, fill):
    n = x.shape[0]
    row = lax.broadcasted_iota(jnp.int32, x.shape, 0)
    s = 1
    while s < n:
        x = op(x, jnp.where(row >= s, pltpu.roll(x, s, axis=0), fill))
        s *= 2
    return x


def _col_of_row(row):
    return jnp.transpose(jnp.broadcast_to(row, (SUBLANES, LANES)))[:, 0:1]


def _mlstm_chunk_tm(qt, k, vt, gift, c_ref, n_ref, m_ref, hm_store, *, lm):
    reps = lm // LANES
    g = gift[0:SUBLANES]
    gmax = pltpu.roll(g, M_HEADS, axis=0)
    b = gift[SUBLANES:2 * SUBLANES]
    m_prev = jnp.concatenate([m_ref[...]] * reps, axis=1)
    mx = jnp.maximum(m_prev, gmax)
    inter = jnp.exp(m_prev - mx)
    nfloor = jnp.exp(-(b + mx))
    mx_last = jnp.broadcast_to(mx[:, lm - 1:lm], (SUBLANES, lm))
    w_in = jnp.exp(g - mx_last)
    decay = jnp.exp(m_prev - mx_last)
    head_row = lax.broadcasted_iota(jnp.int32, (SUBLANES, LANES), 0) < M_HEADS
    m_new = jnp.broadcast_to(b[:, lm - 1:lm], (SUBLANES, lm)) + mx_last
    m_ref[...] = jnp.where(head_row, m_new[:, :LANES], 0.0)
    g_col = jnp.transpose(g)

    src = lax.broadcasted_iota(jnp.int32, (lm, lm), 0)
    dst = lax.broadcasted_iota(jnp.int32, (lm, lm), 1)
    causal = src <= dst
    heads = range(M_HEADS)
    sl = [slice(h * M_HEAD_DIM, (h + 1) * M_HEAD_DIM) for h in heads]
    row = lambda x, h: x[h:h + 1, :]
    c_old = [c_ref[h] for h in heads]
    n_old = [n_ref[h] for h in heads]
    cn = [jnp.concatenate([c_old[h].astype(BF16),
                           jnp.broadcast_to(n_old[h].astype(BF16), (BF16_ROWS, M_HEAD_DIM))], axis=0)
          for h in heads]
    cq = [_dot(cn[h], qt[sl[h], :]) for h in heads]
    dm = [jnp.exp(jnp.where(causal, g_col[:, h:h + 1] - row(mx, h), NEG_INF)) for h in heads]
    st = [_dot(k[:, sl[h]], qt[sl[h], :]) * dm[h] for h in heads]
    num = [row(inter, h) * cq[h][:M_HEAD_DIM] + _dot(vt[sl[h], :], st[h].astype(BF16))
           for h in heads]
    for h in heads:
        den = (row(inter, h) * cq[h][M_HEAD_DIM:M_HEAD_DIM + 1]
               + jnp.sum(st[h], axis=0, keepdims=True))
        rcp = 1.0 / jnp.maximum(jnp.abs(den), row(nfloor, h))
        ms = jnp.mean(num[h] * num[h], axis=0, keepdims=True)
        hm_store(h, num[h] * (rcp * lax.rsqrt(rcp * rcp * ms + EPS)))
    for h in heads:
        w_h = row(w_in, h)
        dec_h = jnp.concatenate([row(decay, h)[:, :LANES]] * (M_HEAD_DIM // LANES), axis=1)
        wv = jnp.concatenate([(vt[sl[h], :].astype(F32) * w_h).astype(BF16),
                              jnp.broadcast_to(w_h.astype(BF16), (BF16_ROWS, lm))], axis=0)
        upd = _dot(wv, k[:, sl[h]])
        c_ref[h] = dec_h * c_old[h] + upd[:M_HEAD_DIM]
        n_ref[h] = dec_h * n_old[h] + upd[M_HEAD_DIM:M_HEAD_DIM + 1]


def _mlstm_chunk(seqs, *, lm):
    row = lax.broadcasted_iota(jnp.int32, (lm, lm), 0)
    col = lax.broadcasted_iota(jnp.int32, (lm, lm), 1)
    causal = col <= row
    sl = [slice(h * M_HEAD_DIM, (h + 1) * M_HEAD_DIM) for h in range(M_HEADS)]
    gates = []
    for sq in seqs:
        cs = _scan_rows(sq.gif, jnp.add, 0.0)
        b = pltpu.roll(cs, LANES - M_HEADS, axis=1)
        g = sq.gif - b
        m_prev = sq.m_src[...]
        mx = jnp.maximum(m_prev, _scan_rows(g, jnp.maximum, NEG_INF))
        mx_last = mx[lm - 1:lm, :]
        gates.append(dict(
            mx=mx,
            inter=jnp.exp(m_prev - mx),
            nfloor=jnp.exp(-(b + mx)),
            g_t=jnp.transpose(g),
            w_in=jnp.exp(g - mx_last),
            decay=jnp.exp(m_prev - mx_last)))
        sq.m_dst[...] = b[lm - 1:lm, :] + mx_last
    units = [(sq, gt, h) for sq, gt in zip(seqs, gates) for h in range(M_HEADS)]
    col_of = lambda x, h: x[:, h:h + 1]
    c_old = [sq.c_src[h] for sq, _, h in units]
    n_old = [sq.n_src[h] for sq, _, h in units]
    dm = [jnp.exp(jnp.where(causal, gt["g_t"][h:h + 1, :] - col_of(gt["mx"], h), NEG_INF))
          for _, gt, h in units]
    qn = [jnp.sum(sq.q[:, sl[h]].astype(F32) * n_old[u], axis=-1, keepdims=True)
          for u, (sq, _, h) in enumerate(units)]
    s = [_dot_nt(sq.q[:, sl[h]], sq.k[:, sl[h]]) * dm[u] for u, (sq, _, h) in enumerate(units)]
    num = [col_of(gt["inter"], h) * _dot_nt(sq.q[:, sl[h]], c_old[u].astype(BF16))
           + _dot(s[u].astype(BF16), sq.v[:, sl[h]]) for u, (sq, gt, h) in enumerate(units)]
    for u, (sq, gt, h) in enumerate(units):
        den = col_of(gt["inter"], h) * qn[u] + jnp.sum(s[u], axis=-1, keepdims=True)
        rcp = 1.0 / jnp.maximum(jnp.abs(den), col_of(gt["nfloor"], h))
        ms = jnp.mean(num[u] * num[u], axis=-1, keepdims=True)
        sq.hm_store(h, num[u] * (rcp * lax.rsqrt(rcp * rcp * ms + EPS)))
    for u, (sq, gt, h) in enumerate(units):
        w_h = col_of(gt["w_in"], h)
        dec_h = col_of(gt["decay"], h)
        wv = (w_h * sq.v[:, sl[h]].astype(F32)).astype(BF16)
        sq.c_dst[h] = dec_h * c_old[u] + _dot_tn(wv, sq.k[:, sl[h]])
        sq.n_dst[h] = dec_h * n_old[u] + jnp.sum(w_h * sq.k[:, sl[h]].astype(F32), axis=0, keepdims=True)


GLA_SAFE_SPAN = 40.0


def _gla_scores_guarded(qh, kh, a, *, lg, sub):
    row = lax.broadcasted_iota(jnp.int32, (lg, lg), 0)
    col = lax.broadcasted_iota(jnp.int32, (lg, lg), 1)
    att = jnp.zeros((lg, lg), F32)
    c = sub
    while c < lg:
        nblk = lg // (2 * c)
        a_ref = jnp.concatenate(
            [jnp.broadcast_to(a[2 * c * i + c - 1:2 * c * i + c, :], (2 * c, G_DK))
             for i in range(nblk)], axis=0)
        qt = (qh * jnp.exp(jnp.minimum(a - a_ref, 0.0))).astype(BF16)
        kt = (kh * jnp.exp(jnp.minimum(a_ref - a, 0.0))).astype(BF16)
        valid = (row // (2 * c) == col // (2 * c)) & (row % (2 * c) >= c) & (col % (2 * c) < c)
        att = att + jnp.where(valid, _dot_nt(qt, kt), 0.0)
        c *= 2
    rows = lax.broadcasted_iota(jnp.int32, (sub, lg), 0)
    cols = lax.broadcasted_iota(jnp.int32, (sub, lg), 1)
    blocks = []
    for blk in range(lg // sub):
        r0 = blk * sub
        a_b = a[r0:r0 + sub, :]
        q_b = qh[r0:r0 + sub, :]
        acc = jnp.zeros((sub, lg), F32)
        for s in range(sub):
            e = jnp.exp(jnp.minimum(a_b - a[r0 + s:r0 + s + 1, :], 0.0))
            p = jnp.sum(q_b * e * kh[r0 + s:r0 + s + 1, :], axis=-1, keepdims=True)
            acc = jnp.where((cols == r0 + s) & (rows >= s), p, acc)
        blocks.append(acc)
    return att + jnp.concatenate(blocks, axis=0)


def _gla_chunk(seqs, gh_ref, *, lg, sub, direct):
    ks = [slice(h * G_DK, (h + 1) * G_DK) for h in range(G_HEADS)]
    vs = [slice(h * G_DV, (h + 1) * G_DV) for h in range(G_HEADS)]
    pre = []
    for sq in seqs:
        a = _scan_rows(sq.la, jnp.add, 0.0)
        a_last = a[lg - 1:lg, :]
        qf = sq.qg.astype(F32)
        kf = sq.kg.astype(F32)
        pre.append(dict(a=a, a_last=a_last, qf=qf, kf=kf,
                        qs=(qf * jnp.exp(a)).astype(BF16),
                        kl=(kf * jnp.exp(a_last - a)).astype(BF16),
                        kt=(kf * jnp.exp(-a)).astype(BF16) if direct else None))
    units = [(sq, p, h) for sq, p in zip(seqs, pre) for h in range(G_HEADS)]
    s_old = [sq.s_src[h] for sq, _, h in units]
    if direct:
        row = lax.broadcasted_iota(jnp.int32, (lg, lg), 0)
        col = lax.broadcasted_iota(jnp.int32, (lg, lg), 1)
        att = [jnp.where(col <= row, _dot_nt(p["qs"][:, ks[h]], p["kt"][:, ks[h]]), 0.0)
               for _, p, h in units]
    else:
        att = [_gla_scores_guarded(p["qf"][:, ks[h]], p["kf"][:, ks[h]], p["a"][:, ks[h]], lg=lg, sub=sub)
               for _, p, h in units]
    o = [_dot(p["qs"][:, ks[h]], s_old[u].astype(BF16)) + _dot(att[u].astype(BF16), sq.vg[:, vs[h]])
         for u, (sq, p, h) in enumerate(units)]
    for u, (sq, _, h) in enumerate(units):
        sq.og_store(h, _rms(o[u], gh_ref[:, vs[h]]))
    for u, (sq, p, h) in enumerate(units):
        sq.s_dst[h] = (jnp.exp(_col_of_row(p["a_last"][:, ks[h]])) * s_old[u]
                       + _dot_tn(p["kl"][:, ks[h]], sq.vg[:, vs[h]]))


def _scan_kernel(*refs, bt, tb, lm, lg, sub, has_state, time_minor, single_step):
    n_in = 9 + (4 if has_state else 0)
    qm_ref, km_ref, vm_ref, gif_ref, qg_ref, kg_ref, la_ref, vg_ref, ggh_ref = refs[:9]
    state_in = refs[9:n_in]
    hm_ref, og_ref = refs[n_in:n_in + 2]
    state_out = refs[n_in + 2:n_in + 6]
    scratch = refs[n_in + 6:]
    t = pl.program_id(1)
    nt = pl.num_programs(1)

    streamed = has_state and single_step
    src = state_in if streamed else scratch
    dst = state_out if streamed else scratch
    if not streamed:
        @pl.when(t == 0)
        def _():
            for i, ref in enumerate(scratch):
                ref[...] = state_in[i][...] if has_state else jnp.zeros(ref.shape, F32)

    def seq_records(bi, m_rows, g_rows):
        def hm_store(h, val):
            if time_minor:
                hm_ref[bi, h * M_HEAD_DIM:(h + 1) * M_HEAD_DIM, m_rows] = val
            else:
                hm_ref[bi, m_rows, h * M_HEAD_DIM:(h + 1) * M_HEAD_DIM] = val

        def og_store(h, val):
            og_ref[bi, g_rows, h * G_DV:(h + 1) * G_DV] = val

        return types.SimpleNamespace(
            bi=bi, hm_store=hm_store, og_store=og_store,
            c_src=src[0].at[bi], n_src=src[1].at[bi], m_src=src[2].at[bi], s_src=src[3].at[bi],
            c_dst=dst[0].at[bi], n_dst=dst[1].at[bi], m_dst=dst[2].at[bi], s_dst=dst[3].at[bi])

    def m_body(ci, carry):
        rows = pl.ds(pl.multiple_of(ci * lm, lm), lm)
        seqs = [seq_records(bi, rows, None) for bi in range(bt)]
        if time_minor:
            for sq in seqs:
                _mlstm_chunk_tm(qm_ref[sq.bi, :, rows], km_ref[sq.bi, rows, :], vm_ref[sq.bi, :, rows],
                                gif_ref[sq.bi, :, rows], sq.c_dst, sq.n_dst, sq.m_dst, sq.hm_store, lm=lm)
        else:
            for sq in seqs:
                sq.q, sq.k, sq.v = qm_ref[sq.bi, rows, :], km_ref[sq.bi, rows, :], vm_ref[sq.bi, rows, :]
                sq.gif = gif_ref[sq.bi, rows, :]
            _mlstm_chunk(seqs, lm=lm)
        return carry

    def g_body(ci, carry, *, direct):
        rows = pl.ds(pl.multiple_of(ci * lg, lg), lg)
        seqs = [seq_records(bi, None, rows) for bi in range(bt)]
        for sq in seqs:
            sq.qg, sq.kg, sq.vg = qg_ref[sq.bi, rows, :], kg_ref[sq.bi, rows, :], vg_ref[sq.bi, rows, :]
            sq.la = la_ref[sq.bi, rows, :]
        _gla_chunk(seqs, ggh_ref, lg=lg, sub=sub, direct=direct)
        return carry

    def loop(body, n):
        if n == 1:
            body(0, 0)
        else:
            lax.fori_loop(0, n, body, 0)

    loop(m_body, tb // lm)
    if lg <= sub:
        loop(functools.partial(g_body, direct=False), tb // lg)
    else:
        chunk_decay = jnp.sum(la_ref[...].reshape(bt * tb // lg, lg, G_KW), axis=1)
        span_ok = jnp.min(chunk_decay) >= -GLA_SAFE_SPAN
        for direct in (True, False):
            @pl.when(span_ok if direct else jnp.logical_not(span_ok))
            def _():
                loop(functools.partial(g_body, direct=direct), tb // lg)

    if not streamed:
        @pl.when(t == nt - 1)
        def _():
            for out, ref in zip(state_out, scratch):
                out[...] = ref[...]


def _scan(qm, km, vm, gif, qg, kg, la, vg, ggh, state, *, bt, tb, lm, lg, sub, time_minor):
    b, t, _ = km.shape
    assert b % bt == 0 and t % tb == 0 and tb % lm == 0 and tb % lg == 0 and lg % sub == 0
    has_state = state is not None
    assert not (time_minor and has_state) and (not time_minor or lm % LANES == 0)
    tok = lambda w: pl.BlockSpec((bt, tb, w), lambda i, j: (i, j, 0))
    tok_t = lambda w: pl.BlockSpec((bt, w, tb), lambda i, j: (i, 0, j))
    c_shape = (M_HEADS, M_HEAD_DIM, M_HEAD_DIM)
    n_shape = (M_HEADS, 1, M_HEAD_DIM)
    m_shape = (SUBLANES, LANES) if time_minor else (1, LANES)
    s_shape = (G_HEADS, G_DK, G_DV)
    st_shapes = [(bt,) + shp for shp in (c_shape, n_shape, m_shape, s_shape)]
    st_specs = [pl.BlockSpec(shp, lambda i, j, nd=len(shp): (i,) + (0,) * (nd - 1)) for shp in st_shapes]
    single_step = t == tb
    carried = not (has_state and single_step)
    if time_minor:
        qv_spec, gif_spec = tok_t(M_WIDTH), tok_t(2 * SUBLANES)
        hm_spec, hm_shape = tok_t(M_WIDTH), (b, M_WIDTH, t)
    else:
        qv_spec, gif_spec = tok(M_WIDTH), tok(LANES)
        hm_spec, hm_shape = tok(M_WIDTH), (b, t, M_WIDTH)
    in_specs = [qv_spec, tok(M_WIDTH), qv_spec, gif_spec, tok(G_KW), tok(G_KW),
                tok(G_KW), tok(G_VW), _const_spec((1, G_VW))]
    args = [qm, km, vm, gif, qg, kg, la, vg, ggh]
    if has_state:
        in_specs += st_specs
        args += list(state)
    return pl.pallas_call(
        functools.partial(_scan_kernel, bt=bt, tb=tb, lm=lm, lg=lg, sub=sub, has_state=has_state,
                          time_minor=time_minor, single_step=single_step),
        grid=(b // bt, t // tb),
        in_specs=in_specs,
        out_specs=[hm_spec, tok(G_VW)] + st_specs,
        out_shape=[jax.ShapeDtypeStruct(hm_shape, F32),
                   jax.ShapeDtypeStruct((b, t, G_VW), F32)]
                  + [jax.ShapeDtypeStruct((b,) + shp[1:], F32) for shp in st_shapes],
        scratch_shapes=[pltpu.VMEM(shp, F32) for shp in st_shapes] if carried else [],
        compiler_params=pltpu.CompilerParams(
            dimension_semantics=("parallel", "arbitrary"), vmem_limit_bytes=VMEM_LIMIT),
        name="mixer_scan_state" if has_state else "mixer_scan",
    )(*args)


POST_O, POST_R, POST_A, POST_B = 0, 1024, 2048, 3072


def _post_kernel(x_ref, hm_ref, og_ref, g_ref, gmh_ref, w_ref, wpa_ref, wpb_ref, wo_ref, o_ref, *,
                 time_minor):
    x = x_ref[...]
    xn = _rms(x, g_ref[...]).astype(BF16)
    gate = lambda off: _dot(xn, w_ref[:, off:off + D_MODEL])
    if time_minor:
        hm = jnp.concatenate(
            [jnp.transpose(hm_ref[0, h * M_HEAD_DIM:(h + 1) * M_HEAD_DIM, :]) for h in range(M_HEADS)],
            axis=1)
    else:
        hm = hm_ref[...]
    hm = (_sigmoid(gate(POST_O)) * (hm * gmh_ref[...])).astype(BF16)
    og = (_silu(gate(POST_R)) * og_ref[...]).astype(BF16)
    y = (_sigmoid(gate(POST_A)) * _dot(hm, wpa_ref[...])
         + _sigmoid(gate(POST_B)) * _dot(og, wpb_ref[...]))
    o_ref[...] = x + _dot(y.astype(BF16), wo_ref[...])


def _post(x2d, hm, og2d, g, gmh, w_post, wpa, wpb, wo, *, tm, time_minor):
    n = x2d.shape[0]
    assert n % tm == 0
    tok = pl.BlockSpec((tm, D_MODEL), lambda i: (i, 0))
    if time_minor:
        per_seq = hm.shape[2] // tm
        assert hm.shape[2] % tm == 0 and tm % LANES == 0
        hm_spec = pl.BlockSpec((1, M_WIDTH, tm), lambda i: (i // per_seq, 0, i % per_seq))
    else:
        hm_spec = tok
    return pl.pallas_call(
        functools.partial(_post_kernel, time_minor=time_minor),
        grid=(n // tm,),
        in_specs=[tok, hm_spec, tok, _const_spec((1, D_MODEL)), _const_spec((1, M_WIDTH)),
                  _const_spec((D_MODEL, 4 * D_MODEL)),
                  _const_spec((M_WIDTH, D_MODEL)), _const_spec((G_VW, D_MODEL)),
                  _const_spec((D_MODEL, D_MODEL))],
        out_specs=tok,
        out_shape=jax.ShapeDtypeStruct((n, D_MODEL), F32),
        compiler_params=pltpu.CompilerParams(
            dimension_semantics=("parallel",), vmem_limit_bytes=VMEM_LIMIT),
        name="mixer_post",
    )(x2d, hm, og2d, g, gmh, w_post, wpa, wpb, wo)


def _pack_weights(g_ffn1, w_ffn1_up, w_ffn1_down, g_mix, w_in, conv_w, conv_b, w_mq, w_mk, b_if,
                  g_mhead, w_a2, b_a, g_ghead, w_pa, w_pb, w_o, g_ffn2, w_ffn2_up, w_ffn2_down,
                  g_final):
    l = 0
    row = lambda v: v.reshape(1, -1).astype(F32)
    edges = [0]
    for wd in (M_WIDTH, M_WIDTH, M_WIDTH, 2 * M_HEADS, G_KW, G_KW, G_VW, G_VW, G_RANK, D_MODEL, D_MODEL):
        edges.append(edges[-1] + wd)
    seg = lambda i: w_in[l][:, edges[i]:edges[i + 1]]
    u_m, v_m, o_m, if_m, q_g, k_g, v_g, r_g, a_g, g_a, g_b = (seg(i) for i in range(11))
    w_pre = jnp.concatenate([u_m, v_m, q_g, k_g, v_g], axis=1).astype(BF16)
    w_small = jnp.concatenate(
        [if_m, a_g, jnp.zeros((D_MODEL, LANES - 2 * M_HEADS - G_RANK), F32)], axis=1).astype(BF16)
    w_post = jnp.concatenate([o_m, r_g, g_a, g_b], axis=1).astype(BF16)
    bif = jnp.concatenate([b_if[l].reshape(-1), jnp.zeros((LANES - 2 * M_HEADS,), F32)]).reshape(1, LANES)
    wa2 = jnp.zeros((LANES, G_KW), F32).at[A_LANE0:A_LANE0 + G_RANK].set(w_a2[l]).astype(BF16)
    return dict(
        g_ffn1=row(g_ffn1[l]), w_ffn1_up=w_ffn1_up[l].astype(BF16), w_ffn1_down=w_ffn1_down[l].astype(BF16),
        g_mix=row(g_mix[l]), w_pre=w_pre, w_small=w_small, w_post=w_post,
        conv_w=conv_w[l].astype(F32), conv_b=row(conv_b[l]),
        w_mq=w_mq[l].astype(BF16), w_mk=w_mk[l].astype(BF16), bif=bif,
        g_mhead=row(g_mhead[l]), wa2=wa2, b_a=row(b_a[l]), g_ghead=row(g_ghead[l]),
        w_pa=w_pa[l].astype(BF16), w_pb=w_pb[l].astype(BF16), w_o=w_o[l].astype(BF16),
        g_ffn2=row(g_ffn2[l]), w_ffn2_up=w_ffn2_up[l].astype(BF16), w_ffn2_down=w_ffn2_down[l].astype(BF16),
        g_final=row(g_final))


def _trunk(x, state, w, *, tm, pre_tile, scan_tile, lm, lg, sub, time_minor):
    b, t, _ = x.shape
    n = b * t
    x2d = x.reshape(n, D_MODEL)
    x1 = _ffn(x2d, w["g_ffn1"], w["w_ffn1_up"], w["w_ffn1_down"], w["g_final"], final_norm=False, tm=tm)
    if state is None:
        cst = jnp.zeros((b, SUBLANES, M_WIDTH), F32)
        scan_state = None
    else:
        conv0, c0, n0, m0, s0 = state
        cst = jnp.concatenate(
            [jnp.zeros((b, SUBLANES - (CONV_W - 1), M_WIDTH), F32), conv0.astype(F32)], axis=1)
        m0p = jnp.concatenate([m0, jnp.zeros((b, LANES - M_HEADS), F32)], axis=1).reshape(b, 1, LANES)
        scan_state = (c0, n0.reshape(b, M_HEADS, 1, M_HEAD_DIM), m0p, s0)
    qm, km, vm, gif, qg, kg, la, vg, cout = _pre(
        x1.reshape(b, t, D_MODEL), cst, w["g_mix"], w["w_pre"], w["w_small"], w["conv_w"], w["conv_b"],
        w["w_mq"], w["w_mk"], w["bif"], w["wa2"], w["b_a"], bt=pre_tile[0], tb=pre_tile[1],
        time_minor=time_minor)
    hm, og, c_new, n_new, m_new, s_new = _scan(
        qm, km, vm, gif, qg, kg, la, vg, w["g_ghead"], scan_state,
        bt=scan_tile[0], tb=scan_tile[1], lm=lm, lg=lg, sub=sub, time_minor=time_minor)
    x2 = _post(x1, hm if time_minor else hm.reshape(n, M_WIDTH), og.reshape(n, G_VW), w["g_mix"],
               w["g_mhead"], w["w_post"], w["w_pa"], w["w_pb"], w["w_o"], tm=tm, time_minor=time_minor)
    y = _ffn(x2, w["g_ffn2"], w["w_ffn2_up"], w["w_ffn2_down"], w["g_final"], final_norm=True, tm=tm)
    m_heads = m_new[:, :M_HEADS, 0] if time_minor else m_new[:, 0, :M_HEADS]
    return (y.reshape(b, t, D_MODEL),
            cout[:, SUBLANES - (CONV_W - 1):, :][None],
            c_new[None],
            n_new.reshape(b, M_HEADS, M_HEAD_DIM)[None],
            m_heads[None],
            s_new[None])


DECODE_SEQS_PER_STEP = 2


def _tiles(b, t):
    n = b * t
    tm = math.gcd(n, 512)
    if t % 256 == 0:
        return dict(tm=math.gcd(t, 512), pre_tile=(1, 256), scan_tile=(1, 256), lm=256, lg=64, sub=16, time_minor=True)
    assert t <= 64 and t % SUBLANES == 0
    bt = max(1, min(b, 512 // t))
    while b % bt:
        bt -= 1
    return dict(tm=tm, pre_tile=(bt, t), scan_tile=(math.gcd(b, DECODE_SEQS_PER_STEP), t), lm=t, lg=t,
                sub=min(16, t), time_minor=False)


def kernel(x_prompt, x_sample, state_conv, state_mlstm_C, state_mlstm_n, state_mlstm_m, state_gla_S,
           g_ffn1, w_ffn1_up, w_ffn1_down, g_mix, w_in, conv_w, conv_b, w_mq, w_mk, b_if, g_mhead,
           w_a2, b_a, g_ghead, w_pa, w_pb, w_o, g_ffn2, w_ffn2_up, w_ffn2_down, g_final):
    w = _pack_weights(g_ffn1, w_ffn1_up, w_ffn1_down, g_mix, w_in, conv_w, conv_b, w_mq, w_mk, b_if,
                      g_mhead, w_a2, b_a, g_ghead, w_pa, w_pb, w_o, g_ffn2, w_ffn2_up, w_ffn2_down,
                      g_final)
    bp, tp, _ = x_prompt.shape
    bs, ts, _ = x_sample.shape
    out_p = _trunk(x_prompt, None, w, **_tiles(bp, tp))
    state = (state_conv[0], state_mlstm_C[0], state_mlstm_n[0], state_mlstm_m[0], state_gla_S[0])
    out_s = _trunk(x_sample, state, w, **_tiles(bs, ts))
    return (out_p[0], out_s[0]) + out_p[1:] + out_s[1:]
```

```python
import functools
import math
import types

import jax
import jax.numpy as jnp
from jax import lax
from jax.experimental import pallas as pl
from jax.experimental.pallas import tpu as pltpu

D_MODEL = 1024
M_HEADS = 4
M_HEAD_DIM = 256
M_WIDTH = 1024
CONV_W = 4
G_HEADS = 4
G_DK = 128
G_DV = 256
G_KW = 512
G_VW = 1024
G_RANK = 16
G_TAU = 16.0
D_FF = 2816
EPS = 1e-6

LANES = 128
SUBLANES = 8
BF16_ROWS = 16
VMEM_LIMIT = 56 * 1024 * 1024

F32 = jnp.float32
BF16 = jnp.bfloat16
NEG_INF = float("-inf")


def _sigmoid(x):
    return 1.0 / (1.0 + jnp.exp(-x))


def _silu(x):
    return x * _sigmoid(x)


def _log_sigmoid(x):
    return jnp.minimum(x, 0.0) - jnp.log(1.0 + jnp.exp(-jnp.abs(x)))


def _rms(x, g):
    return x * lax.rsqrt(jnp.mean(x * x, axis=-1, keepdims=True) + EPS) * g


def _dot(a, b):
    return jnp.dot(a, b, preferred_element_type=F32)


def _dot_nt(a, b):
    return lax.dot_general(a, b, (((1,), (1,)), ((), ())), preferred_element_type=F32)


def _dot_tn(a, b):
    return lax.dot_general(a, b, (((0,), (0,)), ((), ())), preferred_element_type=F32)


def _const_spec(shape):
    nd = len(shape)
    return pl.BlockSpec(shape, lambda *_: (0,) * nd, pipeline_mode=pl.Buffered(1))


FF_CHUNK = 256


def _ffn_kernel(x_ref, g_ref, wup_ref, wdown_ref, gfin_ref, o_ref, h_ref, *, final_norm):
    x = x_ref[...]
    xn = _rms(x, g_ref[...]).astype(BF16)
    for c in range(D_FF // FF_CHUNK):
        a = _dot(xn, wup_ref[:, c * FF_CHUNK:(c + 1) * FF_CHUNK])
        g = _dot(xn, wup_ref[:, D_FF + c * FF_CHUNK:D_FF + (c + 1) * FF_CHUNK])
        h_ref[:, c * FF_CHUNK:(c + 1) * FF_CHUNK] = (_silu(g) * a).astype(BF16)
    y = x + 0.5 * _dot(h_ref[...], wdown_ref[...])
    if final_norm:
        y = _rms(y, gfin_ref[...])
    o_ref[...] = y


def _ffn(x2d, g, wup, wdown, gfin, *, final_norm, tm):
    n = x2d.shape[0]
    assert n % tm == 0
    return pl.pallas_call(
        functools.partial(_ffn_kernel, final_norm=final_norm),
        grid=(n // tm,),
        in_specs=[
            pl.BlockSpec((tm, D_MODEL), lambda i: (i, 0)),
            _const_spec((1, D_MODEL)),
            _const_spec((D_MODEL, 2 * D_FF)),
            _const_spec((D_FF, D_MODEL)),
            _const_spec((1, D_MODEL)),
        ],
        out_specs=pl.BlockSpec((tm, D_MODEL), lambda i: (i, 0)),
        out_shape=jax.ShapeDtypeStruct((n, D_MODEL), F32),
        scratch_shapes=[pltpu.VMEM((tm, D_FF), BF16)],
        compiler_params=pltpu.CompilerParams(
            dimension_semantics=("parallel",), vmem_limit_bytes=VMEM_LIMIT),
        name="ffn_final" if final_norm else "ffn",
    )(x2d, g, wup, wdown, gfin)


PRE_U, PRE_V, PRE_QG, PRE_KG, PRE_VG = 0, 1024, 2048, 2560, 3072
PRE_WIDTH = 4096
A_LANE0 = 8


def _pre_kernel(x_ref, cst_ref, g_ref, w_ref, ws_ref, cw_ref, cb_ref, wq_ref, wk_ref, bif_ref,
                wa2_ref, ba_ref,
                qm_ref, km_ref, vm_ref, gif_ref, qg_ref, kg_ref, la_ref, vg_ref, cout_ref,
                cbuf_ref, *, bt, tb, lg, time_minor):
    t = pl.program_id(1)
    m = bt * tb
    xn = _rms(x_ref[...].reshape(m, D_MODEL), g_ref[...]).astype(BF16)

    small = _dot(xn, ws_ref[...])
    gates = small + bif_ref[...]
    lane = lax.broadcasted_iota(jnp.int32, gates.shape, 1)
    gif = jnp.where(lane < M_HEADS, gates, jnp.where(lane < 2 * M_HEADS, _log_sigmoid(gates), 0.0))
    if time_minor:
        b = pltpu.roll(_scan_rows(gif, jnp.add, 0.0), LANES - M_HEADS, axis=1)
        g = gif - b
        gmax = _scan_rows(g, jnp.maximum, NEG_INF)
        packed = jnp.where(lane < M_HEADS, g,
                           jnp.where(lane < 2 * M_HEADS, pltpu.roll(gmax, M_HEADS, axis=1),
                                     jnp.where(lane < 3 * M_HEADS, pltpu.roll(b, 2 * M_HEADS, axis=1), 0.0)))
        gif_ref[0] = jnp.transpose(packed)[0:2 * SUBLANES, :]
    else:
        gif_ref[...] = gif.reshape(bt, tb, LANES)
    a2 = _dot(small.astype(BF16), wa2_ref[...]) + ba_ref[...]
    la_ref[...] = _scan_rows(_log_sigmoid(a2) * (1.0 / G_TAU), jnp.add, 0.0, segment=lg).reshape(bt, tb, G_KW)

    @pl.when(t == 0)
    def _():
        cbuf_ref[:, 0:SUBLANES, :] = cst_ref[...]

    @pl.when(t > 0)
    def _():
        cbuf_ref[:, 0:SUBLANES, :] = cbuf_ref[:, tb:tb + SUBLANES, :]

    u = _dot(xn, w_ref[:, PRE_U:PRE_U + M_WIDTH])
    cbuf_ref[:, SUBLANES:SUBLANES + tb, :] = u.reshape(bt, tb, M_WIDTH)
    c = cb_ref[...].reshape(1, 1, M_WIDTH)
    for j in range(CONV_W):
        off = SUBLANES - (CONV_W - 1) + j
        c = c + cbuf_ref[:, off:off + tb, :] * cw_ref[j:j + 1, :].reshape(1, 1, M_WIDTH)
    cout_ref[...] = cbuf_ref[:, tb:tb + SUBLANES, :]
    ch = _silu(c).reshape(m, M_WIDTH).astype(BF16)
    vm = _dot(xn, w_ref[:, PRE_V:PRE_V + M_WIDTH])
    for h in range(M_HEADS):
        sl = slice(h * M_HEAD_DIM, (h + 1) * M_HEAD_DIM)
        qh = _dot(ch[:, sl], wq_ref[h])
        if time_minor:
            qm_ref[0, sl, :] = jnp.transpose(qh).astype(BF16)
            vm_ref[0, sl, :] = jnp.transpose(vm[:, sl]).astype(BF16)
        else:
            qm_ref[:, :, sl] = qh.astype(BF16).reshape(bt, tb, M_HEAD_DIM)
        km_ref[:, :, sl] = (_dot(ch[:, sl], wk_ref[h]) * (M_HEAD_DIM ** -0.5)).astype(BF16).reshape(
            bt, tb, M_HEAD_DIM)
    if not time_minor:
        vm_ref[...] = vm.astype(BF16).reshape(bt, tb, M_WIDTH)

    qg_ref[...] = (_dot(xn, w_ref[:, PRE_QG:PRE_QG + G_KW]) * (G_DK ** -0.5)).astype(BF16).reshape(
        bt, tb, G_KW)
    kg_ref[...] = _dot(xn, w_ref[:, PRE_KG:PRE_KG + G_KW]).astype(BF16).reshape(bt, tb, G_KW)
    vg_ref[...] = _dot(xn, w_ref[:, PRE_VG:PRE_VG + G_VW]).astype(BF16).reshape(bt, tb, G_VW)


def _pre(x, cst, g, w_pre, w_small, conv_w, conv_b, wq, wk, bif, wa2, ba, *, bt, tb, lg, time_minor):
    b, t, _ = x.shape
    assert b % bt == 0 and t % tb == 0 and tb % SUBLANES == 0 and tb % lg == 0
    assert not time_minor or (bt == 1 and tb % LANES == 0)
    tok = lambda w: pl.BlockSpec((bt, tb, w), lambda i, j: (i, j, 0))
    per_seq = pl.BlockSpec((bt, SUBLANES, M_WIDTH), lambda i, j: (i, 0, 0))
    sds = lambda w, dt: jax.ShapeDtypeStruct((b, t, w), dt)
    if time_minor:
        tok_t = lambda w: pl.BlockSpec((1, w, tb), lambda i, j: (i, 0, j))
        sds_t = lambda w, dt: jax.ShapeDtypeStruct((b, w, t), dt)
        qv_spec, qv_shape = tok_t(M_WIDTH), sds_t(M_WIDTH, BF16)
        gif_spec, gif_shape = tok_t(2 * SUBLANES), sds_t(2 * SUBLANES, F32)
    else:
        qv_spec, qv_shape = tok(M_WIDTH), sds(M_WIDTH, BF16)
        gif_spec, gif_shape = tok(LANES), sds(LANES, F32)
    return pl.pallas_call(
        functools.partial(_pre_kernel, bt=bt, tb=tb, lg=lg, time_minor=time_minor),
        grid=(b // bt, t // tb),
        in_specs=[
            tok(D_MODEL), per_seq,
            _const_spec((1, D_MODEL)),
            _const_spec((D_MODEL, PRE_WIDTH)),
            _const_spec((D_MODEL, LANES)),
            _const_spec((CONV_W, M_WIDTH)),
            _const_spec((1, M_WIDTH)),
            _const_spec((M_HEADS, M_HEAD_DIM, M_HEAD_DIM)),
            _const_spec((M_HEADS, M_HEAD_DIM, M_HEAD_DIM)),
            _const_spec((1, LANES)),
            _const_spec((LANES, G_KW)),
            _const_spec((1, G_KW)),
        ],
        out_specs=[qv_spec, tok(M_WIDTH), qv_spec, gif_spec, tok(G_KW), tok(G_KW),
                   tok(G_KW), tok(G_VW), per_seq],
        out_shape=[qv_shape, sds(M_WIDTH, BF16), qv_shape, gif_shape,
                   sds(G_KW, BF16), sds(G_KW, BF16), sds(G_KW, F32), sds(G_VW, BF16),
                   jax.ShapeDtypeStruct((b, SUBLANES, M_WIDTH), F32)],
        scratch_shapes=[pltpu.VMEM((bt, tb + SUBLANES, M_WIDTH), F32)],
        compiler_params=pltpu.CompilerParams(
            dimension_semantics=("parallel", "arbitrary"), vmem_limit_bytes=VMEM_LIMIT),
        name="mixer_pre",
    )(x, cst, g, w_pre, w_small, conv_w, conv_b, wq, wk, bif, wa2, ba)


def _scan_rows(x, op, fill, segment=None):
    n = segment or x.shape[0]
    row = lax.broadcasted_iota(jnp.int32, x.shape, 0) % n
    s = 1
    while s < n:
        x = op(x, jnp.where(row >= s, pltpu.roll(x, s, axis=0), fill))
        s *= 2
    return x


def _col_of_row(row):
    return jnp.transpose(jnp.broadcast_to(row, (SUBLANES, LANES)))[:, 0:1]


def _mlstm_chunk_tm(seqs, *, lm):
    reps = lm // LANES
    head_row = lax.broadcasted_iota(jnp.int32, (SUBLANES, LANES), 0) < M_HEADS
    src = lax.broadcasted_iota(jnp.int32, (lm, lm), 0)
    dst = lax.broadcasted_iota(jnp.int32, (lm, lm), 1)
    causal = src <= dst
    sl = [slice(h * M_HEAD_DIM, (h + 1) * M_HEAD_DIM) for h in range(M_HEADS)]
    row = lambda x, h: x[h:h + 1, :]
    gates = []
    for sq in seqs:
        g = sq.gift[0:SUBLANES]
        gmax = pltpu.roll(g, M_HEADS, axis=0)
        b = sq.gift[SUBLANES:2 * SUBLANES]
        m_prev = jnp.concatenate([sq.m_src[...]] * reps, axis=1)
        mx = jnp.maximum(m_prev, gmax)
        mx_last = jnp.broadcast_to(mx[:, lm - 1:lm], (SUBLANES, lm))
        gates.append(dict(
            mx=mx,
            inter=jnp.exp(m_prev - mx),
            nfloor=jnp.exp(-(b + mx)),
            w_in=jnp.exp(g - mx_last),
            decay=jnp.exp(m_prev - mx_last),
            g_col=jnp.transpose(g)))
        m_new = jnp.broadcast_to(b[:, lm - 1:lm], (SUBLANES, lm)) + mx_last
        sq.m_dst[...] = jnp.where(head_row, m_new[:, :LANES], 0.0)
    units = [(sq, gt, h) for sq, gt in zip(seqs, gates) for h in range(M_HEADS)]
    c_old = [sq.c_src[h] for sq, _, h in units]
    n_old = [sq.n_src[h] for sq, _, h in units]
    cn = [jnp.concatenate([c_old[u].astype(BF16),
                           jnp.broadcast_to(n_old[u].astype(BF16), (BF16_ROWS, M_HEAD_DIM))], axis=0)
          for u in range(len(units))]
    cq = [_dot(cn[u], sq.qt[sl[h], :]) for u, (sq, _, h) in enumerate(units)]
    dm = [jnp.exp(jnp.where(causal, gt["g_col"][:, h:h + 1] - row(gt["mx"], h), NEG_INF))
          for _, gt, h in units]
    st = [_dot(sq.k[:, sl[h]], sq.qt[sl[h], :]) * dm[u]
          for u, (sq, _, h) in enumerate(units)]
    num = [row(gt["inter"], h) * cq[u][:M_HEAD_DIM] + _dot(sq.vt[sl[h], :], st[u].astype(BF16))
           for u, (sq, gt, h) in enumerate(units)]
    for u, (sq, gt, h) in enumerate(units):
        den = (row(gt["inter"], h) * cq[u][M_HEAD_DIM:M_HEAD_DIM + 1]
               + jnp.sum(st[u], axis=0, keepdims=True))
        rcp = 1.0 / jnp.maximum(jnp.abs(den), row(gt["nfloor"], h))
        ms = jnp.mean(num[u] * num[u], axis=0, keepdims=True)
        sq.hm_store(h, num[u] * (rcp * lax.rsqrt(rcp * rcp * ms + EPS)))
    for u, (sq, gt, h) in enumerate(units):
        w_h = row(gt["w_in"], h)
        dec_h = jnp.concatenate([row(gt["decay"], h)[:, :LANES]] * (M_HEAD_DIM // LANES), axis=1)
        wv = jnp.concatenate([(sq.vt[sl[h], :].astype(F32) * w_h).astype(BF16),
                              jnp.broadcast_to(w_h.astype(BF16), (BF16_ROWS, lm))], axis=0)
        upd = _dot(wv, sq.k[:, sl[h]])
        sq.c_dst[h] = dec_h * c_old[u] + upd[:M_HEAD_DIM]
        sq.n_dst[h] = dec_h * n_old[u] + upd[M_HEAD_DIM:M_HEAD_DIM + 1]


def _mlstm_chunk(seqs, *, lm):
    row = lax.broadcasted_iota(jnp.int32, (lm, lm), 0)
    col = lax.broadcasted_iota(jnp.int32, (lm, lm), 1)
    causal = col <= row
    sl = [slice(h * M_HEAD_DIM, (h + 1) * M_HEAD_DIM) for h in range(M_HEADS)]
    gates = []
    for sq in seqs:
        cs = _scan_rows(sq.gif, jnp.add, 0.0)
        b = pltpu.roll(cs, LANES - M_HEADS, axis=1)
        g = sq.gif - b
        m_prev = sq.m_src[...]
        mx = jnp.maximum(m_prev, _scan_rows(g, jnp.maximum, NEG_INF))
        mx_last = mx[lm - 1:lm, :]
        gates.append(dict(
            mx=mx,
            inter=jnp.exp(m_prev - mx),
            nfloor=jnp.exp(-(b + mx)),
            g_t=jnp.transpose(g),
            w_in=jnp.exp(g - mx_last),
            decay=jnp.exp(m_prev - mx_last)))
        sq.m_dst[...] = b[lm - 1:lm, :] + mx_last
    units = [(sq, gt, h) for sq, gt in zip(seqs, gates) for h in range(M_HEADS)]
    col_of = lambda x, h: x[:, h:h + 1]
    c_old = [sq.c_src[h] for sq, _, h in units]
    n_old = [sq.n_src[h] for sq, _, h in units]
    dm = [jnp.exp(jnp.where(causal, gt["g_t"][h:h + 1, :] - col_of(gt["mx"], h), NEG_INF))
          for _, gt, h in units]
    qn = [jnp.sum(sq.q[:, sl[h]].astype(F32) * n_old[u], axis=-1, keepdims=True)
          for u, (sq, _, h) in enumerate(units)]
    s = [_dot_nt(sq.q[:, sl[h]], sq.k[:, sl[h]]) * dm[u] for u, (sq, _, h) in enumerate(units)]
    num = [col_of(gt["inter"], h) * _dot_nt(sq.q[:, sl[h]], c_old[u].astype(BF16))
           + _dot(s[u].astype(BF16), sq.v[:, sl[h]]) for u, (sq, gt, h) in enumerate(units)]
    for u, (sq, gt, h) in enumerate(units):
        den = col_of(gt["inter"], h) * qn[u] + jnp.sum(s[u], axis=-1, keepdims=True)
        rcp = 1.0 / jnp.maximum(jnp.abs(den), col_of(gt["nfloor"], h))
        ms = jnp.mean(num[u] * num[u], axis=-1, keepdims=True)
        sq.hm_store(h, num[u] * (rcp * lax.rsqrt(rcp * rcp * ms + EPS)))
    for u, (sq, gt, h) in enumerate(units):
        w_h = col_of(gt["w_in"], h)
        dec_h = col_of(gt["decay"], h)
        wv = (w_h * sq.v[:, sl[h]].astype(F32)).astype(BF16)
        sq.c_dst[h] = dec_h * c_old[u] + _dot_tn(wv, sq.k[:, sl[h]])
        sq.n_dst[h] = dec_h * n_old[u] + jnp.sum(w_h * sq.k[:, sl[h]].astype(F32), axis=0, keepdims=True)


GLA_SAFE_SPAN = 40.0


def _gla_scores_guarded(qh, kh, a, *, lg, sub):
    row = lax.broadcasted_iota(jnp.int32, (lg, lg), 0)
    col = lax.broadcasted_iota(jnp.int32, (lg, lg), 1)
    att = jnp.zeros((lg, lg), F32)
    c = sub
    while c < lg:
        nblk = lg // (2 * c)
        a_ref = jnp.concatenate(
            [jnp.broadcast_to(a[2 * c * i + c - 1:2 * c * i + c, :], (2 * c, G_DK))
             for i in range(nblk)], axis=0)
        qt = (qh * jnp.exp(jnp.minimum(a - a_ref, 0.0))).astype(BF16)
        kt = (kh * jnp.exp(jnp.minimum(a_ref - a, 0.0))).astype(BF16)
        valid = (row // (2 * c) == col // (2 * c)) & (row % (2 * c) >= c) & (col % (2 * c) < c)
        att = att + jnp.where(valid, _dot_nt(qt, kt), 0.0)
        c *= 2
    rows = lax.broadcasted_iota(jnp.int32, (sub, lg), 0)
    cols = lax.broadcasted_iota(jnp.int32, (sub, lg), 1)
    blocks = []
    for blk in range(lg // sub):
        r0 = blk * sub
        a_b = a[r0:r0 + sub, :]
        q_b = qh[r0:r0 + sub, :]
        acc = jnp.zeros((sub, lg), F32)
        for s in range(sub):
            e = jnp.exp(jnp.minimum(a_b - a[r0 + s:r0 + s + 1, :], 0.0))
            p = jnp.sum(q_b * e * kh[r0 + s:r0 + s + 1, :], axis=-1, keepdims=True)
            acc = jnp.where((cols == r0 + s) & (rows >= s), p, acc)
        blocks.append(acc)
    return att + jnp.concatenate(blocks, axis=0)


def _gla_chunk_guarded(seqs, gh_ref, *, lg, sub):
    ks = [slice(h * G_DK, (h + 1) * G_DK) for h in range(G_HEADS)]
    vs = [slice(h * G_DV, (h + 1) * G_DV) for h in range(G_HEADS)]
    pre = []
    for sq in seqs:
        a_last = sq.a[lg - 1:lg, :]
        qf = sq.qg.astype(F32)
        kf = sq.kg.astype(F32)
        pre.append(dict(a=sq.a, a_last=a_last, qf=qf, kf=kf,
                        qs=(qf * jnp.exp(sq.a)).astype(BF16),
                        kl=(kf * jnp.exp(a_last - sq.a)).astype(BF16)))
    units = [(sq, p, h) for sq, p in zip(seqs, pre) for h in range(G_HEADS)]
    s_old = [sq.s_src[h] for sq, _, h in units]
    att = [_gla_scores_guarded(p["qf"][:, ks[h]], p["kf"][:, ks[h]], p["a"][:, ks[h]], lg=lg, sub=sub)
           for _, p, h in units]
    o = [_dot(p["qs"][:, ks[h]], s_old[u].astype(BF16)) + _dot(att[u].astype(BF16), sq.vg[:, vs[h]])
         for u, (sq, p, h) in enumerate(units)]
    for u, (sq, _, h) in enumerate(units):
        sq.og_store(h, _rms(o[u], gh_ref[:, vs[h]]))
    for u, (sq, p, h) in enumerate(units):
        sq.s_dst[h] = (jnp.exp(_col_of_row(p["a_last"][:, ks[h]])) * s_old[u]
                       + _dot_tn(p["kl"][:, ks[h]], sq.vg[:, vs[h]]))


def _gla_block_direct(seqs, gh_ref, *, tb, lg):
    ks = [slice(h * G_DK, (h + 1) * G_DK) for h in range(G_HEADS)]
    vs = [slice(h * G_DV, (h + 1) * G_DV) for h in range(G_HEADS)]
    chunks = [slice(c * lg, (c + 1) * lg) for c in range(tb // lg)]
    row = lax.broadcasted_iota(jnp.int32, (lg, lg), 0)
    col = lax.broadcasted_iota(jnp.int32, (lg, lg), 1)
    causal = col <= row
    pre = []
    for sq in seqs:
        a_last = [sq.a[r.stop - 1:r.stop, :] for r in chunks]
        a_last_rows = jnp.concatenate([jnp.broadcast_to(al, (lg, G_KW)) for al in a_last], axis=0)
        qf = sq.qg.astype(F32)
        kf = sq.kg.astype(F32)
        pre.append(dict(a_last=a_last,
                        qs=(qf * jnp.exp(sq.a)).astype(BF16),
                        kt=(kf * jnp.exp(-sq.a)).astype(BF16),
                        kl=(kf * jnp.exp(a_last_rows - sq.a)).astype(BF16)))
    units = [(sq, p, h) for sq, p in zip(seqs, pre) for h in range(G_HEADS)]
    att = [[jnp.where(causal, _dot_nt(p["qs"][r, ks[h]], p["kt"][r, ks[h]]), 0.0).astype(BF16)
            for r in chunks] for _, p, h in units]
    s_cur = [sq.s_src[h] for sq, _, h in units]
    for c, r in enumerate(chunks):
        o = [_dot(p["qs"][r, ks[h]], s_cur[u].astype(BF16)) + _dot(att[u][c], sq.vg[r, vs[h]])
             for u, (sq, p, h) in enumerate(units)]
        for u, (sq, _, h) in enumerate(units):
            sq.og_rows_store(r, h, _rms(o[u], gh_ref[:, vs[h]]))
        s_cur = [jnp.exp(_col_of_row(p["a_last"][c][:, ks[h]])) * s_cur[u]
                 + _dot_tn(p["kl"][r, ks[h]], sq.vg[r, vs[h]])
                 for u, (sq, p, h) in enumerate(units)]
    for u, (sq, _, h) in enumerate(units):
        sq.s_dst[h] = s_cur[u]


def _scan_kernel(*refs, bt, tb, lm, lg, sub, has_state, time_minor, single_step):
    n_in = 9 + (4 if has_state else 0)
    qm_ref, km_ref, vm_ref, gif_ref, qg_ref, kg_ref, la_ref, vg_ref, ggh_ref = refs[:9]
    state_in = refs[9:n_in]
    hm_ref, og_ref = refs[n_in:n_in + 2]
    state_out = refs[n_in + 2:n_in + 6]
    scratch = refs[n_in + 6:]
    t = pl.program_id(1)
    nt = pl.num_programs(1)

    streamed = has_state and single_step
    src = state_in if streamed else scratch
    dst = state_out if streamed else scratch
    if not streamed:
        @pl.when(t == 0)
        def _():
            for i, ref in enumerate(scratch):
                ref[...] = state_in[i][...] if has_state else jnp.zeros(ref.shape, F32)

    def seq_records(bi, m_rows, g_rows):
        def hm_store(h, val):
            if time_minor:
                hm_ref[bi, h * M_HEAD_DIM:(h + 1) * M_HEAD_DIM, m_rows] = val
            else:
                hm_ref[bi, m_rows, h * M_HEAD_DIM:(h + 1) * M_HEAD_DIM] = val

        def og_rows_store(rows, h, val):
            og_ref[bi, rows, h * G_DV:(h + 1) * G_DV] = val

        return types.SimpleNamespace(
            bi=bi, hm_store=hm_store, og_rows_store=og_rows_store,
            og_store=functools.partial(og_rows_store, g_rows),
            c_src=src[0].at[bi], n_src=src[1].at[bi], m_src=src[2].at[bi], s_src=src[3].at[bi],
            c_dst=dst[0].at[bi], n_dst=dst[1].at[bi], m_dst=dst[2].at[bi], s_dst=dst[3].at[bi])

    def m_body(ci, carry):
        rows = pl.ds(pl.multiple_of(ci * lm, lm), lm)
        seqs = [seq_records(bi, rows, None) for bi in range(bt)]
        if time_minor:
            for sq in seqs:
                sq.qt, sq.k, sq.vt = qm_ref[sq.bi, :, rows], km_ref[sq.bi, rows, :], vm_ref[sq.bi, :, rows]
                sq.gift = gif_ref[sq.bi, :, rows]
            _mlstm_chunk_tm(seqs, lm=lm)
        else:
            for sq in seqs:
                sq.q, sq.k, sq.v = qm_ref[sq.bi, rows, :], km_ref[sq.bi, rows, :], vm_ref[sq.bi, rows, :]
                sq.gif = gif_ref[sq.bi, rows, :]
            _mlstm_chunk(seqs, lm=lm)
        return carry

    def gla_records(rows):
        seqs = [seq_records(bi, None, rows) for bi in range(bt)]
        for sq in seqs:
            sq.qg, sq.kg, sq.vg = qg_ref[sq.bi, rows, :], kg_ref[sq.bi, rows, :], vg_ref[sq.bi, rows, :]
            sq.a = la_ref[sq.bi, rows, :]
        return seqs

    def g_body(ci, carry):
        rows = pl.ds(pl.multiple_of(ci * lg, lg), lg)
        _gla_chunk_guarded(gla_records(rows), ggh_ref, lg=lg, sub=sub)
        return carry

    def loop(body, n):
        if n == 1:
            body(0, 0)
        else:
            lax.fori_loop(0, n, body, 0)

    loop(m_body, tb // lm)
    if lg <= sub:
        loop(g_body, tb // lg)
    else:
        chunk_decay = [la_ref[:, c * lg + lg - 1:(c + 1) * lg, :] for c in range(tb // lg)]
        span_ok = jnp.min(functools.reduce(jnp.minimum, chunk_decay)) >= -GLA_SAFE_SPAN

        @pl.when(span_ok)
        def _():
            _gla_block_direct(gla_records(slice(0, tb)), ggh_ref, tb=tb, lg=lg)

        @pl.when(jnp.logical_not(span_ok))
        def _():
            loop(g_body, tb // lg)

    if not streamed:
        @pl.when(t == nt - 1)
        def _():
            for out, ref in zip(state_out, scratch):
                out[...] = ref[...]


def _scan(qm, km, vm, gif, qg, kg, la, vg, ggh, state, *, bt, tb, lm, lg, sub, time_minor):
    b, t, _ = km.shape
    assert b % bt == 0 and t % tb == 0 and tb % lm == 0 and tb % lg == 0 and lg % sub == 0
    has_state = state is not None
    assert not (time_minor and has_state) and (not time_minor or lm % LANES == 0)
    tok = lambda w: pl.BlockSpec((bt, tb, w), lambda i, j: (i, j, 0))
    tok_t = lambda w: pl.BlockSpec((bt, w, tb), lambda i, j: (i, 0, j))
    c_shape = (M_HEADS, M_HEAD_DIM, M_HEAD_DIM)
    n_shape = (M_HEADS, 1, M_HEAD_DIM)
    m_shape = (SUBLANES, LANES) if time_minor else (1, LANES)
    s_shape = (G_HEADS, G_DK, G_DV)
    st_shapes = [(bt,) + shp for shp in (c_shape, n_shape, m_shape, s_shape)]
    st_specs = [pl.BlockSpec(shp, lambda i, j, nd=len(shp): (i,) + (0,) * (nd - 1)) for shp in st_shapes]
    single_step = t == tb
    carried = not (has_state and single_step)
    if time_minor:
        qv_spec, gif_spec = tok_t(M_WIDTH), tok_t(2 * SUBLANES)
        hm_spec, hm_shape = tok_t(M_WIDTH), (b, M_WIDTH, t)
    else:
        qv_spec, gif_spec = tok(M_WIDTH), tok(LANES)
        hm_spec, hm_shape = tok(M_WIDTH), (b, t, M_WIDTH)
    in_specs = [qv_spec, tok(M_WIDTH), qv_spec, gif_spec, tok(G_KW), tok(G_KW),
                tok(G_KW), tok(G_VW), _const_spec((1, G_VW))]
    args = [qm, km, vm, gif, qg, kg, la, vg, ggh]
    if has_state:
        in_specs += st_specs
        args += list(state)
    return pl.pallas_call(
        functools.partial(_scan_kernel, bt=bt, tb=tb, lm=lm, lg=lg, sub=sub, has_state=has_state,
                          time_minor=time_minor, single_step=single_step),
        grid=(b // bt, t // tb),
        in_specs=in_specs,
        out_specs=[hm_spec, tok(G_VW)] + st_specs,
        out_shape=[jax.ShapeDtypeStruct(hm_shape, F32),
                   jax.ShapeDtypeStruct((b, t, G_VW), F32)]
                  + [jax.ShapeDtypeStruct((b,) + shp[1:], F32) for shp in st_shapes],
        scratch_shapes=[pltpu.VMEM(shp, F32) for shp in st_shapes] if carried else [],
        compiler_params=pltpu.CompilerParams(
            dimension_semantics=("parallel", "arbitrary"), vmem_limit_bytes=VMEM_LIMIT),
        name="mixer_scan_state" if has_state else "mixer_scan",
    )(*args)


POST_O, POST_R, POST_A, POST_B = 0, 1024, 2048, 3072


def _post_kernel(x_ref, hm_ref, og_ref, g_ref, gmh_ref, w_ref, wpa_ref, wpb_ref, wo_ref, o_ref, *,
                 time_minor):
    x = x_ref[...]
    xn = _rms(x, g_ref[...]).astype(BF16)
    gate = lambda off: _dot(xn, w_ref[:, off:off + D_MODEL])
    if time_minor:
        hm = jnp.concatenate(
            [jnp.transpose(hm_ref[0, h * M_HEAD_DIM:(h + 1) * M_HEAD_DIM, :]) for h in range(M_HEADS)],
            axis=1)
    else:
        hm = hm_ref[...]
    hm = (_sigmoid(gate(POST_O)) * (hm * gmh_ref[...])).astype(BF16)
    og = (_silu(gate(POST_R)) * og_ref[...]).astype(BF16)
    y = (_sigmoid(gate(POST_A)) * _dot(hm, wpa_ref[...])
         + _sigmoid(gate(POST_B)) * _dot(og, wpb_ref[...]))
    o_ref[...] = x + _dot(y.astype(BF16), wo_ref[...])


def _post(x2d, hm, og2d, g, gmh, w_post, wpa, wpb, wo, *, tm, time_minor):
    n = x2d.shape[0]
    assert n % tm == 0
    tok = pl.BlockSpec((tm, D_MODEL), lambda i: (i, 0))
    if time_minor:
        per_seq = hm.shape[2] // tm
        assert hm.shape[2] % tm == 0 and tm % LANES == 0
        hm_spec = pl.BlockSpec((1, M_WIDTH, tm), lambda i: (i // per_seq, 0, i % per_seq))
    else:
        hm_spec = tok
    return pl.pallas_call(
        functools.partial(_post_kernel, time_minor=time_minor),
        grid=(n // tm,),
        in_specs=[tok, hm_spec, tok, _const_spec((1, D_MODEL)), _const_spec((1, M_WIDTH)),
                  _const_spec((D_MODEL, 4 * D_MODEL)),
                  _const_spec((M_WIDTH, D_MODEL)), _const_spec((G_VW, D_MODEL)),
                  _const_spec((D_MODEL, D_MODEL))],
        out_specs=tok,
        out_shape=jax.ShapeDtypeStruct((n, D_MODEL), F32),
        compiler_params=pltpu.CompilerParams(
            dimension_semantics=("parallel",), vmem_limit_bytes=VMEM_LIMIT),
        name="mixer_post",
    )(x2d, hm, og2d, g, gmh, w_post, wpa, wpb, wo)


def _pack_weights(g_ffn1, w_ffn1_up, w_ffn1_down, g_mix, w_in, conv_w, conv_b, w_mq, w_mk, b_if,
                  g_mhead, w_a2, b_a, g_ghead, w_pa, w_pb, w_o, g_ffn2, w_ffn2_up, w_ffn2_down,
                  g_final):
    l = 0
    row = lambda v: v.reshape(1, -1).astype(F32)
    edges = [0]
    for wd in (M_WIDTH, M_WIDTH, M_WIDTH, 2 * M_HEADS, G_KW, G_KW, G_VW, G_VW, G_RANK, D_MODEL, D_MODEL):
        edges.append(edges[-1] + wd)
    seg = lambda i: w_in[l][:, edges[i]:edges[i + 1]]
    u_m, v_m, o_m, if_m, q_g, k_g, v_g, r_g, a_g, g_a, g_b = (seg(i) for i in range(11))
    w_pre = jnp.concatenate([u_m, v_m, q_g, k_g, v_g], axis=1).astype(BF16)
    w_small = jnp.concatenate(
        [if_m, a_g, jnp.zeros((D_MODEL, LANES - 2 * M_HEADS - G_RANK), F32)], axis=1).astype(BF16)
    w_post = jnp.concatenate([o_m, r_g, g_a, g_b], axis=1).astype(BF16)
    bif = jnp.concatenate([b_if[l].reshape(-1), jnp.zeros((LANES - 2 * M_HEADS,), F32)]).reshape(1, LANES)
    wa2 = jnp.zeros((LANES, G_KW), F32).at[A_LANE0:A_LANE0 + G_RANK].set(w_a2[l]).astype(BF16)
    return dict(
        g_ffn1=row(g_ffn1[l]), w_ffn1_up=w_ffn1_up[l].astype(BF16), w_ffn1_down=w_ffn1_down[l].astype(BF16),
        g_mix=row(g_mix[l]), w_pre=w_pre, w_small=w_small, w_post=w_post,
        conv_w=conv_w[l].astype(F32), conv_b=row(conv_b[l]),
        w_mq=w_mq[l].astype(BF16), w_mk=w_mk[l].astype(BF16), bif=bif,
        g_mhead=row(g_mhead[l]), wa2=wa2, b_a=row(b_a[l]), g_ghead=row(g_ghead[l]),
        w_pa=w_pa[l].astype(BF16), w_pb=w_pb[l].astype(BF16), w_o=w_o[l].astype(BF16),
        g_ffn2=row(g_ffn2[l]), w_ffn2_up=w_ffn2_up[l].astype(BF16), w_ffn2_down=w_ffn2_down[l].astype(BF16),
        g_final=row(g_final))


def _trunk(x, state, w, *, tm, pre_tile, scan_tile, lm, lg, sub, time_minor):
    b, t, _ = x.shape
    n = b * t
    x2d = x.reshape(n, D_MODEL)
    x1 = _ffn(x2d, w["g_ffn1"], w["w_ffn1_up"], w["w_ffn1_down"], w["g_final"], final_norm=False, tm=tm)
    if state is None:
        cst = jnp.zeros((b, SUBLANES, M_WIDTH), F32)
        scan_state = None
    else:
        conv0, c0, n0, m0, s0 = state
        cst = jnp.concatenate(
            [jnp.zeros((b, SUBLANES - (CONV_W - 1), M_WIDTH), F32), conv0.astype(F32)], axis=1)
        m0p = jnp.concatenate([m0, jnp.zeros((b, LANES - M_HEADS), F32)], axis=1).reshape(b, 1, LANES)
        scan_state = (c0, n0.reshape(b, M_HEADS, 1, M_HEAD_DIM), m0p, s0)
    qm, km, vm, gif, qg, kg, la, vg, cout = _pre(
        x1.reshape(b, t, D_MODEL), cst, w["g_mix"], w["w_pre"], w["w_small"], w["conv_w"], w["conv_b"],
        w["w_mq"], w["w_mk"], w["bif"], w["wa2"], w["b_a"], bt=pre_tile[0], tb=pre_tile[1], lg=lg,
        time_minor=time_minor)
    hm, og, c_new, n_new, m_new, s_new = _scan(
        qm, km, vm, gif, qg, kg, la, vg, w["g_ghead"], scan_state,
        bt=scan_tile[0], tb=scan_tile[1], lm=lm, lg=lg, sub=sub, time_minor=time_minor)
    x2 = _post(x1, hm if time_minor else hm.reshape(n, M_WIDTH), og.reshape(n, G_VW), w["g_mix"],
               w["g_mhead"], w["w_post"], w["w_pa"], w["w_pb"], w["w_o"], tm=tm, time_minor=time_minor)
    y = _ffn(x2, w["g_ffn2"], w["w_ffn2_up"], w["w_ffn2_down"], w["g_final"], final_norm=True, tm=tm)
    m_heads = m_new[:, :M_HEADS, 0] if time_minor else m_new[:, 0, :M_HEADS]
    return (y.reshape(b, t, D_MODEL),
            cout[:, SUBLANES - (CONV_W - 1):, :][None],
            c_new[None],
            n_new.reshape(b, M_HEADS, M_HEAD_DIM)[None],
            m_heads[None],
            s_new[None])


DECODE_SEQS_PER_STEP = 2
PREFILL_SEQS_PER_STEP = 2


def _tiles(b, t):
    n = b * t
    tm = math.gcd(n, 512)
    if t % 256 == 0:
        return dict(tm=math.gcd(t, 512), pre_tile=(1, 256), scan_tile=(math.gcd(b, PREFILL_SEQS_PER_STEP), 256),
                    lm=256, lg=128, sub=16, time_minor=True)
    assert t <= 64 and t % SUBLANES == 0
    bt = max(1, min(b, 512 // t))
    while b % bt:
        bt -= 1
    return dict(tm=tm, pre_tile=(bt, t), scan_tile=(math.gcd(b, DECODE_SEQS_PER_STEP), t), lm=t, lg=t,
                sub=min(16, t), time_minor=False)


def kernel(x_prompt, x_sample, state_conv, state_mlstm_C, state_mlstm_n, state_mlstm_m, state_gla_S,
           g_ffn1, w_ffn1_up, w_ffn1_down, g_mix, w_in, conv_w, conv_b, w_mq, w_mk, b_if, g_mhead,
           w_a2, b_a, g_ghead, w_pa, w_pb, w_o, g_ffn2, w_ffn2_up, w_ffn2_down, g_final):
    w = _pack_weights(g_ffn1, w_ffn1_up, w_ffn1_down, g_mix, w_in, conv_w, conv_b, w_mq, w_mk, b_if,
                      g_mhead, w_a2, b_a, g_ghead, w_pa, w_pb, w_o, g_ffn2, w_ffn2_up, w_ffn2_down,
                      g_final)
    bp, tp, _ = x_prompt.shape
    bs, ts, _ = x_sample.shape
    out_p = _trunk(x_prompt, None, w, **_tiles(bp, tp))
    state = (state_conv[0], state_mlstm_C[0], state_mlstm_n[0], state_mlstm_m[0], state_gla_S[0])
    out_s = _trunk(x_sample, state, w, **_tiles(bs, ts))
    return (out_p[0], out_s[0]) + out_p[1:] + out_s[1:]
```

```python
import functools
import math
import types

import jax
import jax.numpy as jnp
from jax import lax
from jax.experimental import pallas as pl
from jax.experimental.pallas import tpu as pltpu

D_MODEL = 1024
M_HEADS = 4
M_HEAD_DIM = 256
M_WIDTH = 1024
CONV_W = 4
G_HEADS = 4
G_DK = 128
G_DV = 256
G_KW = 512
G_VW = 1024
G_RANK = 16
G_TAU = 16.0
D_FF = 2816
EPS = 1e-6

LANES = 128
SUBLANES = 8
BF16_ROWS = 16
VMEM_LIMIT = 56 * 1024 * 1024

F32 = jnp.float32
BF16 = jnp.bfloat16
NEG_INF = float("-inf")


def _sigmoid(x):
    return 1.0 / (1.0 + jnp.exp(-x))


def _silu(x):
    return x * _sigmoid(x)


def _log_sigmoid(x):
    return jnp.minimum(x, 0.0) - jnp.log(1.0 + jnp.exp(-jnp.abs(x)))


def _rms(x, g):
    return x * lax.rsqrt(jnp.mean(x * x, axis=-1, keepdims=True) + EPS) * g


def _dot(a, b):
    return jnp.dot(a, b, preferred_element_type=F32)


def _dot_nt(a, b):
    return lax.dot_general(a, b, (((1,), (1,)), ((), ())), preferred_element_type=F32)


def _dot_tn(a, b):
    return lax.dot_general(a, b, (((0,), (0,)), ((), ())), preferred_element_type=F32)


def _const_spec(shape):
    nd = len(shape)
    return pl.BlockSpec(shape, lambda *_: (0,) * nd, pipeline_mode=pl.Buffered(1))


FF_CHUNK = 256


def _ffn_kernel(x_ref, g_ref, wup_ref, wdown_ref, gfin_ref, o_ref, h_ref, *, final_norm):
    x = x_ref[...]
    xn = _rms(x, g_ref[...]).astype(BF16)
    for c in range(D_FF // FF_CHUNK):
        a = _dot(xn, wup_ref[:, c * FF_CHUNK:(c + 1) * FF_CHUNK])
        g = _dot(xn, wup_ref[:, D_FF + c * FF_CHUNK:D_FF + (c + 1) * FF_CHUNK])
        h_ref[:, c * FF_CHUNK:(c + 1) * FF_CHUNK] = (_silu(g) * a).astype(BF16)
    y = x + 0.5 * _dot(h_ref[...], wdown_ref[...])
    if final_norm:
        y = _rms(y, gfin_ref[...])
    o_ref[...] = y


def _ffn(x2d, g, wup, wdown, gfin, *, final_norm, tm):
    n = x2d.shape[0]
    assert n % tm == 0
    return pl.pallas_call(
        functools.partial(_ffn_kernel, final_norm=final_norm),
        grid=(n // tm,),
        in_specs=[
            pl.BlockSpec((tm, D_MODEL), lambda i: (i, 0)),
            _const_spec((1, D_MODEL)),
            _const_spec((D_MODEL, 2 * D_FF)),
            _const_spec((D_FF, D_MODEL)),
            _const_spec((1, D_MODEL)),
        ],
        out_specs=pl.BlockSpec((tm, D_MODEL), lambda i: (i, 0)),
        out_shape=jax.ShapeDtypeStruct((n, D_MODEL), F32),
        scratch_shapes=[pltpu.VMEM((tm, D_FF), BF16)],
        compiler_params=pltpu.CompilerParams(
            dimension_semantics=("parallel",), vmem_limit_bytes=VMEM_LIMIT),
        name="ffn_final" if final_norm else "ffn",
    )(x2d, g, wup, wdown, gfin)


PRE_U, PRE_V, PRE_QG, PRE_KG, PRE_VG = 0, 1024, 2048, 2560, 3072
PRE_WIDTH = 4096
A_LANE0 = 8


def _pre_kernel(x_ref, cst_ref, g_ref, w_ref, ws_ref, cw_ref, cb_ref, wq_ref, wk_ref, bif_ref,
                wa2_ref, ba_ref,
                qm_ref, km_ref, vm_ref, gif_ref, qg_ref, kg_ref, la_ref, vg_ref, cout_ref,
                cbuf_ref, *, bt, tb, lg, time_minor):
    t = pl.program_id(1)
    m = bt * tb
    xn = _rms(x_ref[...].reshape(m, D_MODEL), g_ref[...]).astype(BF16)

    small = _dot(xn, ws_ref[...])

    @pl.when(t == 0)
    def _():
        cbuf_ref[:, 0:SUBLANES, :] = cst_ref[...]

    @pl.when(t > 0)
    def _():
        cbuf_ref[:, 0:SUBLANES, :] = cbuf_ref[:, tb:tb + SUBLANES, :]

    u = _dot(xn, w_ref[:, PRE_U:PRE_U + M_WIDTH])
    cbuf_ref[:, SUBLANES:SUBLANES + tb, :] = u.reshape(bt, tb, M_WIDTH)
    c = cb_ref[...].reshape(1, 1, M_WIDTH)
    for j in range(CONV_W):
        off = SUBLANES - (CONV_W - 1) + j
        c = c + cbuf_ref[:, off:off + tb, :] * cw_ref[j:j + 1, :].reshape(1, 1, M_WIDTH)
    cout_ref[...] = cbuf_ref[:, tb:tb + SUBLANES, :]
    ch = _silu(c).reshape(m, M_WIDTH).astype(BF16)

    vm = _dot(xn, w_ref[:, PRE_V:PRE_V + M_WIDTH])
    gates = small + bif_ref[...]
    lane = lax.broadcasted_iota(jnp.int32, gates.shape, 1)
    gif = jnp.where(lane < M_HEADS, gates, jnp.where(lane < 2 * M_HEADS, _log_sigmoid(gates), 0.0))
    if time_minor:
        b = pltpu.roll(_scan_rows(gif, jnp.add, 0.0), LANES - M_HEADS, axis=1)
        g = gif - b
        gmax = _scan_rows(g, jnp.maximum, NEG_INF)
        packed = jnp.where(lane < M_HEADS, g,
                           jnp.where(lane < 2 * M_HEADS, pltpu.roll(gmax, M_HEADS, axis=1),
                                     jnp.where(lane < 3 * M_HEADS, pltpu.roll(b, 2 * M_HEADS, axis=1), 0.0)))
        gif_ref[0] = jnp.transpose(packed)[0:2 * SUBLANES, :]
    else:
        gif_ref[...] = gif.reshape(bt, tb, LANES)
        vm_ref[...] = vm.astype(BF16).reshape(bt, tb, M_WIDTH)

    def decay_columns(h):
        cols = slice(h * G_DK, (h + 1) * G_DK)
        a2 = _dot(small.astype(BF16), wa2_ref[:, cols]) + ba_ref[:, cols]
        la_ref[:, :, cols] = _scan_rows(_log_sigmoid(a2) * (1.0 / G_TAU), jnp.add, 0.0,
                                        segment=lg).reshape(bt, tb, G_DK)

    qg_ref[...] = (_dot(xn, w_ref[:, PRE_QG:PRE_QG + G_KW]) * (G_DK ** -0.5)).astype(BF16).reshape(
        bt, tb, G_KW)
    decay_columns(0)
    kg_ref[...] = _dot(xn, w_ref[:, PRE_KG:PRE_KG + G_KW]).astype(BF16).reshape(bt, tb, G_KW)
    decay_columns(1)
    half = G_VW // 2
    for j in range(2):
        vg_ref[:, :, j * half:(j + 1) * half] = _dot(
            xn, w_ref[:, PRE_VG + j * half:PRE_VG + (j + 1) * half]).astype(BF16).reshape(bt, tb, half)
        decay_columns(2 + j)

    for h in range(M_HEADS):
        sl = slice(h * M_HEAD_DIM, (h + 1) * M_HEAD_DIM)
        qh = _dot(ch[:, sl], wq_ref[h])
        if time_minor:
            qm_ref[0, sl, :] = jnp.transpose(qh).astype(BF16)
            vm_ref[0, sl, :] = jnp.transpose(vm[:, sl]).astype(BF16)
        else:
            qm_ref[:, :, sl] = qh.astype(BF16).reshape(bt, tb, M_HEAD_DIM)
        km_ref[:, :, sl] = (_dot(ch[:, sl], wk_ref[h]) * (M_HEAD_DIM ** -0.5)).astype(BF16).reshape(
            bt, tb, M_HEAD_DIM)


def _pre(x, cst, g, w_pre, w_small, conv_w, conv_b, wq, wk, bif, wa2, ba, *, bt, tb, lg, time_minor):
    b, t, _ = x.shape
    assert b % bt == 0 and t % tb == 0 and tb % SUBLANES == 0 and tb % lg == 0
    assert not time_minor or (bt == 1 and tb % LANES == 0)
    tok = lambda w: pl.BlockSpec((bt, tb, w), lambda i, j: (i, j, 0))
    per_seq = pl.BlockSpec((bt, SUBLANES, M_WIDTH), lambda i, j: (i, 0, 0))
    sds = lambda w, dt: jax.ShapeDtypeStruct((b, t, w), dt)
    if time_minor:
        tok_t = lambda w: pl.BlockSpec((1, w, tb), lambda i, j: (i, 0, j))
        sds_t = lambda w, dt: jax.ShapeDtypeStruct((b, w, t), dt)
        qv_spec, qv_shape = tok_t(M_WIDTH), sds_t(M_WIDTH, BF16)
        gif_spec, gif_shape = tok_t(2 * SUBLANES), sds_t(2 * SUBLANES, F32)
    else:
        qv_spec, qv_shape = tok(M_WIDTH), sds(M_WIDTH, BF16)
        gif_spec, gif_shape = tok(LANES), sds(LANES, F32)
    return pl.pallas_call(
        functools.partial(_pre_kernel, bt=bt, tb=tb, lg=lg, time_minor=time_minor),
        grid=(b // bt, t // tb),
        in_specs=[
            tok(D_MODEL), per_seq,
            _const_spec((1, D_MODEL)),
            _const_spec((D_MODEL, PRE_WIDTH)),
            _const_spec((D_MODEL, LANES)),
            _const_spec((CONV_W, M_WIDTH)),
            _const_spec((1, M_WIDTH)),
            _const_spec((M_HEADS, M_HEAD_DIM, M_HEAD_DIM)),
            _const_spec((M_HEADS, M_HEAD_DIM, M_HEAD_DIM)),
            _const_spec((1, LANES)),
            _const_spec((LANES, G_KW)),
            _const_spec((1, G_KW)),
        ],
        out_specs=[qv_spec, tok(M_WIDTH), qv_spec, gif_spec, tok(G_KW), tok(G_KW),
                   tok(G_KW), tok(G_VW), per_seq],
        out_shape=[qv_shape, sds(M_WIDTH, BF16), qv_shape, gif_shape,
                   sds(G_KW, BF16), sds(G_KW, BF16), sds(G_KW, F32), sds(G_VW, BF16),
                   jax.ShapeDtypeStruct((b, SUBLANES, M_WIDTH), F32)],
        scratch_shapes=[pltpu.VMEM((bt, tb + SUBLANES, M_WIDTH), F32)],
        compiler_params=pltpu.CompilerParams(
            dimension_semantics=("parallel", "arbitrary"), vmem_limit_bytes=VMEM_LIMIT),
        name="mixer_pre",
    )(x, cst, g, w_pre, w_small, conv_w, conv_b, wq, wk, bif, wa2, ba)


def _scan_rows(x, op, fill, segment=None):
    n = segment or x.shape[0]
    row = lax.broadcasted_iota(jnp.int32, x.shape, 0) % n
    s = 1
    while s < n:
        x = op(x, jnp.where(row >= s, pltpu.roll(x, s, axis=0), fill))
        s *= 2
    return x


def _col_of_row(row):
    return jnp.transpose(jnp.broadcast_to(row, (SUBLANES, LANES)))[:, 0:1]


def _mlstm_chunk_tm(seqs, *, lm):
    reps = lm // LANES
    head_row = lax.broadcasted_iota(jnp.int32, (SUBLANES, LANES), 0) < M_HEADS
    src = lax.broadcasted_iota(jnp.int32, (lm, lm), 0)
    dst = lax.broadcasted_iota(jnp.int32, (lm, lm), 1)
    causal = src <= dst
    sl = [slice(h * M_HEAD_DIM, (h + 1) * M_HEAD_DIM) for h in range(M_HEADS)]
    row = lambda x, h: x[h:h + 1, :]
    gates = []
    for sq in seqs:
        g = sq.gift[0:SUBLANES]
        gmax = pltpu.roll(g, M_HEADS, axis=0)
        b = sq.gift[SUBLANES:2 * SUBLANES]
        m_prev = jnp.concatenate([sq.m_src[...]] * reps, axis=1)
        mx = jnp.maximum(m_prev, gmax)
        mx_last = jnp.broadcast_to(mx[:, lm - 1:lm], (SUBLANES, lm))
        gates.append(dict(
            mx=mx,
            inter=jnp.exp(m_prev - mx),
            nfloor=jnp.exp(-(b + mx)),
            w_in=jnp.exp(g - mx_last),
            decay=jnp.exp(m_prev - mx_last),
            g_col=jnp.transpose(g)))
        m_new = jnp.broadcast_to(b[:, lm - 1:lm], (SUBLANES, lm)) + mx_last
        sq.m_dst[...] = jnp.where(head_row, m_new[:, :LANES], 0.0)
    units = [(sq, gt, h) for sq, gt in zip(seqs, gates) for h in range(M_HEADS)]
    c_old = [sq.c_src[h] for sq, _, h in units]
    n_old = [sq.n_src[h] for sq, _, h in units]
    cn = [jnp.concatenate([c_old[u].astype(BF16),
                           jnp.broadcast_to(n_old[u].astype(BF16), (BF16_ROWS, M_HEAD_DIM))], axis=0)
          for u in range(len(units))]
    cq = [_dot(cn[u], sq.qt[sl[h], :]) for u, (sq, _, h) in enumerate(units)]
    dm = [jnp.exp(jnp.where(causal, gt["g_col"][:, h:h + 1] - row(gt["mx"], h), NEG_INF))
          for _, gt, h in units]
    st = [_dot(sq.k[:, sl[h]], sq.qt[sl[h], :]) * dm[u]
          for u, (sq, _, h) in enumerate(units)]
    num = [row(gt["inter"], h) * cq[u][:M_HEAD_DIM] + _dot(sq.vt[sl[h], :], st[u].astype(BF16))
           for u, (sq, gt, h) in enumerate(units)]
    for u, (sq, gt, h) in enumerate(units):
        den = (row(gt["inter"], h) * cq[u][M_HEAD_DIM:M_HEAD_DIM + 1]
               + jnp.sum(st[u], axis=0, keepdims=True))
        rcp = 1.0 / jnp.maximum(jnp.abs(den), row(gt["nfloor"], h))
        ms = jnp.mean(num[u] * num[u], axis=0, keepdims=True)
        sq.hm_store(h, num[u] * (rcp * lax.rsqrt(rcp * rcp * ms + EPS)))
    for u, (sq, gt, h) in enumerate(units):
        w_h = row(gt["w_in"], h)
        dec_h = jnp.concatenate([row(gt["decay"], h)[:, :LANES]] * (M_HEAD_DIM // LANES), axis=1)
        wv = jnp.concatenate([(sq.vt[sl[h], :].astype(F32) * w_h).astype(BF16),
                              jnp.broadcast_to(w_h.astype(BF16), (BF16_ROWS, lm))], axis=0)
        upd = _dot(wv, sq.k[:, sl[h]])
        sq.c_dst[h] = dec_h * c_old[u] + upd[:M_HEAD_DIM]
        sq.n_dst[h] = dec_h * n_old[u] + upd[M_HEAD_DIM:M_HEAD_DIM + 1]


def _mlstm_chunk(seqs, *, lm):
    row = lax.broadcasted_iota(jnp.int32, (lm, lm), 0)
    col = lax.broadcasted_iota(jnp.int32, (lm, lm), 1)
    causal = col <= row
    sl = [slice(h * M_HEAD_DIM, (h + 1) * M_HEAD_DIM) for h in range(M_HEADS)]
    gates = []
    for sq in seqs:
        cs = _scan_rows(sq.gif, jnp.add, 0.0)
        b = pltpu.roll(cs, LANES - M_HEADS, axis=1)
        g = sq.gif - b
        m_prev = sq.m_src[...]
        mx = jnp.maximum(m_prev, _scan_rows(g, jnp.maximum, NEG_INF))
        mx_last = mx[lm - 1:lm, :]
        gates.append(dict(
            mx=mx,
            inter=jnp.exp(m_prev - mx),
            nfloor=jnp.exp(-(b + mx)),
            g_t=jnp.transpose(g),
            w_in=jnp.exp(g - mx_last),
            decay=jnp.exp(m_prev - mx_last)))
        sq.m_dst[...] = b[lm - 1:lm, :] + mx_last
    units = [(sq, gt, h) for sq, gt in zip(seqs, gates) for h in range(M_HEADS)]
    col_of = lambda x, h: x[:, h:h + 1]
    c_old = [sq.c_src[h] for sq, _, h in units]
    n_old = [sq.n_src[h] for sq, _, h in units]
    dm = [jnp.exp(jnp.where(causal, gt["g_t"][h:h + 1, :] - col_of(gt["mx"], h), NEG_INF))
          for _, gt, h in units]
    qn = [jnp.sum(sq.q[:, sl[h]].astype(F32) * n_old[u], axis=-1, keepdims=True)
          for u, (sq, _, h) in enumerate(units)]
    s = [_dot_nt(sq.q[:, sl[h]], sq.k[:, sl[h]]) * dm[u] for u, (sq, _, h) in enumerate(units)]
    num = [col_of(gt["inter"], h) * _dot_nt(sq.q[:, sl[h]], c_old[u].astype(BF16))
           + _dot(s[u].astype(BF16), sq.v[:, sl[h]]) for u, (sq, gt, h) in enumerate(units)]
    for u, (sq, gt, h) in enumerate(units):
        den = col_of(gt["inter"], h) * qn[u] + jnp.sum(s[u], axis=-1, keepdims=True)
        rcp = 1.0 / jnp.maximum(jnp.abs(den), col_of(gt["nfloor"], h))
        ms = jnp.mean(num[u] * num[u], axis=-1, keepdims=True)
        sq.hm_store(h, num[u] * (rcp * lax.rsqrt(rcp * rcp * ms + EPS)))
    for u, (sq, gt, h) in enumerate(units):
        w_h = col_of(gt["w_in"], h)
        dec_h = col_of(gt["decay"], h)
        wv = (w_h * sq.v[:, sl[h]].astype(F32)).astype(BF16)
        sq.c_dst[h] = dec_h * c_old[u] + _dot_tn(wv, sq.k[:, sl[h]])
        sq.n_dst[h] = dec_h * n_old[u] + jnp.sum(w_h * sq.k[:, sl[h]].astype(F32), axis=0, keepdims=True)


GLA_SAFE_SPAN = 40.0


def _gla_scores_guarded(qh, kh, a, *, lg, sub):
    row = lax.broadcasted_iota(jnp.int32, (lg, lg), 0)
    col = lax.broadcasted_iota(jnp.int32, (lg, lg), 1)
    att = jnp.zeros((lg, lg), F32)
    c = sub
    while c < lg:
        nblk = lg // (2 * c)
        a_ref = jnp.concatenate(
            [jnp.broadcast_to(a[2 * c * i + c - 1:2 * c * i + c, :], (2 * c, G_DK))
             for i in range(nblk)], axis=0)
        qt = (qh * jnp.exp(jnp.minimum(a - a_ref, 0.0))).astype(BF16)
        kt = (kh * jnp.exp(jnp.minimum(a_ref - a, 0.0))).astype(BF16)
        valid = (row // (2 * c) == col // (2 * c)) & (row % (2 * c) >= c) & (col % (2 * c) < c)
        att = att + jnp.where(valid, _dot_nt(qt, kt), 0.0)
        c *= 2
    rows = lax.broadcasted_iota(jnp.int32, (sub, lg), 0)
    cols = lax.broadcasted_iota(jnp.int32, (sub, lg), 1)
    blocks = []
    for blk in range(lg // sub):
        r0 = blk * sub
        a_b = a[r0:r0 + sub, :]
        q_b = qh[r0:r0 + sub, :]
        acc = jnp.zeros((sub, lg), F32)
        for s in range(sub):
            e = jnp.exp(jnp.minimum(a_b - a[r0 + s:r0 + s + 1, :], 0.0))
            p = jnp.sum(q_b * e * kh[r0 + s:r0 + s + 1, :], axis=-1, keepdims=True)
            acc = jnp.where((cols == r0 + s) & (rows >= s), p, acc)
        blocks.append(acc)
    return att + jnp.concatenate(blocks, axis=0)


def _gla_chunk_guarded(seqs, gh_ref, *, lg, sub):
    ks = [slice(h * G_DK, (h + 1) * G_DK) for h in range(G_HEADS)]
    vs = [slice(h * G_DV, (h + 1) * G_DV) for h in range(G_HEADS)]
    pre = []
    for sq in seqs:
        a_last = sq.a[lg - 1:lg, :]
        qf = sq.qg.astype(F32)
        kf = sq.kg.astype(F32)
        pre.append(dict(a=sq.a, a_last=a_last, qf=qf, kf=kf,
                        qs=(qf * jnp.exp(sq.a)).astype(BF16),
                        kl=(kf * jnp.exp(a_last - sq.a)).astype(BF16)))
    units = [(sq, p, h) for sq, p in zip(seqs, pre) for h in range(G_HEADS)]
    s_old = [sq.s_src[h] for sq, _, h in units]
    att = [_gla_scores_guarded(p["qf"][:, ks[h]], p["kf"][:, ks[h]], p["a"][:, ks[h]], lg=lg, sub=sub)
           for _, p, h in units]
    o = [_dot(p["qs"][:, ks[h]], s_old[u].astype(BF16)) + _dot(att[u].astype(BF16), sq.vg[:, vs[h]])
         for u, (sq, p, h) in enumerate(units)]
    for u, (sq, _, h) in enumerate(units):
        sq.og_store(h, _rms(o[u], gh_ref[:, vs[h]]))
    for u, (sq, p, h) in enumerate(units):
        sq.s_dst[h] = (jnp.exp(_col_of_row(p["a_last"][:, ks[h]])) * s_old[u]
                       + _dot_tn(p["kl"][:, ks[h]], sq.vg[:, vs[h]]))


def _gla_block_direct(seqs, gh_ref, *, tb, lg):
    ks = [slice(h * G_DK, (h + 1) * G_DK) for h in range(G_HEADS)]
    vs = [slice(h * G_DV, (h + 1) * G_DV) for h in range(G_HEADS)]
    chunks = [slice(c * lg, (c + 1) * lg) for c in range(tb // lg)]
    row = lax.broadcasted_iota(jnp.int32, (lg, lg), 0)
    col = lax.broadcasted_iota(jnp.int32, (lg, lg), 1)
    causal = col <= row
    pre = []
    for sq in seqs:
        a_last = [sq.a[r.stop - 1:r.stop, :] for r in chunks]
        a_last_rows = jnp.concatenate([jnp.broadcast_to(al, (lg, G_KW)) for al in a_last], axis=0)
        qf = sq.qg.astype(F32)
        kf = sq.kg.astype(F32)
        pre.append(dict(a_last=a_last,
                        qs=(qf * jnp.exp(sq.a)).astype(BF16),
                        kt=(kf * jnp.exp(-sq.a)).astype(BF16),
                        kl=(kf * jnp.exp(a_last_rows - sq.a)).astype(BF16)))
    units = [(sq, p, h) for sq, p in zip(seqs, pre) for h in range(G_HEADS)]
    att = [[jnp.where(causal, _dot_nt(p["qs"][r, ks[h]], p["kt"][r, ks[h]]), 0.0).astype(BF16)
            for r in chunks] for _, p, h in units]
    s_cur = [sq.s_src[h] for sq, _, h in units]
    for c, r in enumerate(chunks):
        o = [_dot(p["qs"][r, ks[h]], s_cur[u].astype(BF16)) + _dot(att[u][c], sq.vg[r, vs[h]])
             for u, (sq, p, h) in enumerate(units)]
        for u, (sq, _, h) in enumerate(units):
            sq.og_rows_store(r, h, _rms(o[u], gh_ref[:, vs[h]]))
        s_cur = [jnp.exp(_col_of_row(p["a_last"][c][:, ks[h]])) * s_cur[u]
                 + _dot_tn(p["kl"][r, ks[h]], sq.vg[r, vs[h]])
                 for u, (sq, p, h) in enumerate(units)]
    for u, (sq, _, h) in enumerate(units):
        sq.s_dst[h] = s_cur[u]


def _scan_kernel(*refs, bt, tb, lm, lg, sub, has_state, time_minor, single_step):
    n_in = 9 + (4 if has_state else 0)
    qm_ref, km_ref, vm_ref, gif_ref, qg_ref, kg_ref, la_ref, vg_ref, ggh_ref = refs[:9]
    state_in = refs[9:n_in]
    hm_ref, og_ref = refs[n_in:n_in + 2]
    state_out = refs[n_in + 2:n_in + 6]
    scratch = refs[n_in + 6:]
    t = pl.program_id(1)
    nt = pl.num_programs(1)

    streamed = has_state and single_step
    src = state_in if streamed else scratch
    dst = state_out if streamed else scratch
    if not streamed:
        @pl.when(t == 0)
        def _():
            for i, ref in enumerate(scratch):
                ref[...] = state_in[i][...] if has_state else jnp.zeros(ref.shape, F32)

    def seq_records(bi, m_rows, g_rows):
        def hm_store(h, val):
            if time_minor:
                hm_ref[bi, h * M_HEAD_DIM:(h + 1) * M_HEAD_DIM, m_rows] = val
            else:
                hm_ref[bi, m_rows, h * M_HEAD_DIM:(h + 1) * M_HEAD_DIM] = val

        def og_rows_store(rows, h, val):
            og_ref[bi, rows, h * G_DV:(h + 1) * G_DV] = val

        return types.SimpleNamespace(
            bi=bi, hm_store=hm_store, og_rows_store=og_rows_store,
            og_store=functools.partial(og_rows_store, g_rows),
            c_src=src[0].at[bi], n_src=src[1].at[bi], m_src=src[2].at[bi], s_src=src[3].at[bi],
            c_dst=dst[0].at[bi], n_dst=dst[1].at[bi], m_dst=dst[2].at[bi], s_dst=dst[3].at[bi])

    def m_body(ci, carry):
        rows = pl.ds(pl.multiple_of(ci * lm, lm), lm)
        seqs = [seq_records(bi, rows, None) for bi in range(bt)]
        if time_minor:
            for sq in seqs:
                sq.qt, sq.k, sq.vt = qm_ref[sq.bi, :, rows], km_ref[sq.bi, rows, :], vm_ref[sq.bi, :, rows]
                sq.gift = gif_ref[sq.bi, :, rows]
            _mlstm_chunk_tm(seqs, lm=lm)
        else:
            for sq in seqs:
                sq.q, sq.k, sq.v = qm_ref[sq.bi, rows, :], km_ref[sq.bi, rows, :], vm_ref[sq.bi, rows, :]
                sq.gif = gif_ref[sq.bi, rows, :]
            _mlstm_chunk(seqs, lm=lm)
        return carry

    def gla_records(rows):
        seqs = [seq_records(bi, None, rows) for bi in range(bt)]
        for sq in seqs:
            sq.qg, sq.kg, sq.vg = qg_ref[sq.bi, rows, :], kg_ref[sq.bi, rows, :], vg_ref[sq.bi, rows, :]
            sq.a = la_ref[sq.bi, rows, :]
        return seqs

    def g_body(ci, carry):
        rows = pl.ds(pl.multiple_of(ci * lg, lg), lg)
        _gla_chunk_guarded(gla_records(rows), ggh_ref, lg=lg, sub=sub)
        return carry

    def loop(body, n):
        if n == 1:
            body(0, 0)
        else:
            lax.fori_loop(0, n, body, 0)

    loop(m_body, tb // lm)
    if lg <= sub:
        loop(g_body, tb // lg)
    else:
        chunk_decay = [la_ref[:, c * lg + lg - 1:(c + 1) * lg, :] for c in range(tb // lg)]
        span_ok = jnp.min(functools.reduce(jnp.minimum, chunk_decay)) >= -GLA_SAFE_SPAN

        @pl.when(span_ok)
        def _():
            _gla_block_direct(gla_records(slice(0, tb)), ggh_ref, tb=tb, lg=lg)

        @pl.when(jnp.logical_not(span_ok))
        def _():
            loop(g_body, tb // lg)

    if not streamed:
        @pl.when(t == nt - 1)
        def _():
            for out, ref in zip(state_out, scratch):
                out[...] = ref[...]


def _scan(qm, km, vm, gif, qg, kg, la, vg, ggh, state, *, bt, tb, lm, lg, sub, time_minor):
    b, t, _ = km.shape
    assert b % bt == 0 and t % tb == 0 and tb % lm == 0 and tb % lg == 0 and lg % sub == 0
    has_state = state is not None
    assert not (time_minor and has_state) and (not time_minor or lm % LANES == 0)
    tok = lambda w: pl.BlockSpec((bt, tb, w), lambda i, j: (i, j, 0))
    tok_t = lambda w: pl.BlockSpec((bt, w, tb), lambda i, j: (i, 0, j))
    c_shape = (M_HEADS, M_HEAD_DIM, M_HEAD_DIM)
    n_shape = (M_HEADS, 1, M_HEAD_DIM)
    m_shape = (SUBLANES, LANES) if time_minor else (1, LANES)
    s_shape = (G_HEADS, G_DK, G_DV)
    st_shapes = [(bt,) + shp for shp in (c_shape, n_shape, m_shape, s_shape)]
    st_specs = [pl.BlockSpec(shp, lambda i, j, nd=len(shp): (i,) + (0,) * (nd - 1)) for shp in st_shapes]
    single_step = t == tb
    carried = not (has_state and single_step)
    if time_minor:
        qv_spec, gif_spec = tok_t(M_WIDTH), tok_t(2 * SUBLANES)
        hm_spec, hm_shape = tok_t(M_WIDTH), (b, M_WIDTH, t)
    else:
        qv_spec, gif_spec = tok(M_WIDTH), tok(LANES)
        hm_spec, hm_shape = tok(M_WIDTH), (b, t, M_WIDTH)
    in_specs = [qv_spec, tok(M_WIDTH), qv_spec, gif_spec, tok(G_KW), tok(G_KW),
                tok(G_KW), tok(G_VW), _const_spec((1, G_VW))]
    args = [qm, km, vm, gif, qg, kg, la, vg, ggh]
    if has_state:
        in_specs += st_specs
        args += list(state)
    return pl.pallas_call(
        functools.partial(_scan_kernel, bt=bt, tb=tb, lm=lm, lg=lg, sub=sub, has_state=has_state,
                          time_minor=time_minor, single_step=single_step),
        grid=(b // bt, t // tb),
        in_specs=in_specs,
        out_specs=[hm_spec, tok(G_VW)] + st_specs,
        out_shape=[jax.ShapeDtypeStruct(hm_shape, F32),
                   jax.ShapeDtypeStruct((b, t, G_VW), F32)]
                  + [jax.ShapeDtypeStruct((b,) + shp[1:], F32) for shp in st_shapes],
        scratch_shapes=[pltpu.VMEM(shp, F32) for shp in st_shapes] if carried else [],
        compiler_params=pltpu.CompilerParams(
            dimension_semantics=("parallel", "arbitrary"), vmem_limit_bytes=VMEM_LIMIT),
        name="mixer_scan_state" if has_state else "mixer_scan",
    )(*args)


POST_O, POST_R, POST_A, POST_B = 0, 1024, 2048, 3072


def _post_kernel(x_ref, hm_ref, og_ref, g_ref, gmh_ref, w_ref, wpa_ref, wpb_ref, wo_ref, o_ref, *,
                 time_minor):
    x = x_ref[...]
    xn = _rms(x, g_ref[...]).astype(BF16)
    gate = lambda off: _dot(xn, w_ref[:, off:off + D_MODEL])
    if time_minor:
        hm = jnp.concatenate(
            [jnp.transpose(hm_ref[0, h * M_HEAD_DIM:(h + 1) * M_HEAD_DIM, :]) for h in range(M_HEADS)],
            axis=1)
    else:
        hm = hm_ref[...]
    hm = (_sigmoid(gate(POST_O)) * (hm * gmh_ref[...])).astype(BF16)
    og = (_silu(gate(POST_R)) * og_ref[...]).astype(BF16)
    y = (_sigmoid(gate(POST_A)) * _dot(hm, wpa_ref[...])
         + _sigmoid(gate(POST_B)) * _dot(og, wpb_ref[...]))
    o_ref[...] = x + _dot(y.astype(BF16), wo_ref[...])


def _post(x2d, hm, og2d, g, gmh, w_post, wpa, wpb, wo, *, tm, time_minor):
    n = x2d.shape[0]
    assert n % tm == 0
    tok = pl.BlockSpec((tm, D_MODEL), lambda i: (i, 0))
    if time_minor:
        per_seq = hm.shape[2] // tm
        assert hm.shape[2] % tm == 0 and tm % LANES == 0
        hm_spec = pl.BlockSpec((1, M_WIDTH, tm), lambda i: (i // per_seq, 0, i % per_seq))
    else:
        hm_spec = tok
    return pl.pallas_call(
        functools.partial(_post_kernel, time_minor=time_minor),
        grid=(n // tm,),
        in_specs=[tok, hm_spec, tok, _const_spec((1, D_MODEL)), _const_spec((1, M_WIDTH)),
                  _const_spec((D_MODEL, 4 * D_MODEL)),
                  _const_spec((M_WIDTH, D_MODEL)), _const_spec((G_VW, D_MODEL)),
                  _const_spec((D_MODEL, D_MODEL))],
        out_specs=tok,
        out_shape=jax.ShapeDtypeStruct((n, D_MODEL), F32),
        compiler_params=pltpu.CompilerParams(
            dimension_semantics=("parallel",), vmem_limit_bytes=VMEM_LIMIT),
        name="mixer_post",
    )(x2d, hm, og2d, g, gmh, w_post, wpa, wpb, wo)


def _pack_weights(g_ffn1, w_ffn1_up, w_ffn1_down, g_mix, w_in, conv_w, conv_b, w_mq, w_mk, b_if,
                  g_mhead, w_a2, b_a, g_ghead, w_pa, w_pb, w_o, g_ffn2, w_ffn2_up, w_ffn2_down,
                  g_final):
    l = 0
    row = lambda v: v.reshape(1, -1).astype(F32)
    edges = [0]
    for wd in (M_WIDTH, M_WIDTH, M_WIDTH, 2 * M_HEADS, G_KW, G_KW, G_VW, G_VW, G_RANK, D_MODEL, D_MODEL):
        edges.append(edges[-1] + wd)
    seg = lambda i: w_in[l][:, edges[i]:edges[i + 1]]
    u_m, v_m, o_m, if_m, q_g, k_g, v_g, r_g, a_g, g_a, g_b = (seg(i) for i in range(11))
    w_pre = jnp.concatenate([u_m, v_m, q_g, k_g, v_g], axis=1).astype(BF16)
    w_small = jnp.concatenate(
        [if_m, a_g, jnp.zeros((D_MODEL, LANES - 2 * M_HEADS - G_RANK), F32)], axis=1).astype(BF16)
    w_post = jnp.concatenate([o_m, r_g, g_a, g_b], axis=1).astype(BF16)
    bif = jnp.concatenate([b_if[l].reshape(-1), jnp.zeros((LANES - 2 * M_HEADS,), F32)]).reshape(1, LANES)
    wa2 = jnp.zeros((LANES, G_KW), F32).at[A_LANE0:A_LANE0 + G_RANK].set(w_a2[l]).astype(BF16)
    return dict(
        g_ffn1=row(g_ffn1[l]), w_ffn1_up=w_ffn1_up[l].astype(BF16), w_ffn1_down=w_ffn1_down[l].astype(BF16),
        g_mix=row(g_mix[l]), w_pre=w_pre, w_small=w_small, w_post=w_post,
        conv_w=conv_w[l].astype(F32), conv_b=row(conv_b[l]),
        w_mq=w_mq[l].astype(BF16), w_mk=w_mk[l].astype(BF16), bif=bif,
        g_mhead=row(g_mhead[l]), wa2=wa2, b_a=row(b_a[l]), g_ghead=row(g_ghead[l]),
        w_pa=w_pa[l].astype(BF16), w_pb=w_pb[l].astype(BF16), w_o=w_o[l].astype(BF16),
        g_ffn2=row(g_ffn2[l]), w_ffn2_up=w_ffn2_up[l].astype(BF16), w_ffn2_down=w_ffn2_down[l].astype(BF16),
        g_final=row(g_final))


def _trunk(x, state, w, *, tm, pre_tile, scan_tile, lm, lg, sub, time_minor):
    b, t, _ = x.shape
    n = b * t
    x2d = x.reshape(n, D_MODEL)
    tm_ffn = math.gcd(n, FFN_ROWS)
    x1 = _ffn(x2d, w["g_ffn1"], w["w_ffn1_up"], w["w_ffn1_down"], w["g_final"], final_norm=False, tm=tm_ffn)
    if state is None:
        cst = jnp.zeros((b, SUBLANES, M_WIDTH), F32)
        scan_state = None
    else:
        conv0, c0, n0, m0, s0 = state
        cst = jnp.concatenate(
            [jnp.zeros((b, SUBLANES - (CONV_W - 1), M_WIDTH), F32), conv0.astype(F32)], axis=1)
        m0p = jnp.concatenate([m0, jnp.zeros((b, LANES - M_HEADS), F32)], axis=1).reshape(b, 1, LANES)
        scan_state = (c0, n0.reshape(b, M_HEADS, 1, M_HEAD_DIM), m0p, s0)
    qm, km, vm, gif, qg, kg, la, vg, cout = _pre(
        x1.reshape(b, t, D_MODEL), cst, w["g_mix"], w["w_pre"], w["w_small"], w["conv_w"], w["conv_b"],
        w["w_mq"], w["w_mk"], w["bif"], w["wa2"], w["b_a"], bt=pre_tile[0], tb=pre_tile[1], lg=lg,
        time_minor=time_minor)
    hm, og, c_new, n_new, m_new, s_new = _scan(
        qm, km, vm, gif, qg, kg, la, vg, w["g_ghead"], scan_state,
        bt=scan_tile[0], tb=scan_tile[1], lm=lm, lg=lg, sub=sub, time_minor=time_minor)
    x2 = _post(x1, hm if time_minor else hm.reshape(n, M_WIDTH), og.reshape(n, G_VW), w["g_mix"],
               w["g_mhead"], w["w_post"], w["w_pa"], w["w_pb"], w["w_o"], tm=tm, time_minor=time_minor)
    y = _ffn(x2, w["g_ffn2"], w["w_ffn2_up"], w["w_ffn2_down"], w["g_final"], final_norm=True, tm=tm_ffn)
    m_heads = m_new[:, :M_HEADS, 0] if time_minor else m_new[:, 0, :M_HEADS]
    return (y.reshape(b, t, D_MODEL),
            cout[:, SUBLANES - (CONV_W - 1):, :][None],
            c_new[None],
            n_new.reshape(b, M_HEADS, M_HEAD_DIM)[None],
            m_heads[None],
            s_new[None])


FFN_ROWS = 1024
DECODE_SEQS_PER_STEP = 4
PREFILL_SEQS_PER_STEP = 2


def _tiles(b, t):
    n = b * t
    tm = math.gcd(n, 512)
    if t % 256 == 0:
        return dict(tm=math.gcd(t, 512), pre_tile=(1, 256), scan_tile=(math.gcd(b, PREFILL_SEQS_PER_STEP), 256),
                    lm=256, lg=128, sub=16, time_minor=True)
    assert t <= 64 and t % SUBLANES == 0
    bt = max(1, min(b, 512 // t))
    while b % bt:
        bt -= 1
    return dict(tm=tm, pre_tile=(bt, t), scan_tile=(math.gcd(b, DECODE_SEQS_PER_STEP), t), lm=t, lg=t,
                sub=min(16, t), time_minor=False)


def kernel(x_prompt, x_sample, state_conv, state_mlstm_C, state_mlstm_n, state_mlstm_m, state_gla_S,
           g_ffn1, w_ffn1_up, w_ffn1_down, g_mix, w_in, conv_w, conv_b, w_mq, w_mk, b_if, g_mhead,
           w_a2, b_a, g_ghead, w_pa, w_pb, w_o, g_ffn2, w_ffn2_up, w_ffn2_down, g_final):
    w = _pack_weights(g_ffn1, w_ffn1_up, w_ffn1_down, g_mix, w_in, conv_w, conv_b, w_mq, w_mk, b_if,
                      g_mhead, w_a2, b_a, g_ghead, w_pa, w_pb, w_o, g_ffn2, w_ffn2_up, w_ffn2_down,
                      g_final)
    bp, tp, _ = x_prompt.shape
    bs, ts, _ = x_sample.shape
    out_p = _trunk(x_prompt, None, w, **_tiles(bp, tp))
    state = (state_conv[0], state_mlstm_C[0], state_mlstm_n[0], state_mlstm_m[0], state_gla_S[0])
    out_s = _trunk(x_sample, state, w, **_tiles(bs, ts))
    return (out_p[0], out_s[0]) + out_p[1:] + out_s[1:]
```

```python
import functools
import math
import types

import jax
import jax.numpy as jnp
from jax import lax
from jax.experimental import pallas as pl
from jax.experimental.pallas import tpu as pltpu

D_MODEL = 1024
M_HEADS = 4
M_HEAD_DIM = 256
M_WIDTH = 1024
CONV_W = 4
G_HEADS = 4
G_DK = 128
G_DV = 256
G_KW = 512
G_VW = 1024
G_RANK = 16
G_TAU = 16.0
D_FF = 2816
EPS = 1e-6

LANES = 128
SUBLANES = 8
BF16_ROWS = 16
VMEM_LIMIT = 56 * 1024 * 1024

F32 = jnp.float32
BF16 = jnp.bfloat16
NEG_INF = float("-inf")


def _sigmoid(x):
    return 1.0 / (1.0 + jnp.exp(-x))


def _silu(x):
    return x * _sigmoid(x)


def _log_sigmoid(x):
    return jnp.minimum(x, 0.0) - jnp.log(1.0 + jnp.exp(-jnp.abs(x)))


def _rms(x, g):
    return x * lax.rsqrt(jnp.mean(x * x, axis=-1, keepdims=True) + EPS) * g


def _dot(a, b):
    return jnp.dot(a, b, preferred_element_type=F32)


def _dot_nt(a, b):
    return lax.dot_general(a, b, (((1,), (1,)), ((), ())), preferred_element_type=F32)


def _dot_tn(a, b):
    return lax.dot_general(a, b, (((0,), (0,)), ((), ())), preferred_element_type=F32)


def _const_spec(shape):
    nd = len(shape)
    return pl.BlockSpec(shape, lambda *_: (0,) * nd, pipeline_mode=pl.Buffered(1))


FF_CHUNK = 256


def _ffn_kernel(x_ref, g_ref, wup_ref, wdown_ref, gfin_ref, o_ref, h_ref, *, final_norm):
    x = x_ref[...]
    xn = _rms(x, g_ref[...]).astype(BF16)
    for c in range(D_FF // FF_CHUNK):
        a = _dot(xn, wup_ref[:, c * FF_CHUNK:(c + 1) * FF_CHUNK])
        g = _dot(xn, wup_ref[:, D_FF + c * FF_CHUNK:D_FF + (c + 1) * FF_CHUNK])
        h_ref[:, c * FF_CHUNK:(c + 1) * FF_CHUNK] = (_silu(g) * a).astype(BF16)
    y = x + 0.5 * _dot(h_ref[...], wdown_ref[...])
    if final_norm:
        y = _rms(y, gfin_ref[...])
    o_ref[...] = y


def _ffn(x2d, g, wup, wdown, gfin, *, final_norm, tm):
    n = x2d.shape[0]
    assert n % tm == 0
    return pl.pallas_call(
        functools.partial(_ffn_kernel, final_norm=final_norm),
        grid=(n // tm,),
        in_specs=[
            pl.BlockSpec((tm, D_MODEL), lambda i: (i, 0)),
            _const_spec((1, D_MODEL)),
            _const_spec((D_MODEL, 2 * D_FF)),
            _const_spec((D_FF, D_MODEL)),
            _const_spec((1, D_MODEL)),
        ],
        out_specs=pl.BlockSpec((tm, D_MODEL), lambda i: (i, 0)),
        out_shape=jax.ShapeDtypeStruct((n, D_MODEL), F32),
        scratch_shapes=[pltpu.VMEM((tm, D_FF), BF16)],
        compiler_params=pltpu.CompilerParams(
            dimension_semantics=("parallel",), vmem_limit_bytes=VMEM_LIMIT),
        name="ffn_final" if final_norm else "ffn",
    )(x2d, g, wup, wdown, gfin)


PRE_U, PRE_V, PRE_QG, PRE_KG, PRE_VG = 0, 1024, 2048, 2560, 3072
PRE_WIDTH = 4096
A_LANE0 = 8


def _pre_kernel(x_ref, cst_ref, g_ref, w_ref, ws_ref, cw_ref, cb_ref, wq_ref, wk_ref, bif_ref,
                wa2_ref, ba_ref,
                qm_ref, km_ref, vm_ref, gif_ref, qg_ref, kg_ref, la_ref, vg_ref, cout_ref,
                cbuf_ref, *, bt, tb, lg, time_minor):
    t = pl.program_id(1)
    m = bt * tb
    xn = _rms(x_ref[...].reshape(m, D_MODEL), g_ref[...]).astype(BF16)

    small = _dot(xn, ws_ref[...])

    @pl.when(t == 0)
    def _():
        cbuf_ref[:, 0:SUBLANES, :] = cst_ref[...]

    @pl.when(t > 0)
    def _():
        cbuf_ref[:, 0:SUBLANES, :] = cbuf_ref[:, tb:tb + SUBLANES, :]

    u = _dot(xn, w_ref[:, PRE_U:PRE_U + M_WIDTH])
    cbuf_ref[:, SUBLANES:SUBLANES + tb, :] = u.reshape(bt, tb, M_WIDTH)
    c = cb_ref[...].reshape(1, 1, M_WIDTH)
    for j in range(CONV_W):
        off = SUBLANES - (CONV_W - 1) + j
        c = c + cbuf_ref[:, off:off + tb, :] * cw_ref[j:j + 1, :].reshape(1, 1, M_WIDTH)
    cout_ref[...] = cbuf_ref[:, tb:tb + SUBLANES, :]
    ch = _silu(c).reshape(m, M_WIDTH).astype(BF16)

    vm = _dot(xn, w_ref[:, PRE_V:PRE_V + M_WIDTH])
    gates = small + bif_ref[...]
    lane = lax.broadcasted_iota(jnp.int32, gates.shape, 1)
    gif = jnp.where(lane < M_HEADS, gates, jnp.where(lane < 2 * M_HEADS, _log_sigmoid(gates), 0.0))
    if time_minor:
        b = pltpu.roll(_scan_rows(gif, jnp.add, 0.0), LANES - M_HEADS, axis=1)
        g = gif - b
        gmax = _scan_rows(g, jnp.maximum, NEG_INF)
        packed = jnp.where(lane < M_HEADS, g,
                           jnp.where(lane < 2 * M_HEADS, pltpu.roll(gmax, M_HEADS, axis=1),
                                     jnp.where(lane < 3 * M_HEADS, pltpu.roll(b, 2 * M_HEADS, axis=1), 0.0)))
        gif_ref[0, 0] = jnp.transpose(packed)[0:2 * SUBLANES, :]
    else:
        gif_ref[...] = gif.reshape(bt, tb, LANES)
        vm_ref[...] = vm.astype(BF16).reshape(bt, tb, M_WIDTH)

    def decay_columns(h):
        cols = slice(h * G_DK, (h + 1) * G_DK)
        a2 = _dot(small.astype(BF16), wa2_ref[:, cols]) + ba_ref[:, cols]
        la_ref[:, :, cols] = _scan_rows(_log_sigmoid(a2) * (1.0 / G_TAU), jnp.add, 0.0,
                                        segment=lg).reshape(bt, tb, G_DK)

    qg_ref[...] = (_dot(xn, w_ref[:, PRE_QG:PRE_QG + G_KW]) * (G_DK ** -0.5)).astype(BF16).reshape(
        bt, tb, G_KW)
    decay_columns(0)
    kg_ref[...] = _dot(xn, w_ref[:, PRE_KG:PRE_KG + G_KW]).astype(BF16).reshape(bt, tb, G_KW)
    decay_columns(1)
    half = G_VW // 2
    for j in range(2):
        vg_ref[:, :, j * half:(j + 1) * half] = _dot(
            xn, w_ref[:, PRE_VG + j * half:PRE_VG + (j + 1) * half]).astype(BF16).reshape(bt, tb, half)
        decay_columns(2 + j)

    for h in range(M_HEADS):
        sl = slice(h * M_HEAD_DIM, (h + 1) * M_HEAD_DIM)
        qh = _dot(ch[:, sl], wq_ref[h])
        if time_minor:
            qm_ref[0, 0, sl, :] = jnp.transpose(qh).astype(BF16)
            vm_ref[0, 0, sl, :] = jnp.transpose(vm[:, sl]).astype(BF16)
        else:
            qm_ref[:, :, sl] = qh.astype(BF16).reshape(bt, tb, M_HEAD_DIM)
        km_ref[:, :, sl] = (_dot(ch[:, sl], wk_ref[h]) * (M_HEAD_DIM ** -0.5)).astype(BF16).reshape(
            bt, tb, M_HEAD_DIM)


def _pre(x, cst, g, w_pre, w_small, conv_w, conv_b, wq, wk, bif, wa2, ba, *, bt, tb, lg, time_minor):
    b, t, _ = x.shape
    assert b % bt == 0 and t % tb == 0 and tb % SUBLANES == 0 and tb % lg == 0
    assert not time_minor or (bt == 1 and tb % LANES == 0)
    tok = lambda w: pl.BlockSpec((bt, tb, w), lambda i, j: (i, j, 0))
    per_seq = pl.BlockSpec((bt, SUBLANES, M_WIDTH), lambda i, j: (i, 0, 0))
    sds = lambda w, dt: jax.ShapeDtypeStruct((b, t, w), dt)
    if time_minor:
        tok_t = lambda w: pl.BlockSpec((1, 1, w, tb), lambda i, j: (i, j, 0, 0))
        sds_t = lambda w, dt: jax.ShapeDtypeStruct((b, t // tb, w, tb), dt)
        qv_spec, qv_shape = tok_t(M_WIDTH), sds_t(M_WIDTH, BF16)
        gif_spec, gif_shape = tok_t(2 * SUBLANES), sds_t(2 * SUBLANES, F32)
    else:
        qv_spec, qv_shape = tok(M_WIDTH), sds(M_WIDTH, BF16)
        gif_spec, gif_shape = tok(LANES), sds(LANES, F32)
    return pl.pallas_call(
        functools.partial(_pre_kernel, bt=bt, tb=tb, lg=lg, time_minor=time_minor),
        grid=(b // bt, t // tb),
        in_specs=[
            tok(D_MODEL), per_seq,
            _const_spec((1, D_MODEL)),
            _const_spec((D_MODEL, PRE_WIDTH)),
            _const_spec((D_MODEL, LANES)),
            _const_spec((CONV_W, M_WIDTH)),
            _const_spec((1, M_WIDTH)),
            _const_spec((M_HEADS, M_HEAD_DIM, M_HEAD_DIM)),
            _const_spec((M_HEADS, M_HEAD_DIM, M_HEAD_DIM)),
            _const_spec((1, LANES)),
            _const_spec((LANES, G_KW)),
            _const_spec((1, G_KW)),
        ],
        out_specs=[qv_spec, tok(M_WIDTH), qv_spec, gif_spec, tok(G_KW), tok(G_KW),
                   tok(G_KW), tok(G_VW), per_seq],
        out_shape=[qv_shape, sds(M_WIDTH, BF16), qv_shape, gif_shape,
                   sds(G_KW, BF16), sds(G_KW, BF16), sds(G_KW, F32), sds(G_VW, BF16),
                   jax.ShapeDtypeStruct((b, SUBLANES, M_WIDTH), F32)],
        scratch_shapes=[pltpu.VMEM((bt, tb + SUBLANES, M_WIDTH), F32)],
        compiler_params=pltpu.CompilerParams(
            dimension_semantics=("parallel", "arbitrary"), vmem_limit_bytes=VMEM_LIMIT),
        name="mixer_pre",
    )(x, cst, g, w_pre, w_small, conv_w, conv_b, wq, wk, bif, wa2, ba)


def _scan_rows(x, op, fill, segment=None):
    n = segment or x.shape[0]
    row = lax.broadcasted_iota(jnp.int32, x.shape, 0) % n
    s = 1
    while s < n:
        x = op(x, jnp.where(row >= s, pltpu.roll(x, s, axis=0), fill))
        s *= 2
    return x


def _col_of_row(row):
    return jnp.transpose(jnp.broadcast_to(row, (SUBLANES, LANES)))[:, 0:1]


def _mlstm_chunk_tm(seqs, *, lm):
    reps = lm // LANES
    head_row = lax.broadcasted_iota(jnp.int32, (SUBLANES, LANES), 0) < M_HEADS
    src = lax.broadcasted_iota(jnp.int32, (lm, lm), 0)
    dst = lax.broadcasted_iota(jnp.int32, (lm, lm), 1)
    causal = src <= dst
    sl = [slice(h * M_HEAD_DIM, (h + 1) * M_HEAD_DIM) for h in range(M_HEADS)]
    row = lambda x, h: x[h:h + 1, :]
    gates = []
    for sq in seqs:
        g = sq.gift[0:SUBLANES]
        gmax = pltpu.roll(g, M_HEADS, axis=0)
        b = sq.gift[SUBLANES:2 * SUBLANES]
        m_prev = jnp.concatenate([sq.m_src[...]] * reps, axis=1)
        mx = jnp.maximum(m_prev, gmax)
        mx_last = jnp.broadcast_to(mx[:, lm - 1:lm], (SUBLANES, lm))
        gates.append(dict(
            mx=mx,
            inter=jnp.exp(m_prev - mx),
            nfloor=jnp.exp(-(b + mx)),
            w_in=jnp.exp(g - mx_last),
            decay=jnp.exp(m_prev - mx_last),
            g_col=jnp.transpose(g)))
        m_new = jnp.broadcast_to(b[:, lm - 1:lm], (SUBLANES, lm)) + mx_last
        sq.m_dst[...] = jnp.where(head_row, m_new[:, :LANES], 0.0)
    units = [(sq, gt, h) for sq, gt in zip(seqs, gates) for h in range(M_HEADS)]
    c_old = [sq.c_src[h] for sq, _, h in units]
    n_old = [sq.n_src[h] for sq, _, h in units]
    cn = [jnp.concatenate([c_old[u].astype(BF16),
                           jnp.broadcast_to(n_old[u].astype(BF16), (BF16_ROWS, M_HEAD_DIM))], axis=0)
          for u in range(len(units))]
    cq = [_dot(cn[u], sq.qt[sl[h], :]) for u, (sq, _, h) in enumerate(units)]
    dm = [jnp.exp(jnp.where(causal, gt["g_col"][:, h:h + 1] - row(gt["mx"], h), NEG_INF))
          for _, gt, h in units]
    st = [_dot(sq.k[:, sl[h]], sq.qt[sl[h], :]) * dm[u]
          for u, (sq, _, h) in enumerate(units)]
    num = [row(gt["inter"], h) * cq[u][:M_HEAD_DIM] + _dot(sq.vt[sl[h], :], st[u].astype(BF16))
           for u, (sq, gt, h) in enumerate(units)]
    for u, (sq, gt, h) in enumerate(units):
        den = (row(gt["inter"], h) * cq[u][M_HEAD_DIM:M_HEAD_DIM + 1]
               + jnp.sum(st[u], axis=0, keepdims=True))
        rcp = 1.0 / jnp.maximum(jnp.abs(den), row(gt["nfloor"], h))
        ms = jnp.mean(num[u] * num[u], axis=0, keepdims=True)
        sq.hm_store(h, num[u] * (rcp * lax.rsqrt(rcp * rcp * ms + EPS)))
    for u, (sq, gt, h) in enumerate(units):
        w_h = row(gt["w_in"], h)
        dec_h = jnp.concatenate([row(gt["decay"], h)[:, :LANES]] * (M_HEAD_DIM // LANES), axis=1)
        wv = jnp.concatenate([(sq.vt[sl[h], :].astype(F32) * w_h).astype(BF16),
                              jnp.broadcast_to(w_h.astype(BF16), (BF16_ROWS, lm))], axis=0)
        upd = _dot(wv, sq.k[:, sl[h]])
        sq.c_dst[h] = dec_h * c_old[u] + upd[:M_HEAD_DIM]
        sq.n_dst[h] = dec_h * n_old[u] + upd[M_HEAD_DIM:M_HEAD_DIM + 1]


def _mlstm_chunk(seqs, *, lm):
    row = lax.broadcasted_iota(jnp.int32, (lm, lm), 0)
    col = lax.broadcasted_iota(jnp.int32, (lm, lm), 1)
    causal = col <= row
    sl = [slice(h * M_HEAD_DIM, (h + 1) * M_HEAD_DIM) for h in range(M_HEADS)]
    gates = []
    for sq in seqs:
        cs = _scan_rows(sq.gif, jnp.add, 0.0)
        b = pltpu.roll(cs, LANES - M_HEADS, axis=1)
        g = sq.gif - b
        m_prev = sq.m_src[...]
        mx = jnp.maximum(m_prev, _scan_rows(g, jnp.maximum, NEG_INF))
        mx_last = mx[lm - 1:lm, :]
        gates.append(dict(
            mx=mx,
            inter=jnp.exp(m_prev - mx),
            nfloor=jnp.exp(-(b + mx)),
            g_t=jnp.transpose(g),
            w_in=jnp.exp(g - mx_last),
            decay=jnp.exp(m_prev - mx_last)))
        sq.m_dst[...] = b[lm - 1:lm, :] + mx_last
    units = [(sq, gt, h) for sq, gt in zip(seqs, gates) for h in range(M_HEADS)]
    col_of = lambda x, h: x[:, h:h + 1]
    c_old = [sq.c_src[h] for sq, _, h in units]
    n_old = [sq.n_src[h] for sq, _, h in units]
    dm = [jnp.exp(jnp.where(causal, gt["g_t"][h:h + 1, :] - col_of(gt["mx"], h), NEG_INF))
          for _, gt, h in units]
    qn = [jnp.sum(sq.q[:, sl[h]].astype(F32) * n_old[u], axis=-1, keepdims=True)
          for u, (sq, _, h) in enumerate(units)]
    s = [_dot_nt(sq.q[:, sl[h]], sq.k[:, sl[h]]) * dm[u] for u, (sq, _, h) in enumerate(units)]
    num = [col_of(gt["inter"], h) * _dot_nt(sq.q[:, sl[h]], c_old[u].astype(BF16))
           + _dot(s[u].astype(BF16), sq.v[:, sl[h]]) for u, (sq, gt, h) in enumerate(units)]
    for u, (sq, gt, h) in enumerate(units):
        den = col_of(gt["inter"], h) * qn[u] + jnp.sum(s[u], axis=-1, keepdims=True)
        rcp = 1.0 / jnp.maximum(jnp.abs(den), col_of(gt["nfloor"], h))
        ms = jnp.mean(num[u] * num[u], axis=-1, keepdims=True)
        sq.hm_store(h, num[u] * (rcp * lax.rsqrt(rcp * rcp * ms + EPS)))
    for u, (sq, gt, h) in enumerate(units):
        w_h = col_of(gt["w_in"], h)
        dec_h = col_of(gt["decay"], h)
        wv = (w_h * sq.v[:, sl[h]].astype(F32)).astype(BF16)
        sq.c_dst[h] = dec_h * c_old[u] + _dot_tn(wv, sq.k[:, sl[h]])
        sq.n_dst[h] = dec_h * n_old[u] + jnp.sum(w_h * sq.k[:, sl[h]].astype(F32), axis=0, keepdims=True)


GLA_SAFE_SPAN = 40.0


def _gla_scores_guarded(qh, kh, a, *, lg, sub):
    row = lax.broadcasted_iota(jnp.int32, (lg, lg), 0)
    col = lax.broadcasted_iota(jnp.int32, (lg, lg), 1)
    att = jnp.zeros((lg, lg), F32)
    c = sub
    while c < lg:
        nblk = lg // (2 * c)
        a_ref = jnp.concatenate(
            [jnp.broadcast_to(a[2 * c * i + c - 1:2 * c * i + c, :], (2 * c, G_DK))
             for i in range(nblk)], axis=0)
        qt = (qh * jnp.exp(jnp.minimum(a - a_ref, 0.0))).astype(BF16)
        kt = (kh * jnp.exp(jnp.minimum(a_ref - a, 0.0))).astype(BF16)
        valid = (row // (2 * c) == col // (2 * c)) & (row % (2 * c) >= c) & (col % (2 * c) < c)
        att = att + jnp.where(valid, _dot_nt(qt, kt), 0.0)
        c *= 2
    rows = lax.broadcasted_iota(jnp.int32, (sub, lg), 0)
    cols = lax.broadcasted_iota(jnp.int32, (sub, lg), 1)
    blocks = []
    for blk in range(lg // sub):
        r0 = blk * sub
        a_b = a[r0:r0 + sub, :]
        q_b = qh[r0:r0 + sub, :]
        acc = jnp.zeros((sub, lg), F32)
        for s in range(sub):
            e = jnp.exp(jnp.minimum(a_b - a[r0 + s:r0 + s + 1, :], 0.0))
            p = jnp.sum(q_b * e * kh[r0 + s:r0 + s + 1, :], axis=-1, keepdims=True)
            acc = jnp.where((cols == r0 + s) & (rows >= s), p, acc)
        blocks.append(acc)
    return att + jnp.concatenate(blocks, axis=0)


def _gla_chunk_guarded(seqs, gh_ref, *, lg, sub):
    ks = [slice(h * G_DK, (h + 1) * G_DK) for h in range(G_HEADS)]
    vs = [slice(h * G_DV, (h + 1) * G_DV) for h in range(G_HEADS)]
    pre = []
    for sq in seqs:
        a_last = sq.a[lg - 1:lg, :]
        qf = sq.qg.astype(F32)
        kf = sq.kg.astype(F32)
        pre.append(dict(a=sq.a, a_last=a_last, qf=qf, kf=kf,
                        qs=(qf * jnp.exp(sq.a)).astype(BF16),
                        kl=(kf * jnp.exp(a_last - sq.a)).astype(BF16)))
    units = [(sq, p, h) for sq, p in zip(seqs, pre) for h in range(G_HEADS)]
    s_old = [sq.s_src[h] for sq, _, h in units]
    att = [_gla_scores_guarded(p["qf"][:, ks[h]], p["kf"][:, ks[h]], p["a"][:, ks[h]], lg=lg, sub=sub)
           for _, p, h in units]
    o = [_dot(p["qs"][:, ks[h]], s_old[u].astype(BF16)) + _dot(att[u].astype(BF16), sq.vg[:, vs[h]])
         for u, (sq, p, h) in enumerate(units)]
    for u, (sq, _, h) in enumerate(units):
        sq.og_store(h, _rms(o[u], gh_ref[:, vs[h]]))
    for u, (sq, p, h) in enumerate(units):
        sq.s_dst[h] = (jnp.exp(_col_of_row(p["a_last"][:, ks[h]])) * s_old[u]
                       + _dot_tn(p["kl"][:, ks[h]], sq.vg[:, vs[h]]))


def _gla_block_direct(seqs, gh_ref, *, tb, lg):
    ks = [slice(h * G_DK, (h + 1) * G_DK) for h in range(G_HEADS)]
    vs = [slice(h * G_DV, (h + 1) * G_DV) for h in range(G_HEADS)]
    chunks = [slice(c * lg, (c + 1) * lg) for c in range(tb // lg)]
    row = lax.broadcasted_iota(jnp.int32, (lg, lg), 0)
    col = lax.broadcasted_iota(jnp.int32, (lg, lg), 1)
    causal = col <= row
    pre = []
    for sq in seqs:
        a_last = [sq.a[r.stop - 1:r.stop, :] for r in chunks]
        a_last_rows = jnp.concatenate([jnp.broadcast_to(al, (lg, G_KW)) for al in a_last], axis=0)
        qf = sq.qg.astype(F32)
        kf = sq.kg.astype(F32)
        pre.append(dict(a_last=a_last,
                        qs=(qf * jnp.exp(sq.a)).astype(BF16),
                        kt=(kf * jnp.exp(-sq.a)).astype(BF16),
                        kl=(kf * jnp.exp(a_last_rows - sq.a)).astype(BF16)))
    units = [(sq, p, h) for sq, p in zip(seqs, pre) for h in range(G_HEADS)]
    att = [[jnp.where(causal, _dot_nt(p["qs"][r, ks[h]], p["kt"][r, ks[h]]), 0.0).astype(BF16)
            for r in chunks] for _, p, h in units]
    s_cur = [sq.s_src[h] for sq, _, h in units]
    for c, r in enumerate(chunks):
        o = [_dot(p["qs"][r, ks[h]], s_cur[u].astype(BF16)) + _dot(att[u][c], sq.vg[r, vs[h]])
             for u, (sq, p, h) in enumerate(units)]
        for u, (sq, _, h) in enumerate(units):
            sq.og_rows_store(r, h, _rms(o[u], gh_ref[:, vs[h]]))
        s_cur = [jnp.exp(_col_of_row(p["a_last"][c][:, ks[h]])) * s_cur[u]
                 + _dot_tn(p["kl"][r, ks[h]], sq.vg[r, vs[h]])
                 for u, (sq, p, h) in enumerate(units)]
    for u, (sq, _, h) in enumerate(units):
        sq.s_dst[h] = s_cur[u]


def _scan_kernel(*refs, bt, tb, lm, lg, sub, has_state, time_minor, single_step):
    n_in = 9 + (4 if has_state else 0)
    qm_ref, km_ref, vm_ref, gif_ref, qg_ref, kg_ref, la_ref, vg_ref, ggh_ref = refs[:9]
    state_in = refs[9:n_in]
    hm_ref, og_ref = refs[n_in:n_in + 2]
    state_out = refs[n_in + 2:n_in + 6]
    scratch = refs[n_in + 6:]
    t = pl.program_id(1)
    nt = pl.num_programs(1)

    streamed = has_state and single_step
    src = state_in if streamed else scratch
    dst = state_out if streamed else scratch
    if not streamed:
        @pl.when(t == 0)
        def _():
            for i, ref in enumerate(scratch):
                ref[...] = state_in[i][...] if has_state else jnp.zeros(ref.shape, F32)

    def seq_records(bi, m_rows, g_rows):
        def hm_store(h, val):
            if time_minor:
                hm_ref[bi, 0, h * M_HEAD_DIM:(h + 1) * M_HEAD_DIM, :] = val
            else:
                hm_ref[bi, m_rows, h * M_HEAD_DIM:(h + 1) * M_HEAD_DIM] = val

        def og_rows_store(rows, h, val):
            og_ref[bi, rows, h * G_DV:(h + 1) * G_DV] = val

        return types.SimpleNamespace(
            bi=bi, hm_store=hm_store, og_rows_store=og_rows_store,
            og_store=functools.partial(og_rows_store, g_rows),
            c_src=src[0].at[bi], n_src=src[1].at[bi], m_src=src[2].at[bi], s_src=src[3].at[bi],
            c_dst=dst[0].at[bi], n_dst=dst[1].at[bi], m_dst=dst[2].at[bi], s_dst=dst[3].at[bi])

    def m_body(ci, carry):
        rows = pl.ds(pl.multiple_of(ci * lm, lm), lm)
        seqs = [seq_records(bi, rows, None) for bi in range(bt)]
        if time_minor:
            for sq in seqs:
                sq.qt, sq.k, sq.vt = qm_ref[sq.bi, 0], km_ref[sq.bi, rows, :], vm_ref[sq.bi, 0]
                sq.gift = gif_ref[sq.bi, 0]
            _mlstm_chunk_tm(seqs, lm=lm)
        else:
            for sq in seqs:
                sq.q, sq.k, sq.v = qm_ref[sq.bi, rows, :], km_ref[sq.bi, rows, :], vm_ref[sq.bi, rows, :]
                sq.gif = gif_ref[sq.bi, rows, :]
            _mlstm_chunk(seqs, lm=lm)
        return carry

    def gla_records(rows):
        seqs = [seq_records(bi, None, rows) for bi in range(bt)]
        for sq in seqs:
            sq.qg, sq.kg, sq.vg = qg_ref[sq.bi, rows, :], kg_ref[sq.bi, rows, :], vg_ref[sq.bi, rows, :]
            sq.a = la_ref[sq.bi, rows, :]
        return seqs

    def g_body(ci, carry):
        rows = pl.ds(pl.multiple_of(ci * lg, lg), lg)
        _gla_chunk_guarded(gla_records(rows), ggh_ref, lg=lg, sub=sub)
        return carry

    def loop(body, n):
        if n == 1:
            body(0, 0)
        else:
            lax.fori_loop(0, n, body, 0)

    loop(m_body, tb // lm)
    if lg <= sub:
        loop(g_body, tb // lg)
    else:
        chunk_decay = [la_ref[:, c * lg + lg - 1:(c + 1) * lg, :] for c in range(tb // lg)]
        span_ok = jnp.min(functools.reduce(jnp.minimum, chunk_decay)) >= -GLA_SAFE_SPAN

        @pl.when(span_ok)
        def _():
            _gla_block_direct(gla_records(slice(0, tb)), ggh_ref, tb=tb, lg=lg)

        @pl.when(jnp.logical_not(span_ok))
        def _():
            loop(g_body, tb // lg)

    if not streamed:
        @pl.when(t == nt - 1)
        def _():
            for out, ref in zip(state_out, scratch):
                out[...] = ref[...]


def _scan(qm, km, vm, gif, qg, kg, la, vg, ggh, state, *, bt, tb, lm, lg, sub, time_minor):
    b, t, _ = km.shape
    assert b % bt == 0 and t % tb == 0 and tb % lm == 0 and tb % lg == 0 and lg % sub == 0
    has_state = state is not None
    assert not (time_minor and has_state) and (not time_minor or (lm % LANES == 0 and tb == lm))
    tok = lambda w: pl.BlockSpec((bt, tb, w), lambda i, j: (i, j, 0))
    tok_t = lambda w: pl.BlockSpec((bt, 1, w, tb), lambda i, j: (i, j, 0, 0))
    c_shape = (M_HEADS, M_HEAD_DIM, M_HEAD_DIM)
    n_shape = (M_HEADS, 1, M_HEAD_DIM)
    m_shape = (SUBLANES, LANES) if time_minor else (1, LANES)
    s_shape = (G_HEADS, G_DK, G_DV)
    st_shapes = [(bt,) + shp for shp in (c_shape, n_shape, m_shape, s_shape)]
    st_specs = [pl.BlockSpec(shp, lambda i, j, nd=len(shp): (i,) + (0,) * (nd - 1)) for shp in st_shapes]
    single_step = t == tb
    carried = not (has_state and single_step)
    if time_minor:
        qv_spec, gif_spec = tok_t(M_WIDTH), tok_t(2 * SUBLANES)
        hm_spec, hm_shape = tok_t(M_WIDTH), (b, t // tb, M_WIDTH, tb)
    else:
        qv_spec, gif_spec = tok(M_WIDTH), tok(LANES)
        hm_spec, hm_shape = tok(M_WIDTH), (b, t, M_WIDTH)
    in_specs = [qv_spec, tok(M_WIDTH), qv_spec, gif_spec, tok(G_KW), tok(G_KW),
                tok(G_KW), tok(G_VW), _const_spec((1, G_VW))]
    args = [qm, km, vm, gif, qg, kg, la, vg, ggh]
    if has_state:
        in_specs += st_specs
        args += list(state)
    return pl.pallas_call(
        functools.partial(_scan_kernel, bt=bt, tb=tb, lm=lm, lg=lg, sub=sub, has_state=has_state,
                          time_minor=time_minor, single_step=single_step),
        grid=(b // bt, t // tb),
        in_specs=in_specs,
        out_specs=[hm_spec, tok(G_VW)] + st_specs,
        out_shape=[jax.ShapeDtypeStruct(hm_shape, F32),
                   jax.ShapeDtypeStruct((b, t, G_VW), F32)]
                  + [jax.ShapeDtypeStruct((b,) + shp[1:], F32) for shp in st_shapes],
        scratch_shapes=[pltpu.VMEM(shp, F32) for shp in st_shapes] if carried else [],
        compiler_params=pltpu.CompilerParams(
            dimension_semantics=("parallel", "arbitrary"), vmem_limit_bytes=VMEM_LIMIT),
        name="mixer_scan_state" if has_state else "mixer_scan",
    )(*args)


POST_O, POST_R, POST_A, POST_B = 0, 1024, 2048, 3072


def _post_kernel(x_ref, hm_ref, og_ref, g_ref, gmh_ref, w_ref, wpa_ref, wpb_ref, wo_ref, o_ref, *,
                 time_minor):
    x = x_ref[...]
    xn = _rms(x, g_ref[...]).astype(BF16)
    gate = lambda off: _dot(xn, w_ref[:, off:off + D_MODEL])
    if time_minor:
        hm = jnp.concatenate(
            [jnp.concatenate([jnp.transpose(hm_ref[0, j, h * M_HEAD_DIM:(h + 1) * M_HEAD_DIM, :])
                              for h in range(M_HEADS)], axis=1)
             for j in range(hm_ref.shape[1])], axis=0)
    else:
        hm = hm_ref[...]
    hm = (_sigmoid(gate(POST_O)) * (hm * gmh_ref[...])).astype(BF16)
    og = (_silu(gate(POST_R)) * og_ref[...]).astype(BF16)
    y = (_sigmoid(gate(POST_A)) * _dot(hm, wpa_ref[...])
         + _sigmoid(gate(POST_B)) * _dot(og, wpb_ref[...]))
    o_ref[...] = x + _dot(y.astype(BF16), wo_ref[...])


def _post(x2d, hm, og2d, g, gmh, w_post, wpa, wpb, wo, *, tm, time_minor):
    n = x2d.shape[0]
    assert n % tm == 0
    tok = pl.BlockSpec((tm, D_MODEL), lambda i: (i, 0))
    if time_minor:
        nblk, blk = hm.shape[1], hm.shape[3]
        assert tm % blk == 0 and (nblk * blk) % tm == 0
        per_seq = nblk * blk // tm
        hm_spec = pl.BlockSpec((1, tm // blk, M_WIDTH, blk), lambda i: (i // per_seq, i % per_seq, 0, 0))
    else:
        hm_spec = tok
    return pl.pallas_call(
        functools.partial(_post_kernel, time_minor=time_minor),
        grid=(n // tm,),
        in_specs=[tok, hm_spec, tok, _const_spec((1, D_MODEL)), _const_spec((1, M_WIDTH)),
                  _const_spec((D_MODEL, 4 * D_MODEL)),
                  _const_spec((M_WIDTH, D_MODEL)), _const_spec((G_VW, D_MODEL)),
                  _const_spec((D_MODEL, D_MODEL))],
        out_specs=tok,
        out_shape=jax.ShapeDtypeStruct((n, D_MODEL), F32),
        compiler_params=pltpu.CompilerParams(
            dimension_semantics=("parallel",), vmem_limit_bytes=VMEM_LIMIT),
        name="mixer_post",
    )(x2d, hm, og2d, g, gmh, w_post, wpa, wpb, wo)


def _pack_weights(g_ffn1, w_ffn1_up, w_ffn1_down, g_mix, w_in, conv_w, conv_b, w_mq, w_mk, b_if,
                  g_mhead, w_a2, b_a, g_ghead, w_pa, w_pb, w_o, g_ffn2, w_ffn2_up, w_ffn2_down,
                  g_final):
    l = 0
    row = lambda v: v.reshape(1, -1).astype(F32)
    edges = [0]
    for wd in (M_WIDTH, M_WIDTH, M_WIDTH, 2 * M_HEADS, G_KW, G_KW, G_VW, G_VW, G_RANK, D_MODEL, D_MODEL):
        edges.append(edges[-1] + wd)
    seg = lambda i: w_in[l][:, edges[i]:edges[i + 1]]
    u_m, v_m, o_m, if_m, q_g, k_g, v_g, r_g, a_g, g_a, g_b = (seg(i) for i in range(11))
    w_pre = jnp.concatenate([u_m, v_m, q_g, k_g, v_g], axis=1).astype(BF16)
    w_small = jnp.concatenate(
        [if_m, a_g, jnp.zeros((D_MODEL, LANES - 2 * M_HEADS - G_RANK), F32)], axis=1).astype(BF16)
    w_post = jnp.concatenate([o_m, r_g, g_a, g_b], axis=1).astype(BF16)
    bif = jnp.concatenate([b_if[l].reshape(-1), jnp.zeros((LANES - 2 * M_HEADS,), F32)]).reshape(1, LANES)
    wa2 = jnp.zeros((LANES, G_KW), F32).at[A_LANE0:A_LANE0 + G_RANK].set(w_a2[l]).astype(BF16)
    return dict(
        g_ffn1=row(g_ffn1[l]), w_ffn1_up=w_ffn1_up[l].astype(BF16), w_ffn1_down=w_ffn1_down[l].astype(BF16),
        g_mix=row(g_mix[l]), w_pre=w_pre, w_small=w_small, w_post=w_post,
        conv_w=conv_w[l].astype(F32), conv_b=row(conv_b[l]),
        w_mq=w_mq[l].astype(BF16), w_mk=w_mk[l].astype(BF16), bif=bif,
        g_mhead=row(g_mhead[l]), wa2=wa2, b_a=row(b_a[l]), g_ghead=row(g_ghead[l]),
        w_pa=w_pa[l].astype(BF16), w_pb=w_pb[l].astype(BF16), w_o=w_o[l].astype(BF16),
        g_ffn2=row(g_ffn2[l]), w_ffn2_up=w_ffn2_up[l].astype(BF16), w_ffn2_down=w_ffn2_down[l].astype(BF16),
        g_final=row(g_final))


def _trunk(x, state, w, *, tm, pre_tile, scan_tile, lm, lg, sub, time_minor):
    b, t, _ = x.shape
    n = b * t
    x2d = x.reshape(n, D_MODEL)
    tm_ffn = math.gcd(n, FFN_ROWS)
    x1 = _ffn(x2d, w["g_ffn1"], w["w_ffn1_up"], w["w_ffn1_down"], w["g_final"], final_norm=False, tm=tm_ffn)
    if state is None:
        cst = jnp.zeros((b, SUBLANES, M_WIDTH), F32)
        scan_state = None
    else:
        conv0, c0, n0, m0, s0 = state
        cst = jnp.concatenate(
            [jnp.zeros((b, SUBLANES - (CONV_W - 1), M_WIDTH), F32), conv0.astype(F32)], axis=1)
        m0p = jnp.concatenate([m0, jnp.zeros((b, LANES - M_HEADS), F32)], axis=1).reshape(b, 1, LANES)
        scan_state = (c0, n0.reshape(b, M_HEADS, 1, M_HEAD_DIM), m0p, s0)
    qm, km, vm, gif, qg, kg, la, vg, cout = _pre(
        x1.reshape(b, t, D_MODEL), cst, w["g_mix"], w["w_pre"], w["w_small"], w["conv_w"], w["conv_b"],
        w["w_mq"], w["w_mk"], w["bif"], w["wa2"], w["b_a"], bt=pre_tile[0], tb=pre_tile[1], lg=lg,
        time_minor=time_minor)
    hm, og, c_new, n_new, m_new, s_new = _scan(
        qm, km, vm, gif, qg, kg, la, vg, w["g_ghead"], scan_state,
        bt=scan_tile[0], tb=scan_tile[1], lm=lm, lg=lg, sub=sub, time_minor=time_minor)
    x2 = _post(x1, hm if time_minor else hm.reshape(n, M_WIDTH), og.reshape(n, G_VW), w["g_mix"],
               w["g_mhead"], w["w_post"], w["w_pa"], w["w_pb"], w["w_o"], tm=tm, time_minor=time_minor)
    y = _ffn(x2, w["g_ffn2"], w["w_ffn2_up"], w["w_ffn2_down"], w["g_final"], final_norm=True, tm=tm_ffn)
    m_heads = m_new[:, :M_HEADS, 0] if time_minor else m_new[:, 0, :M_HEADS]
    return (y.reshape(b, t, D_MODEL),
            cout[:, SUBLANES - (CONV_W - 1):, :][None],
            c_new[None],
            n_new.reshape(b, M_HEADS, M_HEAD_DIM)[None],
            m_heads[None],
            s_new[None])


FFN_ROWS = 1024
DECODE_SEQS_PER_STEP = 4
PREFILL_SEQS_PER_STEP = 2


def _tiles(b, t):
    n = b * t
    tm = math.gcd(n, 512)
    if t % 256 == 0:
        return dict(tm=math.gcd(t, 512), pre_tile=(1, 256), scan_tile=(math.gcd(b, PREFILL_SEQS_PER_STEP), 256),
                    lm=256, lg=128, sub=16, time_minor=True)
    assert t <= 64 and t % SUBLANES == 0
    bt = max(1, min(b, 512 // t))
    while b % bt:
        bt -= 1
    return dict(tm=tm, pre_tile=(bt, t), scan_tile=(math.gcd(b, DECODE_SEQS_PER_STEP), t), lm=t, lg=t,
                sub=min(16, t), time_minor=False)


def kernel(x_prompt, x_sample, state_conv, state_mlstm_C, state_mlstm_n, state_mlstm_m, state_gla_S,
           g_ffn1, w_ffn1_up, w_ffn1_down, g_mix, w_in, conv_w, conv_b, w_mq, w_mk, b_if, g_mhead,
           w_a2, b_a, g_ghead, w_pa, w_pb, w_o, g_ffn2, w_ffn2_up, w_ffn2_down, g_final):
    w = _pack_weights(g_ffn1, w_ffn1_up, w_ffn1_down, g_mix, w_in, conv_w, conv_b, w_mq, w_mk, b_if,
                      g_mhead, w_a2, b_a, g_ghead, w_pa, w_pb, w_o, g_ffn2, w_ffn2_up, w_ffn2_down,
                      g_final)
    bp, tp, _ = x_prompt.shape
    bs, ts, _ = x_sample.shape
    out_p = _trunk(x_prompt, None, w, **_tiles(bp, tp))
    state = (state_conv[0], state_mlstm_C[0], state_mlstm_n[0], state_mlstm_m[0], state_gla_S[0])
    out_s = _trunk(x_sample, state, w, **_tiles(bs, ts))
    return (out_p[0], out_s[0]) + out_p[1:] + out_s[1:]
```

```python
import functools
import math
import types

import jax
import jax.numpy as jnp
from jax import lax
from jax.experimental import pallas as pl
from jax.experimental.pallas import tpu as pltpu

D_MODEL = 1024
M_HEADS = 4
M_HEAD_DIM = 256
M_WIDTH = 1024
CONV_W = 4
G_HEADS = 4
G_DK = 128
G_DV = 256
G_KW = 512
G_VW = 1024
G_RANK = 16
G_TAU = 16.0
D_FF = 2816
EPS = 1e-6

LANES = 128
SUBLANES = 8
BF16_ROWS = 16
VMEM_LIMIT = 56 * 1024 * 1024

F32 = jnp.float32
BF16 = jnp.bfloat16
NEG_INF = float("-inf")


def _sigmoid(x):
    return 1.0 / (1.0 + jnp.exp(-x))


def _silu(x):
    return x * _sigmoid(x)


def _log_sigmoid(x):
    return jnp.minimum(x, 0.0) - jnp.log(1.0 + jnp.exp(-jnp.abs(x)))


def _rms(x, g):
    return x * lax.rsqrt(jnp.mean(x * x, axis=-1, keepdims=True) + EPS) * g


def _dot(a, b):
    return jnp.dot(a, b, preferred_element_type=F32)


def _dot_nt(a, b):
    return lax.dot_general(a, b, (((1,), (1,)), ((), ())), preferred_element_type=F32)


def _dot_tn(a, b):
    return lax.dot_general(a, b, (((0,), (0,)), ((), ())), preferred_element_type=F32)


def _const_spec(shape):
    nd = len(shape)
    return pl.BlockSpec(shape, lambda *_: (0,) * nd, pipeline_mode=pl.Buffered(1))


FF_CHUNK = 256


def _ffn_kernel(xa_ref, xb_ref, g_ref, wup_ref, wdown_ref, gfin_ref, oa_ref, ob_ref, h_ref, *,
                final_norm, tiles_a):
    def half_step(x_ref, o_ref):
        x = x_ref[...]
        xn = _rms(x, g_ref[...]).astype(BF16)
        for c in range(D_FF // FF_CHUNK):
            a = _dot(xn, wup_ref[:, c * FF_CHUNK:(c + 1) * FF_CHUNK])
            g = _dot(xn, wup_ref[:, D_FF + c * FF_CHUNK:D_FF + (c + 1) * FF_CHUNK])
            h_ref[:, c * FF_CHUNK:(c + 1) * FF_CHUNK] = (_silu(g) * a).astype(BF16)
        y = x + 0.5 * _dot(h_ref[...], wdown_ref[...])
        o_ref[...] = _rms(y, gfin_ref[...]) if final_norm else y

    on_a = pl.program_id(0) < tiles_a
    pl.when(on_a)(lambda: half_step(xa_ref, oa_ref))
    pl.when(jnp.logical_not(on_a))(lambda: half_step(xb_ref, ob_ref))


def _ffn(xa, xb, g, wup, wdown, gfin, *, final_norm, tm):
    na, nb = xa.shape[0], xb.shape[0]
    assert na % tm == 0 and nb % tm == 0
    tiles_a, tiles_b = na // tm, nb // tm
    spec_a = pl.BlockSpec((tm, D_MODEL), lambda i: (jnp.minimum(i, tiles_a - 1), 0))
    spec_b = pl.BlockSpec((tm, D_MODEL), lambda i: (jnp.maximum(i - tiles_a, 0), 0),
                          **(dict(pipeline_mode=pl.Buffered(1)) if tiles_b == 1 else {}))
    return pl.pallas_call(
        functools.partial(_ffn_kernel, final_norm=final_norm, tiles_a=tiles_a),
        grid=(tiles_a + tiles_b,),
        in_specs=[
            spec_a, spec_b,
            _const_spec((1, D_MODEL)),
            _const_spec((D_MODEL, 2 * D_FF)),
            _const_spec((D_FF, D_MODEL)),
            _const_spec((1, D_MODEL)),
        ],
        out_specs=[spec_a, spec_b],
        out_shape=[jax.ShapeDtypeStruct((na, D_MODEL), F32), jax.ShapeDtypeStruct((nb, D_MODEL), F32)],
        scratch_shapes=[pltpu.VMEM((tm, D_FF), BF16)],
        compiler_params=pltpu.CompilerParams(
            dimension_semantics=("arbitrary",), vmem_limit_bytes=VMEM_LIMIT),
        name="ffn_final" if final_norm else "ffn",
    )(xa, xb, g, wup, wdown, gfin)


PRE_U, PRE_V, PRE_QG, PRE_KG, PRE_VG = 0, 1024, 2048, 2560, 3072
PRE_WIDTH = 4096
A_LANE0 = 8


def _pre_kernel(x_ref, cst_ref, g_ref, w_ref, ws_ref, cw_ref, cb_ref, wq_ref, wk_ref, bif_ref,
                wa2_ref, ba_ref,
                qm_ref, km_ref, vm_ref, gif_ref, qg_ref, kg_ref, la_ref, vg_ref, cout_ref,
                cbuf_ref, *, bt, tb, sub, lg, time_minor):
    t = pl.program_id(1)
    m = bt * sub

    @pl.when(t == 0)
    def _():
        cbuf_ref[:, 0:SUBLANES, :] = cst_ref[...]

    @pl.when(t > 0)
    def _():
        cbuf_ref[:, 0:SUBLANES, :] = cbuf_ref[:, tb:tb + SUBLANES, :]

    for j in range(tb // sub):
        r0 = j * sub
        rows = slice(r0, r0 + sub)
        to_rows = lambda y, w: y.reshape(bt, sub, w)
        xn = _rms(x_ref[:, rows, :].reshape(m, D_MODEL), g_ref[...]).astype(BF16)
        small = _dot(xn, ws_ref[...])

        u = _dot(xn, w_ref[:, PRE_U:PRE_U + M_WIDTH])
        cbuf_ref[:, SUBLANES + r0:SUBLANES + r0 + sub, :] = to_rows(u, M_WIDTH)
        c = cb_ref[...].reshape(1, 1, M_WIDTH)
        for tap in range(CONV_W):
            off = SUBLANES - (CONV_W - 1) + tap + r0
            c = c + cbuf_ref[:, off:off + sub, :] * cw_ref[tap:tap + 1, :].reshape(1, 1, M_WIDTH)
        ch = _silu(c).reshape(m, M_WIDTH).astype(BF16)

        vm = _dot(xn, w_ref[:, PRE_V:PRE_V + M_WIDTH])
        gates = small + bif_ref[...]
        lane = lax.broadcasted_iota(jnp.int32, gates.shape, 1)
        gif = jnp.where(lane < M_HEADS, gates, jnp.where(lane < 2 * M_HEADS, _log_sigmoid(gates), 0.0))
        if time_minor:
            b = pltpu.roll(_scan_rows(gif, jnp.add, 0.0), LANES - M_HEADS, axis=1)
            g = gif - b
            gmax = _scan_rows(g, jnp.maximum, NEG_INF)
            packed = jnp.where(lane < M_HEADS, g,
                               jnp.where(lane < 2 * M_HEADS, pltpu.roll(gmax, M_HEADS, axis=1),
                                         jnp.where(lane < 3 * M_HEADS, pltpu.roll(b, 2 * M_HEADS, axis=1), 0.0)))
            gif_ref[0, j] = jnp.transpose(packed)[0:2 * SUBLANES, :]
        else:
            gif_ref[:, rows, :] = to_rows(gif, LANES)
            vm_ref[:, rows, :] = to_rows(vm.astype(BF16), M_WIDTH)

        def decay_columns(h):
            cols = slice(h * G_DK, (h + 1) * G_DK)
            a2 = _dot(small.astype(BF16), wa2_ref[:, cols]) + ba_ref[:, cols]
            la_ref[:, rows, cols] = to_rows(
                _scan_rows(_log_sigmoid(a2) * (1.0 / G_TAU), jnp.add, 0.0, segment=lg), G_DK)

        qg_ref[:, rows, :] = to_rows(
            (_dot(xn, w_ref[:, PRE_QG:PRE_QG + G_KW]) * (G_DK ** -0.5)).astype(BF16), G_KW)
        decay_columns(0)
        kg_ref[:, rows, :] = to_rows(_dot(xn, w_ref[:, PRE_KG:PRE_KG + G_KW]).astype(BF16), G_KW)
        decay_columns(1)
        half = G_VW // 2
        for p in range(2):
            vg_ref[:, rows, p * half:(p + 1) * half] = to_rows(
                _dot(xn, w_ref[:, PRE_VG + p * half:PRE_VG + (p + 1) * half]).astype(BF16), half)
            decay_columns(2 + p)

        for h in range(M_HEADS):
            sl = slice(h * M_HEAD_DIM, (h + 1) * M_HEAD_DIM)
            qh = _dot(ch[:, sl], wq_ref[h])
            if time_minor:
                qm_ref[0, j, sl, :] = jnp.transpose(qh).astype(BF16)
                vm_ref[0, j, sl, :] = jnp.transpose(vm[:, sl]).astype(BF16)
            else:
                qm_ref[:, rows, sl] = to_rows(qh.astype(BF16), M_HEAD_DIM)
            km_ref[:, rows, sl] = to_rows(
                (_dot(ch[:, sl], wk_ref[h]) * (M_HEAD_DIM ** -0.5)).astype(BF16), M_HEAD_DIM)
    cout_ref[...] = cbuf_ref[:, tb:tb + SUBLANES, :]


def _pre(x, cst, g, w_pre, w_small, conv_w, conv_b, wq, wk, bif, wa2, ba, *, bt, tb, sub, lg, time_minor):
    b, t, _ = x.shape
    assert b % bt == 0 and t % tb == 0 and tb % sub == 0 and sub % SUBLANES == 0 and sub % lg == 0
    assert not time_minor or (bt == 1 and sub % LANES == 0)
    tok = lambda w: pl.BlockSpec((bt, tb, w), lambda i, j: (i, j, 0))
    per_seq = pl.BlockSpec((bt, SUBLANES, M_WIDTH), lambda i, j: (i, 0, 0))
    sds = lambda w, dt: jax.ShapeDtypeStruct((b, t, w), dt)
    if time_minor:
        tok_t = lambda w: pl.BlockSpec((1, tb // sub, w, sub), lambda i, j: (i, j, 0, 0))
        sds_t = lambda w, dt: jax.ShapeDtypeStruct((b, t // sub, w, sub), dt)
        qv_spec, qv_shape = tok_t(M_WIDTH), sds_t(M_WIDTH, BF16)
        gif_spec, gif_shape = tok_t(2 * SUBLANES), sds_t(2 * SUBLANES, F32)
    else:
        qv_spec, qv_shape = tok(M_WIDTH), sds(M_WIDTH, BF16)
        gif_spec, gif_shape = tok(LANES), sds(LANES, F32)
    return pl.pallas_call(
        functools.partial(_pre_kernel, bt=bt, tb=tb, sub=sub, lg=lg, time_minor=time_minor),
        grid=(b // bt, t // tb),
        in_specs=[
            tok(D_MODEL), per_seq,
            _const_spec((1, D_MODEL)),
            _const_spec((D_MODEL, PRE_WIDTH)),
            _const_spec((D_MODEL, LANES)),
            _const_spec((CONV_W, M_WIDTH)),
            _const_spec((1, M_WIDTH)),
            _const_spec((M_HEADS, M_HEAD_DIM, M_HEAD_DIM)),
            _const_spec((M_HEADS, M_HEAD_DIM, M_HEAD_DIM)),
            _const_spec((1, LANES)),
            _const_spec((LANES, G_KW)),
            _const_spec((1, G_KW)),
        ],
        out_specs=[qv_spec, tok(M_WIDTH), qv_spec, gif_spec, tok(G_KW), tok(G_KW),
                   tok(G_KW), tok(G_VW), per_seq],
        out_shape=[qv_shape, sds(M_WIDTH, BF16), qv_shape, gif_shape,
                   sds(G_KW, BF16), sds(G_KW, BF16), sds(G_KW, F32), sds(G_VW, BF16),
                   jax.ShapeDtypeStruct((b, SUBLANES, M_WIDTH), F32)],
        scratch_shapes=[pltpu.VMEM((bt, tb + SUBLANES, M_WIDTH), F32)],
        compiler_params=pltpu.CompilerParams(
            dimension_semantics=("parallel", "arbitrary"), vmem_limit_bytes=VMEM_LIMIT),
        name="mixer_pre",
    )(x, cst, g, w_pre, w_small, conv_w, conv_b, wq, wk, bif, wa2, ba)


def _scan_rows(x, op, fill, segment=None):
    n = segment or x.shape[0]
    row = lax.broadcasted_iota(jnp.int32, x.shape, 0) % n
    s = 1
    while s < n:
        x = op(x, jnp.where(row >= s, pltpu.roll(x, s, axis=0), fill))
        s *= 2
    return x


def _col_of_row(row):
    return jnp.transpose(jnp.broadcast_to(row, (SUBLANES, LANES)))[:, 0:1]


def _mlstm_chunk_tm(seqs, *, lm):
    reps = lm // LANES
    head_row = lax.broadcasted_iota(jnp.int32, (SUBLANES, LANES), 0) < M_HEADS
    src = lax.broadcasted_iota(jnp.int32, (lm, lm), 0)
    dst = lax.broadcasted_iota(jnp.int32, (lm, lm), 1)
    causal = src <= dst
    sl = [slice(h * M_HEAD_DIM, (h + 1) * M_HEAD_DIM) for h in range(M_HEADS)]
    row = lambda x, h: x[h:h + 1, :]
    gates = []
    for sq in seqs:
        g = sq.gift[0:SUBLANES]
        gmax = pltpu.roll(g, M_HEADS, axis=0)
        b = sq.gift[SUBLANES:2 * SUBLANES]
        m_prev = jnp.concatenate([sq.m_src[...]] * reps, axis=1)
        mx = jnp.maximum(m_prev, gmax)
        mx_last = jnp.broadcast_to(mx[:, lm - 1:lm], (SUBLANES, lm))
        gates.append(dict(
            mx=mx,
            inter=jnp.exp(m_prev - mx),
            nfloor=jnp.exp(-(b + mx)),
            w_in=jnp.exp(g - mx_last),
            decay=jnp.exp(m_prev - mx_last),
            g_col=jnp.transpose(g)))
        m_new = jnp.broadcast_to(b[:, lm - 1:lm], (SUBLANES, lm)) + mx_last
        sq.m_dst[...] = jnp.where(head_row, m_new[:, :LANES], 0.0)
    units = [(sq, gt, h) for sq, gt in zip(seqs, gates) for h in range(M_HEADS)]
    c_old = [sq.c_src[h] for sq, _, h in units]
    n_old = [sq.n_src[h] for sq, _, h in units]
    cn = [jnp.concatenate([c_old[u].astype(BF16),
                           jnp.broadcast_to(n_old[u].astype(BF16), (BF16_ROWS, M_HEAD_DIM))], axis=0)
          for u in range(len(units))]
    cq = [_dot(cn[u], sq.qt[sl[h], :]) for u, (sq, _, h) in enumerate(units)]
    dm = [jnp.exp(jnp.where(causal, gt["g_col"][:, h:h + 1] - row(gt["mx"], h), NEG_INF))
          for _, gt, h in units]
    st = [_dot(sq.k[:, sl[h]], sq.qt[sl[h], :]) * dm[u]
          for u, (sq, _, h) in enumerate(units)]
    num = [row(gt["inter"], h) * cq[u][:M_HEAD_DIM] + _dot(sq.vt[sl[h], :], st[u].astype(BF16))
           for u, (sq, gt, h) in enumerate(units)]
    for u, (sq, gt, h) in enumerate(units):
        den = (row(gt["inter"], h) * cq[u][M_HEAD_DIM:M_HEAD_DIM + 1]
               + jnp.sum(st[u], axis=0, keepdims=True))
        rcp = 1.0 / jnp.maximum(jnp.abs(den), row(gt["nfloor"], h))
        ms = jnp.mean(num[u] * num[u], axis=0, keepdims=True)
        sq.hm_store(h, num[u] * (rcp * lax.rsqrt(rcp * rcp * ms + EPS)))
    for u, (sq, gt, h) in enumerate(units):
        w_h = row(gt["w_in"], h)
        dec_h = jnp.concatenate([row(gt["decay"], h)[:, :LANES]] * (M_HEAD_DIM // LANES), axis=1)
        wv = jnp.concatenate([(sq.vt[sl[h], :].astype(F32) * w_h).astype(BF16),
                              jnp.broadcast_to(w_h.astype(BF16), (BF16_ROWS, lm))], axis=0)
        upd = _dot(wv, sq.k[:, sl[h]])
        sq.c_dst[h] = dec_h * c_old[u] + upd[:M_HEAD_DIM]
        sq.n_dst[h] = dec_h * n_old[u] + upd[M_HEAD_DIM:M_HEAD_DIM + 1]


def _mlstm_chunk(seqs, *, lm):
    row = lax.broadcasted_iota(jnp.int32, (lm, lm), 0)
    col = lax.broadcasted_iota(jnp.int32, (lm, lm), 1)
    causal = col <= row
    sl = [slice(h * M_HEAD_DIM, (h + 1) * M_HEAD_DIM) for h in range(M_HEADS)]
    gates = []
    for sq in seqs:
        cs = _scan_rows(sq.gif, jnp.add, 0.0)
        b = pltpu.roll(cs, LANES - M_HEADS, axis=1)
        g = sq.gif - b
        m_prev = sq.m_src[...]
        mx = jnp.maximum(m_prev, _scan_rows(g, jnp.maximum, NEG_INF))
        mx_last = mx[lm - 1:lm, :]
        gates.append(dict(
            mx=mx,
            inter=jnp.exp(m_prev - mx),
            nfloor=jnp.exp(-(b + mx)),
            g_t=jnp.transpose(g),
            w_in=jnp.exp(g - mx_last),
            decay=jnp.exp(m_prev - mx_last)))
        sq.m_dst[...] = b[lm - 1:lm, :] + mx_last
    units = [(sq, gt, h) for sq, gt in zip(seqs, gates) for h in range(M_HEADS)]
    col_of = lambda x, h: x[:, h:h + 1]
    c_old = [sq.c_src[h] for sq, _, h in units]
    n_old = [sq.n_src[h] for sq, _, h in units]
    dm = [jnp.exp(jnp.where(causal, gt["g_t"][h:h + 1, :] - col_of(gt["mx"], h), NEG_INF))
          for _, gt, h in units]
    qn = [jnp.sum(sq.q[:, sl[h]].astype(F32) * n_old[u], axis=-1, keepdims=True)
          for u, (sq, _, h) in enumerate(units)]
    s = [_dot_nt(sq.q[:, sl[h]], sq.k[:, sl[h]]) * dm[u] for u, (sq, _, h) in enumerate(units)]
    num = [col_of(gt["inter"], h) * _dot_nt(sq.q[:, sl[h]], c_old[u].astype(BF16))
           + _dot(s[u].astype(BF16), sq.v[:, sl[h]]) for u, (sq, gt, h) in enumerate(units)]
    for u, (sq, gt, h) in enumerate(units):
        den = col_of(gt["inter"], h) * qn[u] + jnp.sum(s[u], axis=-1, keepdims=True)
        rcp = 1.0 / jnp.maximum(jnp.abs(den), col_of(gt["nfloor"], h))
        ms = jnp.mean(num[u] * num[u], axis=-1, keepdims=True)
        sq.hm_store(h, num[u] * (rcp * lax.rsqrt(rcp * rcp * ms + EPS)))
    for u, (sq, gt, h) in enumerate(units):
        w_h = col_of(gt["w_in"], h)
        dec_h = col_of(gt["decay"], h)
        wv = (w_h * sq.v[:, sl[h]].astype(F32)).astype(BF16)
        sq.c_dst[h] = dec_h * c_old[u] + _dot_tn(wv, sq.k[:, sl[h]])
        sq.n_dst[h] = dec_h * n_old[u] + jnp.sum(w_h * sq.k[:, sl[h]].astype(F32), axis=0, keepdims=True)


GLA_SAFE_SPAN = 40.0


def _gla_scores_guarded(qh, kh, a, *, lg, sub):
    row = lax.broadcasted_iota(jnp.int32, (lg, lg), 0)
    col = lax.broadcasted_iota(jnp.int32, (lg, lg), 1)
    att = jnp.zeros((lg, lg), F32)
    c = sub
    while c < lg:
        nblk = lg // (2 * c)
        a_ref = jnp.concatenate(
            [jnp.broadcast_to(a[2 * c * i + c - 1:2 * c * i + c, :], (2 * c, G_DK))
             for i in range(nblk)], axis=0)
        qt = (qh * jnp.exp(jnp.minimum(a - a_ref, 0.0))).astype(BF16)
        kt = (kh * jnp.exp(jnp.minimum(a_ref - a, 0.0))).astype(BF16)
        valid = (row // (2 * c) == col // (2 * c)) & (row % (2 * c) >= c) & (col % (2 * c) < c)
        att = att + jnp.where(valid, _dot_nt(qt, kt), 0.0)
        c *= 2
    rows = lax.broadcasted_iota(jnp.int32, (sub, lg), 0)
    cols = lax.broadcasted_iota(jnp.int32, (sub, lg), 1)
    blocks = []
    for blk in range(lg // sub):
        r0 = blk * sub
        a_b = a[r0:r0 + sub, :]
        q_b = qh[r0:r0 + sub, :]
        acc = jnp.zeros((sub, lg), F32)
        for s in range(sub):
            e = jnp.exp(jnp.minimum(a_b - a[r0 + s:r0 + s + 1, :], 0.0))
            p = jnp.sum(q_b * e * kh[r0 + s:r0 + s + 1, :], axis=-1, keepdims=True)
            acc = jnp.where((cols == r0 + s) & (rows >= s), p, acc)
        blocks.append(acc)
    return att + jnp.concatenate(blocks, axis=0)


def _gla_chunk_guarded(seqs, gh_ref, *, lg, sub):
    ks = [slice(h * G_DK, (h + 1) * G_DK) for h in range(G_HEADS)]
    vs = [slice(h * G_DV, (h + 1) * G_DV) for h in range(G_HEADS)]
    pre = []
    for sq in seqs:
        a_last = sq.a[lg - 1:lg, :]
        qf = sq.qg.astype(F32)
        kf = sq.kg.astype(F32)
        pre.append(dict(a=sq.a, a_last=a_last, qf=qf, kf=kf,
                        qs=(qf * jnp.exp(sq.a)).astype(BF16),
                        kl=(kf * jnp.exp(a_last - sq.a)).astype(BF16)))
    units = [(sq, p, h) for sq, p in zip(seqs, pre) for h in range(G_HEADS)]
    s_old = [sq.s_src[h] for sq, _, h in units]
    att = [_gla_scores_guarded(p["qf"][:, ks[h]], p["kf"][:, ks[h]], p["a"][:, ks[h]], lg=lg, sub=sub)
           for _, p, h in units]
    o = [_dot(p["qs"][:, ks[h]], s_old[u].astype(BF16)) + _dot(att[u].astype(BF16), sq.vg[:, vs[h]])
         for u, (sq, p, h) in enumerate(units)]
    for u, (sq, _, h) in enumerate(units):
        sq.og_store(h, _rms(o[u], gh_ref[:, vs[h]]))
    for u, (sq, p, h) in enumerate(units):
        sq.s_dst[h] = (jnp.exp(_col_of_row(p["a_last"][:, ks[h]])) * s_old[u]
                       + _dot_tn(p["kl"][:, ks[h]], sq.vg[:, vs[h]]))


def _gla_block_direct(seqs, gh_ref, *, tb, lg):
    ks = [slice(h * G_DK, (h + 1) * G_DK) for h in range(G_HEADS)]
    vs = [slice(h * G_DV, (h + 1) * G_DV) for h in range(G_HEADS)]
    chunks = [slice(c * lg, (c + 1) * lg) for c in range(tb // lg)]
    row = lax.broadcasted_iota(jnp.int32, (lg, lg), 0)
    col = lax.broadcasted_iota(jnp.int32, (lg, lg), 1)
    causal = col <= row
    pre = []
    for sq in seqs:
        a_last = [sq.a[r.stop - 1:r.stop, :] for r in chunks]
        a_last_rows = jnp.concatenate([jnp.broadcast_to(al, (lg, G_KW)) for al in a_last], axis=0)
        qf = sq.qg.astype(F32)
        kf = sq.kg.astype(F32)
        pre.append(dict(a_last=a_last,
                        qs=(qf * jnp.exp(sq.a)).astype(BF16),
                        kt=(kf * jnp.exp(-sq.a)).astype(BF16),
                        kl=(kf * jnp.exp(a_last_rows - sq.a)).astype(BF16)))
    units = [(sq, p, h) for sq, p in zip(seqs, pre) for h in range(G_HEADS)]
    att = [[jnp.where(causal, _dot_nt(p["qs"][r, ks[h]], p["kt"][r, ks[h]]), 0.0).astype(BF16)
            for r in chunks] for _, p, h in units]
    s_cur = [sq.s_src[h] for sq, _, h in units]
    for c, r in enumerate(chunks):
        o = [_dot(p["qs"][r, ks[h]], s_cur[u].astype(BF16)) + _dot(att[u][c], sq.vg[r, vs[h]])
             for u, (sq, p, h) in enumerate(units)]
        for u, (sq, _, h) in enumerate(units):
            sq.og_rows_store(r, h, _rms(o[u], gh_ref[:, vs[h]]))
        s_cur = [jnp.exp(_col_of_row(p["a_last"][c][:, ks[h]])) * s_cur[u]
                 + _dot_tn(p["kl"][r, ks[h]], sq.vg[r, vs[h]])
                 for u, (sq, p, h) in enumerate(units)]
    for u, (sq, _, h) in enumerate(units):
        sq.s_dst[h] = s_cur[u]


def _scan_kernel(*refs, bt, tb, lm, lg, sub, has_state, time_minor, single_step):
    n_in = 9 + (4 if has_state else 0)
    qm_ref, km_ref, vm_ref, gif_ref, qg_ref, kg_ref, la_ref, vg_ref, ggh_ref = refs[:9]
    state_in = refs[9:n_in]
    hm_ref, og_ref = refs[n_in:n_in + 2]
    state_out = refs[n_in + 2:n_in + 6]
    scratch = refs[n_in + 6:]
    t = pl.program_id(1)
    nt = pl.num_programs(1)

    streamed = has_state and single_step
    src = state_in if streamed else scratch
    dst = state_out if streamed else scratch
    if not streamed:
        @pl.when(t == 0)
        def _():
            for i, ref in enumerate(scratch):
                ref[...] = state_in[i][...] if has_state else jnp.zeros(ref.shape, F32)

    def seq_records(bi, m_rows, g_rows):
        def hm_store(h, val):
            if time_minor:
                hm_ref[bi, 0, h * M_HEAD_DIM:(h + 1) * M_HEAD_DIM, :] = val
            else:
                hm_ref[bi, m_rows, h * M_HEAD_DIM:(h + 1) * M_HEAD_DIM] = val

        def og_rows_store(rows, h, val):
            og_ref[bi, rows, h * G_DV:(h + 1) * G_DV] = val

        return types.SimpleNamespace(
            bi=bi, hm_store=hm_store, og_rows_store=og_rows_store,
            og_store=functools.partial(og_rows_store, g_rows),
            c_src=src[0].at[bi], n_src=src[1].at[bi], m_src=src[2].at[bi], s_src=src[3].at[bi],
            c_dst=dst[0].at[bi], n_dst=dst[1].at[bi], m_dst=dst[2].at[bi], s_dst=dst[3].at[bi])

    def m_body(ci, carry):
        rows = pl.ds(pl.multiple_of(ci * lm, lm), lm)
        seqs = [seq_records(bi, rows, None) for bi in range(bt)]
        if time_minor:
            for sq in seqs:
                sq.qt, sq.k, sq.vt = qm_ref[sq.bi, 0], km_ref[sq.bi, rows, :], vm_ref[sq.bi, 0]
                sq.gift = gif_ref[sq.bi, 0]
            _mlstm_chunk_tm(seqs, lm=lm)
        else:
            for sq in seqs:
                sq.q, sq.k, sq.v = qm_ref[sq.bi, rows, :], km_ref[sq.bi, rows, :], vm_ref[sq.bi, rows, :]
                sq.gif = gif_ref[sq.bi, rows, :]
            _mlstm_chunk(seqs, lm=lm)
        return carry

    def gla_records(rows):
        seqs = [seq_records(bi, None, rows) for bi in range(bt)]
        for sq in seqs:
            sq.qg, sq.kg, sq.vg = qg_ref[sq.bi, rows, :], kg_ref[sq.bi, rows, :], vg_ref[sq.bi, rows, :]
            sq.a = la_ref[sq.bi, rows, :]
        return seqs

    def g_body(ci, carry):
        rows = pl.ds(pl.multiple_of(ci * lg, lg), lg)
        _gla_chunk_guarded(gla_records(rows), ggh_ref, lg=lg, sub=sub)
        return carry

    def loop(body, n):
        if n == 1:
            body(0, 0)
        else:
            lax.fori_loop(0, n, body, 0)

    loop(m_body, tb // lm)
    if lg <= sub:
        loop(g_body, tb // lg)
    else:
        chunk_decay = [la_ref[:, c * lg + lg - 1:(c + 1) * lg, :] for c in range(tb // lg)]
        span_ok = jnp.min(functools.reduce(jnp.minimum, chunk_decay)) >= -GLA_SAFE_SPAN

        @pl.when(span_ok)
        def _():
            _gla_block_direct(gla_records(slice(0, tb)), ggh_ref, tb=tb, lg=lg)

        @pl.when(jnp.logical_not(span_ok))
        def _():
            loop(g_body, tb // lg)

    if not streamed:
        @pl.when(t == nt - 1)
        def _():
            for out, ref in zip(state_out, scratch):
                out[...] = ref[...]


def _scan(qm, km, vm, gif, qg, kg, la, vg, ggh, state, *, bt, tb, lm, lg, sub, time_minor):
    b, t, _ = km.shape
    assert b % bt == 0 and t % tb == 0 and tb % lm == 0 and tb % lg == 0 and lg % sub == 0
    has_state = state is not None
    assert not (time_minor and has_state) and (not time_minor or (lm % LANES == 0 and tb == lm))
    tok = lambda w: pl.BlockSpec((bt, tb, w), lambda i, j: (i, j, 0))
    tok_t = lambda w: pl.BlockSpec((bt, 1, w, tb), lambda i, j: (i, j, 0, 0))
    c_shape = (M_HEADS, M_HEAD_DIM, M_HEAD_DIM)
    n_shape = (M_HEADS, 1, M_HEAD_DIM)
    m_shape = (SUBLANES, LANES) if time_minor else (1, LANES)
    s_shape = (G_HEADS, G_DK, G_DV)
    st_shapes = [(bt,) + shp for shp in (c_shape, n_shape, m_shape, s_shape)]
    st_specs = [pl.BlockSpec(shp, lambda i, j, nd=len(shp): (i,) + (0,) * (nd - 1)) for shp in st_shapes]
    single_step = t == tb
    carried = not (has_state and single_step)
    if time_minor:
        qv_spec, gif_spec = tok_t(M_WIDTH), tok_t(2 * SUBLANES)
        hm_spec, hm_shape = tok_t(M_WIDTH), (b, t // tb, M_WIDTH, tb)
    else:
        qv_spec, gif_spec = tok(M_WIDTH), tok(LANES)
        hm_spec, hm_shape = tok(M_WIDTH), (b, t, M_WIDTH)
    in_specs = [qv_spec, tok(M_WIDTH), qv_spec, gif_spec, tok(G_KW), tok(G_KW),
                tok(G_KW), tok(G_VW), _const_spec((1, G_VW))]
    args = [qm, km, vm, gif, qg, kg, la, vg, ggh]
    if has_state:
        in_specs += st_specs
        args += list(state)
    return pl.pallas_call(
        functools.partial(_scan_kernel, bt=bt, tb=tb, lm=lm, lg=lg, sub=sub, has_state=has_state,
                          time_minor=time_minor, single_step=single_step),
        grid=(b // bt, t // tb),
        in_specs=in_specs,
        out_specs=[hm_spec, tok(G_VW)] + st_specs,
        out_shape=[jax.ShapeDtypeStruct(hm_shape, F32),
                   jax.ShapeDtypeStruct((b, t, G_VW), F32)]
                  + [jax.ShapeDtypeStruct((b,) + shp[1:], F32) for shp in st_shapes],
        scratch_shapes=[pltpu.VMEM(shp, F32) for shp in st_shapes] if carried else [],
        compiler_params=pltpu.CompilerParams(
            dimension_semantics=("parallel", "arbitrary"), vmem_limit_bytes=VMEM_LIMIT),
        name="mixer_scan_state" if has_state else "mixer_scan",
    )(*args)


POST_O, POST_R, POST_A, POST_B = 0, 1024, 2048, 3072


def _post_kernel(x_ref, hm_ref, og_ref, g_ref, gmh_ref, w_ref, wpa_ref, wpb_ref, wo_ref, o_ref, *,
                 time_minor):
    x = x_ref[...]
    xn = _rms(x, g_ref[...]).astype(BF16)
    gate = lambda off: _dot(xn, w_ref[:, off:off + D_MODEL])
    if time_minor:
        hm = jnp.concatenate(
            [jnp.concatenate([jnp.transpose(hm_ref[0, j, h * M_HEAD_DIM:(h + 1) * M_HEAD_DIM, :])
                              for h in range(M_HEADS)], axis=1)
             for j in range(hm_ref.shape[1])], axis=0)
    else:
        hm = hm_ref[...]
    hm = (_sigmoid(gate(POST_O)) * (hm * gmh_ref[...])).astype(BF16)
    og = (_silu(gate(POST_R)) * og_ref[...]).astype(BF16)
    y = (_sigmoid(gate(POST_A)) * _dot(hm, wpa_ref[...])
         + _sigmoid(gate(POST_B)) * _dot(og, wpb_ref[...]))
    o_ref[...] = x + _dot(y.astype(BF16), wo_ref[...])


def _post(x2d, hm, og2d, g, gmh, w_post, wpa, wpb, wo, *, tm, time_minor):
    n = x2d.shape[0]
    assert n % tm == 0
    tok = pl.BlockSpec((tm, D_MODEL), lambda i: (i, 0))
    if time_minor:
        nblk, blk = hm.shape[1], hm.shape[3]
        assert tm % blk == 0 and (nblk * blk) % tm == 0
        per_seq = nblk * blk // tm
        hm_spec = pl.BlockSpec((1, tm // blk, M_WIDTH, blk), lambda i: (i // per_seq, i % per_seq, 0, 0))
    else:
        hm_spec = tok
    return pl.pallas_call(
        functools.partial(_post_kernel, time_minor=time_minor),
        grid=(n // tm,),
        in_specs=[tok, hm_spec, tok, _const_spec((1, D_MODEL)), _const_spec((1, M_WIDTH)),
                  _const_spec((D_MODEL, 4 * D_MODEL)),
                  _const_spec((M_WIDTH, D_MODEL)), _const_spec((G_VW, D_MODEL)),
                  _const_spec((D_MODEL, D_MODEL))],
        out_specs=tok,
        out_shape=jax.ShapeDtypeStruct((n, D_MODEL), F32),
        compiler_params=pltpu.CompilerParams(
            dimension_semantics=("parallel",), vmem_limit_bytes=VMEM_LIMIT),
        name="mixer_post",
    )(x2d, hm, og2d, g, gmh, w_post, wpa, wpb, wo)


def _pack_weights(g_ffn1, w_ffn1_up, w_ffn1_down, g_mix, w_in, conv_w, conv_b, w_mq, w_mk, b_if,
                  g_mhead, w_a2, b_a, g_ghead, w_pa, w_pb, w_o, g_ffn2, w_ffn2_up, w_ffn2_down,
                  g_final):
    l = 0
    row = lambda v: v.reshape(1, -1).astype(F32)
    edges = [0]
    for wd in (M_WIDTH, M_WIDTH, M_WIDTH, 2 * M_HEADS, G_KW, G_KW, G_VW, G_VW, G_RANK, D_MODEL, D_MODEL):
        edges.append(edges[-1] + wd)
    seg = lambda i: w_in[l][:, edges[i]:edges[i + 1]]
    u_m, v_m, o_m, if_m, q_g, k_g, v_g, r_g, a_g, g_a, g_b = (seg(i) for i in range(11))
    w_pre = jnp.concatenate([u_m, v_m, q_g, k_g, v_g], axis=1).astype(BF16)
    w_small = jnp.concatenate(
        [if_m, a_g, jnp.zeros((D_MODEL, LANES - 2 * M_HEADS - G_RANK), F32)], axis=1).astype(BF16)
    w_post = jnp.concatenate([o_m, r_g, g_a, g_b], axis=1).astype(BF16)
    bif = jnp.concatenate([b_if[l].reshape(-1), jnp.zeros((LANES - 2 * M_HEADS,), F32)]).reshape(1, LANES)
    wa2 = jnp.zeros((LANES, G_KW), F32).at[A_LANE0:A_LANE0 + G_RANK].set(w_a2[l]).astype(BF16)
    return dict(
        g_ffn1=row(g_ffn1[l]), w_ffn1_up=w_ffn1_up[l].astype(BF16), w_ffn1_down=w_ffn1_down[l].astype(BF16),
        g_mix=row(g_mix[l]), w_pre=w_pre, w_small=w_small, w_post=w_post,
        conv_w=conv_w[l].astype(F32), conv_b=row(conv_b[l]),
        w_mq=w_mq[l].astype(BF16), w_mk=w_mk[l].astype(BF16), bif=bif,
        g_mhead=row(g_mhead[l]), wa2=wa2, b_a=row(b_a[l]), g_ghead=row(g_ghead[l]),
        w_pa=w_pa[l].astype(BF16), w_pb=w_pb[l].astype(BF16), w_o=w_o[l].astype(BF16),
        g_ffn2=row(g_ffn2[l]), w_ffn2_up=w_ffn2_up[l].astype(BF16), w_ffn2_down=w_ffn2_down[l].astype(BF16),
        g_final=row(g_final))


def _mixer(x1, b, t, state, w, *, tm, pre_tile, scan_tile, lm, lg, sub, time_minor):
    n = b * t
    if state is None:
        cst = jnp.zeros((b, SUBLANES, M_WIDTH), F32)
        scan_state = None
    else:
        conv0, c0, n0, m0, s0 = state
        cst = jnp.concatenate(
            [jnp.zeros((b, SUBLANES - (CONV_W - 1), M_WIDTH), F32), conv0.astype(F32)], axis=1)
        m0p = jnp.concatenate([m0, jnp.zeros((b, LANES - M_HEADS), F32)], axis=1).reshape(b, 1, LANES)
        scan_state = (c0, n0.reshape(b, M_HEADS, 1, M_HEAD_DIM), m0p, s0)
    qm, km, vm, gif, qg, kg, la, vg, cout = _pre(
        x1.reshape(b, t, D_MODEL), cst, w["g_mix"], w["w_pre"], w["w_small"], w["conv_w"], w["conv_b"],
        w["w_mq"], w["w_mk"], w["bif"], w["wa2"], w["b_a"], bt=pre_tile[0], tb=pre_tile[1], sub=lm, lg=lg,
        time_minor=time_minor)
    hm, og, c_new, n_new, m_new, s_new = _scan(
        qm, km, vm, gif, qg, kg, la, vg, w["g_ghead"], scan_state,
        bt=scan_tile[0], tb=scan_tile[1], lm=lm, lg=lg, sub=sub, time_minor=time_minor)
    x2 = _post(x1, hm if time_minor else hm.reshape(n, M_WIDTH), og.reshape(n, G_VW), w["g_mix"],
               w["g_mhead"], w["w_post"], w["w_pa"], w["w_pb"], w["w_o"], tm=tm, time_minor=time_minor)
    m_heads = m_new[:, :M_HEADS, 0] if time_minor else m_new[:, 0, :M_HEADS]
    return (x2,
            cout[:, SUBLANES - (CONV_W - 1):, :][None],
            c_new[None],
            n_new.reshape(b, M_HEADS, M_HEAD_DIM)[None],
            m_heads[None],
            s_new[None])


FFN_ROWS = 1024
DECODE_SEQS_PER_STEP = 4
PREFILL_SEQS_PER_STEP = 2


def _tiles(b, t):
    n = b * t
    tm = math.gcd(n, 512)
    if t % 256 == 0:
        return dict(tm=math.gcd(t, 512), pre_tile=(1, math.gcd(t, 512)), scan_tile=(math.gcd(b, PREFILL_SEQS_PER_STEP), 256),
                    lm=256, lg=128, sub=16, time_minor=True)
    assert t <= 64 and t % SUBLANES == 0
    bt = max(1, min(b, 512 // t))
    while b % bt:
        bt -= 1
    return dict(tm=tm, pre_tile=(bt, t), scan_tile=(math.gcd(b, DECODE_SEQS_PER_STEP), t), lm=t, lg=t,
                sub=min(16, t), time_minor=False)


def kernel(x_prompt, x_sample, state_conv, state_mlstm_C, state_mlstm_n, state_mlstm_m, state_gla_S,
           g_ffn1, w_ffn1_up, w_ffn1_down, g_mix, w_in, conv_w, conv_b, w_mq, w_mk, b_if, g_mhead,
           w_a2, b_a, g_ghead, w_pa, w_pb, w_o, g_ffn2, w_ffn2_up, w_ffn2_down, g_final):
    w = _pack_weights(g_ffn1, w_ffn1_up, w_ffn1_down, g_mix, w_in, conv_w, conv_b, w_mq, w_mk, b_if,
                      g_mhead, w_a2, b_a, g_ghead, w_pa, w_pb, w_o, g_ffn2, w_ffn2_up, w_ffn2_down,
                      g_final)
    bp, tp, _ = x_prompt.shape
    bs, ts, _ = x_sample.shape
    tm_ffn = math.gcd(math.gcd(bp * tp, bs * ts), FFN_ROWS)
    x1_p, x1_s = _ffn(x_prompt.reshape(bp * tp, D_MODEL), x_sample.reshape(bs * ts, D_MODEL),
                      w["g_ffn1"], w["w_ffn1_up"], w["w_ffn1_down"], w["g_final"], final_norm=False, tm=tm_ffn)
    out_p = _mixer(x1_p, bp, tp, None, w, **_tiles(bp, tp))
    state = (state_conv[0], state_mlstm_C[0], state_mlstm_n[0], state_mlstm_m[0], state_gla_S[0])
    out_s = _mixer(x1_s, bs, ts, state, w, **_tiles(bs, ts))
    y_p, y_s = _ffn(out_p[0], out_s[0], w["g_ffn2"], w["w_ffn2_up"], w["w_ffn2_down"], w["g_final"],
                    final_norm=True, tm=tm_ffn)
    return (y_p.reshape(bp, tp, D_MODEL), y_s.reshape(bs, ts, D_MODEL)) + out_p[1:] + out_s[1:]
```

```python
import functools
import itertools
import math
import types

import jax
import jax.numpy as jnp
from jax import lax
from jax.experimental import pallas as pl
from jax.experimental.pallas import tpu as pltpu

D_MODEL = 1024
M_HEADS = 4
M_HEAD_DIM = 256
M_WIDTH = 1024
CONV_W = 4
G_HEADS = 4
G_DK = 128
G_DV = 256
G_KW = 512
G_VW = 1024
G_RANK = 16
G_TAU = 16.0
D_FF = 2816
EPS = 1e-6

LANES = 128
SUBLANES = 8
BF16_ROWS = 16
VMEM_LIMIT = 56 * 1024 * 1024

F32 = jnp.float32
BF16 = jnp.bfloat16
NEG_INF = float("-inf")


def _sigmoid(x):
    return 1.0 / (1.0 + jnp.exp(-x))


def _silu(x):
    return x * _sigmoid(x)


def _log_sigmoid(x):
    return jnp.minimum(x, 0.0) - jnp.log(1.0 + jnp.exp(-jnp.abs(x)))


def _rms(x, g):
    return x * lax.rsqrt(jnp.mean(x * x, axis=-1, keepdims=True) + EPS) * g


def _dot(a, b):
    return jnp.dot(a, b, preferred_element_type=F32)


def _dot_nt(a, b):
    return lax.dot_general(a, b, (((1,), (1,)), ((), ())), preferred_element_type=F32)


def _dot_tn(a, b):
    return lax.dot_general(a, b, (((0,), (0,)), ((), ())), preferred_element_type=F32)


def _const_spec(shape):
    nd = len(shape)
    return pl.BlockSpec(shape, lambda *_: (0,) * nd, pipeline_mode=pl.Buffered(1))


FF_CHUNK = 256


def _ffn_kernel(xa_ref, xb_ref, g_ref, wup_ref, wdown_ref, gfin_ref, oa_ref, ob_ref, h_ref, *,
                final_norm, tiles_a):
    def half_step(x_ref, o_ref):
        x = x_ref[...]
        xn = _rms(x, g_ref[...]).astype(BF16)
        for c in range(D_FF // FF_CHUNK):
            a = _dot(xn, wup_ref[:, c * FF_CHUNK:(c + 1) * FF_CHUNK])
            g = _dot(xn, wup_ref[:, D_FF + c * FF_CHUNK:D_FF + (c + 1) * FF_CHUNK])
            h_ref[:, c * FF_CHUNK:(c + 1) * FF_CHUNK] = (_silu(g) * a).astype(BF16)
        y = x + 0.5 * _dot(h_ref[...], wdown_ref[...])
        o_ref[...] = _rms(y, gfin_ref[...]) if final_norm else y

    on_a = pl.program_id(0) < tiles_a
    pl.when(on_a)(lambda: half_step(xa_ref, oa_ref))
    pl.when(jnp.logical_not(on_a))(lambda: half_step(xb_ref, ob_ref))


def _ffn(xa, xb, g, wup, wdown, gfin, *, final_norm, tm):
    na, nb = xa.shape[0], xb.shape[0]
    assert na % tm == 0 and nb % tm == 0
    tiles_a, tiles_b = na // tm, nb // tm
    spec_a = pl.BlockSpec((tm, D_MODEL), lambda i: (jnp.minimum(i, tiles_a - 1), 0))
    spec_b = pl.BlockSpec((tm, D_MODEL), lambda i: (jnp.maximum(i - tiles_a, 0), 0),
                          **(dict(pipeline_mode=pl.Buffered(1)) if tiles_b == 1 else {}))
    return pl.pallas_call(
        functools.partial(_ffn_kernel, final_norm=final_norm, tiles_a=tiles_a),
        grid=(tiles_a + tiles_b,),
        in_specs=[
            spec_a, spec_b,
            _const_spec((1, D_MODEL)),
            _const_spec((D_MODEL, 2 * D_FF)),
            _const_spec((D_FF, D_MODEL)),
            _const_spec((1, D_MODEL)),
        ],
        out_specs=[spec_a, spec_b],
        out_shape=[jax.ShapeDtypeStruct((na, D_MODEL), F32), jax.ShapeDtypeStruct((nb, D_MODEL), F32)],
        scratch_shapes=[pltpu.VMEM((tm, D_FF), BF16)],
        compiler_params=pltpu.CompilerParams(
            dimension_semantics=("arbitrary",), vmem_limit_bytes=VMEM_LIMIT),
        name="ffn_final" if final_norm else "ffn",
    )(xa, xb, g, wup, wdown, gfin)


PRE_U, PRE_V, PRE_QG, PRE_KG, PRE_VG = 0, 1024, 2048, 2560, 3072
PRE_WIDTH = 4096
A_LANE0 = 8


def _pre_kernel(x_ref, cst_ref, g_ref, w_ref, ws_ref, cw_ref, cb_ref, wq_ref, wk_ref, bif_ref,
                wa2_ref, ba_ref,
                qm_ref, km_ref, vm_ref, gif_ref, qg_ref, kg_ref, la_ref, vg_ref, cout_ref,
                cbuf_ref, *, bt, tb, sub, lg, time_minor):
    t = pl.program_id(1)
    m = bt * sub

    @pl.when(t == 0)
    def _():
        cbuf_ref[:, 0:SUBLANES, :] = cst_ref[...]

    @pl.when(t > 0)
    def _():
        cbuf_ref[:, 0:SUBLANES, :] = cbuf_ref[:, tb:tb + SUBLANES, :]

    for j in range(tb // sub):
        r0 = j * sub
        rows = slice(r0, r0 + sub)
        to_rows = lambda y, w: y.reshape(bt, sub, w)
        xn = _rms(x_ref[:, rows, :].reshape(m, D_MODEL), g_ref[...]).astype(BF16)
        small = _dot(xn, ws_ref[...])

        u = _dot(xn, w_ref[:, PRE_U:PRE_U + M_WIDTH])
        cbuf_ref[:, SUBLANES + r0:SUBLANES + r0 + sub, :] = to_rows(u, M_WIDTH)
        c = cb_ref[...].reshape(1, 1, M_WIDTH)
        for tap in range(CONV_W):
            off = SUBLANES - (CONV_W - 1) + tap + r0
            c = c + cbuf_ref[:, off:off + sub, :] * cw_ref[tap:tap + 1, :].reshape(1, 1, M_WIDTH)
        ch = _silu(c).reshape(m, M_WIDTH).astype(BF16)

        vm = _dot(xn, w_ref[:, PRE_V:PRE_V + M_WIDTH])
        gates = small + bif_ref[...]
        lane = lax.broadcasted_iota(jnp.int32, gates.shape, 1)
        gif = jnp.where(lane < M_HEADS, gates, jnp.where(lane < 2 * M_HEADS, _log_sigmoid(gates), 0.0))
        if time_minor:
            b = pltpu.roll(_scan_rows(gif, jnp.add, 0.0), LANES - M_HEADS, axis=1)
            g = gif - b
            gmax = _scan_rows(g, jnp.maximum, NEG_INF)
            packed = jnp.where(lane < M_HEADS, g,
                               jnp.where(lane < 2 * M_HEADS, pltpu.roll(gmax, M_HEADS, axis=1),
                                         jnp.where(lane < 3 * M_HEADS, pltpu.roll(b, 2 * M_HEADS, axis=1), 0.0)))
            gif_ref[0, j] = jnp.transpose(packed)[0:2 * SUBLANES, :]
        else:
            gif_ref[:, rows, :] = to_rows(gif, LANES)
            vm_ref[:, rows, :] = to_rows(vm.astype(BF16), M_WIDTH)

        def decay_columns(h):
            cols = slice(h * G_DK, (h + 1) * G_DK)
            a2 = _dot(small.astype(BF16), wa2_ref[:, cols]) + ba_ref[:, cols]
            la_ref[:, rows, cols] = to_rows(
                _scan_rows(_log_sigmoid(a2) * (1.0 / G_TAU), jnp.add, 0.0, segment=lg), G_DK)

        qg_ref[:, rows, :] = to_rows(
            (_dot(xn, w_ref[:, PRE_QG:PRE_QG + G_KW]) * (G_DK ** -0.5)).astype(BF16), G_KW)
        decay_columns(0)
        kg_ref[:, rows, :] = to_rows(_dot(xn, w_ref[:, PRE_KG:PRE_KG + G_KW]).astype(BF16), G_KW)
        decay_columns(1)
        half = G_VW // 2
        for p in range(2):
            vg_ref[:, rows, p * half:(p + 1) * half] = to_rows(
                _dot(xn, w_ref[:, PRE_VG + p * half:PRE_VG + (p + 1) * half]).astype(BF16), half)
            decay_columns(2 + p)

        for h in range(M_HEADS):
            sl = slice(h * M_HEAD_DIM, (h + 1) * M_HEAD_DIM)
            qh = _dot(ch[:, sl], wq_ref[h])
            if time_minor:
                qm_ref[0, j, sl, :] = jnp.transpose(qh).astype(BF16)
                vm_ref[0, j, sl, :] = jnp.transpose(vm[:, sl]).astype(BF16)
            else:
                qm_ref[:, rows, sl] = to_rows(qh.astype(BF16), M_HEAD_DIM)
            km_ref[:, rows, sl] = to_rows(
                (_dot(ch[:, sl], wk_ref[h]) * (M_HEAD_DIM ** -0.5)).astype(BF16), M_HEAD_DIM)
    cout_ref[...] = cbuf_ref[:, tb:tb + SUBLANES, :]


def _pre(x, cst, g, w_pre, w_small, conv_w, conv_b, wq, wk, bif, wa2, ba, *, bt, tb, sub, lg, time_minor):
    b, t, _ = x.shape
    assert b % bt == 0 and t % tb == 0 and tb % sub == 0 and sub % SUBLANES == 0 and sub % lg == 0
    assert not time_minor or (bt == 1 and sub % LANES == 0)
    tok = lambda w: pl.BlockSpec((bt, tb, w), lambda i, j: (i, j, 0))
    per_seq = pl.BlockSpec((bt, SUBLANES, M_WIDTH), lambda i, j: (i, 0, 0))
    sds = lambda w, dt: jax.ShapeDtypeStruct((b, t, w), dt)
    if time_minor:
        tok_t = lambda w: pl.BlockSpec((1, tb // sub, w, sub), lambda i, j: (i, j, 0, 0))
        sds_t = lambda w, dt: jax.ShapeDtypeStruct((b, t // sub, w, sub), dt)
        qv_spec, qv_shape = tok_t(M_WIDTH), sds_t(M_WIDTH, BF16)
        gif_spec, gif_shape = tok_t(2 * SUBLANES), sds_t(2 * SUBLANES, F32)
    else:
        qv_spec, qv_shape = tok(M_WIDTH), sds(M_WIDTH, BF16)
        gif_spec, gif_shape = tok(LANES), sds(LANES, F32)
    return pl.pallas_call(
        functools.partial(_pre_kernel, bt=bt, tb=tb, sub=sub, lg=lg, time_minor=time_minor),
        grid=(b // bt, t // tb),
        in_specs=[
            tok(D_MODEL), per_seq,
            _const_spec((1, D_MODEL)),
            _const_spec((D_MODEL, PRE_WIDTH)),
            _const_spec((D_MODEL, LANES)),
            _const_spec((CONV_W, M_WIDTH)),
            _const_spec((1, M_WIDTH)),
            _const_spec((M_HEADS, M_HEAD_DIM, M_HEAD_DIM)),
            _const_spec((M_HEADS, M_HEAD_DIM, M_HEAD_DIM)),
            _const_spec((1, LANES)),
            _const_spec((LANES, G_KW)),
            _const_spec((1, G_KW)),
        ],
        out_specs=[qv_spec, tok(M_WIDTH), qv_spec, gif_spec, tok(G_KW), tok(G_KW),
                   tok(G_KW), tok(G_VW), per_seq],
        out_shape=[qv_shape, sds(M_WIDTH, BF16), qv_shape, gif_shape,
                   sds(G_KW, BF16), sds(G_KW, BF16), sds(G_KW, F32), sds(G_VW, BF16),
                   jax.ShapeDtypeStruct((b, SUBLANES, M_WIDTH), F32)],
        scratch_shapes=[pltpu.VMEM((bt, tb + SUBLANES, M_WIDTH), F32)],
        compiler_params=pltpu.CompilerParams(
            dimension_semantics=("parallel", "arbitrary"), vmem_limit_bytes=VMEM_LIMIT),
        name="mixer_pre",
    )(x, cst, g, w_pre, w_small, conv_w, conv_b, wq, wk, bif, wa2, ba)


def _scan_rows(x, op, fill, segment=None):
    n = segment or x.shape[0]
    row = lax.broadcasted_iota(jnp.int32, x.shape, 0) % n
    s = 1
    while s < n:
        x = op(x, jnp.where(row >= s, pltpu.roll(x, s, axis=0), fill))
        s *= 2
    return x


def _col_of_row(row):
    return jnp.transpose(jnp.broadcast_to(row, (SUBLANES, LANES)))[:, 0:1]


def _mlstm_chunk_tm(seqs, *, lm):
    reps = lm // LANES
    head_row = lax.broadcasted_iota(jnp.int32, (SUBLANES, LANES), 0) < M_HEADS
    src = lax.broadcasted_iota(jnp.int32, (lm, lm), 0)
    dst = lax.broadcasted_iota(jnp.int32, (lm, lm), 1)
    causal = src <= dst
    sl = [slice(h * M_HEAD_DIM, (h + 1) * M_HEAD_DIM) for h in range(M_HEADS)]
    row = lambda x, h: x[h:h + 1, :]
    gates = []
    for sq in seqs:
        g = sq.gift[0:SUBLANES]
        gmax = pltpu.roll(g, M_HEADS, axis=0)
        b = sq.gift[SUBLANES:2 * SUBLANES]
        m_prev = jnp.concatenate([sq.m_src[...]] * reps, axis=1)
        mx = jnp.maximum(m_prev, gmax)
        mx_last = jnp.broadcast_to(mx[:, lm - 1:lm], (SUBLANES, lm))
        gates.append(dict(
            mx=mx,
            inter=jnp.exp(m_prev - mx),
            nfloor=jnp.exp(-(b + mx)),
            w_in=jnp.exp(g - mx_last),
            decay=jnp.exp(m_prev - mx_last),
            g_col=jnp.transpose(g)))
        m_new = jnp.broadcast_to(b[:, lm - 1:lm], (SUBLANES, lm)) + mx_last
        sq.m_dst[...] = jnp.where(head_row, m_new[:, :LANES], 0.0)
    yield
    units = [(sq, gt, h) for sq, gt in zip(seqs, gates) for h in range(M_HEADS)]
    c_old = [sq.c_src[h] for sq, _, h in units]
    n_old = [sq.n_src[h] for sq, _, h in units]
    cn = [jnp.concatenate([c_old[u].astype(BF16),
                           jnp.broadcast_to(n_old[u].astype(BF16), (BF16_ROWS, M_HEAD_DIM))], axis=0)
          for u in range(len(units))]
    cq = [_dot(cn[u], sq.qt[sl[h], :]) for u, (sq, _, h) in enumerate(units)]
    yield
    dm = [jnp.exp(jnp.where(causal, gt["g_col"][:, h:h + 1] - row(gt["mx"], h), NEG_INF))
          for _, gt, h in units]
    st = [_dot(sq.k[:, sl[h]], sq.qt[sl[h], :]) * dm[u]
          for u, (sq, _, h) in enumerate(units)]
    yield
    num = [row(gt["inter"], h) * cq[u][:M_HEAD_DIM] + _dot(sq.vt[sl[h], :], st[u].astype(BF16))
           for u, (sq, gt, h) in enumerate(units)]
    yield
    for u, (sq, gt, h) in enumerate(units):
        den = (row(gt["inter"], h) * cq[u][M_HEAD_DIM:M_HEAD_DIM + 1]
               + jnp.sum(st[u], axis=0, keepdims=True))
        rcp = 1.0 / jnp.maximum(jnp.abs(den), row(gt["nfloor"], h))
        ms = jnp.mean(num[u] * num[u], axis=0, keepdims=True)
        sq.hm_store(h, num[u] * (rcp * lax.rsqrt(rcp * rcp * ms + EPS)))
    yield
    for u, (sq, gt, h) in enumerate(units):
        w_h = row(gt["w_in"], h)
        dec_h = jnp.concatenate([row(gt["decay"], h)[:, :LANES]] * (M_HEAD_DIM // LANES), axis=1)
        wv = jnp.concatenate([(sq.vt[sl[h], :].astype(F32) * w_h).astype(BF16),
                              jnp.broadcast_to(w_h.astype(BF16), (BF16_ROWS, lm))], axis=0)
        upd = _dot(wv, sq.k[:, sl[h]])
        sq.c_dst[h] = dec_h * c_old[u] + upd[:M_HEAD_DIM]
        sq.n_dst[h] = dec_h * n_old[u] + upd[M_HEAD_DIM:M_HEAD_DIM + 1]


def _mlstm_chunk(seqs, *, lm):
    row = lax.broadcasted_iota(jnp.int32, (lm, lm), 0)
    col = lax.broadcasted_iota(jnp.int32, (lm, lm), 1)
    causal = col <= row
    sl = [slice(h * M_HEAD_DIM, (h + 1) * M_HEAD_DIM) for h in range(M_HEADS)]
    gates = []
    for sq in seqs:
        cs = _scan_rows(sq.gif, jnp.add, 0.0)
        b = pltpu.roll(cs, LANES - M_HEADS, axis=1)
        g = sq.gif - b
        m_prev = sq.m_src[...]
        mx = jnp.maximum(m_prev, _scan_rows(g, jnp.maximum, NEG_INF))
        mx_last = mx[lm - 1:lm, :]
        gates.append(dict(
            mx=mx,
            inter=jnp.exp(m_prev - mx),
            nfloor=jnp.exp(-(b + mx)),
            g_t=jnp.transpose(g),
            w_in=jnp.exp(g - mx_last),
            decay=jnp.exp(m_prev - mx_last)))
        sq.m_dst[...] = b[lm - 1:lm, :] + mx_last
    yield
    units = [(sq, gt, h) for sq, gt in zip(seqs, gates) for h in range(M_HEADS)]
    col_of = lambda x, h: x[:, h:h + 1]
    c_old = [sq.c_src[h] for sq, _, h in units]
    n_old = [sq.n_src[h] for sq, _, h in units]
    dm = [jnp.exp(jnp.where(causal, gt["g_t"][h:h + 1, :] - col_of(gt["mx"], h), NEG_INF))
          for _, gt, h in units]
    qn = [jnp.sum(sq.q[:, sl[h]].astype(F32) * n_old[u], axis=-1, keepdims=True)
          for u, (sq, _, h) in enumerate(units)]
    s = [_dot_nt(sq.q[:, sl[h]], sq.k[:, sl[h]]) * dm[u] for u, (sq, _, h) in enumerate(units)]
    yield
    num = [col_of(gt["inter"], h) * _dot_nt(sq.q[:, sl[h]], c_old[u].astype(BF16))
           + _dot(s[u].astype(BF16), sq.v[:, sl[h]]) for u, (sq, gt, h) in enumerate(units)]
    yield
    for u, (sq, gt, h) in enumerate(units):
        den = col_of(gt["inter"], h) * qn[u] + jnp.sum(s[u], axis=-1, keepdims=True)
        rcp = 1.0 / jnp.maximum(jnp.abs(den), col_of(gt["nfloor"], h))
        ms = jnp.mean(num[u] * num[u], axis=-1, keepdims=True)
        sq.hm_store(h, num[u] * (rcp * lax.rsqrt(rcp * rcp * ms + EPS)))
    yield
    for u, (sq, gt, h) in enumerate(units):
        w_h = col_of(gt["w_in"], h)
        dec_h = col_of(gt["decay"], h)
        wv = (w_h * sq.v[:, sl[h]].astype(F32)).astype(BF16)
        sq.c_dst[h] = dec_h * c_old[u] + _dot_tn(wv, sq.k[:, sl[h]])
        sq.n_dst[h] = dec_h * n_old[u] + jnp.sum(w_h * sq.k[:, sl[h]].astype(F32), axis=0, keepdims=True)


GLA_SAFE_SPAN = 40.0


def _gla_scores_guarded(qh, kh, a, *, lg, sub):
    row = lax.broadcasted_iota(jnp.int32, (lg, lg), 0)
    col = lax.broadcasted_iota(jnp.int32, (lg, lg), 1)
    att = jnp.zeros((lg, lg), F32)
    c = sub
    while c < lg:
        nblk = lg // (2 * c)
        a_ref = jnp.concatenate(
            [jnp.broadcast_to(a[2 * c * i + c - 1:2 * c * i + c, :], (2 * c, G_DK))
             for i in range(nblk)], axis=0)
        qt = (qh * jnp.exp(jnp.minimum(a - a_ref, 0.0))).astype(BF16)
        kt = (kh * jnp.exp(jnp.minimum(a_ref - a, 0.0))).astype(BF16)
        valid = (row // (2 * c) == col // (2 * c)) & (row % (2 * c) >= c) & (col % (2 * c) < c)
        att = att + jnp.where(valid, _dot_nt(qt, kt), 0.0)
        c *= 2
    rows = lax.broadcasted_iota(jnp.int32, (sub, lg), 0)
    cols = lax.broadcasted_iota(jnp.int32, (sub, lg), 1)
    blocks = []
    for blk in range(lg // sub):
        r0 = blk * sub
        a_b = a[r0:r0 + sub, :]
        q_b = qh[r0:r0 + sub, :]
        acc = jnp.zeros((sub, lg), F32)
        for s in range(sub):
            e = jnp.exp(jnp.minimum(a_b - a[r0 + s:r0 + s + 1, :], 0.0))
            p = jnp.sum(q_b * e * kh[r0 + s:r0 + s + 1, :], axis=-1, keepdims=True)
            acc = jnp.where((cols == r0 + s) & (rows >= s), p, acc)
        blocks.append(acc)
    return att + jnp.concatenate(blocks, axis=0)


def _gla_chunk_guarded(seqs, gh_ref, *, lg, sub):
    ks = [slice(h * G_DK, (h + 1) * G_DK) for h in range(G_HEADS)]
    vs = [slice(h * G_DV, (h + 1) * G_DV) for h in range(G_HEADS)]
    pre = []
    for sq in seqs:
        a_last = sq.a[lg - 1:lg, :]
        qf = sq.qg.astype(F32)
        kf = sq.kg.astype(F32)
        pre.append(dict(a=sq.a, a_last=a_last, qf=qf, kf=kf,
                        qs=(qf * jnp.exp(sq.a)).astype(BF16),
                        kl=(kf * jnp.exp(a_last - sq.a)).astype(BF16)))
    yield
    units = [(sq, p, h) for sq, p in zip(seqs, pre) for h in range(G_HEADS)]
    s_old = [sq.s_src[h] for sq, _, h in units]
    att = [_gla_scores_guarded(p["qf"][:, ks[h]], p["kf"][:, ks[h]], p["a"][:, ks[h]], lg=lg, sub=sub)
           for _, p, h in units]
    yield
    o = [_dot(p["qs"][:, ks[h]], s_old[u].astype(BF16)) + _dot(att[u].astype(BF16), sq.vg[:, vs[h]])
         for u, (sq, p, h) in enumerate(units)]
    yield
    for u, (sq, _, h) in enumerate(units):
        sq.og_store(h, _rms(o[u], gh_ref[:, vs[h]]))
    yield
    for u, (sq, p, h) in enumerate(units):
        sq.s_dst[h] = (jnp.exp(_col_of_row(p["a_last"][:, ks[h]])) * s_old[u]
                       + _dot_tn(p["kl"][:, ks[h]], sq.vg[:, vs[h]]))


def _gla_block_direct(seqs, gh_ref, *, tb, lg):
    ks = [slice(h * G_DK, (h + 1) * G_DK) for h in range(G_HEADS)]
    vs = [slice(h * G_DV, (h + 1) * G_DV) for h in range(G_HEADS)]
    chunks = [slice(c * lg, (c + 1) * lg) for c in range(tb // lg)]
    row = lax.broadcasted_iota(jnp.int32, (lg, lg), 0)
    col = lax.broadcasted_iota(jnp.int32, (lg, lg), 1)
    causal = col <= row
    pre = []
    for sq in seqs:
        a_last = [sq.a[r.stop - 1:r.stop, :] for r in chunks]
        a_last_rows = jnp.concatenate([jnp.broadcast_to(al, (lg, G_KW)) for al in a_last], axis=0)
        qf = sq.qg.astype(F32)
        kf = sq.kg.astype(F32)
        pre.append(dict(a_last=a_last,
                        qs=(qf * jnp.exp(sq.a)).astype(BF16),
                        kt=(kf * jnp.exp(-sq.a)).astype(BF16),
                        kl=(kf * jnp.exp(a_last_rows - sq.a)).astype(BF16)))
    units = [(sq, p, h) for sq, p in zip(seqs, pre) for h in range(G_HEADS)]
    att = [[jnp.where(causal, _dot_nt(p["qs"][r, ks[h]], p["kt"][r, ks[h]]), 0.0).astype(BF16)
            for r in chunks] for _, p, h in units]
    s_cur = [sq.s_src[h] for sq, _, h in units]
    for c, r in enumerate(chunks):
        o = [_dot(p["qs"][r, ks[h]], s_cur[u].astype(BF16)) + _dot(att[u][c], sq.vg[r, vs[h]])
             for u, (sq, p, h) in enumerate(units)]
        for u, (sq, _, h) in enumerate(units):
            sq.og_rows_store(r, h, _rms(o[u], gh_ref[:, vs[h]]))
        s_cur = [jnp.exp(_col_of_row(p["a_last"][c][:, ks[h]])) * s_cur[u]
                 + _dot_tn(p["kl"][r, ks[h]], sq.vg[r, vs[h]])
                 for u, (sq, p, h) in enumerate(units)]
    for u, (sq, _, h) in enumerate(units):
        sq.s_dst[h] = s_cur[u]


def _interleave(*stage_generators):
    for _ in itertools.zip_longest(*stage_generators):
        pass


def _scan_kernel(*refs, pbt, tb, lg, sub, sbt, ts):
    (pqm, pkm, pvm, pgif, pqg, pkg, pla, pvg,
     sqm, skm, svm, sgif, sqg, skg, sla, svg, ggh_ref,
     sc_in, sn_in, sm_in, ss_in,
     phm, pog, pc_out, pn_out, pm_out, ps_out,
     shm, sog, sc_out, sn_out, sm_out, ss_out,
     c_scr, n_scr, m_scr, s_scr) = refs
    t = pl.program_id(1)
    nt = pl.num_programs(1)
    carried = (c_scr, n_scr, m_scr, s_scr)

    @pl.when(t == 0)
    def _():
        for ref in carried:
            ref[...] = jnp.zeros(ref.shape, F32)

    def long_record(bi, rows):
        def hm_store(h, val):
            phm[bi, 0, h * M_HEAD_DIM:(h + 1) * M_HEAD_DIM, :] = val

        def og_rows_store(r, h, val):
            pog[bi, r, h * G_DV:(h + 1) * G_DV] = val

        return types.SimpleNamespace(
            qt=pqm[bi, 0], k=pkm[bi], vt=pvm[bi, 0], gift=pgif[bi, 0],
            qg=pqg[bi, rows, :], kg=pkg[bi, rows, :], vg=pvg[bi, rows, :], a=pla[bi, rows, :],
            hm_store=hm_store, og_rows_store=og_rows_store, og_store=functools.partial(og_rows_store, rows),
            c_src=c_scr.at[bi], n_src=n_scr.at[bi], m_src=m_scr.at[bi], s_src=s_scr.at[bi],
            c_dst=c_scr.at[bi], n_dst=n_scr.at[bi], m_dst=m_scr.at[bi], s_dst=s_scr.at[bi])

    def short_record(bi):
        def hm_store(h, val):
            shm[bi, :, h * M_HEAD_DIM:(h + 1) * M_HEAD_DIM] = val

        def og_store(h, val):
            sog[bi, :, h * G_DV:(h + 1) * G_DV] = val

        return types.SimpleNamespace(
            q=sqm[bi], k=skm[bi], v=svm[bi], gif=sgif[bi],
            qg=sqg[bi], kg=skg[bi], vg=svg[bi], a=sla[bi],
            hm_store=hm_store, og_store=og_store,
            c_src=sc_in.at[bi], n_src=sn_in.at[bi], m_src=sm_in.at[bi], s_src=ss_in.at[bi],
            c_dst=sc_out.at[bi], n_dst=sn_out.at[bi], m_dst=sm_out.at[bi], s_dst=ss_out.at[bi])

    whole = slice(0, tb)
    long_seqs = [long_record(bi, whole) for bi in range(pbt)]
    short_seqs = [short_record(bi) for bi in range(sbt)]
    _interleave(_mlstm_chunk_tm(long_seqs, lm=tb),
                _mlstm_chunk(short_seqs, lm=ts),
                _gla_chunk_guarded(short_seqs, ggh_ref, lg=ts, sub=ts))

    chunk_decay = [pla[:, c * lg + lg - 1:(c + 1) * lg, :] for c in range(tb // lg)]
    span_ok = jnp.min(functools.reduce(jnp.minimum, chunk_decay)) >= -GLA_SAFE_SPAN

    @pl.when(span_ok)
    def _():
        _gla_block_direct(long_seqs, ggh_ref, tb=tb, lg=lg)

    @pl.when(jnp.logical_not(span_ok))
    def _():
        def body(ci, carry):
            rows = pl.ds(pl.multiple_of(ci * lg, lg), lg)
            _interleave(_gla_chunk_guarded([long_record(bi, rows) for bi in range(pbt)], ggh_ref,
                                           lg=lg, sub=sub))
            return carry

        lax.fori_loop(0, tb // lg, body, 0)

    @pl.when(t == nt - 1)
    def _():
        for out, ref in zip((pc_out, pn_out, pm_out, ps_out), carried):
            out[...] = ref[...]


def _scan(long_in, short_in, ggh, short_state, *, pbt, tb, lg, sub, sbt):
    bp, tp, _ = long_in[1].shape
    bs, ts, _ = short_in[1].shape
    nb, nt = bp // pbt, tp // tb
    assert bp % pbt == 0 and tp % tb == 0 and tb % lg == 0 and lg % sub == 0 and tb % LANES == 0
    assert bs == sbt * nb * nt and ts % SUBLANES == 0
    c_shape = (M_HEADS, M_HEAD_DIM, M_HEAD_DIM)
    n_shape = (M_HEADS, 1, M_HEAD_DIM)
    s_shape = (G_HEADS, G_DK, G_DV)
    long_states = [(pbt,) + shp for shp in (c_shape, n_shape, (SUBLANES, LANES), s_shape)]
    short_states = [(sbt,) + shp for shp in (c_shape, n_shape, (1, LANES), s_shape)]
    lead = lambda shp, idx: pl.BlockSpec(shp, lambda i, j, nd=len(shp): (idx(i, j),) + (0,) * (nd - 1))
    long_tok = lambda w: pl.BlockSpec((pbt, tb, w), lambda i, j: (i, j, 0))
    long_slab = lambda w: pl.BlockSpec((pbt, 1, w, tb), lambda i, j: (i, j, 0, 0))
    short_tok = lambda w: lead((sbt, ts, w), lambda i, j: i * nt + j)
    long_specs = [long_slab(M_WIDTH), long_tok(M_WIDTH), long_slab(M_WIDTH), long_slab(2 * SUBLANES),
                  long_tok(G_KW), long_tok(G_KW), long_tok(G_KW), long_tok(G_VW)]
    short_specs = [short_tok(M_WIDTH), short_tok(M_WIDTH), short_tok(M_WIDTH), short_tok(LANES),
                   short_tok(G_KW), short_tok(G_KW), short_tok(G_KW), short_tok(G_VW)]
    long_state_specs = [lead(shp, lambda i, j: i) for shp in long_states]
    short_state_specs = [lead(shp, lambda i, j: i * nt + j) for shp in short_states]
    f32 = lambda shp: jax.ShapeDtypeStruct(shp, F32)
    outs = pl.pallas_call(
        functools.partial(_scan_kernel, pbt=pbt, tb=tb, lg=lg, sub=sub, sbt=sbt, ts=ts),
        grid=(nb, nt),
        in_specs=long_specs + short_specs + [_const_spec((1, G_VW))] + short_state_specs,
        out_specs=([long_slab(M_WIDTH), long_tok(G_VW)] + long_state_specs
                   + [short_tok(M_WIDTH), short_tok(G_VW)] + short_state_specs),
        out_shape=([f32((bp, nt, M_WIDTH, tb)), f32((bp, tp, G_VW))]
                   + [f32((bp,) + shp[1:]) for shp in long_states]
                   + [f32((bs, ts, M_WIDTH)), f32((bs, ts, G_VW))]
                   + [f32((bs,) + shp[1:]) for shp in short_states]),
        scratch_shapes=[pltpu.VMEM(shp, F32) for shp in long_states],
        compiler_params=pltpu.CompilerParams(
            dimension_semantics=("arbitrary", "arbitrary"), vmem_limit_bytes=VMEM_LIMIT),
        name="mixer_scan",
    )(*long_in, *short_in, ggh, *short_state)
    return outs[:6], outs[6:]


POST_O, POST_R, POST_A, POST_B = 0, 1024, 2048, 3072


def _post_kernel(x_ref, hm_ref, og_ref, g_ref, gmh_ref, w_ref, wpa_ref, wpb_ref, wo_ref, o_ref, *,
                 time_minor):
    x = x_ref[...]
    xn = _rms(x, g_ref[...]).astype(BF16)
    gate = lambda off: _dot(xn, w_ref[:, off:off + D_MODEL])
    if time_minor:
        hm = jnp.concatenate(
            [jnp.concatenate([jnp.transpose(hm_ref[0, j, h * M_HEAD_DIM:(h + 1) * M_HEAD_DIM, :])
                              for h in range(M_HEADS)], axis=1)
             for j in range(hm_ref.shape[1])], axis=0)
    else:
        hm = hm_ref[...]
    hm = (_sigmoid(gate(POST_O)) * (hm * gmh_ref[...])).astype(BF16)
    og = (_silu(gate(POST_R)) * og_ref[...]).astype(BF16)
    y = (_sigmoid(gate(POST_A)) * _dot(hm, wpa_ref[...])
         + _sigmoid(gate(POST_B)) * _dot(og, wpb_ref[...]))
    o_ref[...] = x + _dot(y.astype(BF16), wo_ref[...])


def _post(x2d, hm, og2d, g, gmh, w_post, wpa, wpb, wo, *, tm, time_minor):
    n = x2d.shape[0]
    assert n % tm == 0
    tok = pl.BlockSpec((tm, D_MODEL), lambda i: (i, 0))
    if time_minor:
        nblk, blk = hm.shape[1], hm.shape[3]
        assert tm % blk == 0 and (nblk * blk) % tm == 0
        per_seq = nblk * blk // tm
        hm_spec = pl.BlockSpec((1, tm // blk, M_WIDTH, blk), lambda i: (i // per_seq, i % per_seq, 0, 0))
    else:
        hm_spec = tok
    return pl.pallas_call(
        functools.partial(_post_kernel, time_minor=time_minor),
        grid=(n // tm,),
        in_specs=[tok, hm_spec, tok, _const_spec((1, D_MODEL)), _const_spec((1, M_WIDTH)),
                  _const_spec((D_MODEL, 4 * D_MODEL)),
                  _const_spec((M_WIDTH, D_MODEL)), _const_spec((G_VW, D_MODEL)),
                  _const_spec((D_MODEL, D_MODEL))],
        out_specs=tok,
        out_shape=jax.ShapeDtypeStruct((n, D_MODEL), F32),
        compiler_params=pltpu.CompilerParams(
            dimension_semantics=("parallel",), vmem_limit_bytes=VMEM_LIMIT),
        name="mixer_post",
    )(x2d, hm, og2d, g, gmh, w_post, wpa, wpb, wo)


def _pack_weights(g_ffn1, w_ffn1_up, w_ffn1_down, g_mix, w_in, conv_w, conv_b, w_mq, w_mk, b_if,
                  g_mhead, w_a2, b_a, g_ghead, w_pa, w_pb, w_o, g_ffn2, w_ffn2_up, w_ffn2_down,
                  g_final):
    l = 0
    row = lambda v: v.reshape(1, -1).astype(F32)
    edges = [0]
    for wd in (M_WIDTH, M_WIDTH, M_WIDTH, 2 * M_HEADS, G_KW, G_KW, G_VW, G_VW, G_RANK, D_MODEL, D_MODEL):
        edges.append(edges[-1] + wd)
    seg = lambda i: w_in[l][:, edges[i]:edges[i + 1]]
    u_m, v_m, o_m, if_m, q_g, k_g, v_g, r_g, a_g, g_a, g_b = (seg(i) for i in range(11))
    w_pre = jnp.concatenate([u_m, v_m, q_g, k_g, v_g], axis=1).astype(BF16)
    w_small = jnp.concatenate(
        [if_m, a_g, jnp.zeros((D_MODEL, LANES - 2 * M_HEADS - G_RANK), F32)], axis=1).astype(BF16)
    w_post = jnp.concatenate([o_m, r_g, g_a, g_b], axis=1).astype(BF16)
    bif = jnp.concatenate([b_if[l].reshape(-1), jnp.zeros((LANES - 2 * M_HEADS,), F32)]).reshape(1, LANES)
    wa2 = jnp.zeros((LANES, G_KW), F32).at[A_LANE0:A_LANE0 + G_RANK].set(w_a2[l]).astype(BF16)
    return dict(
        g_ffn1=row(g_ffn1[l]), w_ffn1_up=w_ffn1_up[l].astype(BF16), w_ffn1_down=w_ffn1_down[l].astype(BF16),
        g_mix=row(g_mix[l]), w_pre=w_pre, w_small=w_small, w_post=w_post,
        conv_w=conv_w[l].astype(F32), conv_b=row(conv_b[l]),
        w_mq=w_mq[l].astype(BF16), w_mk=w_mk[l].astype(BF16), bif=bif,
        g_mhead=row(g_mhead[l]), wa2=wa2, b_a=row(b_a[l]), g_ghead=row(g_ghead[l]),
        w_pa=w_pa[l].astype(BF16), w_pb=w_pb[l].astype(BF16), w_o=w_o[l].astype(BF16),
        g_ffn2=row(g_ffn2[l]), w_ffn2_up=w_ffn2_up[l].astype(BF16), w_ffn2_down=w_ffn2_down[l].astype(BF16),
        g_final=row(g_final))


def _pre_group(x1, b, t, state, w, *, pre_tile, lm, lg, time_minor, **_):
    if state is None:
        cst = jnp.zeros((b, SUBLANES, M_WIDTH), F32)
        scan_state = None
    else:
        conv0, c0, n0, m0, s0 = state
        cst = jnp.concatenate(
            [jnp.zeros((b, SUBLANES - (CONV_W - 1), M_WIDTH), F32), conv0.astype(F32)], axis=1)
        m0p = jnp.concatenate([m0, jnp.zeros((b, LANES - M_HEADS), F32)], axis=1).reshape(b, 1, LANES)
        scan_state = (c0, n0.reshape(b, M_HEADS, 1, M_HEAD_DIM), m0p, s0)
    *per_token, cout = _pre(
        x1.reshape(b, t, D_MODEL), cst, w["g_mix"], w["w_pre"], w["w_small"], w["conv_w"], w["conv_b"],
        w["w_mq"], w["w_mk"], w["bif"], w["wa2"], w["b_a"], bt=pre_tile[0], tb=pre_tile[1], sub=lm, lg=lg,
        time_minor=time_minor)
    return per_token, cout[:, SUBLANES - (CONV_W - 1):, :][None], scan_state


def _post_group(x1, b, t, scanned, w, *, tm, time_minor, **_):
    hm, og, c_new, n_new, m_new, s_new = scanned
    n = b * t
    x2 = _post(x1, hm if time_minor else hm.reshape(n, M_WIDTH), og.reshape(n, G_VW), w["g_mix"],
               w["g_mhead"], w["w_post"], w["w_pa"], w["w_pb"], w["w_o"], tm=tm, time_minor=time_minor)
    m_heads = m_new[:, :M_HEADS, 0] if time_minor else m_new[:, 0, :M_HEADS]
    return x2, (c_new[None], n_new.reshape(b, M_HEADS, M_HEAD_DIM)[None], m_heads[None], s_new[None])


FFN_ROWS = 1024
PREFILL_SEQS_PER_STEP = 1


def _tiles(b, t):
    n = b * t
    tm = math.gcd(n, 512)
    if t % 256 == 0:
        return dict(tm=math.gcd(t, 512), pre_tile=(1, math.gcd(t, 512)), scan_tile=(math.gcd(b, PREFILL_SEQS_PER_STEP), 256),
                    lm=256, lg=128, sub=16, time_minor=True)
    assert t <= 64 and t % SUBLANES == 0
    bt = max(1, min(b, 512 // t))
    while b % bt:
        bt -= 1
    return dict(tm=tm, pre_tile=(bt, t), lm=t, lg=t, sub=min(16, t), time_minor=False)


def kernel(x_prompt, x_sample, state_conv, state_mlstm_C, state_mlstm_n, state_mlstm_m, state_gla_S,
           g_ffn1, w_ffn1_up, w_ffn1_down, g_mix, w_in, conv_w, conv_b, w_mq, w_mk, b_if, g_mhead,
           w_a2, b_a, g_ghead, w_pa, w_pb, w_o, g_ffn2, w_ffn2_up, w_ffn2_down, g_final):
    w = _pack_weights(g_ffn1, w_ffn1_up, w_ffn1_down, g_mix, w_in, conv_w, conv_b, w_mq, w_mk, b_if,
                      g_mhead, w_a2, b_a, g_ghead, w_pa, w_pb, w_o, g_ffn2, w_ffn2_up, w_ffn2_down,
                      g_final)
    bp, tp, _ = x_prompt.shape
    bs, ts, _ = x_sample.shape
    tm_ffn = math.gcd(math.gcd(bp * tp, bs * ts), FFN_ROWS)
    x1_p, x1_s = _ffn(x_prompt.reshape(bp * tp, D_MODEL), x_sample.reshape(bs * ts, D_MODEL),
                      w["g_ffn1"], w["w_ffn1_up"], w["w_ffn1_down"], w["g_final"], final_norm=False, tm=tm_ffn)
    tiles_p, tiles_s = _tiles(bp, tp), _tiles(bs, ts)
    assert tiles_p["time_minor"] and not tiles_s["time_minor"]
    state = (state_conv[0], state_mlstm_C[0], state_mlstm_n[0], state_mlstm_m[0], state_gla_S[0])
    tok_p, conv_p, _ = _pre_group(x1_p, bp, tp, None, w, **tiles_p)
    tok_s, conv_s, scan_state = _pre_group(x1_s, bs, ts, state, w, **tiles_s)
    pbt, tb = tiles_p["scan_tile"]
    steps = (bp // pbt) * (tp // tb)
    assert bs % steps == 0
    scanned_p, scanned_s = _scan(tok_p, tok_s, w["g_ghead"], scan_state, pbt=pbt, tb=tb, lg=tiles_p["lg"],
                                 sub=tiles_p["sub"], sbt=bs // steps)
    x2_p, states_p = _post_group(x1_p, bp, tp, scanned_p, w, **tiles_p)
    x2_s, states_s = _post_group(x1_s, bs, ts, scanned_s, w, **tiles_s)
    y_p, y_s = _ffn(x2_p, x2_s, w["g_ffn2"], w["w_ffn2_up"], w["w_ffn2_down"], w["g_final"],
                    final_norm=True, tm=tm_ffn)
    return ((y_p.reshape(bp, tp, D_MODEL), y_s.reshape(bs, ts, D_MODEL), conv_p) + states_p
            + (conv_s,) + states_s)
```

```python
import functools
import itertools
import math
import types

import jax
import jax.numpy as jnp
from jax import lax
from jax.experimental import pallas as pl
from jax.experimental.pallas import tpu as pltpu

D_MODEL = 1024
M_HEADS = 4
M_HEAD_DIM = 256
M_WIDTH = 1024
CONV_W = 4
G_HEADS = 4
G_DK = 128
G_DV = 256
G_KW = 512
G_VW = 1024
G_RANK = 16
G_TAU = 16.0
D_FF = 2816
EPS = 1e-6

LANES = 128
SUBLANES = 8
BF16_ROWS = 16
VMEM_LIMIT = 56 * 1024 * 1024

F32 = jnp.float32
BF16 = jnp.bfloat16
NEG_INF = float("-inf")


def _sigmoid(x):
    return 1.0 / (1.0 + jnp.exp(-x))


def _silu(x):
    return x * _sigmoid(x)


def _log_sigmoid(x):
    return jnp.minimum(x, 0.0) - jnp.log(1.0 + jnp.exp(-jnp.abs(x)))


def _rms(x, g):
    return x * lax.rsqrt(jnp.mean(x * x, axis=-1, keepdims=True) + EPS) * g


def _dot(a, b):
    return jnp.dot(a, b, preferred_element_type=F32)


def _dot_nt(a, b):
    return lax.dot_general(a, b, (((1,), (1,)), ((), ())), preferred_element_type=F32)


def _dot_tn(a, b):
    return lax.dot_general(a, b, (((0,), (0,)), ((), ())), preferred_element_type=F32)


def _const_spec(shape):
    nd = len(shape)
    return pl.BlockSpec(shape, lambda *_: (0,) * nd, pipeline_mode=pl.Buffered(1))


FF_CHUNK = 256


def _ffn_kernel(xa_ref, xb_ref, g_ref, wup_ref, wdown_ref, gfin_ref, oa_ref, ob_ref, h_ref, *,
                final_norm, tiles_a):
    def half_step(x_ref, o_ref):
        x = x_ref[...]
        xn = _rms(x, g_ref[...]).astype(BF16)
        for c in range(D_FF // FF_CHUNK):
            a = _dot(xn, wup_ref[:, c * FF_CHUNK:(c + 1) * FF_CHUNK])
            g = _dot(xn, wup_ref[:, D_FF + c * FF_CHUNK:D_FF + (c + 1) * FF_CHUNK])
            h_ref[:, c * FF_CHUNK:(c + 1) * FF_CHUNK] = (_silu(g) * a).astype(BF16)
        y = x + 0.5 * _dot(h_ref[...], wdown_ref[...])
        o_ref[...] = _rms(y, gfin_ref[...]) if final_norm else y

    on_a = pl.program_id(0) < tiles_a
    pl.when(on_a)(lambda: half_step(xa_ref, oa_ref))
    pl.when(jnp.logical_not(on_a))(lambda: half_step(xb_ref, ob_ref))


def _ffn(xa, xb, g, wup, wdown, gfin, *, final_norm, tm):
    na, nb = xa.shape[0], xb.shape[0]
    assert na % tm == 0 and nb % tm == 0
    tiles_a, tiles_b = na // tm, nb // tm
    spec_a = pl.BlockSpec((tm, D_MODEL), lambda i: (jnp.minimum(i, tiles_a - 1), 0))
    spec_b = pl.BlockSpec((tm, D_MODEL), lambda i: (jnp.maximum(i - tiles_a, 0), 0),
                          **(dict(pipeline_mode=pl.Buffered(1)) if tiles_b == 1 else {}))
    return pl.pallas_call(
        functools.partial(_ffn_kernel, final_norm=final_norm, tiles_a=tiles_a),
        grid=(tiles_a + tiles_b,),
        in_specs=[
            spec_a, spec_b,
            _const_spec((1, D_MODEL)),
            _const_spec((D_MODEL, 2 * D_FF)),
            _const_spec((D_FF, D_MODEL)),
            _const_spec((1, D_MODEL)),
        ],
        out_specs=[spec_a, spec_b],
        out_shape=[jax.ShapeDtypeStruct((na, D_MODEL), F32), jax.ShapeDtypeStruct((nb, D_MODEL), F32)],
        scratch_shapes=[pltpu.VMEM((tm, D_FF), BF16)],
        compiler_params=pltpu.CompilerParams(
            dimension_semantics=("arbitrary",), vmem_limit_bytes=VMEM_LIMIT),
        name="ffn_final" if final_norm else "ffn",
    )(xa, xb, g, wup, wdown, gfin)


def _interleave(*stage_generators):
    for _ in itertools.zip_longest(*stage_generators):
        pass


PRE_U, PRE_V, PRE_QG, PRE_KG, PRE_VG = 0, 1024, 2048, 2560, 3072
PRE_WIDTH = 4096
A_LANE0 = 8


def _pre_sub_block(x_ref, x_rows, cbuf_ref, w, out, o_rows, o_slab, *, bt, sub, lg, time_minor):
    m = bt * sub
    r0 = x_rows.start
    to_rows = lambda y, width: y.reshape(bt, sub, width)
    xn = _rms(x_ref[:, x_rows, :].reshape(m, D_MODEL), w.g[...]).astype(BF16)
    small = _dot(xn, w.small[...])

    u = _dot(xn, w.proj[:, PRE_U:PRE_U + M_WIDTH])
    cbuf_ref[:, SUBLANES + r0:SUBLANES + r0 + sub, :] = to_rows(u, M_WIDTH)
    c = w.conv_b[...].reshape(1, 1, M_WIDTH)
    for tap in range(CONV_W):
        off = SUBLANES - (CONV_W - 1) + tap + r0
        c = c + cbuf_ref[:, off:off + sub, :] * w.conv_w[tap:tap + 1, :].reshape(1, 1, M_WIDTH)
    ch = _silu(c).reshape(m, M_WIDTH).astype(BF16)
    yield

    vm = _dot(xn, w.proj[:, PRE_V:PRE_V + M_WIDTH])
    gates = small + w.bif[...]
    lane = lax.broadcasted_iota(jnp.int32, gates.shape, 1)
    gif = jnp.where(lane < M_HEADS, gates, jnp.where(lane < 2 * M_HEADS, _log_sigmoid(gates), 0.0))
    if time_minor:
        b = pltpu.roll(_scan_rows(gif, jnp.add, 0.0), LANES - M_HEADS, axis=1)
        g = gif - b
        gmax = _scan_rows(g, jnp.maximum, NEG_INF)
        packed = jnp.where(lane < M_HEADS, g,
                           jnp.where(lane < 2 * M_HEADS, pltpu.roll(gmax, M_HEADS, axis=1),
                                     jnp.where(lane < 3 * M_HEADS, pltpu.roll(b, 2 * M_HEADS, axis=1), 0.0)))
        out.gif[0, o_slab] = jnp.transpose(packed)[0:2 * SUBLANES, :]
    else:
        out.gif[:, o_rows, :] = to_rows(gif, LANES)
        out.vm[:, o_rows, :] = to_rows(vm.astype(BF16), M_WIDTH)
    yield

    def decay_columns(h):
        cols = slice(h * G_DK, (h + 1) * G_DK)
        a2 = _dot(small.astype(BF16), w.a2[:, cols]) + w.b_a[:, cols]
        out.la[:, o_rows, cols] = to_rows(
            _scan_rows(_log_sigmoid(a2) * (1.0 / G_TAU), jnp.add, 0.0, segment=lg), G_DK)

    out.qg[:, o_rows, :] = to_rows(
        (_dot(xn, w.proj[:, PRE_QG:PRE_QG + G_KW]) * (G_DK ** -0.5)).astype(BF16), G_KW)
    decay_columns(0)
    yield
    out.kg[:, o_rows, :] = to_rows(_dot(xn, w.proj[:, PRE_KG:PRE_KG + G_KW]).astype(BF16), G_KW)
    decay_columns(1)
    yield
    half = G_VW // 2
    for p in range(2):
        out.vg[:, o_rows, p * half:(p + 1) * half] = to_rows(
            _dot(xn, w.proj[:, PRE_VG + p * half:PRE_VG + (p + 1) * half]).astype(BF16), half)
        decay_columns(2 + p)
        yield

    for h in range(M_HEADS):
        sl = slice(h * M_HEAD_DIM, (h + 1) * M_HEAD_DIM)
        qh = _dot(ch[:, sl], w.mq[h])
        if time_minor:
            out.qm[0, o_slab, sl, :] = jnp.transpose(qh).astype(BF16)
            out.vm[0, o_slab, sl, :] = jnp.transpose(vm[:, sl]).astype(BF16)
        else:
            out.qm[:, o_rows, sl] = to_rows(qh.astype(BF16), M_HEAD_DIM)
        out.km[:, o_rows, sl] = to_rows(
            (_dot(ch[:, sl], w.mk[h]) * (M_HEAD_DIM ** -0.5)).astype(BF16), M_HEAD_DIM)
        if h % 2:
            yield


def _pre_weights(g_ref, w_ref, ws_ref, cw_ref, cb_ref, wq_ref, wk_ref, bif_ref, wa2_ref, ba_ref):
    return types.SimpleNamespace(g=g_ref, proj=w_ref, small=ws_ref, conv_w=cw_ref, conv_b=cb_ref,
                                 mq=wq_ref, mk=wk_ref, bif=bif_ref, a2=wa2_ref, b_a=ba_ref)


def _load_conv_history(cbuf_ref, cst_ref, tb):
    t = pl.program_id(1)

    @pl.when(t == 0)
    def _():
        cbuf_ref[:, 0:SUBLANES, :] = cst_ref[...]

    @pl.when(t > 0)
    def _():
        cbuf_ref[:, 0:SUBLANES, :] = cbuf_ref[:, tb:tb + SUBLANES, :]


PRE_WEIGHT_SPECS = (
    (1, D_MODEL), (D_MODEL, PRE_WIDTH), (D_MODEL, LANES), (CONV_W, M_WIDTH), (1, M_WIDTH),
    (M_HEADS, M_HEAD_DIM, M_HEAD_DIM), (M_HEADS, M_HEAD_DIM, M_HEAD_DIM), (1, LANES), (LANES, G_KW),
    (1, G_KW))


def _pre_kernel(x_ref, cst_ref, *refs, bt, tb, lg):
    w = _pre_weights(*refs[:10])
    qm_ref, km_ref, vm_ref, gif_ref, qg_ref, kg_ref, la_ref, vg_ref, cout_ref, cbuf_ref = refs[10:]
    out = types.SimpleNamespace(qm=qm_ref, km=km_ref, vm=vm_ref, gif=gif_ref, qg=qg_ref, kg=kg_ref,
                                la=la_ref, vg=vg_ref)
    _load_conv_history(cbuf_ref, cst_ref, tb)
    rows = slice(0, tb)
    _interleave(_pre_sub_block(x_ref, rows, cbuf_ref, w, out, rows, 0, bt=bt, sub=tb, lg=lg,
                               time_minor=False))
    cout_ref[...] = cbuf_ref[:, tb:tb + SUBLANES, :]


def _pre(x, cst, weights, *, bt, tb, lg):
    b, t, _ = x.shape
    assert b % bt == 0 and t % tb == 0 and tb % SUBLANES == 0 and tb % lg == 0
    tok = lambda w: pl.BlockSpec((bt, tb, w), lambda i, j: (i, j, 0))
    per_seq = pl.BlockSpec((bt, SUBLANES, M_WIDTH), lambda i, j: (i, 0, 0))
    sds = lambda w, dt: jax.ShapeDtypeStruct((b, t, w), dt)
    return pl.pallas_call(
        functools.partial(_pre_kernel, bt=bt, tb=tb, lg=lg),
        grid=(b // bt, t // tb),
        in_specs=[tok(D_MODEL), per_seq] + [_const_spec(shp) for shp in PRE_WEIGHT_SPECS],
        out_specs=[tok(M_WIDTH), tok(M_WIDTH), tok(M_WIDTH), tok(LANES), tok(G_KW), tok(G_KW),
                   tok(G_KW), tok(G_VW), per_seq],
        out_shape=[sds(M_WIDTH, BF16), sds(M_WIDTH, BF16), sds(M_WIDTH, BF16), sds(LANES, F32),
                   sds(G_KW, BF16), sds(G_KW, BF16), sds(G_KW, F32), sds(G_VW, BF16),
                   jax.ShapeDtypeStruct((b, SUBLANES, M_WIDTH), F32)],
        scratch_shapes=[pltpu.VMEM((bt, tb + SUBLANES, M_WIDTH), F32)],
        compiler_params=pltpu.CompilerParams(
            dimension_semantics=("parallel", "arbitrary"), vmem_limit_bytes=VMEM_LIMIT),
        name="mixer_pre",
    )(x, cst, *weights)


def _scan_rows(x, op, fill, segment=None):
    n = segment or x.shape[0]
    row = lax.broadcasted_iota(jnp.int32, x.shape, 0) % n
    s = 1
    while s < n:
        x = op(x, jnp.where(row >= s, pltpu.roll(x, s, axis=0), fill))
        s *= 2
    return x


def _col_of_row(row):
    return jnp.transpose(jnp.broadcast_to(row, (SUBLANES, LANES)))[:, 0:1]


def _mlstm_chunk_tm(seqs, *, lm):
    reps = lm // LANES
    head_row = lax.broadcasted_iota(jnp.int32, (SUBLANES, LANES), 0) < M_HEADS
    src = lax.broadcasted_iota(jnp.int32, (lm, lm), 0)
    dst = lax.broadcasted_iota(jnp.int32, (lm, lm), 1)
    causal = src <= dst
    sl = [slice(h * M_HEAD_DIM, (h + 1) * M_HEAD_DIM) for h in range(M_HEADS)]
    row = lambda x, h: x[h:h + 1, :]
    gates = []
    for sq in seqs:
        g = sq.gift[0:SUBLANES]
        gmax = pltpu.roll(g, M_HEADS, axis=0)
        b = sq.gift[SUBLANES:2 * SUBLANES]
        m_prev = jnp.concatenate([sq.m_src[...]] * reps, axis=1)
        mx = jnp.maximum(m_prev, gmax)
        mx_last = jnp.broadcast_to(mx[:, lm - 1:lm], (SUBLANES, lm))
        gates.append(dict(
            mx=mx,
            inter=jnp.exp(m_prev - mx),
            nfloor=jnp.exp(-(b + mx)),
            w_in=jnp.exp(g - mx_last),
            decay=jnp.exp(m_prev - mx_last),
            g_col=jnp.transpose(g)))
        m_new = jnp.broadcast_to(b[:, lm - 1:lm], (SUBLANES, lm)) + mx_last
        sq.m_dst[...] = jnp.where(head_row, m_new[:, :LANES], 0.0)
    yield
    units = [(sq, gt, h) for sq, gt in zip(seqs, gates) for h in range(M_HEADS)]
    c_old = [sq.c_src[h] for sq, _, h in units]
    n_old = [sq.n_src[h] for sq, _, h in units]
    cn = [jnp.concatenate([c_old[u].astype(BF16),
                           jnp.broadcast_to(n_old[u].astype(BF16), (BF16_ROWS, M_HEAD_DIM))], axis=0)
          for u in range(len(units))]
    cq = [_dot(cn[u], sq.qt[sl[h], :]) for u, (sq, _, h) in enumerate(units)]
    yield
    dm = [jnp.exp(jnp.where(causal, gt["g_col"][:, h:h + 1] - row(gt["mx"], h), NEG_INF))
          for _, gt, h in units]
    st = [_dot(sq.k[:, sl[h]], sq.qt[sl[h], :]) * dm[u]
          for u, (sq, _, h) in enumerate(units)]
    yield
    num = [row(gt["inter"], h) * cq[u][:M_HEAD_DIM] + _dot(sq.vt[sl[h], :], st[u].astype(BF16))
           for u, (sq, gt, h) in enumerate(units)]
    yield
    for u, (sq, gt, h) in enumerate(units):
        den = (row(gt["inter"], h) * cq[u][M_HEAD_DIM:M_HEAD_DIM + 1]
               + jnp.sum(st[u], axis=0, keepdims=True))
        rcp = 1.0 / jnp.maximum(jnp.abs(den), row(gt["nfloor"], h))
        ms = jnp.mean(num[u] * num[u], axis=0, keepdims=True)
        sq.hm_store(h, num[u] * (rcp * lax.rsqrt(rcp * rcp * ms + EPS)))
    yield
    for u, (sq, gt, h) in enumerate(units):
        w_h = row(gt["w_in"], h)
        dec_h = jnp.concatenate([row(gt["decay"], h)[:, :LANES]] * (M_HEAD_DIM // LANES), axis=1)
        wv = jnp.concatenate([(sq.vt[sl[h], :].astype(F32) * w_h).astype(BF16),
                              jnp.broadcast_to(w_h.astype(BF16), (BF16_ROWS, lm))], axis=0)
        upd = _dot(wv, sq.k[:, sl[h]])
        sq.c_dst[h] = dec_h * c_old[u] + upd[:M_HEAD_DIM]
        sq.n_dst[h] = dec_h * n_old[u] + upd[M_HEAD_DIM:M_HEAD_DIM + 1]


def _mlstm_chunk(seqs, *, lm):
    row = lax.broadcasted_iota(jnp.int32, (lm, lm), 0)
    col = lax.broadcasted_iota(jnp.int32, (lm, lm), 1)
    causal = col <= row
    sl = [slice(h * M_HEAD_DIM, (h + 1) * M_HEAD_DIM) for h in range(M_HEADS)]
    gates = []
    for sq in seqs:
        cs = _scan_rows(sq.gif, jnp.add, 0.0)
        b = pltpu.roll(cs, LANES - M_HEADS, axis=1)
        g = sq.gif - b
        m_prev = sq.m_src[...]
        mx = jnp.maximum(m_prev, _scan_rows(g, jnp.maximum, NEG_INF))
        mx_last = mx[lm - 1:lm, :]
        gates.append(dict(
            mx=mx,
            inter=jnp.exp(m_prev - mx),
            nfloor=jnp.exp(-(b + mx)),
            g_t=jnp.transpose(g),
            w_in=jnp.exp(g - mx_last),
            decay=jnp.exp(m_prev - mx_last)))
        sq.m_dst[...] = b[lm - 1:lm, :] + mx_last
    yield
    units = [(sq, gt, h) for sq, gt in zip(seqs, gates) for h in range(M_HEADS)]
    col_of = lambda x, h: x[:, h:h + 1]
    c_old = [sq.c_src[h] for sq, _, h in units]
    n_old = [sq.n_src[h] for sq, _, h in units]
    dm = [jnp.exp(jnp.where(causal, gt["g_t"][h:h + 1, :] - col_of(gt["mx"], h), NEG_INF))
          for _, gt, h in units]
    qn = [jnp.sum(sq.q[:, sl[h]].astype(F32) * n_old[u], axis=-1, keepdims=True)
          for u, (sq, _, h) in enumerate(units)]
    s = [_dot_nt(sq.q[:, sl[h]], sq.k[:, sl[h]]) * dm[u] for u, (sq, _, h) in enumerate(units)]
    yield
    num = [col_of(gt["inter"], h) * _dot_nt(sq.q[:, sl[h]], c_old[u].astype(BF16))
           + _dot(s[u].astype(BF16), sq.v[:, sl[h]]) for u, (sq, gt, h) in enumerate(units)]
    yield
    for u, (sq, gt, h) in enumerate(units):
        den = col_of(gt["inter"], h) * qn[u] + jnp.sum(s[u], axis=-1, keepdims=True)
        rcp = 1.0 / jnp.maximum(jnp.abs(den), col_of(gt["nfloor"], h))
        ms = jnp.mean(num[u] * num[u], axis=-1, keepdims=True)
        sq.hm_store(h, num[u] * (rcp * lax.rsqrt(rcp * rcp * ms + EPS)))
    yield
    for u, (sq, gt, h) in enumerate(units):
        w_h = col_of(gt["w_in"], h)
        dec_h = col_of(gt["decay"], h)
        wv = (w_h * sq.v[:, sl[h]].astype(F32)).astype(BF16)
        sq.c_dst[h] = dec_h * c_old[u] + _dot_tn(wv, sq.k[:, sl[h]])
        sq.n_dst[h] = dec_h * n_old[u] + jnp.sum(w_h * sq.k[:, sl[h]].astype(F32), axis=0, keepdims=True)


GLA_SAFE_SPAN = 40.0


def _gla_scores_guarded(qh, kh, a, *, lg, sub):
    row = lax.broadcasted_iota(jnp.int32, (lg, lg), 0)
    col = lax.broadcasted_iota(jnp.int32, (lg, lg), 1)
    att = jnp.zeros((lg, lg), F32)
    c = sub
    while c < lg:
        nblk = lg // (2 * c)
        a_ref = jnp.concatenate(
            [jnp.broadcast_to(a[2 * c * i + c - 1:2 * c * i + c, :], (2 * c, G_DK))
             for i in range(nblk)], axis=0)
        qt = (qh * jnp.exp(jnp.minimum(a - a_ref, 0.0))).astype(BF16)
        kt = (kh * jnp.exp(jnp.minimum(a_ref - a, 0.0))).astype(BF16)
        valid = (row // (2 * c) == col // (2 * c)) & (row % (2 * c) >= c) & (col % (2 * c) < c)
        att = att + jnp.where(valid, _dot_nt(qt, kt), 0.0)
        c *= 2
    rows = lax.broadcasted_iota(jnp.int32, (sub, lg), 0)
    cols = lax.broadcasted_iota(jnp.int32, (sub, lg), 1)
    blocks = []
    for blk in range(lg // sub):
        r0 = blk * sub
        a_b = a[r0:r0 + sub, :]
        q_b = qh[r0:r0 + sub, :]
        acc = jnp.zeros((sub, lg), F32)
        for s in range(sub):
            e = jnp.exp(jnp.minimum(a_b - a[r0 + s:r0 + s + 1, :], 0.0))
            p = jnp.sum(q_b * e * kh[r0 + s:r0 + s + 1, :], axis=-1, keepdims=True)
            acc = jnp.where((cols == r0 + s) & (rows >= s), p, acc)
        blocks.append(acc)
    return att + jnp.concatenate(blocks, axis=0)


def _gla_chunk_guarded(seqs, gh_ref, *, lg, sub):
    ks = [slice(h * G_DK, (h + 1) * G_DK) for h in range(G_HEADS)]
    vs = [slice(h * G_DV, (h + 1) * G_DV) for h in range(G_HEADS)]
    pre = []
    for sq in seqs:
        a_last = sq.a[lg - 1:lg, :]
        qf = sq.qg.astype(F32)
        kf = sq.kg.astype(F32)
        pre.append(dict(a=sq.a, a_last=a_last, qf=qf, kf=kf,
                        qs=(qf * jnp.exp(sq.a)).astype(BF16),
                        kl=(kf * jnp.exp(a_last - sq.a)).astype(BF16)))
    yield
    units = [(sq, p, h) for sq, p in zip(seqs, pre) for h in range(G_HEADS)]
    s_old = [sq.s_src[h] for sq, _, h in units]
    att = [_gla_scores_guarded(p["qf"][:, ks[h]], p["kf"][:, ks[h]], p["a"][:, ks[h]], lg=lg, sub=sub)
           for _, p, h in units]
    yield
    o = [_dot(p["qs"][:, ks[h]], s_old[u].astype(BF16)) + _dot(att[u].astype(BF16), sq.vg[:, vs[h]])
         for u, (sq, p, h) in enumerate(units)]
    yield
    for u, (sq, _, h) in enumerate(units):
        sq.og_store(h, _rms(o[u], gh_ref[:, vs[h]]))
    yield
    for u, (sq, p, h) in enumerate(units):
        sq.s_dst[h] = (jnp.exp(_col_of_row(p["a_last"][:, ks[h]])) * s_old[u]
                       + _dot_tn(p["kl"][:, ks[h]], sq.vg[:, vs[h]]))


def _gla_block_direct(seqs, gh_ref, *, tb, lg):
    ks = [slice(h * G_DK, (h + 1) * G_DK) for h in range(G_HEADS)]
    vs = [slice(h * G_DV, (h + 1) * G_DV) for h in range(G_HEADS)]
    chunks = [slice(c * lg, (c + 1) * lg) for c in range(tb // lg)]
    row = lax.broadcasted_iota(jnp.int32, (lg, lg), 0)
    col = lax.broadcasted_iota(jnp.int32, (lg, lg), 1)
    causal = col <= row
    pre = []
    for sq in seqs:
        a_last = [sq.a[r.stop - 1:r.stop, :] for r in chunks]
        a_last_rows = jnp.concatenate([jnp.broadcast_to(al, (lg, G_KW)) for al in a_last], axis=0)
        qf = sq.qg.astype(F32)
        kf = sq.kg.astype(F32)
        pre.append(dict(a_last=a_last,
                        qs=(qf * jnp.exp(sq.a)).astype(BF16),
                        kt=(kf * jnp.exp(-sq.a)).astype(BF16),
                        kl=(kf * jnp.exp(a_last_rows - sq.a)).astype(BF16)))
    units = [(sq, p, h) for sq, p in zip(seqs, pre) for h in range(G_HEADS)]
    att = [[jnp.where(causal, _dot_nt(p["qs"][r, ks[h]], p["kt"][r, ks[h]]), 0.0).astype(BF16)
            for r in chunks] for _, p, h in units]
    s_cur = [sq.s_src[h] for sq, _, h in units]
    for c, r in enumerate(chunks):
        o = [_dot(p["qs"][r, ks[h]], s_cur[u].astype(BF16)) + _dot(att[u][c], sq.vg[r, vs[h]])
             for u, (sq, p, h) in enumerate(units)]
        for u, (sq, _, h) in enumerate(units):
            sq.og_rows_store(r, h, _rms(o[u], gh_ref[:, vs[h]]))
        s_cur = [jnp.exp(_col_of_row(p["a_last"][c][:, ks[h]])) * s_cur[u]
                 + _dot_tn(p["kl"][r, ks[h]], sq.vg[r, vs[h]])
                 for u, (sq, p, h) in enumerate(units)]
    for u, (sq, _, h) in enumerate(units):
        sq.s_dst[h] = s_cur[u]


def _shift_rows(rows, offset):
    if isinstance(rows, slice):
        return slice(rows.start + offset, rows.stop + offset)
    return pl.ds(rows.start + offset, rows.size)


def _gla_dispatch(la_ref, bi, row0, tb, lg, sub, gh_ref, make_record):
    chunk_decay = [la_ref[bi, row0 + c * lg + lg - 1:row0 + (c + 1) * lg, :] for c in range(tb // lg)]
    span_ok = jnp.min(functools.reduce(jnp.minimum, chunk_decay)) >= -GLA_SAFE_SPAN

    @pl.when(span_ok)
    def _():
        _gla_block_direct([make_record(slice(0, tb))], gh_ref, tb=tb, lg=lg)

    @pl.when(jnp.logical_not(span_ok))
    def _():
        def body(ci, carry):
            rows = pl.ds(pl.multiple_of(ci * lg, lg), lg)
            _interleave(_gla_chunk_guarded([make_record(rows)], gh_ref, lg=lg, sub=sub))
            return carry

        lax.fori_loop(0, tb // lg, body, 0)


def _long_mixer_kernel(x_ref, cst_ref, *refs, tb, sub, lg, gsub):
    w = _pre_weights(*refs[:10])
    ggh_ref = refs[10]
    hm_ref, og_ref, cout_ref, c_out, n_out, m_out, s_out = refs[11:18]
    cbuf_ref, qm_s, km_s, vm_s, gif_s, qg_s, kg_s, la_s, vg_s, c_scr, n_scr, m_scr, s_scr = refs[18:]
    inter = types.SimpleNamespace(qm=qm_s, km=km_s, vm=vm_s, gif=gif_s, qg=qg_s, kg=kg_s, la=la_s, vg=vg_s)
    carried = (c_scr, n_scr, m_scr, s_scr)
    t = pl.program_id(1)
    nt = pl.num_programs(1)
    _load_conv_history(cbuf_ref, cst_ref, tb)

    @pl.when(t == 0)
    def _():
        for ref in carried:
            ref[...] = jnp.zeros(ref.shape, F32)

    def pre_stage(j):
        slot = j % 2
        return _pre_sub_block(x_ref, slice(j * sub, (j + 1) * sub), cbuf_ref, w, inter,
                              slice(slot * sub, (slot + 1) * sub), slot, bt=1, sub=sub, lg=lg,
                              time_minor=True)

    def record(j, window=slice(0, sub)):
        slot = j % 2
        rows = _shift_rows(window, slot * sub)

        def hm_store(h, val):
            hm_ref[0, j, h * M_HEAD_DIM:(h + 1) * M_HEAD_DIM, :] = val

        def og_rows_store(r, h, val):
            og_ref[0, _shift_rows(r, j * sub), h * G_DV:(h + 1) * G_DV] = val

        return types.SimpleNamespace(
            qt=qm_s[0, slot], k=km_s[0, slot * sub:(slot + 1) * sub, :], vt=vm_s[0, slot], gift=gif_s[0, slot],
            qg=qg_s[0, rows, :], kg=kg_s[0, rows, :], vg=vg_s[0, rows, :], a=la_s[0, rows, :],
            hm_store=hm_store, og_rows_store=og_rows_store, og_store=functools.partial(og_rows_store, window),
            c_src=c_scr.at[0], n_src=n_scr.at[0], m_src=m_scr.at[0], s_src=s_scr.at[0],
            c_dst=c_scr.at[0], n_dst=n_scr.at[0], m_dst=m_scr.at[0], s_dst=s_scr.at[0])

    nsub = tb // sub
    _interleave(pre_stage(0))
    for j in range(nsub):
        stages = [_mlstm_chunk_tm([record(j)], lm=sub)]
        if j + 1 < nsub:
            stages.append(pre_stage(j + 1))
        _interleave(*stages)
        slot = j % 2
        _gla_dispatch(la_s, 0, slot * sub, sub, lg, gsub, ggh_ref, functools.partial(record, j))
    cout_ref[...] = cbuf_ref[:, tb:tb + SUBLANES, :]

    @pl.when(t == nt - 1)
    def _():
        for out, ref in zip((c_out, n_out, m_out, s_out), carried):
            out[...] = ref[...]


STATE_SHAPES = ((M_HEADS, M_HEAD_DIM, M_HEAD_DIM), (M_HEADS, 1, M_HEAD_DIM), None, (G_HEADS, G_DK, G_DV))


def _state_shapes(bt, m_shape):
    return [(bt,) + (m_shape if shp is None else shp) for shp in STATE_SHAPES]


def _long_mixer(x, cst, weights, ggh, *, tb, sub, lg, gsub):
    b, t, _ = x.shape
    assert t % tb == 0 and tb % sub == 0 and sub % lg == 0 and lg % gsub == 0 and sub % LANES == 0
    nsub = tb // sub
    tok = lambda w: pl.BlockSpec((1, tb, w), lambda i, j: (i, j, 0))
    per_seq = pl.BlockSpec((1, SUBLANES, M_WIDTH), lambda i, j: (i, 0, 0))
    states = _state_shapes(1, (SUBLANES, LANES))
    state_specs = [pl.BlockSpec(shp, lambda i, j, nd=len(shp): (i,) + (0,) * (nd - 1)) for shp in states]
    f32 = lambda shp: jax.ShapeDtypeStruct(shp, F32)
    slab = lambda w, dt: pltpu.VMEM((1, 2, w, sub), dt)
    rows2 = lambda w, dt: pltpu.VMEM((1, 2 * sub, w), dt)
    return pl.pallas_call(
        functools.partial(_long_mixer_kernel, tb=tb, sub=sub, lg=lg, gsub=gsub),
        grid=(b, t // tb),
        in_specs=([tok(D_MODEL), per_seq] + [_const_spec(shp) for shp in PRE_WEIGHT_SPECS]
                  + [_const_spec((1, G_VW))]),
        out_specs=[pl.BlockSpec((1, nsub, M_WIDTH, sub), lambda i, j: (i, j, 0, 0)), tok(G_VW), per_seq]
                  + state_specs,
        out_shape=[f32((b, t // sub, M_WIDTH, sub)), f32((b, t, G_VW)), f32((b, SUBLANES, M_WIDTH))]
                  + [f32((b,) + shp[1:]) for shp in states],
        scratch_shapes=[pltpu.VMEM((1, tb + SUBLANES, M_WIDTH), F32),
                        slab(M_WIDTH, BF16), rows2(M_WIDTH, BF16), slab(M_WIDTH, BF16),
                        slab(2 * SUBLANES, F32), rows2(G_KW, BF16), rows2(G_KW, BF16), rows2(G_KW, F32),
                        rows2(G_VW, BF16)] + [pltpu.VMEM(shp, F32) for shp in states],
        compiler_params=pltpu.CompilerParams(
            dimension_semantics=("parallel", "arbitrary"), vmem_limit_bytes=VMEM_LIMIT),
        name="mixer_long",
    )(x, cst, *weights, ggh)


def _short_scan_kernel(*refs, bt, ts):
    qm, km, vm, gif, qg, kg, la, vg, ggh_ref, c_in, n_in, m_in, s_in = refs[:13]
    hm, og, c_out, n_out, m_out, s_out = refs[13:]

    def record(bi):
        def hm_store(h, val):
            hm[bi, :, h * M_HEAD_DIM:(h + 1) * M_HEAD_DIM] = val

        def og_store(h, val):
            og[bi, :, h * G_DV:(h + 1) * G_DV] = val

        return types.SimpleNamespace(
            q=qm[bi], k=km[bi], v=vm[bi], gif=gif[bi], qg=qg[bi], kg=kg[bi], vg=vg[bi], a=la[bi],
            hm_store=hm_store, og_store=og_store,
            c_src=c_in.at[bi], n_src=n_in.at[bi], m_src=m_in.at[bi], s_src=s_in.at[bi],
            c_dst=c_out.at[bi], n_dst=n_out.at[bi], m_dst=m_out.at[bi], s_dst=s_out.at[bi])

    seqs = [record(bi) for bi in range(bt)]
    _interleave(_mlstm_chunk(seqs, lm=ts), _gla_chunk_guarded(seqs, ggh_ref, lg=ts, sub=ts))


def _short_scan(per_token, ggh, state, *, bt):
    b, ts, _ = per_token[1].shape
    assert b % bt == 0 and ts % SUBLANES == 0
    lead = lambda shp: pl.BlockSpec(shp, lambda i, nd=len(shp): (i,) + (0,) * (nd - 1))
    tok = lambda w: lead((bt, ts, w))
    states = _state_shapes(bt, (1, LANES))
    f32 = lambda shp: jax.ShapeDtypeStruct(shp, F32)
    return pl.pallas_call(
        functools.partial(_short_scan_kernel, bt=bt, ts=ts),
        grid=(b // bt,),
        in_specs=[tok(M_WIDTH), tok(M_WIDTH), tok(M_WIDTH), tok(LANES), tok(G_KW), tok(G_KW), tok(G_KW),
                  tok(G_VW), _const_spec((1, G_VW))] + [lead(shp) for shp in states],
        out_specs=[tok(M_WIDTH), tok(G_VW)] + [lead(shp) for shp in states],
        out_shape=[f32((b, ts, M_WIDTH)), f32((b, ts, G_VW))] + [f32((b,) + shp[1:]) for shp in states],
        compiler_params=pltpu.CompilerParams(
            dimension_semantics=("parallel",), vmem_limit_bytes=VMEM_LIMIT),
        name="mixer_scan_short",
    )(*per_token, ggh, *state)


POST_O, POST_R, POST_A, POST_B = 0, 1024, 2048, 3072


def _post_kernel(x_ref, hm_ref, og_ref, g_ref, gmh_ref, w_ref, wpa_ref, wpb_ref, wo_ref, o_ref, *,
                 time_minor):
    x = x_ref[...]
    xn = _rms(x, g_ref[...]).astype(BF16)
    gate = lambda off: _dot(xn, w_ref[:, off:off + D_MODEL])
    if time_minor:
        hm = jnp.concatenate(
            [jnp.concatenate([jnp.transpose(hm_ref[0, j, h * M_HEAD_DIM:(h + 1) * M_HEAD_DIM, :])
                              for h in range(M_HEADS)], axis=1)
             for j in range(hm_ref.shape[1])], axis=0)
    else:
        hm = hm_ref[...]
    hm = (_sigmoid(gate(POST_O)) * (hm * gmh_ref[...])).astype(BF16)
    og = (_silu(gate(POST_R)) * og_ref[...]).astype(BF16)
    y = (_sigmoid(gate(POST_A)) * _dot(hm, wpa_ref[...])
         + _sigmoid(gate(POST_B)) * _dot(og, wpb_ref[...]))
    o_ref[...] = x + _dot(y.astype(BF16), wo_ref[...])


def _post(x2d, hm, og2d, g, gmh, w_post, wpa, wpb, wo, *, tm, time_minor):
    n = x2d.shape[0]
    assert n % tm == 0
    tok = pl.BlockSpec((tm, D_MODEL), lambda i: (i, 0))
    if time_minor:
        nblk, blk = hm.shape[1], hm.shape[3]
        assert tm % blk == 0 and (nblk * blk) % tm == 0
        per_seq = nblk * blk // tm
        hm_spec = pl.BlockSpec((1, tm // blk, M_WIDTH, blk), lambda i: (i // per_seq, i % per_seq, 0, 0))
    else:
        hm_spec = tok
    return pl.pallas_call(
        functools.partial(_post_kernel, time_minor=time_minor),
        grid=(n // tm,),
        in_specs=[tok, hm_spec, tok, _const_spec((1, D_MODEL)), _const_spec((1, M_WIDTH)),
                  _const_spec((D_MODEL, 4 * D_MODEL)),
                  _const_spec((M_WIDTH, D_MODEL)), _const_spec((G_VW, D_MODEL)),
                  _const_spec((D_MODEL, D_MODEL))],
        out_specs=tok,
        out_shape=jax.ShapeDtypeStruct((n, D_MODEL), F32),
        compiler_params=pltpu.CompilerParams(
            dimension_semantics=("parallel",), vmem_limit_bytes=VMEM_LIMIT),
        name="mixer_post",
    )(x2d, hm, og2d, g, gmh, w_post, wpa, wpb, wo)


def _pack_weights(g_ffn1, w_ffn1_up, w_ffn1_down, g_mix, w_in, conv_w, conv_b, w_mq, w_mk, b_if,
                  g_mhead, w_a2, b_a, g_ghead, w_pa, w_pb, w_o, g_ffn2, w_ffn2_up, w_ffn2_down,
                  g_final):
    l = 0
    row = lambda v: v.reshape(1, -1).astype(F32)
    edges = [0]
    for wd in (M_WIDTH, M_WIDTH, M_WIDTH, 2 * M_HEADS, G_KW, G_KW, G_VW, G_VW, G_RANK, D_MODEL, D_MODEL):
        edges.append(edges[-1] + wd)
    seg = lambda i: w_in[l][:, edges[i]:edges[i + 1]]
    u_m, v_m, o_m, if_m, q_g, k_g, v_g, r_g, a_g, g_a, g_b = (seg(i) for i in range(11))
    w_pre = jnp.concatenate([u_m, v_m, q_g, k_g, v_g], axis=1).astype(BF16)
    w_small = jnp.concatenate(
        [if_m, a_g, jnp.zeros((D_MODEL, LANES - 2 * M_HEADS - G_RANK), F32)], axis=1).astype(BF16)
    w_post = jnp.concatenate([o_m, r_g, g_a, g_b], axis=1).astype(BF16)
    bif = jnp.concatenate([b_if[l].reshape(-1), jnp.zeros((LANES - 2 * M_HEADS,), F32)]).reshape(1, LANES)
    wa2 = jnp.zeros((LANES, G_KW), F32).at[A_LANE0:A_LANE0 + G_RANK].set(w_a2[l]).astype(BF16)
    return dict(
        g_ffn1=row(g_ffn1[l]), w_ffn1_up=w_ffn1_up[l].astype(BF16), w_ffn1_down=w_ffn1_down[l].astype(BF16),
        g_mix=row(g_mix[l]), w_pre=w_pre, w_small=w_small, w_post=w_post,
        conv_w=conv_w[l].astype(F32), conv_b=row(conv_b[l]),
        w_mq=w_mq[l].astype(BF16), w_mk=w_mk[l].astype(BF16), bif=bif,
        g_mhead=row(g_mhead[l]), wa2=wa2, b_a=row(b_a[l]), g_ghead=row(g_ghead[l]),
        w_pa=w_pa[l].astype(BF16), w_pb=w_pb[l].astype(BF16), w_o=w_o[l].astype(BF16),
        g_ffn2=row(g_ffn2[l]), w_ffn2_up=w_ffn2_up[l].astype(BF16), w_ffn2_down=w_ffn2_down[l].astype(BF16),
        g_final=row(g_final))


LONG_BLOCK = 512
LONG_CHUNK = 256
LONG_GLA_CHUNK = 128
GLA_PAIRWISE_BLOCK = 16
SHORT_SEQS_PER_STEP = 4
SHORT_PRE_ROWS = 512
POST_ROWS = 512
FFN_ROWS = 1024


def _pre_weight_args(w):
    return (w["g_mix"], w["w_pre"], w["w_small"], w["conv_w"], w["conv_b"], w["w_mq"], w["w_mk"], w["bif"],
            w["wa2"], w["b_a"])


def _long_group(x1, b, t, w):
    assert t % LONG_BLOCK == 0
    cst = jnp.zeros((b, SUBLANES, M_WIDTH), F32)
    hm, og, cout, c_new, n_new, m_new, s_new = _long_mixer(
        x1.reshape(b, t, D_MODEL), cst, _pre_weight_args(w), w["g_ghead"], tb=LONG_BLOCK, sub=LONG_CHUNK,
        lg=LONG_GLA_CHUNK, gsub=GLA_PAIRWISE_BLOCK)
    x2 = _post(x1, hm, og.reshape(b * t, G_VW), w["g_mix"], w["g_mhead"], w["w_post"], w["w_pa"], w["w_pb"],
               w["w_o"], tm=math.gcd(t, POST_ROWS), time_minor=True)
    return x2, (cout[:, SUBLANES - (CONV_W - 1):, :][None], c_new[None],
                n_new.reshape(b, M_HEADS, M_HEAD_DIM)[None], m_new[:, :M_HEADS, 0][None], s_new[None])


def _short_group(x1, b, t, state, w):
    assert t <= 64 and t % SUBLANES == 0
    conv0, c0, n0, m0, s0 = state
    cst = jnp.concatenate([jnp.zeros((b, SUBLANES - (CONV_W - 1), M_WIDTH), F32), conv0.astype(F32)], axis=1)
    m0p = jnp.concatenate([m0, jnp.zeros((b, LANES - M_HEADS), F32)], axis=1).reshape(b, 1, LANES)
    bt = max(1, min(b, SHORT_PRE_ROWS // t))
    while b % bt:
        bt -= 1
    *per_token, cout = _pre(x1.reshape(b, t, D_MODEL), cst, _pre_weight_args(w), bt=bt, tb=t, lg=t)
    hm, og, c_new, n_new, m_new, s_new = _short_scan(
        per_token, w["g_ghead"], (c0, n0.reshape(b, M_HEADS, 1, M_HEAD_DIM), m0p, s0),
        bt=math.gcd(b, SHORT_SEQS_PER_STEP))
    n = b * t
    x2 = _post(x1, hm.reshape(n, M_WIDTH), og.reshape(n, G_VW), w["g_mix"], w["g_mhead"], w["w_post"],
               w["w_pa"], w["w_pb"], w["w_o"], tm=math.gcd(n, POST_ROWS), time_minor=False)
    return x2, (cout[:, SUBLANES - (CONV_W - 1):, :][None], c_new[None],
                n_new.reshape(b, M_HEADS, M_HEAD_DIM)[None], m_new[:, 0, :M_HEADS][None], s_new[None])


def kernel(x_prompt, x_sample, state_conv, state_mlstm_C, state_mlstm_n, state_mlstm_m, state_gla_S,
           g_ffn1, w_ffn1_up, w_ffn1_down, g_mix, w_in, conv_w, conv_b, w_mq, w_mk, b_if, g_mhead,
           w_a2, b_a, g_ghead, w_pa, w_pb, w_o, g_ffn2, w_ffn2_up, w_ffn2_down, g_final):
    w = _pack_weights(g_ffn1, w_ffn1_up, w_ffn1_down, g_mix, w_in, conv_w, conv_b, w_mq, w_mk, b_if,
                      g_mhead, w_a2, b_a, g_ghead, w_pa, w_pb, w_o, g_ffn2, w_ffn2_up, w_ffn2_down,
                      g_final)
    bp, tp, _ = x_prompt.shape
    bs, ts, _ = x_sample.shape
    tm_ffn = math.gcd(math.gcd(bp * tp, bs * ts), FFN_ROWS)
    x1_p, x1_s = _ffn(x_prompt.reshape(bp * tp, D_MODEL), x_sample.reshape(bs * ts, D_MODEL),
                      w["g_ffn1"], w["w_ffn1_up"], w["w_ffn1_down"], w["g_final"], final_norm=False, tm=tm_ffn)
    x2_p, states_p = _long_group(x1_p, bp, tp, w)
    state = (state_conv[0], state_mlstm_C[0], state_mlstm_n[0], state_mlstm_m[0], state_gla_S[0])
    x2_s, states_s = _short_group(x1_s, bs, ts, state, w)
    y_p, y_s = _ffn(x2_p, x2_s, w["g_ffn2"], w["w_ffn2_up"], w["w_ffn2_down"], w["g_final"],
                    final_norm=True, tm=tm_ffn)
    return (y_p.reshape(bp, tp, D_MODEL), y_s.reshape(bs, ts, D_MODEL)) + states_p + states_s
```

```python
import functools
import itertools
import math
import types

import jax
import jax.numpy as jnp
from jax import lax
from jax.experimental import pallas as pl
from jax.experimental.pallas import tpu as pltpu

D_MODEL = 1024
M_HEADS = 4
M_HEAD_DIM = 256
M_WIDTH = 1024
CONV_W = 4
G_HEADS = 4
G_DK = 128
G_DV = 256
G_KW = 512
G_VW = 1024
G_RANK = 16
G_TAU = 16.0
D_FF = 2816
EPS = 1e-6

LANES = 128
SUBLANES = 8
BF16_ROWS = 16
VMEM_LIMIT = 56 * 1024 * 1024

F32 = jnp.float32
BF16 = jnp.bfloat16
NEG_INF = float("-inf")


def _sigmoid(x):
    return 1.0 / (1.0 + jnp.exp(-x))


def _silu(x):
    return x * _sigmoid(x)


def _log_sigmoid(x):
    return jnp.minimum(x, 0.0) - jnp.log(1.0 + jnp.exp(-jnp.abs(x)))


def _rms(x, g):
    return x * lax.rsqrt(jnp.mean(x * x, axis=-1, keepdims=True) + EPS) * g


def _dot(a, b):
    return jnp.dot(a, b, preferred_element_type=F32)


def _dot_nt(a, b):
    return lax.dot_general(a, b, (((1,), (1,)), ((), ())), preferred_element_type=F32)


def _dot_tn(a, b):
    return lax.dot_general(a, b, (((0,), (0,)), ((), ())), preferred_element_type=F32)


def _const_spec(shape):
    nd = len(shape)
    return pl.BlockSpec(shape, lambda *_: (0,) * nd, pipeline_mode=pl.Buffered(1))


FF_CHUNK = 256


def _mixer_gates(x1, gw, gif_ref, la_ref, *, chunk, lg, time_minor):
    g_ref, ws_ref, bif_ref, wa2_ref, ba_ref = gw
    xn = _rms(x1, g_ref[...]).astype(BF16)
    small = _dot(xn, ws_ref[...])
    gates = small + bif_ref[...]
    lane = lax.broadcasted_iota(jnp.int32, gates.shape, 1)
    gif = jnp.where(lane < M_HEADS, gates, jnp.where(lane < 2 * M_HEADS, _log_sigmoid(gates), 0.0))
    if time_minor:
        b = pltpu.roll(_scan_rows(gif, jnp.add, 0.0, segment=chunk), LANES - M_HEADS, axis=1)
        g = gif - b
        gmax = _scan_rows(g, jnp.maximum, NEG_INF, segment=chunk)
        packed = jnp.where(lane < M_HEADS, g,
                           jnp.where(lane < 2 * M_HEADS, pltpu.roll(gmax, M_HEADS, axis=1),
                                     jnp.where(lane < 3 * M_HEADS, pltpu.roll(b, 2 * M_HEADS, axis=1), 0.0)))
        for j in range(x1.shape[0] // chunk):
            gif_ref[j] = jnp.transpose(packed[j * chunk:(j + 1) * chunk])[0:2 * SUBLANES, :]
    else:
        gif_ref[...] = gif
    for h in range(G_HEADS):
        cols = slice(h * G_DK, (h + 1) * G_DK)
        a2 = _dot(small.astype(BF16), wa2_ref[:, cols]) + ba_ref[:, cols]
        la_ref[:, cols] = _scan_rows(_log_sigmoid(a2) * (1.0 / G_TAU), jnp.add, 0.0, segment=lg)


def _ffn_kernel(xa_ref, xb_ref, g_ref, wup_ref, wdown_ref, gfin_ref, *refs, final_norm, tiles_a, gate_cfg):
    if gate_cfg is None:
        oa_ref, ob_ref, h_ref = refs
        gw = gates_a = gates_b = None
    else:
        gw = refs[:5]
        oa_ref, ob_ref, gifa_ref, laa_ref, gifb_ref, lab_ref, h_ref = refs[5:]
        gates_a, gates_b = (gifa_ref, laa_ref), (gifb_ref, lab_ref)

    def half_step(x_ref, o_ref, gate_refs, cfg):
        x = x_ref[...]
        xn = _rms(x, g_ref[...]).astype(BF16)
        for c in range(D_FF // FF_CHUNK):
            a = _dot(xn, wup_ref[:, c * FF_CHUNK:(c + 1) * FF_CHUNK])
            g = _dot(xn, wup_ref[:, D_FF + c * FF_CHUNK:D_FF + (c + 1) * FF_CHUNK])
            h_ref[:, c * FF_CHUNK:(c + 1) * FF_CHUNK] = (_silu(g) * a).astype(BF16)
        y = x + 0.5 * _dot(h_ref[...], wdown_ref[...])
        o_ref[...] = _rms(y, gfin_ref[...]) if final_norm else y
        if gate_refs is not None:
            _mixer_gates(y, gw, *gate_refs, **cfg)

    on_a = pl.program_id(0) < tiles_a
    pl.when(on_a)(lambda: half_step(xa_ref, oa_ref, gates_a, gate_cfg and gate_cfg[0]))
    pl.when(jnp.logical_not(on_a))(lambda: half_step(xb_ref, ob_ref, gates_b, gate_cfg and gate_cfg[1]))


def _ffn(xa, xb, g, wup, wdown, gfin, *, final_norm, tm, gate_weights=None, gate_cfg=None):
    na, nb = xa.shape[0], xb.shape[0]
    assert na % tm == 0 and nb % tm == 0
    tiles_a, tiles_b = na // tm, nb // tm
    idx_a = lambda i: jnp.minimum(i, tiles_a - 1)
    idx_b = lambda i: jnp.maximum(i - tiles_a, 0)
    once = dict(pipeline_mode=pl.Buffered(1)) if tiles_b == 1 else {}
    spec_a = lambda w: pl.BlockSpec((tm, w), lambda i: (idx_a(i), 0))
    spec_b = lambda w: pl.BlockSpec((tm, w), lambda i: (idx_b(i), 0), **once)
    in_specs = [spec_a(D_MODEL), spec_b(D_MODEL), _const_spec((1, D_MODEL)), _const_spec((D_MODEL, 2 * D_FF)),
                _const_spec((D_FF, D_MODEL)), _const_spec((1, D_MODEL))]
    out_specs = [spec_a(D_MODEL), spec_b(D_MODEL)]
    out_shape = [jax.ShapeDtypeStruct((na, D_MODEL), F32), jax.ShapeDtypeStruct((nb, D_MODEL), F32)]
    args = [xa, xb, g, wup, wdown, gfin]
    if gate_cfg is not None:
        in_specs += [_const_spec(w.shape) for w in gate_weights]
        args += list(gate_weights)
        for n, tiles_idx, spec, cfg, extra in ((na, idx_a, spec_a, gate_cfg[0], {}), (nb, idx_b, spec_b, gate_cfg[1], once)):
            if cfg["time_minor"]:
                chunk = cfg["chunk"]
                assert tm % chunk == 0
                out_specs.append(pl.BlockSpec((tm // chunk, 2 * SUBLANES, chunk),
                                              lambda i, f=tiles_idx: (f(i), 0, 0), **extra))
                out_shape.append(jax.ShapeDtypeStruct((n // chunk, 2 * SUBLANES, chunk), F32))
            else:
                out_specs.append(spec(LANES))
                out_shape.append(jax.ShapeDtypeStruct((n, LANES), F32))
            out_specs.append(spec(G_KW))
            out_shape.append(jax.ShapeDtypeStruct((n, G_KW), F32))
    return pl.pallas_call(
        functools.partial(_ffn_kernel, final_norm=final_norm, tiles_a=tiles_a, gate_cfg=gate_cfg),
        grid=(tiles_a + tiles_b,),
        in_specs=in_specs,
        out_specs=out_specs,
        out_shape=out_shape,
        scratch_shapes=[pltpu.VMEM((tm, D_FF), BF16)],
        compiler_params=pltpu.CompilerParams(
            dimension_semantics=("arbitrary",), vmem_limit_bytes=VMEM_LIMIT),
        name="ffn_final" if final_norm else "ffn",
    )(*args)


def _interleave(*stage_generators):
    for _ in itertools.zip_longest(*stage_generators):
        pass


PRE_U, PRE_V, PRE_QG, PRE_KG, PRE_VG = 0, 1024, 2048, 2560, 3072
PRE_WIDTH = 4096
A_LANE0 = 8


def _pre_sub_block(x_ref, x_rows, cbuf_ref, w, out, o_rows, o_slab, *, bt, sub, time_minor):
    m = bt * sub
    r0 = x_rows.start
    to_rows = lambda y, width: y.reshape(bt, sub, width)
    xn = _rms(x_ref[:, x_rows, :].reshape(m, D_MODEL), w.g[...]).astype(BF16)

    u = _dot(xn, w.proj[:, PRE_U:PRE_U + M_WIDTH])
    cbuf_ref[:, SUBLANES + r0:SUBLANES + r0 + sub, :] = to_rows(u, M_WIDTH)
    c = w.conv_b[...].reshape(1, 1, M_WIDTH)
    for tap in range(CONV_W):
        off = SUBLANES - (CONV_W - 1) + tap + r0
        c = c + cbuf_ref[:, off:off + sub, :] * w.conv_w[tap:tap + 1, :].reshape(1, 1, M_WIDTH)
    ch = _silu(c).reshape(m, M_WIDTH).astype(BF16)
    yield

    vm = _dot(xn, w.proj[:, PRE_V:PRE_V + M_WIDTH])
    if not time_minor:
        out.vm[:, o_rows, :] = to_rows(vm.astype(BF16), M_WIDTH)
    yield

    out.qg[:, o_rows, :] = to_rows(
        (_dot(xn, w.proj[:, PRE_QG:PRE_QG + G_KW]) * (G_DK ** -0.5)).astype(BF16), G_KW)
    yield
    out.kg[:, o_rows, :] = to_rows(_dot(xn, w.proj[:, PRE_KG:PRE_KG + G_KW]).astype(BF16), G_KW)
    yield
    half = G_VW // 2
    for p in range(2):
        out.vg[:, o_rows, p * half:(p + 1) * half] = to_rows(
            _dot(xn, w.proj[:, PRE_VG + p * half:PRE_VG + (p + 1) * half]).astype(BF16), half)
        yield

    for h in range(M_HEADS):
        sl = slice(h * M_HEAD_DIM, (h + 1) * M_HEAD_DIM)
        qh = _dot(ch[:, sl], w.mq[h])
        if time_minor:
            out.qm[0, o_slab, sl, :] = jnp.transpose(qh).astype(BF16)
            out.vm[0, o_slab, sl, :] = jnp.transpose(vm[:, sl]).astype(BF16)
        else:
            out.qm[:, o_rows, sl] = to_rows(qh.astype(BF16), M_HEAD_DIM)
        out.km[:, o_rows, sl] = to_rows(
            (_dot(ch[:, sl], w.mk[h]) * (M_HEAD_DIM ** -0.5)).astype(BF16), M_HEAD_DIM)
        if h % 2:
            yield


def _pre_weights(g_ref, w_ref, cw_ref, cb_ref, wq_ref, wk_ref):
    return types.SimpleNamespace(g=g_ref, proj=w_ref, conv_w=cw_ref, conv_b=cb_ref, mq=wq_ref, mk=wk_ref)


def _load_conv_history(cbuf_ref, cst_ref, tb):
    t = pl.program_id(1)

    @pl.when(t == 0)
    def _():
        cbuf_ref[:, 0:SUBLANES, :] = cst_ref[...]

    @pl.when(t > 0)
    def _():
        cbuf_ref[:, 0:SUBLANES, :] = cbuf_ref[:, tb:tb + SUBLANES, :]


PRE_WEIGHT_SPECS = (
    (1, D_MODEL), (D_MODEL, PRE_WIDTH), (CONV_W, M_WIDTH), (1, M_WIDTH),
    (M_HEADS, M_HEAD_DIM, M_HEAD_DIM), (M_HEADS, M_HEAD_DIM, M_HEAD_DIM))
N_PRE_WEIGHTS = len(PRE_WEIGHT_SPECS)


def _pre_kernel(x_ref, cst_ref, *refs, bt, tb):
    w = _pre_weights(*refs[:N_PRE_WEIGHTS])
    qm_ref, km_ref, vm_ref, qg_ref, kg_ref, vg_ref, cout_ref, cbuf_ref = refs[N_PRE_WEIGHTS:]
    out = types.SimpleNamespace(qm=qm_ref, km=km_ref, vm=vm_ref, qg=qg_ref, kg=kg_ref, vg=vg_ref)
    _load_conv_history(cbuf_ref, cst_ref, tb)
    rows = slice(0, tb)
    _interleave(_pre_sub_block(x_ref, rows, cbuf_ref, w, out, rows, 0, bt=bt, sub=tb, time_minor=False))
    cout_ref[...] = cbuf_ref[:, tb:tb + SUBLANES, :]


def _pre(x, cst, weights, *, bt, tb):
    b, t, _ = x.shape
    assert b % bt == 0 and t % tb == 0 and tb % SUBLANES == 0
    tok = lambda w: pl.BlockSpec((bt, tb, w), lambda i, j: (i, j, 0))
    per_seq = pl.BlockSpec((bt, SUBLANES, M_WIDTH), lambda i, j: (i, 0, 0))
    sds = lambda w, dt: jax.ShapeDtypeStruct((b, t, w), dt)
    return pl.pallas_call(
        functools.partial(_pre_kernel, bt=bt, tb=tb),
        grid=(b // bt, t // tb),
        in_specs=[tok(D_MODEL), per_seq] + [_const_spec(shp) for shp in PRE_WEIGHT_SPECS],
        out_specs=[tok(M_WIDTH), tok(M_WIDTH), tok(M_WIDTH), tok(G_KW), tok(G_KW), tok(G_VW), per_seq],
        out_shape=[sds(M_WIDTH, BF16), sds(M_WIDTH, BF16), sds(M_WIDTH, BF16),
                   sds(G_KW, BF16), sds(G_KW, BF16), sds(G_VW, BF16),
                   jax.ShapeDtypeStruct((b, SUBLANES, M_WIDTH), F32)],
        scratch_shapes=[pltpu.VMEM((bt, tb + SUBLANES, M_WIDTH), F32)],
        compiler_params=pltpu.CompilerParams(
            dimension_semantics=("parallel", "arbitrary"), vmem_limit_bytes=VMEM_LIMIT),
        name="mixer_pre",
    )(x, cst, *weights)


def _scan_rows(x, op, fill, segment=None):
    n = segment or x.shape[0]
    row = lax.broadcasted_iota(jnp.int32, x.shape, 0) % n
    s = 1
    while s < n:
        x = op(x, jnp.where(row >= s, pltpu.roll(x, s, axis=0), fill))
        s *= 2
    return x


def _col_of_row(row):
    return jnp.transpose(jnp.broadcast_to(row, (SUBLANES, LANES)))[:, 0:1]


def _mlstm_chunk_tm(seqs, *, lm):
    reps = lm // LANES
    head_row = lax.broadcasted_iota(jnp.int32, (SUBLANES, LANES), 0) < M_HEADS
    src = lax.broadcasted_iota(jnp.int32, (lm, lm), 0)
    dst = lax.broadcasted_iota(jnp.int32, (lm, lm), 1)
    causal = src <= dst
    sl = [slice(h * M_HEAD_DIM, (h + 1) * M_HEAD_DIM) for h in range(M_HEADS)]
    row = lambda x, h: x[h:h + 1, :]
    gates = []
    for sq in seqs:
        g = sq.gift[0:SUBLANES]
        gmax = pltpu.roll(g, M_HEADS, axis=0)
        b = sq.gift[SUBLANES:2 * SUBLANES]
        m_prev = jnp.concatenate([sq.m_src[...]] * reps, axis=1)
        mx = jnp.maximum(m_prev, gmax)
        mx_last = jnp.broadcast_to(mx[:, lm - 1:lm], (SUBLANES, lm))
        gates.append(dict(
            mx=mx,
            inter=jnp.exp(m_prev - mx),
            nfloor=jnp.exp(-(b + mx)),
            w_in=jnp.exp(g - mx_last),
            decay=jnp.exp(m_prev - mx_last),
            g_col=jnp.transpose(g)))
        m_new = jnp.broadcast_to(b[:, lm - 1:lm], (SUBLANES, lm)) + mx_last
        sq.m_dst[...] = jnp.where(head_row, m_new[:, :LANES], 0.0)
    yield
    units = [(sq, gt, h) for sq, gt in zip(seqs, gates) for h in range(M_HEADS)]
    c_old = [sq.c_src[h] for sq, _, h in units]
    n_old = [sq.n_src[h] for sq, _, h in units]
    cn = [jnp.concatenate([c_old[u].astype(BF16),
                           jnp.broadcast_to(n_old[u].astype(BF16), (BF16_ROWS, M_HEAD_DIM))], axis=0)
          for u in range(len(units))]
    cq = [_dot(cn[u], sq.qt[sl[h], :]) for u, (sq, _, h) in enumerate(units)]
    yield
    dm = [jnp.exp(jnp.where(causal, gt["g_col"][:, h:h + 1] - row(gt["mx"], h), NEG_INF))
          for _, gt, h in units]
    st = [_dot(sq.k[:, sl[h]], sq.qt[sl[h], :]) * dm[u]
          for u, (sq, _, h) in enumerate(units)]
    yield
    num = [row(gt["inter"], h) * cq[u][:M_HEAD_DIM] + _dot(sq.vt[sl[h], :], st[u].astype(BF16))
           for u, (sq, gt, h) in enumerate(units)]
    yield
    for u, (sq, gt, h) in enumerate(units):
        den = (row(gt["inter"], h) * cq[u][M_HEAD_DIM:M_HEAD_DIM + 1]
               + jnp.sum(st[u], axis=0, keepdims=True))
        rcp = 1.0 / jnp.maximum(jnp.abs(den), row(gt["nfloor"], h))
        ms = jnp.mean(num[u] * num[u], axis=0, keepdims=True)
        sq.hm_store(h, num[u] * (rcp * lax.rsqrt(rcp * rcp * ms + EPS)))
    yield
    for u, (sq, gt, h) in enumerate(units):
        w_h = row(gt["w_in"], h)
        dec_h = jnp.concatenate([row(gt["decay"], h)[:, :LANES]] * (M_HEAD_DIM // LANES), axis=1)
        wv = jnp.concatenate([(sq.vt[sl[h], :].astype(F32) * w_h).astype(BF16),
                              jnp.broadcast_to(w_h.astype(BF16), (BF16_ROWS, lm))], axis=0)
        upd = _dot(wv, sq.k[:, sl[h]])
        sq.c_dst[h] = dec_h * c_old[u] + upd[:M_HEAD_DIM]
        sq.n_dst[h] = dec_h * n_old[u] + upd[M_HEAD_DIM:M_HEAD_DIM + 1]


def _mlstm_chunk(seqs, *, lm):
    row = lax.broadcasted_iota(jnp.int32, (lm, lm), 0)
    col = lax.broadcasted_iota(jnp.int32, (lm, lm), 1)
    causal = col <= row
    sl = [slice(h * M_HEAD_DIM, (h + 1) * M_HEAD_DIM) for h in range(M_HEADS)]
    gates = []
    for sq in seqs:
        cs = _scan_rows(sq.gif, jnp.add, 0.0)
        b = pltpu.roll(cs, LANES - M_HEADS, axis=1)
        g = sq.gif - b
        m_prev = sq.m_src[...]
        mx = jnp.maximum(m_prev, _scan_rows(g, jnp.maximum, NEG_INF))
        mx_last = mx[lm - 1:lm, :]
        gates.append(dict(
            mx=mx,
            inter=jnp.exp(m_prev - mx),
            nfloor=jnp.exp(-(b + mx)),
            g_t=jnp.transpose(g),
            w_in=jnp.exp(g - mx_last),
            decay=jnp.exp(m_prev - mx_last)))
        sq.m_dst[...] = b[lm - 1:lm, :] + mx_last
    yield
    units = [(sq, gt, h) for sq, gt in zip(seqs, gates) for h in range(M_HEADS)]
    col_of = lambda x, h: x[:, h:h + 1]
    c_old = [sq.c_src[h] for sq, _, h in units]
    n_old = [sq.n_src[h] for sq, _, h in units]
    dm = [jnp.exp(jnp.where(causal, gt["g_t"][h:h + 1, :] - col_of(gt["mx"], h), NEG_INF))
          for _, gt, h in units]
    qn = [jnp.sum(sq.q[:, sl[h]].astype(F32) * n_old[u], axis=-1, keepdims=True)
          for u, (sq, _, h) in enumerate(units)]
    s = [_dot_nt(sq.q[:, sl[h]], sq.k[:, sl[h]]) * dm[u] for u, (sq, _, h) in enumerate(units)]
    yield
    num = [col_of(gt["inter"], h) * _dot_nt(sq.q[:, sl[h]], c_old[u].astype(BF16))
           + _dot(s[u].astype(BF16), sq.v[:, sl[h]]) for u, (sq, gt, h) in enumerate(units)]
    yield
    for u, (sq, gt, h) in enumerate(units):
        den = col_of(gt["inter"], h) * qn[u] + jnp.sum(s[u], axis=-1, keepdims=True)
        rcp = 1.0 / jnp.maximum(jnp.abs(den), col_of(gt["nfloor"], h))
        ms = jnp.mean(num[u] * num[u], axis=-1, keepdims=True)
        sq.hm_store(h, num[u] * (rcp * lax.rsqrt(rcp * rcp * ms + EPS)))
    yield
    for u, (sq, gt, h) in enumerate(units):
        w_h = col_of(gt["w_in"], h)
        dec_h = col_of(gt["decay"], h)
        wv = (w_h * sq.v[:, sl[h]].astype(F32)).astype(BF16)
        sq.c_dst[h] = dec_h * c_old[u] + _dot_tn(wv, sq.k[:, sl[h]])
        sq.n_dst[h] = dec_h * n_old[u] + jnp.sum(w_h * sq.k[:, sl[h]].astype(F32), axis=0, keepdims=True)


GLA_SAFE_SPAN = 40.0


def _gla_scores_guarded(qh, kh, a, *, lg, sub):
    row = lax.broadcasted_iota(jnp.int32, (lg, lg), 0)
    col = lax.broadcasted_iota(jnp.int32, (lg, lg), 1)
    att = jnp.zeros((lg, lg), F32)
    c = sub
    while c < lg:
        nblk = lg // (2 * c)
        a_ref = jnp.concatenate(
            [jnp.broadcast_to(a[2 * c * i + c - 1:2 * c * i + c, :], (2 * c, G_DK))
             for i in range(nblk)], axis=0)
        qt = (qh * jnp.exp(jnp.minimum(a - a_ref, 0.0))).astype(BF16)
        kt = (kh * jnp.exp(jnp.minimum(a_ref - a, 0.0))).astype(BF16)
        valid = (row // (2 * c) == col // (2 * c)) & (row % (2 * c) >= c) & (col % (2 * c) < c)
        att = att + jnp.where(valid, _dot_nt(qt, kt), 0.0)
        c *= 2
    rows = lax.broadcasted_iota(jnp.int32, (sub, lg), 0)
    cols = lax.broadcasted_iota(jnp.int32, (sub, lg), 1)
    blocks = []
    for blk in range(lg // sub):
        r0 = blk * sub
        a_b = a[r0:r0 + sub, :]
        q_b = qh[r0:r0 + sub, :]
        acc = jnp.zeros((sub, lg), F32)
        for s in range(sub):
            e = jnp.exp(jnp.minimum(a_b - a[r0 + s:r0 + s + 1, :], 0.0))
            p = jnp.sum(q_b * e * kh[r0 + s:r0 + s + 1, :], axis=-1, keepdims=True)
            acc = jnp.where((cols == r0 + s) & (rows >= s), p, acc)
        blocks.append(acc)
    return att + jnp.concatenate(blocks, axis=0)


def _gla_chunk_guarded(seqs, gh_ref, *, lg, sub):
    ks = [slice(h * G_DK, (h + 1) * G_DK) for h in range(G_HEADS)]
    vs = [slice(h * G_DV, (h + 1) * G_DV) for h in range(G_HEADS)]
    pre = []
    for sq in seqs:
        a_last = sq.a[lg - 1:lg, :]
        qf = sq.qg.astype(F32)
        kf = sq.kg.astype(F32)
        pre.append(dict(a=sq.a, a_last=a_last, qf=qf, kf=kf,
                        qs=(qf * jnp.exp(sq.a)).astype(BF16),
                        kl=(kf * jnp.exp(a_last - sq.a)).astype(BF16)))
    yield
    units = [(sq, p, h) for sq, p in zip(seqs, pre) for h in range(G_HEADS)]
    s_old = [sq.s_src[h] for sq, _, h in units]
    att = [_gla_scores_guarded(p["qf"][:, ks[h]], p["kf"][:, ks[h]], p["a"][:, ks[h]], lg=lg, sub=sub)
           for _, p, h in units]
    yield
    o = [_dot(p["qs"][:, ks[h]], s_old[u].astype(BF16)) + _dot(att[u].astype(BF16), sq.vg[:, vs[h]])
         for u, (sq, p, h) in enumerate(units)]
    yield
    for u, (sq, _, h) in enumerate(units):
        sq.og_store(h, _rms(o[u], gh_ref[:, vs[h]]))
    yield
    for u, (sq, p, h) in enumerate(units):
        sq.s_dst[h] = (jnp.exp(_col_of_row(p["a_last"][:, ks[h]])) * s_old[u]
                       + _dot_tn(p["kl"][:, ks[h]], sq.vg[:, vs[h]]))


def _gla_block_direct(seqs, gh_ref, *, tb, lg):
    ks = [slice(h * G_DK, (h + 1) * G_DK) for h in range(G_HEADS)]
    vs = [slice(h * G_DV, (h + 1) * G_DV) for h in range(G_HEADS)]
    chunks = [slice(c * lg, (c + 1) * lg) for c in range(tb // lg)]
    row = lax.broadcasted_iota(jnp.int32, (lg, lg), 0)
    col = lax.broadcasted_iota(jnp.int32, (lg, lg), 1)
    causal = col <= row
    pre = []
    for sq in seqs:
        a_last = [sq.a[r.stop - 1:r.stop, :] for r in chunks]
        a_last_rows = jnp.concatenate([jnp.broadcast_to(al, (lg, G_KW)) for al in a_last], axis=0)
        qf = sq.qg.astype(F32)
        kf = sq.kg.astype(F32)
        pre.append(dict(a_last=a_last,
                        qs=(qf * jnp.exp(sq.a)).astype(BF16),
                        kt=(kf * jnp.exp(-sq.a)).astype(BF16),
                        kl=(kf * jnp.exp(a_last_rows - sq.a)).astype(BF16)))
    units = [(sq, p, h) for sq, p in zip(seqs, pre) for h in range(G_HEADS)]
    att = [[jnp.where(causal, _dot_nt(p["qs"][r, ks[h]], p["kt"][r, ks[h]]), 0.0).astype(BF16)
            for r in chunks] for _, p, h in units]
    s_cur = [sq.s_src[h] for sq, _, h in units]
    for c, r in enumerate(chunks):
        o = [_dot(p["qs"][r, ks[h]], s_cur[u].astype(BF16)) + _dot(att[u][c], sq.vg[r, vs[h]])
             for u, (sq, p, h) in enumerate(units)]
        for u, (sq, _, h) in enumerate(units):
            sq.og_rows_store(r, h, _rms(o[u], gh_ref[:, vs[h]]))
        s_cur = [jnp.exp(_col_of_row(p["a_last"][c][:, ks[h]])) * s_cur[u]
                 + _dot_tn(p["kl"][r, ks[h]], sq.vg[r, vs[h]])
                 for u, (sq, p, h) in enumerate(units)]
    for u, (sq, _, h) in enumerate(units):
        sq.s_dst[h] = s_cur[u]


def _shift_rows(rows, offset):
    if isinstance(rows, slice):
        return slice(rows.start + offset, rows.stop + offset)
    return pl.ds(rows.start + offset, rows.size)


def _gla_dispatch(la_ref, bi, row0, tb, lg, sub, gh_ref, make_record):
    chunk_decay = [la_ref[bi, row0 + c * lg + lg - 1:row0 + (c + 1) * lg, :] for c in range(tb // lg)]
    span_ok = jnp.min(functools.reduce(jnp.minimum, chunk_decay)) >= -GLA_SAFE_SPAN

    @pl.when(span_ok)
    def _():
        _gla_block_direct([make_record(slice(0, tb))], gh_ref, tb=tb, lg=lg)

    @pl.when(jnp.logical_not(span_ok))
    def _():
        def body(ci, carry):
            rows = pl.ds(pl.multiple_of(ci * lg, lg), lg)
            _interleave(_gla_chunk_guarded([make_record(rows)], gh_ref, lg=lg, sub=sub))
            return carry

        lax.fori_loop(0, tb // lg, body, 0)


def _long_mixer_kernel(x_ref, gif_ref, la_ref, cst_ref, *refs, tb, sub, lg, gsub):
    w = _pre_weights(*refs[:N_PRE_WEIGHTS])
    ggh_ref = refs[N_PRE_WEIGHTS]
    hm_ref, og_ref, cout_ref, c_out, n_out, m_out, s_out = refs[N_PRE_WEIGHTS + 1:N_PRE_WEIGHTS + 8]
    cbuf_ref, qm_s, km_s, vm_s, qg_s, kg_s, vg_s, c_scr, n_scr, m_scr, s_scr = refs[N_PRE_WEIGHTS + 8:]
    inter = types.SimpleNamespace(qm=qm_s, km=km_s, vm=vm_s, qg=qg_s, kg=kg_s, vg=vg_s)
    carried = (c_scr, n_scr, m_scr, s_scr)
    t = pl.program_id(1)
    nt = pl.num_programs(1)
    _load_conv_history(cbuf_ref, cst_ref, tb)

    @pl.when(t == 0)
    def _():
        for ref in carried:
            ref[...] = jnp.zeros(ref.shape, F32)

    def pre_stage(j):
        slot = j % 2
        return _pre_sub_block(x_ref, slice(j * sub, (j + 1) * sub), cbuf_ref, w, inter,
                              slice(slot * sub, (slot + 1) * sub), slot, bt=1, sub=sub, time_minor=True)

    def record(j, window=slice(0, sub)):
        slot = j % 2
        rows = _shift_rows(window, slot * sub)

        def hm_store(h, val):
            hm_ref[0, j, h * M_HEAD_DIM:(h + 1) * M_HEAD_DIM, :] = val

        def og_rows_store(r, h, val):
            og_ref[0, _shift_rows(r, j * sub), h * G_DV:(h + 1) * G_DV] = val

        return types.SimpleNamespace(
            qt=qm_s[0, slot], k=km_s[0, slot * sub:(slot + 1) * sub, :], vt=vm_s[0, slot], gift=gif_ref[0, j],
            qg=qg_s[0, rows, :], kg=kg_s[0, rows, :], vg=vg_s[0, rows, :],
            a=la_ref[0, _shift_rows(window, j * sub), :],
            hm_store=hm_store, og_rows_store=og_rows_store, og_store=functools.partial(og_rows_store, window),
            c_src=c_scr.at[0], n_src=n_scr.at[0], m_src=m_scr.at[0], s_src=s_scr.at[0],
            c_dst=c_scr.at[0], n_dst=n_scr.at[0], m_dst=m_scr.at[0], s_dst=s_scr.at[0])

    nsub = tb // sub
    _interleave(pre_stage(0))
    for j in range(nsub):
        stages = [_mlstm_chunk_tm([record(j)], lm=sub)]
        if j + 1 < nsub:
            stages.append(pre_stage(j + 1))
        _interleave(*stages)
        _gla_dispatch(la_ref, 0, j * sub, sub, lg, gsub, ggh_ref, functools.partial(record, j))
    cout_ref[...] = cbuf_ref[:, tb:tb + SUBLANES, :]

    @pl.when(t == nt - 1)
    def _():
        for out, ref in zip((c_out, n_out, m_out, s_out), carried):
            out[...] = ref[...]


STATE_SHAPES = ((M_HEADS, M_HEAD_DIM, M_HEAD_DIM), (M_HEADS, 1, M_HEAD_DIM), None, (G_HEADS, G_DK, G_DV))


def _state_shapes(bt, m_shape):
    return [(bt,) + (m_shape if shp is None else shp) for shp in STATE_SHAPES]


def _long_mixer(x, gif, la, cst, weights, ggh, *, tb, sub, lg, gsub):
    b, t, _ = x.shape
    assert t % tb == 0 and tb % sub == 0 and sub % lg == 0 and lg % gsub == 0 and sub % LANES == 0
    nsub = tb // sub
    tok = lambda w: pl.BlockSpec((1, tb, w), lambda i, j: (i, j, 0))
    per_seq = pl.BlockSpec((1, SUBLANES, M_WIDTH), lambda i, j: (i, 0, 0))
    states = _state_shapes(1, (SUBLANES, LANES))
    state_specs = [pl.BlockSpec(shp, lambda i, j, nd=len(shp): (i,) + (0,) * (nd - 1)) for shp in states]
    f32 = lambda shp: jax.ShapeDtypeStruct(shp, F32)
    slab = lambda w, dt: pltpu.VMEM((1, 2, w, sub), dt)
    rows2 = lambda w, dt: pltpu.VMEM((1, 2 * sub, w), dt)
    return pl.pallas_call(
        functools.partial(_long_mixer_kernel, tb=tb, sub=sub, lg=lg, gsub=gsub),
        grid=(b, t // tb),
        in_specs=([tok(D_MODEL), pl.BlockSpec((1, nsub, 2 * SUBLANES, sub), lambda i, j: (i, j, 0, 0)),
                   tok(G_KW), per_seq] + [_const_spec(shp) for shp in PRE_WEIGHT_SPECS]
                  + [_const_spec((1, G_VW))]),
        out_specs=[pl.BlockSpec((1, nsub, M_WIDTH, sub), lambda i, j: (i, j, 0, 0)), tok(G_VW), per_seq]
                  + state_specs,
        out_shape=[f32((b, t // sub, M_WIDTH, sub)), f32((b, t, G_VW)), f32((b, SUBLANES, M_WIDTH))]
                  + [f32((b,) + shp[1:]) for shp in states],
        scratch_shapes=[pltpu.VMEM((1, tb + SUBLANES, M_WIDTH), F32),
                        slab(M_WIDTH, BF16), rows2(M_WIDTH, BF16), slab(M_WIDTH, BF16),
                        rows2(G_KW, BF16), rows2(G_KW, BF16), rows2(G_VW, BF16)]
                       + [pltpu.VMEM(shp, F32) for shp in states],
        compiler_params=pltpu.CompilerParams(
            dimension_semantics=("parallel", "arbitrary"), vmem_limit_bytes=VMEM_LIMIT),
        name="mixer_long",
    )(x, gif, la, cst, *weights, ggh)


def _short_scan_kernel(*refs, bt, ts):
    qm, km, vm, gif, qg, kg, la, vg, ggh_ref, c_in, n_in, m_in, s_in = refs[:13]
    hm, og, c_out, n_out, m_out, s_out = refs[13:]

    def record(bi):
        def hm_store(h, val):
            hm[bi, :, h * M_HEAD_DIM:(h + 1) * M_HEAD_DIM] = val

        def og_store(h, val):
            og[bi, :, h * G_DV:(h + 1) * G_DV] = val

        return types.SimpleNamespace(
            q=qm[bi], k=km[bi], v=vm[bi], gif=gif[bi], qg=qg[bi], kg=kg[bi], vg=vg[bi], a=la[bi],
            hm_store=hm_store, og_store=og_store,
            c_src=c_in.at[bi], n_src=n_in.at[bi], m_src=m_in.at[bi], s_src=s_in.at[bi],
            c_dst=c_out.at[bi], n_dst=n_out.at[bi], m_dst=m_out.at[bi], s_dst=s_out.at[bi])

    seqs = [record(bi) for bi in range(bt)]
    _interleave(_mlstm_chunk(seqs, lm=ts), _gla_chunk_guarded(seqs, ggh_ref, lg=ts, sub=ts))


def _short_scan(per_token, ggh, state, *, bt):
    b, ts, _ = per_token[1].shape
    assert b % bt == 0 and ts % SUBLANES == 0
    lead = lambda shp: pl.BlockSpec(shp, lambda i, nd=len(shp): (i,) + (0,) * (nd - 1))
    tok = lambda w: lead((bt, ts, w))
    states = _state_shapes(bt, (1, LANES))
    f32 = lambda shp: jax.ShapeDtypeStruct(shp, F32)
    return pl.pallas_call(
        functools.partial(_short_scan_kernel, bt=bt, ts=ts),
        grid=(b // bt,),
        in_specs=[tok(M_WIDTH), tok(M_WIDTH), tok(M_WIDTH), tok(LANES), tok(G_KW), tok(G_KW), tok(G_KW),
                  tok(G_VW), _const_spec((1, G_VW))] + [lead(shp) for shp in states],
        out_specs=[tok(M_WIDTH), tok(G_VW)] + [lead(shp) for shp in states],
        out_shape=[f32((b, ts, M_WIDTH)), f32((b, ts, G_VW))] + [f32((b,) + shp[1:]) for shp in states],
        compiler_params=pltpu.CompilerParams(
            dimension_semantics=("parallel",), vmem_limit_bytes=VMEM_LIMIT),
        name="mixer_scan_short",
    )(*per_token, ggh, *state)


POST_O, POST_R, POST_A, POST_B = 0, 1024, 2048, 3072


def _post_kernel(x_ref, hm_ref, og_ref, g_ref, gmh_ref, w_ref, wpa_ref, wpb_ref, wo_ref, o_ref, *,
                 time_minor):
    x = x_ref[...]
    xn = _rms(x, g_ref[...]).astype(BF16)
    gate = lambda off: _dot(xn, w_ref[:, off:off + D_MODEL])
    if time_minor:
        hm = jnp.concatenate(
            [jnp.concatenate([jnp.transpose(hm_ref[0, j, h * M_HEAD_DIM:(h + 1) * M_HEAD_DIM, :])
                              for h in range(M_HEADS)], axis=1)
             for j in range(hm_ref.shape[1])], axis=0)
    else:
        hm = hm_ref[...]
    hm = (_sigmoid(gate(POST_O)) * (hm * gmh_ref[...])).astype(BF16)
    og = (_silu(gate(POST_R)) * og_ref[...]).astype(BF16)
    y = (_sigmoid(gate(POST_A)) * _dot(hm, wpa_ref[...])
         + _sigmoid(gate(POST_B)) * _dot(og, wpb_ref[...]))
    o_ref[...] = x + _dot(y.astype(BF16), wo_ref[...])


def _post(x2d, hm, og2d, g, gmh, w_post, wpa, wpb, wo, *, tm, time_minor):
    n = x2d.shape[0]
    assert n % tm == 0
    tok = pl.BlockSpec((tm, D_MODEL), lambda i: (i, 0))
    if time_minor:
        nblk, blk = hm.shape[1], hm.shape[3]
        assert tm % blk == 0 and (nblk * blk) % tm == 0
        per_seq = nblk * blk // tm
        hm_spec = pl.BlockSpec((1, tm // blk, M_WIDTH, blk), lambda i: (i // per_seq, i % per_seq, 0, 0))
    else:
        hm_spec = tok
    return pl.pallas_call(
        functools.partial(_post_kernel, time_minor=time_minor),
        grid=(n // tm,),
        in_specs=[tok, hm_spec, tok, _const_spec((1, D_MODEL)), _const_spec((1, M_WIDTH)),
                  _const_spec((D_MODEL, 4 * D_MODEL)),
                  _const_spec((M_WIDTH, D_MODEL)), _const_spec((G_VW, D_MODEL)),
                  _const_spec((D_MODEL, D_MODEL))],
        out_specs=tok,
        out_shape=jax.ShapeDtypeStruct((n, D_MODEL), F32),
        compiler_params=pltpu.CompilerParams(
            dimension_semantics=("parallel",), vmem_limit_bytes=VMEM_LIMIT),
        name="mixer_post",
    )(x2d, hm, og2d, g, gmh, w_post, wpa, wpb, wo)


def _pack_weights(g_ffn1, w_ffn1_up, w_ffn1_down, g_mix, w_in, conv_w, conv_b, w_mq, w_mk, b_if,
                  g_mhead, w_a2, b_a, g_ghead, w_pa, w_pb, w_o, g_ffn2, w_ffn2_up, w_ffn2_down,
                  g_final):
    l = 0
    row = lambda v: v.reshape(1, -1).astype(F32)
    edges = [0]
    for wd in (M_WIDTH, M_WIDTH, M_WIDTH, 2 * M_HEADS, G_KW, G_KW, G_VW, G_VW, G_RANK, D_MODEL, D_MODEL):
        edges.append(edges[-1] + wd)
    seg = lambda i: w_in[l][:, edges[i]:edges[i + 1]]
    u_m, v_m, o_m, if_m, q_g, k_g, v_g, r_g, a_g, g_a, g_b = (seg(i) for i in range(11))
    w_pre = jnp.concatenate([u_m, v_m, q_g, k_g, v_g], axis=1).astype(BF16)
    w_small = jnp.concatenate(
        [if_m, a_g, jnp.zeros((D_MODEL, LANES - 2 * M_HEADS - G_RANK), F32)], axis=1).astype(BF16)
    w_post = jnp.concatenate([o_m, r_g, g_a, g_b], axis=1).astype(BF16)
    bif = jnp.concatenate([b_if[l].reshape(-1), jnp.zeros((LANES - 2 * M_HEADS,), F32)]).reshape(1, LANES)
    wa2 = jnp.zeros((LANES, G_KW), F32).at[A_LANE0:A_LANE0 + G_RANK].set(w_a2[l]).astype(BF16)
    return dict(
        g_ffn1=row(g_ffn1[l]), w_ffn1_up=w_ffn1_up[l].astype(BF16), w_ffn1_down=w_ffn1_down[l].astype(BF16),
        g_mix=row(g_mix[l]), w_pre=w_pre, w_small=w_small, w_post=w_post,
        conv_w=conv_w[l].astype(F32), conv_b=row(conv_b[l]),
        w_mq=w_mq[l].astype(BF16), w_mk=w_mk[l].astype(BF16), bif=bif,
        g_mhead=row(g_mhead[l]), wa2=wa2, b_a=row(b_a[l]), g_ghead=row(g_ghead[l]),
        w_pa=w_pa[l].astype(BF16), w_pb=w_pb[l].astype(BF16), w_o=w_o[l].astype(BF16),
        g_ffn2=row(g_ffn2[l]), w_ffn2_up=w_ffn2_up[l].astype(BF16), w_ffn2_down=w_ffn2_down[l].astype(BF16),
        g_final=row(g_final))


LONG_BLOCK = 512
LONG_CHUNK = 256
LONG_GLA_CHUNK = 128
GLA_PAIRWISE_BLOCK = 16
SHORT_SEQS_PER_STEP = 4
SHORT_PRE_ROWS = 512
POST_ROWS = 512
FFN_ROWS = 1024
FFN_GATES_ROWS = 512


def _pre_weight_args(w):
    return (w["g_mix"], w["w_pre"], w["conv_w"], w["conv_b"], w["w_mq"], w["w_mk"])


def _gate_weight_args(w):
    return (w["g_mix"], w["w_small"], w["bif"], w["wa2"], w["b_a"])


def _long_group(x1, gif, la, b, t, w):
    assert t % LONG_BLOCK == 0
    cst = jnp.zeros((b, SUBLANES, M_WIDTH), F32)
    hm, og, cout, c_new, n_new, m_new, s_new = _long_mixer(
        x1.reshape(b, t, D_MODEL), gif.reshape(b, t // LONG_CHUNK, 2 * SUBLANES, LONG_CHUNK),
        la.reshape(b, t, G_KW), cst, _pre_weight_args(w), w["g_ghead"], tb=LONG_BLOCK, sub=LONG_CHUNK,
        lg=LONG_GLA_CHUNK, gsub=GLA_PAIRWISE_BLOCK)
    x2 = _post(x1, hm, og.reshape(b * t, G_VW), w["g_mix"], w["g_mhead"], w["w_post"], w["w_pa"], w["w_pb"],
               w["w_o"], tm=math.gcd(t, POST_ROWS), time_minor=True)
    return x2, (cout[:, SUBLANES - (CONV_W - 1):, :][None], c_new[None],
                n_new.reshape(b, M_HEADS, M_HEAD_DIM)[None], m_new[:, :M_HEADS, 0][None], s_new[None])


def _short_group(x1, gif, la, b, t, state, w):
    assert t <= 64 and t % SUBLANES == 0
    conv0, c0, n0, m0, s0 = state
    cst = jnp.concatenate([jnp.zeros((b, SUBLANES - (CONV_W - 1), M_WIDTH), F32), conv0.astype(F32)], axis=1)
    m0p = jnp.concatenate([m0, jnp.zeros((b, LANES - M_HEADS), F32)], axis=1).reshape(b, 1, LANES)
    bt = max(1, min(b, SHORT_PRE_ROWS // t))
    while b % bt:
        bt -= 1
    qm, km, vm, qg, kg, vg, cout = _pre(x1.reshape(b, t, D_MODEL), cst, _pre_weight_args(w), bt=bt, tb=t)
    per_token = (qm, km, vm, gif.reshape(b, t, LANES), qg, kg, la.reshape(b, t, G_KW), vg)
    hm, og, c_new, n_new, m_new, s_new = _short_scan(
        per_token, w["g_ghead"], (c0, n0.reshape(b, M_HEADS, 1, M_HEAD_DIM), m0p, s0),
        bt=math.gcd(b, SHORT_SEQS_PER_STEP))
    n = b * t
    x2 = _post(x1, hm.reshape(n, M_WIDTH), og.reshape(n, G_VW), w["g_mix"], w["g_mhead"], w["w_post"],
               w["w_pa"], w["w_pb"], w["w_o"], tm=math.gcd(n, POST_ROWS), time_minor=False)
    return x2, (cout[:, SUBLANES - (CONV_W - 1):, :][None], c_new[None],
                n_new.reshape(b, M_HEADS, M_HEAD_DIM)[None], m_new[:, 0, :M_HEADS][None], s_new[None])


def kernel(x_prompt, x_sample, state_conv, state_mlstm_C, state_mlstm_n, state_mlstm_m, state_gla_S,
           g_ffn1, w_ffn1_up, w_ffn1_down, g_mix, w_in, conv_w, conv_b, w_mq, w_mk, b_if, g_mhead,
           w_a2, b_a, g_ghead, w_pa, w_pb, w_o, g_ffn2, w_ffn2_up, w_ffn2_down, g_final):
    w = _pack_weights(g_ffn1, w_ffn1_up, w_ffn1_down, g_mix, w_in, conv_w, conv_b, w_mq, w_mk, b_if,
                      g_mhead, w_a2, b_a, g_ghead, w_pa, w_pb, w_o, g_ffn2, w_ffn2_up, w_ffn2_down,
                      g_final)
    bp, tp, _ = x_prompt.shape
    bs, ts, _ = x_sample.shape
    n_common = math.gcd(bp * tp, bs * ts)
    tm_ffn = math.gcd(n_common, FFN_ROWS)
    gate_cfg = (dict(chunk=LONG_CHUNK, lg=LONG_GLA_CHUNK, time_minor=True), dict(chunk=ts, lg=ts, time_minor=False))
    x1_p, x1_s, gif_p, la_p, gif_s, la_s = _ffn(
        x_prompt.reshape(bp * tp, D_MODEL), x_sample.reshape(bs * ts, D_MODEL), w["g_ffn1"], w["w_ffn1_up"],
        w["w_ffn1_down"], w["g_final"], final_norm=False, tm=math.gcd(n_common, FFN_GATES_ROWS),
        gate_weights=_gate_weight_args(w), gate_cfg=gate_cfg)
    x2_p, states_p = _long_group(x1_p, gif_p, la_p, bp, tp, w)
    state = (state_conv[0], state_mlstm_C[0], state_mlstm_n[0], state_mlstm_m[0], state_gla_S[0])
    x2_s, states_s = _short_group(x1_s, gif_s, la_s, bs, ts, state, w)
    y_p, y_s = _ffn(x2_p, x2_s, w["g_ffn2"], w["w_ffn2_up"], w["w_ffn2_down"], w["g_final"],
                    final_norm=True, tm=tm_ffn)
    return (y_p.reshape(bp, tp, D_MODEL), y_s.reshape(bs, ts, D_MODEL)) + states_p + states_s
```

```python
import functools
import itertools
import math
import types

import jax
import jax.numpy as jnp
from jax import lax
from jax.experimental import pallas as pl
from jax.experimental.pallas import tpu as pltpu

D_MODEL = 1024
M_HEADS = 4
M_HEAD_DIM = 256
M_WIDTH = 1024
CONV_W = 4
G_HEADS = 4
G_DK = 128
G_DV = 256
G_KW = 512
G_VW = 1024
G_RANK = 16
G_TAU = 16.0
D_FF = 2816
EPS = 1e-6

LANES = 128
SUBLANES = 8
BF16_ROWS = 16
VMEM_LIMIT = 56 * 1024 * 1024

F32 = jnp.float32
BF16 = jnp.bfloat16
NEG_INF = float("-inf")


def _sigmoid(x):
    return 1.0 / (1.0 + jnp.exp(-x))


def _silu(x):
    return x * _sigmoid(x)


def _log_sigmoid(x):
    return jnp.minimum(x, 0.0) - jnp.log(1.0 + jnp.exp(-jnp.abs(x)))


def _rms(x, g):
    return x * lax.rsqrt(jnp.mean(x * x, axis=-1, keepdims=True) + EPS) * g


def _dot(a, b):
    return jnp.dot(a, b, preferred_element_type=F32)


def _dot_nt(a, b):
    return lax.dot_general(a, b, (((1,), (1,)), ((), ())), preferred_element_type=F32)


def _dot_tn(a, b):
    return lax.dot_general(a, b, (((0,), (0,)), ((), ())), preferred_element_type=F32)


def _const_spec(shape):
    nd = len(shape)
    return pl.BlockSpec(shape, lambda *_: (0,) * nd, pipeline_mode=pl.Buffered(1))


FF_CHUNK = 256


def _ffn_kernel(xa_ref, xb_ref, g_ref, wup_ref, wdown_ref, gfin_ref, oa_ref, ob_ref, h_ref, *,
                final_norm, tiles_a):
    def half_step(x_ref, o_ref):
        x = x_ref[...]
        xn = _rms(x, g_ref[...]).astype(BF16)
        for c in range(D_FF // FF_CHUNK):
            a = _dot(xn, wup_ref[:, c * FF_CHUNK:(c + 1) * FF_CHUNK])
            g = _dot(xn, wup_ref[:, D_FF + c * FF_CHUNK:D_FF + (c + 1) * FF_CHUNK])
            h_ref[:, c * FF_CHUNK:(c + 1) * FF_CHUNK] = (_silu(g) * a).astype(BF16)
        y = x + 0.5 * _dot(h_ref[...], wdown_ref[...])
        o_ref[...] = _rms(y, gfin_ref[...]) if final_norm else y

    on_a = pl.program_id(0) < tiles_a
    pl.when(on_a)(lambda: half_step(xa_ref, oa_ref))
    pl.when(jnp.logical_not(on_a))(lambda: half_step(xb_ref, ob_ref))


def _ffn(xa, xb, g, wup, wdown, gfin, *, final_norm, tm):
    na, nb = xa.shape[0], xb.shape[0]
    assert na % tm == 0 and nb % tm == 0
    tiles_a, tiles_b = na // tm, nb // tm
    spec_a = pl.BlockSpec((tm, D_MODEL), lambda i: (jnp.minimum(i, tiles_a - 1), 0))
    spec_b = pl.BlockSpec((tm, D_MODEL), lambda i: (jnp.maximum(i - tiles_a, 0), 0),
                          **(dict(pipeline_mode=pl.Buffered(1)) if tiles_b == 1 else {}))
    return pl.pallas_call(
        functools.partial(_ffn_kernel, final_norm=final_norm, tiles_a=tiles_a),
        grid=(tiles_a + tiles_b,),
        in_specs=[
            spec_a, spec_b,
            _const_spec((1, D_MODEL)),
            _const_spec((D_MODEL, 2 * D_FF)),
            _const_spec((D_FF, D_MODEL)),
            _const_spec((1, D_MODEL)),
        ],
        out_specs=[spec_a, spec_b],
        out_shape=[jax.ShapeDtypeStruct((na, D_MODEL), F32), jax.ShapeDtypeStruct((nb, D_MODEL), F32)],
        scratch_shapes=[pltpu.VMEM((tm, D_FF), BF16)],
        compiler_params=pltpu.CompilerParams(
            dimension_semantics=("arbitrary",), vmem_limit_bytes=VMEM_LIMIT),
        name="ffn_final" if final_norm else "ffn",
    )(xa, xb, g, wup, wdown, gfin)


def _interleave(*stage_generators):
    for _ in itertools.zip_longest(*stage_generators):
        pass


PRE_U, PRE_V, PRE_QG, PRE_KG, PRE_VG = 0, 1024, 2048, 2560, 3072
PRE_WIDTH = 4096
A_LANE0 = 8


def _pre_sub_block(x_ref, x_rows, cbuf_ref, w, out, o_rows, o_slab, *, bt, sub, lg, time_minor):
    m = bt * sub
    r0 = x_rows.start
    to_rows = lambda y, width: y.reshape(bt, sub, width)
    xn = _rms(x_ref[:, x_rows, :].reshape(m, D_MODEL), w.g[...]).astype(BF16)
    small = _dot(xn, w.small[...])

    u = _dot(xn, w.proj[:, PRE_U:PRE_U + M_WIDTH])
    cbuf_ref[:, SUBLANES + r0:SUBLANES + r0 + sub, :] = to_rows(u, M_WIDTH)
    c = w.conv_b[...].reshape(1, 1, M_WIDTH)
    for tap in range(CONV_W):
        off = SUBLANES - (CONV_W - 1) + tap + r0
        c = c + cbuf_ref[:, off:off + sub, :] * w.conv_w[tap:tap + 1, :].reshape(1, 1, M_WIDTH)
    ch = _silu(c).reshape(m, M_WIDTH).astype(BF16)
    yield

    vm = _dot(xn, w.proj[:, PRE_V:PRE_V + M_WIDTH])
    gates = small + w.bif[...]
    lane = lax.broadcasted_iota(jnp.int32, gates.shape, 1)
    gif = jnp.where(lane < M_HEADS, gates, jnp.where(lane < 2 * M_HEADS, _log_sigmoid(gates), 0.0))
    if time_minor:
        b = pltpu.roll(_scan_rows(gif, jnp.add, 0.0), LANES - M_HEADS, axis=1)
        g = gif - b
        gmax = _scan_rows(g, jnp.maximum, NEG_INF)
        packed = jnp.where(lane < M_HEADS, g,
                           jnp.where(lane < 2 * M_HEADS, pltpu.roll(gmax, M_HEADS, axis=1),
                                     jnp.where(lane < 3 * M_HEADS, pltpu.roll(b, 2 * M_HEADS, axis=1), 0.0)))
        out.gif[0, o_slab] = jnp.transpose(packed)[0:2 * SUBLANES, :]
    else:
        out.gif[:, o_rows, :] = to_rows(gif, LANES)
        out.vm[:, o_rows, :] = to_rows(vm.astype(BF16), M_WIDTH)
    yield

    def decay_columns(h):
        cols = slice(h * G_DK, (h + 1) * G_DK)
        a2 = _dot(small.astype(BF16), w.a2[:, cols]) + w.b_a[:, cols]
        out.la[:, o_rows, cols] = to_rows(
            _scan_rows(_log_sigmoid(a2) * (1.0 / G_TAU), jnp.add, 0.0, segment=lg), G_DK)

    out.qg[:, o_rows, :] = to_rows(
        (_dot(xn, w.proj[:, PRE_QG:PRE_QG + G_KW]) * (G_DK ** -0.5)).astype(BF16), G_KW)
    decay_columns(0)
    yield
    out.kg[:, o_rows, :] = to_rows(_dot(xn, w.proj[:, PRE_KG:PRE_KG + G_KW]).astype(BF16), G_KW)
    decay_columns(1)
    yield
    half = G_VW // 2
    for p in range(2):
        out.vg[:, o_rows, p * half:(p + 1) * half] = to_rows(
            _dot(xn, w.proj[:, PRE_VG + p * half:PRE_VG + (p + 1) * half]).astype(BF16), half)
        decay_columns(2 + p)
        yield

    for h in range(M_HEADS):
        sl = slice(h * M_HEAD_DIM, (h + 1) * M_HEAD_DIM)
        qh = _dot(ch[:, sl], w.mq[h])
        if time_minor:
            out.qm[0, o_slab, sl, :] = jnp.transpose(qh).astype(BF16)
            out.vm[0, o_slab, sl, :] = jnp.transpose(vm[:, sl]).astype(BF16)
        else:
            out.qm[:, o_rows, sl] = to_rows(qh.astype(BF16), M_HEAD_DIM)
        out.km[:, o_rows, sl] = to_rows(
            (_dot(ch[:, sl], w.mk[h]) * (M_HEAD_DIM ** -0.5)).astype(BF16), M_HEAD_DIM)
        if h % 2:
            yield


def _pre_weights(g_ref, w_ref, ws_ref, cw_ref, cb_ref, wq_ref, wk_ref, bif_ref, wa2_ref, ba_ref):
    return types.SimpleNamespace(g=g_ref, proj=w_ref, small=ws_ref, conv_w=cw_ref, conv_b=cb_ref,
                                 mq=wq_ref, mk=wk_ref, bif=bif_ref, a2=wa2_ref, b_a=ba_ref)


def _load_conv_history(cbuf_ref, cst_ref, tb):
    t = pl.program_id(1)

    @pl.when(t == 0)
    def _():
        cbuf_ref[:, 0:SUBLANES, :] = cst_ref[...]

    @pl.when(t > 0)
    def _():
        cbuf_ref[:, 0:SUBLANES, :] = cbuf_ref[:, tb:tb + SUBLANES, :]


PRE_WEIGHT_SPECS = (
    (1, D_MODEL), (D_MODEL, PRE_WIDTH), (D_MODEL, LANES), (CONV_W, M_WIDTH), (1, M_WIDTH),
    (M_HEADS, M_HEAD_DIM, M_HEAD_DIM), (M_HEADS, M_HEAD_DIM, M_HEAD_DIM), (1, LANES), (LANES, G_KW),
    (1, G_KW))


def _pre_kernel(x_ref, cst_ref, *refs, bt, tb, lg):
    w = _pre_weights(*refs[:10])
    qm_ref, km_ref, vm_ref, gif_ref, qg_ref, kg_ref, la_ref, vg_ref, cout_ref, cbuf_ref = refs[10:]
    out = types.SimpleNamespace(qm=qm_ref, km=km_ref, vm=vm_ref, gif=gif_ref, qg=qg_ref, kg=kg_ref,
                                la=la_ref, vg=vg_ref)
    _load_conv_history(cbuf_ref, cst_ref, tb)
    rows = slice(0, tb)
    _interleave(_pre_sub_block(x_ref, rows, cbuf_ref, w, out, rows, 0, bt=bt, sub=tb, lg=lg,
                               time_minor=False))
    cout_ref[...] = cbuf_ref[:, tb:tb + SUBLANES, :]


def _pre(x, cst, weights, *, bt, tb, lg):
    b, t, _ = x.shape
    assert b % bt == 0 and t % tb == 0 and tb % SUBLANES == 0 and tb % lg == 0
    tok = lambda w: pl.BlockSpec((bt, tb, w), lambda i, j: (i, j, 0))
    per_seq = pl.BlockSpec((bt, SUBLANES, M_WIDTH), lambda i, j: (i, 0, 0))
    sds = lambda w, dt: jax.ShapeDtypeStruct((b, t, w), dt)
    return pl.pallas_call(
        functools.partial(_pre_kernel, bt=bt, tb=tb, lg=lg),
        grid=(b // bt, t // tb),
        in_specs=[tok(D_MODEL), per_seq] + [_const_spec(shp) for shp in PRE_WEIGHT_SPECS],
        out_specs=[tok(M_WIDTH), tok(M_WIDTH), tok(M_WIDTH), tok(LANES), tok(G_KW), tok(G_KW),
                   tok(G_KW), tok(G_VW), per_seq],
        out_shape=[sds(M_WIDTH, BF16), sds(M_WIDTH, BF16), sds(M_WIDTH, BF16), sds(LANES, F32),
                   sds(G_KW, BF16), sds(G_KW, BF16), sds(G_KW, F32), sds(G_VW, BF16),
                   jax.ShapeDtypeStruct((b, SUBLANES, M_WIDTH), F32)],
        scratch_shapes=[pltpu.VMEM((bt, tb + SUBLANES, M_WIDTH), F32)],
        compiler_params=pltpu.CompilerParams(
            dimension_semantics=("parallel", "arbitrary"), vmem_limit_bytes=VMEM_LIMIT),
        name="mixer_pre",
    )(x, cst, *weights)


def _scan_rows(x, op, fill, segment=None):
    n = segment or x.shape[0]
    row = lax.broadcasted_iota(jnp.int32, x.shape, 0) % n
    s = 1
    while s < n:
        x = op(x, jnp.where(row >= s, pltpu.roll(x, s, axis=0), fill))
        s *= 2
    return x


def _col_of_row(row):
    return jnp.transpose(jnp.broadcast_to(row, (SUBLANES, LANES)))[:, 0:1]


def _mlstm_chunk_tm(seqs, *, lm):
    reps = lm // LANES
    head_row = lax.broadcasted_iota(jnp.int32, (SUBLANES, LANES), 0) < M_HEADS
    src = lax.broadcasted_iota(jnp.int32, (lm, lm), 0)
    dst = lax.broadcasted_iota(jnp.int32, (lm, lm), 1)
    causal = src <= dst
    sl = [slice(h * M_HEAD_DIM, (h + 1) * M_HEAD_DIM) for h in range(M_HEADS)]
    row = lambda x, h: x[h:h + 1, :]
    gates = []
    for sq in seqs:
        g = sq.gift[0:SUBLANES]
        gmax = pltpu.roll(g, M_HEADS, axis=0)
        b = sq.gift[SUBLANES:2 * SUBLANES]
        m_prev = jnp.concatenate([sq.m_src[...]] * reps, axis=1)
        mx = jnp.maximum(m_prev, gmax)
        mx_last = jnp.broadcast_to(mx[:, lm - 1:lm], (SUBLANES, lm))
        gates.append(dict(
            mx=mx,
            inter=jnp.exp(m_prev - mx),
            nfloor=jnp.exp(-(b + mx)),
            w_in=jnp.exp(g - mx_last),
            decay=jnp.exp(m_prev - mx_last),
            g_col=jnp.transpose(g)))
        m_new = jnp.broadcast_to(b[:, lm - 1:lm], (SUBLANES, lm)) + mx_last
        sq.m_dst[...] = jnp.where(head_row, m_new[:, :LANES], 0.0)
    yield
    units = [(sq, gt, h) for sq, gt in zip(seqs, gates) for h in range(M_HEADS)]
    c_old = [sq.c_src[h] for sq, _, h in units]
    n_old = [sq.n_src[h] for sq, _, h in units]
    cn = [jnp.concatenate([c_old[u].astype(BF16),
                           jnp.broadcast_to(n_old[u].astype(BF16), (BF16_ROWS, M_HEAD_DIM))], axis=0)
          for u in range(len(units))]
    cq = [_dot(cn[u], sq.qt[sl[h], :]) for u, (sq, _, h) in enumerate(units)]
    yield
    dm = [jnp.exp(jnp.where(causal, gt["g_col"][:, h:h + 1] - row(gt["mx"], h), NEG_INF))
          for _, gt, h in units]
    st = [_dot(sq.k[:, sl[h]], sq.qt[sl[h], :]) * dm[u]
          for u, (sq, _, h) in enumerate(units)]
    yield
    num = [row(gt["inter"], h) * cq[u][:M_HEAD_DIM] + _dot(sq.vt[sl[h], :], st[u].astype(BF16))
           for u, (sq, gt, h) in enumerate(units)]
    yield
    for u, (sq, gt, h) in enumerate(units):
        den = (row(gt["inter"], h) * cq[u][M_HEAD_DIM:M_HEAD_DIM + 1]
               + jnp.sum(st[u], axis=0, keepdims=True))
        rcp = 1.0 / jnp.maximum(jnp.abs(den), row(gt["nfloor"], h))
        ms = jnp.mean(num[u] * num[u], axis=0, keepdims=True)
        sq.hm_store(h, num[u] * (rcp * lax.rsqrt(rcp * rcp * ms + EPS)))
    yield
    for u, (sq, gt, h) in enumerate(units):
        w_h = row(gt["w_in"], h)
        dec_h = jnp.concatenate([row(gt["decay"], h)[:, :LANES]] * (M_HEAD_DIM // LANES), axis=1)
        wv = jnp.concatenate([(sq.vt[sl[h], :].astype(F32) * w_h).astype(BF16),
                              jnp.broadcast_to(w_h.astype(BF16), (BF16_ROWS, lm))], axis=0)
        upd = _dot(wv, sq.k[:, sl[h]])
        sq.c_dst[h] = dec_h * c_old[u] + upd[:M_HEAD_DIM]
        sq.n_dst[h] = dec_h * n_old[u] + upd[M_HEAD_DIM:M_HEAD_DIM + 1]


def _mlstm_chunk(seqs, *, lm):
    row = lax.broadcasted_iota(jnp.int32, (lm, lm), 0)
    col = lax.broadcasted_iota(jnp.int32, (lm, lm), 1)
    causal = col <= row
    sl = [slice(h * M_HEAD_DIM, (h + 1) * M_HEAD_DIM) for h in range(M_HEADS)]
    gates = []
    for sq in seqs:
        cs = _scan_rows(sq.gif, jnp.add, 0.0)
        b = pltpu.roll(cs, LANES - M_HEADS, axis=1)
        g = sq.gif - b
        m_prev = sq.m_src[...]
        mx = jnp.maximum(m_prev, _scan_rows(g, jnp.maximum, NEG_INF))
        mx_last = mx[lm - 1:lm, :]
        gates.append(dict(
            mx=mx,
            inter=jnp.exp(m_prev - mx),
            nfloor=jnp.exp(-(b + mx)),
            g_t=jnp.transpose(g),
            w_in=jnp.exp(g - mx_last),
            decay=jnp.exp(m_prev - mx_last)))
        sq.m_dst[...] = b[lm - 1:lm, :] + mx_last
    yield
    units = [(sq, gt, h) for sq, gt in zip(seqs, gates) for h in range(M_HEADS)]
    col_of = lambda x, h: x[:, h:h + 1]
    c_old = [sq.c_src[h] for sq, _, h in units]
    n_old = [sq.n_src[h] for sq, _, h in units]
    dm = [jnp.exp(jnp.where(causal, gt["g_t"][h:h + 1, :] - col_of(gt["mx"], h), NEG_INF))
          for _, gt, h in units]
    qn = [jnp.sum(sq.q[:, sl[h]].astype(F32) * n_old[u], axis=-1, keepdims=True)
          for u, (sq, _, h) in enumerate(units)]
    s = [_dot_nt(sq.q[:, sl[h]], sq.k[:, sl[h]]) * dm[u] for u, (sq, _, h) in enumerate(units)]
    yield
    num = [col_of(gt["inter"], h) * _dot_nt(sq.q[:, sl[h]], c_old[u].astype(BF16))
           + _dot(s[u].astype(BF16), sq.v[:, sl[h]]) for u, (sq, gt, h) in enumerate(units)]
    yield
    for u, (sq, gt, h) in enumerate(units):
        den = col_of(gt["inter"], h) * qn[u] + jnp.sum(s[u], axis=-1, keepdims=True)
        rcp = 1.0 / jnp.maximum(jnp.abs(den), col_of(gt["nfloor"], h))
        ms = jnp.mean(num[u] * num[u], axis=-1, keepdims=True)
        sq.hm_store(h, num[u] * (rcp * lax.rsqrt(rcp * rcp * ms + EPS)))
    yield
    for u, (sq, gt, h) in enumerate(units):
        w_h = col_of(gt["w_in"], h)
        dec_h = col_of(gt["decay"], h)
        wv = (w_h * sq.v[:, sl[h]].astype(F32)).astype(BF16)
        sq.c_dst[h] = dec_h * c_old[u] + _dot_tn(wv, sq.k[:, sl[h]])
        sq.n_dst[h] = dec_h * n_old[u] + jnp.sum(w_h * sq.k[:, sl[h]].astype(F32), axis=0, keepdims=True)


GLA_SAFE_SPAN = 40.0


def _gla_scores_guarded(qh, kh, a, *, lg, sub):
    row = lax.broadcasted_iota(jnp.int32, (lg, lg), 0)
    col = lax.broadcasted_iota(jnp.int32, (lg, lg), 1)
    att = jnp.zeros((lg, lg), F32)
    c = sub
    while c < lg:
        nblk = lg // (2 * c)
        a_ref = jnp.concatenate(
            [jnp.broadcast_to(a[2 * c * i + c - 1:2 * c * i + c, :], (2 * c, G_DK))
             for i in range(nblk)], axis=0)
        qt = (qh * jnp.exp(jnp.minimum(a - a_ref, 0.0))).astype(BF16)
        kt = (kh * jnp.exp(jnp.minimum(a_ref - a, 0.0))).astype(BF16)
        valid = (row // (2 * c) == col // (2 * c)) & (row % (2 * c) >= c) & (col % (2 * c) < c)
        att = att + jnp.where(valid, _dot_nt(qt, kt), 0.0)
        c *= 2
    rows = lax.broadcasted_iota(jnp.int32, (sub, lg), 0)
    cols = lax.broadcasted_iota(jnp.int32, (sub, lg), 1)
    blocks = []
    for blk in range(lg // sub):
        r0 = blk * sub
        a_b = a[r0:r0 + sub, :]
        q_b = qh[r0:r0 + sub, :]
        acc = jnp.zeros((sub, lg), F32)
        for s in range(sub):
            e = jnp.exp(jnp.minimum(a_b - a[r0 + s:r0 + s + 1, :], 0.0))
            p = jnp.sum(q_b * e * kh[r0 + s:r0 + s + 1, :], axis=-1, keepdims=True)
            acc = jnp.where((cols == r0 + s) & (rows >= s), p, acc)
        blocks.append(acc)
    return att + jnp.concatenate(blocks, axis=0)


def _gla_chunk_guarded(seqs, gh_ref, *, lg, sub):
    ks = [slice(h * G_DK, (h + 1) * G_DK) for h in range(G_HEADS)]
    vs = [slice(h * G_DV, (h + 1) * G_DV) for h in range(G_HEADS)]
    pre = []
    for sq in seqs:
        a_last = sq.a[lg - 1:lg, :]
        qf = sq.qg.astype(F32)
        kf = sq.kg.astype(F32)
        pre.append(dict(a=sq.a, a_last=a_last, qf=qf, kf=kf,
                        qs=(qf * jnp.exp(sq.a)).astype(BF16),
                        kl=(kf * jnp.exp(a_last - sq.a)).astype(BF16)))
    yield
    units = [(sq, p, h) for sq, p in zip(seqs, pre) for h in range(G_HEADS)]
    s_old = [sq.s_src[h] for sq, _, h in units]
    att = [_gla_scores_guarded(p["qf"][:, ks[h]], p["kf"][:, ks[h]], p["a"][:, ks[h]], lg=lg, sub=sub)
           for _, p, h in units]
    yield
    o = [_dot(p["qs"][:, ks[h]], s_old[u].astype(BF16)) + _dot(att[u].astype(BF16), sq.vg[:, vs[h]])
         for u, (sq, p, h) in enumerate(units)]
    yield
    for u, (sq, _, h) in enumerate(units):
        sq.og_store(h, _rms(o[u], gh_ref[:, vs[h]]))
    yield
    for u, (sq, p, h) in enumerate(units):
        sq.s_dst[h] = (jnp.exp(_col_of_row(p["a_last"][:, ks[h]])) * s_old[u]
                       + _dot_tn(p["kl"][:, ks[h]], sq.vg[:, vs[h]]))


def _gla_block_direct(seqs, gh_ref, *, tb, lg):
    ks = [slice(h * G_DK, (h + 1) * G_DK) for h in range(G_HEADS)]
    vs = [slice(h * G_DV, (h + 1) * G_DV) for h in range(G_HEADS)]
    chunks = [slice(c * lg, (c + 1) * lg) for c in range(tb // lg)]
    row = lax.broadcasted_iota(jnp.int32, (lg, lg), 0)
    col = lax.broadcasted_iota(jnp.int32, (lg, lg), 1)
    causal = col <= row
    pre = []
    for sq in seqs:
        a_last = [sq.a[r.stop - 1:r.stop, :] for r in chunks]
        a_last_rows = jnp.concatenate([jnp.broadcast_to(al, (lg, G_KW)) for al in a_last], axis=0)
        qf = sq.qg.astype(F32)
        kf = sq.kg.astype(F32)
        pre.append(dict(a_last=a_last,
                        qs=(qf * jnp.exp(sq.a)).astype(BF16),
                        kt=(kf * jnp.exp(-sq.a)).astype(BF16),
                        kl=(kf * jnp.exp(a_last_rows - sq.a)).astype(BF16)))
    units = [(sq, p, h) for sq, p in zip(seqs, pre) for h in range(G_HEADS)]
    att = [[jnp.where(causal, _dot_nt(p["qs"][r, ks[h]], p["kt"][r, ks[h]]), 0.0).astype(BF16)
            for r in chunks] for _, p, h in units]
    s_cur = [sq.s_src[h] for sq, _, h in units]
    for c, r in enumerate(chunks):
        o = [_dot(p["qs"][r, ks[h]], s_cur[u].astype(BF16)) + _dot(att[u][c], sq.vg[r, vs[h]])
             for u, (sq, p, h) in enumerate(units)]
        for u, (sq, _, h) in enumerate(units):
            sq.og_rows_store(r, h, _rms(o[u], gh_ref[:, vs[h]]))
        s_cur = [jnp.exp(_col_of_row(p["a_last"][c][:, ks[h]])) * s_cur[u]
                 + _dot_tn(p["kl"][r, ks[h]], sq.vg[r, vs[h]])
                 for u, (sq, p, h) in enumerate(units)]
    for u, (sq, _, h) in enumerate(units):
        sq.s_dst[h] = s_cur[u]


def _shift_rows(rows, offset):
    if isinstance(rows, slice):
        return slice(rows.start + offset, rows.stop + offset)
    return pl.ds(rows.start + offset, rows.size)


def _gla_dispatch(la_ref, bi, row0, tb, lg, sub, gh_ref, make_record):
    chunk_decay = [la_ref[bi, row0 + c * lg + lg - 1:row0 + (c + 1) * lg, :] for c in range(tb // lg)]
    span_ok = jnp.min(functools.reduce(jnp.minimum, chunk_decay)) >= -GLA_SAFE_SPAN

    @pl.when(span_ok)
    def _():
        _gla_block_direct([make_record(slice(0, tb))], gh_ref, tb=tb, lg=lg)

    @pl.when(jnp.logical_not(span_ok))
    def _():
        def body(ci, carry):
            rows = pl.ds(pl.multiple_of(ci * lg, lg), lg)
            _interleave(_gla_chunk_guarded([make_record(rows)], gh_ref, lg=lg, sub=sub))
            return carry

        lax.fori_loop(0, tb // lg, body, 0)


def _long_mixer_kernel(x_ref, cst_ref, *refs, tb, sub, lg, gsub):
    w = _pre_weights(*refs[:10])
    ggh_ref = refs[10]
    hm_ref, og_ref, cout_ref, c_out, n_out, m_out, s_out = refs[11:18]
    cbuf_ref, qm_s, km_s, vm_s, gif_s, qg_s, kg_s, la_s, vg_s, c_scr, n_scr, m_scr, s_scr = refs[18:]
    inter = types.SimpleNamespace(qm=qm_s, km=km_s, vm=vm_s, gif=gif_s, qg=qg_s, kg=kg_s, la=la_s, vg=vg_s)
    carried = (c_scr, n_scr, m_scr, s_scr)
    t = pl.program_id(1)
    nt = pl.num_programs(1)
    _load_conv_history(cbuf_ref, cst_ref, tb)

    @pl.when(t == 0)
    def _():
        for ref in carried:
            ref[...] = jnp.zeros(ref.shape, F32)

    def pre_stage(j):
        slot = j % 2
        return _pre_sub_block(x_ref, slice(j * sub, (j + 1) * sub), cbuf_ref, w, inter,
                              slice(slot * sub, (slot + 1) * sub), slot, bt=1, sub=sub, lg=lg,
                              time_minor=True)

    def record(j, window=slice(0, sub)):
        slot = j % 2
        rows = _shift_rows(window, slot * sub)

        def hm_store(h, val):
            hm_ref[0, j, h * M_HEAD_DIM:(h + 1) * M_HEAD_DIM, :] = val.astype(hm_ref.dtype)

        def og_rows_store(r, h, val):
            og_ref[0, _shift_rows(r, j * sub), h * G_DV:(h + 1) * G_DV] = val.astype(og_ref.dtype)

        return types.SimpleNamespace(
            qt=qm_s[0, slot], k=km_s[0, slot * sub:(slot + 1) * sub, :], vt=vm_s[0, slot], gift=gif_s[0, slot],
            qg=qg_s[0, rows, :], kg=kg_s[0, rows, :], vg=vg_s[0, rows, :], a=la_s[0, rows, :],
            hm_store=hm_store, og_rows_store=og_rows_store, og_store=functools.partial(og_rows_store, window),
            c_src=c_scr.at[0], n_src=n_scr.at[0], m_src=m_scr.at[0], s_src=s_scr.at[0],
            c_dst=c_scr.at[0], n_dst=n_scr.at[0], m_dst=m_scr.at[0], s_dst=s_scr.at[0])

    nsub = tb // sub
    _interleave(pre_stage(0))
    for j in range(nsub):
        stages = [_mlstm_chunk_tm([record(j)], lm=sub)]
        if j + 1 < nsub:
            stages.append(pre_stage(j + 1))
        _interleave(*stages)
        slot = j % 2
        _gla_dispatch(la_s, 0, slot * sub, sub, lg, gsub, ggh_ref, functools.partial(record, j))
    cout_ref[...] = cbuf_ref[:, tb:tb + SUBLANES, :]

    @pl.when(t == nt - 1)
    def _():
        for out, ref in zip((c_out, n_out, m_out, s_out), carried):
            out[...] = ref[...]


STATE_SHAPES = ((M_HEADS, M_HEAD_DIM, M_HEAD_DIM), (M_HEADS, 1, M_HEAD_DIM), None, (G_HEADS, G_DK, G_DV))


def _state_shapes(bt, m_shape):
    return [(bt,) + (m_shape if shp is None else shp) for shp in STATE_SHAPES]


def _long_mixer(x, cst, weights, ggh, *, tb, sub, lg, gsub):
    b, t, _ = x.shape
    assert t % tb == 0 and tb % sub == 0 and sub % lg == 0 and lg % gsub == 0 and sub % LANES == 0
    nsub = tb // sub
    tok = lambda w: pl.BlockSpec((1, tb, w), lambda i, j: (i, j, 0))
    per_seq = pl.BlockSpec((1, SUBLANES, M_WIDTH), lambda i, j: (i, 0, 0))
    states = _state_shapes(1, (SUBLANES, LANES))
    state_specs = [pl.BlockSpec(shp, lambda i, j, nd=len(shp): (i,) + (0,) * (nd - 1)) for shp in states]
    f32 = lambda shp: jax.ShapeDtypeStruct(shp, F32)
    slab = lambda w, dt: pltpu.VMEM((1, 2, w, sub), dt)
    rows2 = lambda w, dt: pltpu.VMEM((1, 2 * sub, w), dt)
    return pl.pallas_call(
        functools.partial(_long_mixer_kernel, tb=tb, sub=sub, lg=lg, gsub=gsub),
        grid=(b, t // tb),
        in_specs=([tok(D_MODEL), per_seq] + [_const_spec(shp) for shp in PRE_WEIGHT_SPECS]
                  + [_const_spec((1, G_VW))]),
        out_specs=[pl.BlockSpec((1, nsub, M_WIDTH, sub), lambda i, j: (i, j, 0, 0)), tok(G_VW), per_seq]
                  + state_specs,
        out_shape=[jax.ShapeDtypeStruct((b, t // sub, M_WIDTH, sub), BF16),
                   jax.ShapeDtypeStruct((b, t, G_VW), BF16), f32((b, SUBLANES, M_WIDTH))]
                  + [f32((b,) + shp[1:]) for shp in states],
        scratch_shapes=[pltpu.VMEM((1, tb + SUBLANES, M_WIDTH), F32),
                        slab(M_WIDTH, BF16), rows2(M_WIDTH, BF16), slab(M_WIDTH, BF16),
                        slab(2 * SUBLANES, F32), rows2(G_KW, BF16), rows2(G_KW, BF16), rows2(G_KW, F32),
                        rows2(G_VW, BF16)] + [pltpu.VMEM(shp, F32) for shp in states],
        compiler_params=pltpu.CompilerParams(
            dimension_semantics=("parallel", "arbitrary"), vmem_limit_bytes=VMEM_LIMIT),
        name="mixer_long",
    )(x, cst, *weights, ggh)


def _short_scan_kernel(*refs, bt, ts):
    qm, km, vm, gif, qg, kg, la, vg, ggh_ref, c_in, n_in, m_in, s_in = refs[:13]
    hm, og, c_out, n_out, m_out, s_out = refs[13:]

    def record(bi):
        def hm_store(h, val):
            hm[bi, :, h * M_HEAD_DIM:(h + 1) * M_HEAD_DIM] = val

        def og_store(h, val):
            og[bi, :, h * G_DV:(h + 1) * G_DV] = val

        return types.SimpleNamespace(
            q=qm[bi], k=km[bi], v=vm[bi], gif=gif[bi], qg=qg[bi], kg=kg[bi], vg=vg[bi], a=la[bi],
            hm_store=hm_store, og_store=og_store,
            c_src=c_in.at[bi], n_src=n_in.at[bi], m_src=m_in.at[bi], s_src=s_in.at[bi],
            c_dst=c_out.at[bi], n_dst=n_out.at[bi], m_dst=m_out.at[bi], s_dst=s_out.at[bi])

    seqs = [record(bi) for bi in range(bt)]
    _interleave(_mlstm_chunk(seqs, lm=ts), _gla_chunk_guarded(seqs, ggh_ref, lg=ts, sub=ts))


def _short_scan(per_token, ggh, state, *, bt):
    b, ts, _ = per_token[1].shape
    assert b % bt == 0 and ts % SUBLANES == 0
    lead = lambda shp: pl.BlockSpec(shp, lambda i, nd=len(shp): (i,) + (0,) * (nd - 1))
    tok = lambda w: lead((bt, ts, w))
    states = _state_shapes(bt, (1, LANES))
    f32 = lambda shp: jax.ShapeDtypeStruct(shp, F32)
    return pl.pallas_call(
        functools.partial(_short_scan_kernel, bt=bt, ts=ts),
        grid=(b // bt,),
        in_specs=[tok(M_WIDTH), tok(M_WIDTH), tok(M_WIDTH), tok(LANES), tok(G_KW), tok(G_KW), tok(G_KW),
                  tok(G_VW), _const_spec((1, G_VW))] + [lead(shp) for shp in states],
        out_specs=[tok(M_WIDTH), tok(G_VW)] + [lead(shp) for shp in states],
        out_shape=[f32((b, ts, M_WIDTH)), f32((b, ts, G_VW))] + [f32((b,) + shp[1:]) for shp in states],
        compiler_params=pltpu.CompilerParams(
            dimension_semantics=("parallel",), vmem_limit_bytes=VMEM_LIMIT),
        name="mixer_scan_short",
    )(*per_token, ggh, *state)


POST_O, POST_R, POST_A, POST_B = 0, 1024, 2048, 3072


def _post_kernel(x_ref, hm_ref, og_ref, g_ref, gmh_ref, w_ref, wpa_ref, wpb_ref, wo_ref, o_ref, *,
                 time_minor):
    x = x_ref[...]
    xn = _rms(x, g_ref[...]).astype(BF16)
    gate = lambda off: _dot(xn, w_ref[:, off:off + D_MODEL])
    if time_minor:
        hm = jnp.concatenate(
            [jnp.concatenate([jnp.transpose(hm_ref[0, j, h * M_HEAD_DIM:(h + 1) * M_HEAD_DIM, :].astype(F32))
                              for h in range(M_HEADS)], axis=1)
             for j in range(hm_ref.shape[1])], axis=0)
    else:
        hm = hm_ref[...]
    hm = (_sigmoid(gate(POST_O)) * (hm * gmh_ref[...])).astype(BF16)
    og = (_silu(gate(POST_R)) * og_ref[...].astype(F32)).astype(BF16)
    y = (_sigmoid(gate(POST_A)) * _dot(hm, wpa_ref[...])
         + _sigmoid(gate(POST_B)) * _dot(og, wpb_ref[...]))
    o_ref[...] = x + _dot(y.astype(BF16), wo_ref[...])


def _post(x2d, hm, og2d, g, gmh, w_post, wpa, wpb, wo, *, tm, time_minor):
    n = x2d.shape[0]
    assert n % tm == 0
    tok = pl.BlockSpec((tm, D_MODEL), lambda i: (i, 0))
    if time_minor:
        nblk, blk = hm.shape[1], hm.shape[3]
        assert tm % blk == 0 and (nblk * blk) % tm == 0
        per_seq = nblk * blk // tm
        hm_spec = pl.BlockSpec((1, tm // blk, M_WIDTH, blk), lambda i: (i // per_seq, i % per_seq, 0, 0))
    else:
        hm_spec = tok
    return pl.pallas_call(
        functools.partial(_post_kernel, time_minor=time_minor),
        grid=(n // tm,),
        in_specs=[tok, hm_spec, tok, _const_spec((1, D_MODEL)), _const_spec((1, M_WIDTH)),
                  _const_spec((D_MODEL, 4 * D_MODEL)),
                  _const_spec((M_WIDTH, D_MODEL)), _const_spec((G_VW, D_MODEL)),
                  _const_spec((D_MODEL, D_MODEL))],
        out_specs=tok,
        out_shape=jax.ShapeDtypeStruct((n, D_MODEL), F32),
        compiler_params=pltpu.CompilerParams(
            dimension_semantics=("parallel",), vmem_limit_bytes=VMEM_LIMIT),
        name="mixer_post",
    )(x2d, hm, og2d, g, gmh, w_post, wpa, wpb, wo)


def _pack_weights(g_ffn1, w_ffn1_up, w_ffn1_down, g_mix, w_in, conv_w, conv_b, w_mq, w_mk, b_if,
                  g_mhead, w_a2, b_a, g_ghead, w_pa, w_pb, w_o, g_ffn2, w_ffn2_up, w_ffn2_down,
                  g_final):
    l = 0
    row = lambda v: v.reshape(1, -1).astype(F32)
    edges = [0]
    for wd in (M_WIDTH, M_WIDTH, M_WIDTH, 2 * M_HEADS, G_KW, G_KW, G_VW, G_VW, G_RANK, D_MODEL, D_MODEL):
        edges.append(edges[-1] + wd)
    seg = lambda i: w_in[l][:, edges[i]:edges[i + 1]]
    u_m, v_m, o_m, if_m, q_g, k_g, v_g, r_g, a_g, g_a, g_b = (seg(i) for i in range(11))
    w_pre = jnp.concatenate([u_m, v_m, q_g, k_g, v_g], axis=1).astype(BF16)
    w_small = jnp.concatenate(
        [if_m, a_g, jnp.zeros((D_MODEL, LANES - 2 * M_HEADS - G_RANK), F32)], axis=1).astype(BF16)
    w_post = jnp.concatenate([o_m, r_g, g_a, g_b], axis=1).astype(BF16)
    bif = jnp.concatenate([b_if[l].reshape(-1), jnp.zeros((LANES - 2 * M_HEADS,), F32)]).reshape(1, LANES)
    wa2 = jnp.zeros((LANES, G_KW), F32).at[A_LANE0:A_LANE0 + G_RANK].set(w_a2[l]).astype(BF16)
    return dict(
        g_ffn1=row(g_ffn1[l]), w_ffn1_up=w_ffn1_up[l].astype(BF16), w_ffn1_down=w_ffn1_down[l].astype(BF16),
        g_mix=row(g_mix[l]), w_pre=w_pre, w_small=w_small, w_post=w_post,
        conv_w=conv_w[l].astype(F32), conv_b=row(conv_b[l]),
        w_mq=w_mq[l].astype(BF16), w_mk=w_mk[l].astype(BF16), bif=bif,
        g_mhead=row(g_mhead[l]), wa2=wa2, b_a=row(b_a[l]), g_ghead=row(g_ghead[l]),
        w_pa=w_pa[l].astype(BF16), w_pb=w_pb[l].astype(BF16), w_o=w_o[l].astype(BF16),
        g_ffn2=row(g_ffn2[l]), w_ffn2_up=w_ffn2_up[l].astype(BF16), w_ffn2_down=w_ffn2_down[l].astype(BF16),
        g_final=row(g_final))


LONG_BLOCK = 1024
LONG_CHUNK = 256
LONG_GLA_CHUNK = 128
GLA_PAIRWISE_BLOCK = 16
SHORT_SEQS_PER_STEP = 4
SHORT_PRE_ROWS = 512
POST_ROWS = 512
FFN_ROWS = 1024


def _pre_weight_args(w):
    return (w["g_mix"], w["w_pre"], w["w_small"], w["conv_w"], w["conv_b"], w["w_mq"], w["w_mk"], w["bif"],
            w["wa2"], w["b_a"])


def _long_group(x1, b, t, w):
    assert t % LONG_BLOCK == 0
    cst = jnp.zeros((b, SUBLANES, M_WIDTH), F32)
    hm, og, cout, c_new, n_new, m_new, s_new = _long_mixer(
        x1.reshape(b, t, D_MODEL), cst, _pre_weight_args(w), w["g_ghead"], tb=LONG_BLOCK, sub=LONG_CHUNK,
        lg=LONG_GLA_CHUNK, gsub=GLA_PAIRWISE_BLOCK)
    x2 = _post(x1, hm, og.reshape(b * t, G_VW), w["g_mix"], w["g_mhead"], w["w_post"], w["w_pa"], w["w_pb"],
               w["w_o"], tm=math.gcd(t, POST_ROWS), time_minor=True)
    return x2, (cout[:, SUBLANES - (CONV_W - 1):, :][None], c_new[None],
                n_new.reshape(b, M_HEADS, M_HEAD_DIM)[None], m_new[:, :M_HEADS, 0][None], s_new[None])


def _short_group(x1, b, t, state, w):
    assert t <= 64 and t % SUBLANES == 0
    conv0, c0, n0, m0, s0 = state
    cst = jnp.concatenate([jnp.zeros((b, SUBLANES - (CONV_W - 1), M_WIDTH), F32), conv0.astype(F32)], axis=1)
    m0p = jnp.concatenate([m0, jnp.zeros((b, LANES - M_HEADS), F32)], axis=1).reshape(b, 1, LANES)
    bt = max(1, min(b, SHORT_PRE_ROWS // t))
    while b % bt:
        bt -= 1
    *per_token, cout = _pre(x1.reshape(b, t, D_MODEL), cst, _pre_weight_args(w), bt=bt, tb=t, lg=t)
    hm, og, c_new, n_new, m_new, s_new = _short_scan(
        per_token, w["g_ghead"], (c0, n0.reshape(b, M_HEADS, 1, M_HEAD_DIM), m0p, s0),
        bt=math.gcd(b, SHORT_SEQS_PER_STEP))
    n = b * t
    x2 = _post(x1, hm.reshape(n, M_WIDTH), og.reshape(n, G_VW), w["g_mix"], w["g_mhead"], w["w_post"],
               w["w_pa"], w["w_pb"], w["w_o"], tm=math.gcd(n, POST_ROWS), time_minor=False)
    return x2, (cout[:, SUBLANES - (CONV_W - 1):, :][None], c_new[None],
                n_new.reshape(b, M_HEADS, M_HEAD_DIM)[None], m_new[:, 0, :M_HEADS][None], s_new[None])


def kernel(x_prompt, x_sample, state_conv, state_mlstm_C, state_mlstm_n, state_mlstm_m, state_gla_S,
           g_ffn1, w_ffn1_up, w_ffn1_down, g_mix, w_in, conv_w, conv_b, w_mq, w_mk, b_if, g_mhead,
           w_a2, b_a, g_ghead, w_pa, w_pb, w_o, g_ffn2, w_ffn2_up, w_ffn2_down, g_final):
    w = _pack_weights(g_ffn1, w_ffn1_up, w_ffn1_down, g_mix, w_in, conv_w, conv_b, w_mq, w_mk, b_if,
                      g_mhead, w_a2, b_a, g_ghead, w_pa, w_pb, w_o, g_ffn2, w_ffn2_up, w_ffn2_down,
                      g_final)
    bp, tp, _ = x_prompt.shape
    bs, ts, _ = x_sample.shape
    tm_ffn = math.gcd(math.gcd(bp * tp, bs * ts), FFN_ROWS)
    x1_p, x1_s = _ffn(x_prompt.reshape(bp * tp, D_MODEL), x_sample.reshape(bs * ts, D_MODEL),
                      w["g_ffn1"], w["w_ffn1_up"], w["w_ffn1_down"], w["g_final"], final_norm=False, tm=tm_ffn)
    x2_p, states_p = _long_group(x1_p, bp, tp, w)
    state = (state_conv[0], state_mlstm_C[0], state_mlstm_n[0], state_mlstm_m[0], state_gla_S[0])
    x2_s, states_s = _short_group(x1_s, bs, ts, state, w)
    y_p, y_s = _ffn(x2_p, x2_s, w["g_ffn2"], w["w_ffn2_up"], w["w_ffn2_down"], w["g_final"],
                    final_norm=True, tm=tm_ffn)
    return (y_p.reshape(bp, tp, D_MODEL), y_s.reshape(bs, ts, D_MODEL)) + states_p + states_s
```

```python
import functools
import math
import types

import jax
import jax.numpy as jnp
from jax import lax
from jax.experimental import pallas as pl
from jax.experimental.pallas import tpu as pltpu

D_MODEL = 1024
M_HEADS = 4
M_HEAD_DIM = 256
M_WIDTH = 1024
CONV_W = 4
G_HEADS = 4
G_DK = 128
G_DV = 256
G_KW = 512
G_VW = 1024
G_RANK = 16
G_TAU = 16.0
D_FF = 2816
EPS = 1e-6

LANES = 128
SUBLANES = 8
BF16_ROWS = 16
VMEM_LIMIT = 56 * 1024 * 1024

F32 = jnp.float32
BF16 = jnp.bfloat16
NEG_INF = float("-inf")


def _sigmoid(x):
    return 1.0 / (1.0 + jnp.exp(-x))


def _silu(x):
    return x * _sigmoid(x)


def _log_sigmoid(x):
    return jnp.minimum(x, 0.0) - jnp.log(1.0 + jnp.exp(-jnp.abs(x)))


def _rms(x, g):
    return x * lax.rsqrt(jnp.mean(x * x, axis=-1, keepdims=True) + EPS) * g


def _dot(a, b):
    return jnp.dot(a, b, preferred_element_type=F32)


def _dot_nt(a, b):
    return lax.dot_general(a, b, (((1,), (1,)), ((), ())), preferred_element_type=F32)


def _dot_tn(a, b):
    return lax.dot_general(a, b, (((0,), (0,)), ((), ())), preferred_element_type=F32)


def _const_spec(shape):
    nd = len(shape)
    return pl.BlockSpec(shape, lambda *_: (0,) * nd, pipeline_mode=pl.Buffered(1))


FF_CHUNK = 256


def _ffn_kernel(xa_ref, xb_ref, g_ref, wup_ref, wdown_ref, gfin_ref, oa_ref, ob_ref, h_ref, *,
                final_norm, tiles_a):
    def half_step(x_ref, o_ref):
        x = x_ref[...]
        xn = _rms(x, g_ref[...]).astype(BF16)
        for c in range(D_FF // FF_CHUNK):
            a = _dot(xn, wup_ref[:, c * FF_CHUNK:(c + 1) * FF_CHUNK])
            g = _dot(xn, wup_ref[:, D_FF + c * FF_CHUNK:D_FF + (c + 1) * FF_CHUNK])
            h_ref[:, c * FF_CHUNK:(c + 1) * FF_CHUNK] = (_silu(g) * a).astype(BF16)
        y = x + 0.5 * _dot(h_ref[...], wdown_ref[...])
        o_ref[...] = _rms(y, gfin_ref[...]) if final_norm else y

    on_a = pl.program_id(0) < tiles_a
    pl.when(on_a)(lambda: half_step(xa_ref, oa_ref))
    pl.when(jnp.logical_not(on_a))(lambda: half_step(xb_ref, ob_ref))


def _ffn(xa, xb, g, wup, wdown, gfin, *, final_norm, tm):
    na, nb = xa.shape[0], xb.shape[0]
    assert na % tm == 0 and nb % tm == 0
    tiles_a, tiles_b = na // tm, nb // tm
    spec_a = pl.BlockSpec((tm, D_MODEL), lambda i: (jnp.minimum(i, tiles_a - 1), 0))
    spec_b = pl.BlockSpec((tm, D_MODEL), lambda i: (jnp.maximum(i - tiles_a, 0), 0),
                          **(dict(pipeline_mode=pl.Buffered(1)) if tiles_b == 1 else {}))
    return pl.pallas_call(
        functools.partial(_ffn_kernel, final_norm=final_norm, tiles_a=tiles_a),
        grid=(tiles_a + tiles_b,),
        in_specs=[
            spec_a, spec_b,
            _const_spec((1, D_MODEL)),
            _const_spec((D_MODEL, 2 * D_FF)),
            _const_spec((D_FF, D_MODEL)),
            _const_spec((1, D_MODEL)),
        ],
        out_specs=[spec_a, spec_b],
        out_shape=[jax.ShapeDtypeStruct((na, D_MODEL), F32), jax.ShapeDtypeStruct((nb, D_MODEL), F32)],
        scratch_shapes=[pltpu.VMEM((tm, D_FF), BF16)],
        compiler_params=pltpu.CompilerParams(
            dimension_semantics=("arbitrary",), vmem_limit_bytes=VMEM_LIMIT),
        name="ffn_final" if final_norm else "ffn",
    )(xa, xb, g, wup, wdown, gfin)


PRE_U, PRE_V, PRE_QG, PRE_KG, PRE_VG = 0, 1024, 2048, 2560, 3072
PRE_WIDTH = 4096
A_LANE0 = 8


def _pre_kernel(x_ref, cst_ref, g_ref, w_ref, ws_ref, cw_ref, cb_ref, wq_ref, wk_ref, bif_ref,
                wa2_ref, ba_ref,
                qm_ref, km_ref, vm_ref, gif_ref, qg_ref, kg_ref, la_ref, vg_ref, cout_ref,
                cbuf_ref, *, bt, tb, sub, lg, time_minor):
    t = pl.program_id(1)
    m = bt * sub

    @pl.when(t == 0)
    def _():
        cbuf_ref[:, 0:SUBLANES, :] = cst_ref[...]

    @pl.when(t > 0)
    def _():
        cbuf_ref[:, 0:SUBLANES, :] = cbuf_ref[:, tb:tb + SUBLANES, :]

    for j in range(tb // sub):
        r0 = j * sub
        rows = slice(r0, r0 + sub)
        to_rows = lambda y, w: y.reshape(bt, sub, w)
        xn = _rms(x_ref[:, rows, :].reshape(m, D_MODEL), g_ref[...]).astype(BF16)
        small = _dot(xn, ws_ref[...])

        u = _dot(xn, w_ref[:, PRE_U:PRE_U + M_WIDTH])
        cbuf_ref[:, SUBLANES + r0:SUBLANES + r0 + sub, :] = to_rows(u, M_WIDTH)
        c = cb_ref[...].reshape(1, 1, M_WIDTH)
        for tap in range(CONV_W):
            off = SUBLANES - (CONV_W - 1) + tap + r0
            c = c + cbuf_ref[:, off:off + sub, :] * cw_ref[tap:tap + 1, :].reshape(1, 1, M_WIDTH)
        ch = _silu(c).reshape(m, M_WIDTH).astype(BF16)

        vm = _dot(xn, w_ref[:, PRE_V:PRE_V + M_WIDTH])
        gates = small + bif_ref[...]
        lane = lax.broadcasted_iota(jnp.int32, gates.shape, 1)
        gif = jnp.where(lane < M_HEADS, gates, jnp.where(lane < 2 * M_HEADS, _log_sigmoid(gates), 0.0))
        if time_minor:
            b = pltpu.roll(_scan_rows(gif, jnp.add, 0.0), LANES - M_HEADS, axis=1)
            g = gif - b
            gmax = _scan_rows(g, jnp.maximum, NEG_INF)
            packed = jnp.where(lane < M_HEADS, g,
                               jnp.where(lane < 2 * M_HEADS, pltpu.roll(gmax, M_HEADS, axis=1),
                                         jnp.where(lane < 3 * M_HEADS, pltpu.roll(b, 2 * M_HEADS, axis=1), 0.0)))
            gif_ref[0, j] = jnp.transpose(packed)[0:2 * SUBLANES, :]
        else:
            gif_ref[:, rows, :] = to_rows(gif, LANES)
            vm_ref[:, rows, :] = to_rows(vm.astype(BF16), M_WIDTH)

        def decay_columns(h):
            cols = slice(h * G_DK, (h + 1) * G_DK)
            a2 = _dot(small.astype(BF16), wa2_ref[:, cols]) + ba_ref[:, cols]
            la_ref[:, rows, cols] = to_rows(
                _scan_rows(_log_sigmoid(a2) * (1.0 / G_TAU), jnp.add, 0.0, segment=lg), G_DK)

        qg_ref[:, rows, :] = to_rows(
            (_dot(xn, w_ref[:, PRE_QG:PRE_QG + G_KW]) * (G_DK ** -0.5)).astype(BF16), G_KW)
        decay_columns(0)
        kg_ref[:, rows, :] = to_rows(_dot(xn, w_ref[:, PRE_KG:PRE_KG + G_KW]).astype(BF16), G_KW)
        decay_columns(1)
        half = G_VW // 2
        for p in range(2):
            vg_ref[:, rows, p * half:(p + 1) * half] = to_rows(
                _dot(xn, w_ref[:, PRE_VG + p * half:PRE_VG + (p + 1) * half]).astype(BF16), half)
            decay_columns(2 + p)

        for h in range(M_HEADS):
            sl = slice(h * M_HEAD_DIM, (h + 1) * M_HEAD_DIM)
            qh = _dot(ch[:, sl], wq_ref[h])
            if time_minor:
                qm_ref[0, j, sl, :] = jnp.transpose(qh).astype(BF16)
                vm_ref[0, j, sl, :] = jnp.transpose(vm[:, sl]).astype(BF16)
            else:
                qm_ref[:, rows, sl] = to_rows(qh.astype(BF16), M_HEAD_DIM)
            km_ref[:, rows, sl] = to_rows(
                (_dot(ch[:, sl], wk_ref[h]) * (M_HEAD_DIM ** -0.5)).astype(BF16), M_HEAD_DIM)
    cout_ref[...] = cbuf_ref[:, tb:tb + SUBLANES, :]


def _pre(x, cst, g, w_pre, w_small, conv_w, conv_b, wq, wk, bif, wa2, ba, *, bt, tb, sub, lg, time_minor):
    b, t, _ = x.shape
    assert b % bt == 0 and t % tb == 0 and tb % sub == 0 and sub % SUBLANES == 0 and sub % lg == 0
    assert not time_minor or (bt == 1 and sub % LANES == 0)
    tok = lambda w: pl.BlockSpec((bt, tb, w), lambda i, j: (i, j, 0))
    per_seq = pl.BlockSpec((bt, SUBLANES, M_WIDTH), lambda i, j: (i, 0, 0))
    sds = lambda w, dt: jax.ShapeDtypeStruct((b, t, w), dt)
    if time_minor:
        tok_t = lambda w: pl.BlockSpec((1, tb // sub, w, sub), lambda i, j: (i, j, 0, 0))
        sds_t = lambda w, dt: jax.ShapeDtypeStruct((b, t // sub, w, sub), dt)
        qv_spec, qv_shape = tok_t(M_WIDTH), sds_t(M_WIDTH, BF16)
        gif_spec, gif_shape = tok_t(2 * SUBLANES), sds_t(2 * SUBLANES, F32)
    else:
        qv_spec, qv_shape = tok(M_WIDTH), sds(M_WIDTH, BF16)
        gif_spec, gif_shape = tok(LANES), sds(LANES, F32)
    return pl.pallas_call(
        functools.partial(_pre_kernel, bt=bt, tb=tb, sub=sub, lg=lg, time_minor=time_minor),
        grid=(b // bt, t // tb),
        in_specs=[
            tok(D_MODEL), per_seq,
            _const_spec((1, D_MODEL)),
            _const_spec((D_MODEL, PRE_WIDTH)),
            _const_spec((D_MODEL, LANES)),
            _const_spec((CONV_W, M_WIDTH)),
            _const_spec((1, M_WIDTH)),
            _const_spec((M_HEADS, M_HEAD_DIM, M_HEAD_DIM)),
            _const_spec((M_HEADS, M_HEAD_DIM, M_HEAD_DIM)),
            _const_spec((1, LANES)),
            _const_spec((LANES, G_KW)),
            _const_spec((1, G_KW)),
        ],
        out_specs=[qv_spec, tok(M_WIDTH), qv_spec, gif_spec, tok(G_KW), tok(G_KW),
                   tok(G_KW), tok(G_VW), per_seq],
        out_shape=[qv_shape, sds(M_WIDTH, BF16), qv_shape, gif_shape,
                   sds(G_KW, BF16), sds(G_KW, BF16), sds(G_KW, F32), sds(G_VW, BF16),
                   jax.ShapeDtypeStruct((b, SUBLANES, M_WIDTH), F32)],
        scratch_shapes=[pltpu.VMEM((bt, tb + SUBLANES, M_WIDTH), F32)],
        compiler_params=pltpu.CompilerParams(
            dimension_semantics=("parallel", "arbitrary"), vmem_limit_bytes=VMEM_LIMIT),
        name="mixer_pre",
    )(x, cst, g, w_pre, w_small, conv_w, conv_b, wq, wk, bif, wa2, ba)


def _scan_rows(x, op, fill, segment=None):
    n = segment or x.shape[0]
    row = lax.broadcasted_iota(jnp.int32, x.shape, 0) % n
    s = 1
    while s < n:
        x = op(x, jnp.where(row >= s, pltpu.roll(x, s, axis=0), fill))
        s *= 2
    return x


def _col_of_row(row):
    return jnp.transpose(jnp.broadcast_to(row, (SUBLANES, LANES)))[:, 0:1]


def _mlstm_chunk_tm(seqs, *, lm):
    reps = lm // LANES
    head_row = lax.broadcasted_iota(jnp.int32, (SUBLANES, LANES), 0) < M_HEADS
    src = lax.broadcasted_iota(jnp.int32, (lm, lm), 0)
    dst = lax.broadcasted_iota(jnp.int32, (lm, lm), 1)
    causal = src <= dst
    sl = [slice(h * M_HEAD_DIM, (h + 1) * M_HEAD_DIM) for h in range(M_HEADS)]
    row = lambda x, h: x[h:h + 1, :]
    gates = []
    for sq in seqs:
        g = sq.gift[0:SUBLANES]
        gmax = pltpu.roll(g, M_HEADS, axis=0)
        b = sq.gift[SUBLANES:2 * SUBLANES]
        m_prev = jnp.concatenate([sq.m_src[...]] * reps, axis=1)
        mx = jnp.maximum(m_prev, gmax)
        mx_last = jnp.broadcast_to(mx[:, lm - 1:lm], (SUBLANES, lm))
        gates.append(dict(
            mx=mx,
            inter=jnp.exp(m_prev - mx),
            nfloor=jnp.exp(-(b + mx)),
            w_in=jnp.exp(g - mx_last),
            decay=jnp.exp(m_prev - mx_last),
            g_col=jnp.transpose(g)))
        m_new = jnp.broadcast_to(b[:, lm - 1:lm], (SUBLANES, lm)) + mx_last
        sq.m_dst[...] = jnp.where(head_row, m_new[:, :LANES], 0.0)
    units = [(sq, gt, h) for sq, gt in zip(seqs, gates) for h in range(M_HEADS)]
    c_old = [sq.c_src[h] for sq, _, h in units]
    n_old = [sq.n_src[h] for sq, _, h in units]
    cn = [jnp.concatenate([c_old[u].astype(BF16),
                           jnp.broadcast_to(n_old[u].astype(BF16), (BF16_ROWS, M_HEAD_DIM))], axis=0)
          for u in range(len(units))]
    cq = [_dot(cn[u], sq.qt[sl[h], :]) for u, (sq, _, h) in enumerate(units)]
    dm = [jnp.exp(jnp.where(causal, gt["g_col"][:, h:h + 1] - row(gt["mx"], h), NEG_INF))
          for _, gt, h in units]
    st = [_dot(sq.k[:, sl[h]], sq.qt[sl[h], :]) * dm[u]
          for u, (sq, _, h) in enumerate(units)]
    num = [row(gt["inter"], h) * cq[u][:M_HEAD_DIM] + _dot(sq.vt[sl[h], :], st[u].astype(BF16))
           for u, (sq, gt, h) in enumerate(units)]
    for u, (sq, gt, h) in enumerate(units):
        den = (row(gt["inter"], h) * cq[u][M_HEAD_DIM:M_HEAD_DIM + 1]
               + jnp.sum(st[u], axis=0, keepdims=True))
        rcp = 1.0 / jnp.maximum(jnp.abs(den), row(gt["nfloor"], h))
        ms = jnp.mean(num[u] * num[u], axis=0, keepdims=True)
        sq.hm_store(h, num[u] * (rcp * lax.rsqrt(rcp * rcp * ms + EPS)))
    for u, (sq, gt, h) in enumerate(units):
        w_h = row(gt["w_in"], h)
        dec_h = jnp.concatenate([row(gt["decay"], h)[:, :LANES]] * (M_HEAD_DIM // LANES), axis=1)
        wv = jnp.concatenate([(sq.vt[sl[h], :].astype(F32) * w_h).astype(BF16),
                              jnp.broadcast_to(w_h.astype(BF16), (BF16_ROWS, lm))], axis=0)
        upd = _dot(wv, sq.k[:, sl[h]])
        sq.c_dst[h] = dec_h * c_old[u] + upd[:M_HEAD_DIM]
        sq.n_dst[h] = dec_h * n_old[u] + upd[M_HEAD_DIM:M_HEAD_DIM + 1]


def _mlstm_chunk(seqs, *, lm):
    row = lax.broadcasted_iota(jnp.int32, (lm, lm), 0)
    col = lax.broadcasted_iota(jnp.int32, (lm, lm), 1)
    causal = col <= row
    sl = [slice(h * M_HEAD_DIM, (h + 1) * M_HEAD_DIM) for h in range(M_HEADS)]
    gates = []
    for sq in seqs:
        cs = _scan_rows(sq.gif, jnp.add, 0.0)
        b = pltpu.roll(cs, LANES - M_HEADS, axis=1)
        g = sq.gif - b
        m_prev = sq.m_src[...]
        mx = jnp.maximum(m_prev, _scan_rows(g, jnp.maximum, NEG_INF))
        mx_last = mx[lm - 1:lm, :]
        gates.append(dict(
            mx=mx,
            inter=jnp.exp(m_prev - mx),
            nfloor=jnp.exp(-(b + mx)),
            g_t=jnp.transpose(g),
            w_in=jnp.exp(g - mx_last),
            decay=jnp.exp(m_prev - mx_last)))
        sq.m_dst[...] = b[lm - 1:lm, :] + mx_last
    units = [(sq, gt, h) for sq, gt in zip(seqs, gates) for h in range(M_HEADS)]
    col_of = lambda x, h: x[:, h:h + 1]
    c_old = [sq.c_src[h] for sq, _, h in units]
    n_old = [sq.n_src[h] for sq, _, h in units]
    dm = [jnp.exp(jnp.where(causal, gt["g_t"][h:h + 1, :] - col_of(gt["mx"], h), NEG_INF))
          for _, gt, h in units]
    qn = [jnp.sum(sq.q[:, sl[h]].astype(F32) * n_old[u], axis=-1, keepdims=True)
          for u, (sq, _, h) in enumerate(units)]
    s = [_dot_nt(sq.q[:, sl[h]], sq.k[:, sl[h]]) * dm[u] for u, (sq, _, h) in enumerate(units)]
    num = [col_of(gt["inter"], h) * _dot_nt(sq.q[:, sl[h]], c_old[u].astype(BF16))
           + _dot(s[u].astype(BF16), sq.v[:, sl[h]]) for u, (sq, gt, h) in enumerate(units)]
    for u, (sq, gt, h) in enumerate(units):
        den = col_of(gt["inter"], h) * qn[u] + jnp.sum(s[u], axis=-1, keepdims=True)
        rcp = 1.0 / jnp.maximum(jnp.abs(den), col_of(gt["nfloor"], h))
        ms = jnp.mean(num[u] * num[u], axis=-1, keepdims=True)
        sq.hm_store(h, num[u] * (rcp * lax.rsqrt(rcp * rcp * ms + EPS)))
    for u, (sq, gt, h) in enumerate(units):
        w_h = col_of(gt["w_in"], h)
        dec_h = col_of(gt["decay"], h)
        wv = (w_h * sq.v[:, sl[h]].astype(F32)).astype(BF16)
        sq.c_dst[h] = dec_h * c_old[u] + _dot_tn(wv, sq.k[:, sl[h]])
        sq.n_dst[h] = dec_h * n_old[u] + jnp.sum(w_h * sq.k[:, sl[h]].astype(F32), axis=0, keepdims=True)


GLA_SAFE_SPAN = 40.0


def _gla_scores_guarded(qh, kh, a, *, lg, sub):
    row = lax.broadcasted_iota(jnp.int32, (lg, lg), 0)
    col = lax.broadcasted_iota(jnp.int32, (lg, lg), 1)
    att = jnp.zeros((lg, lg), F32)
    c = sub
    while c < lg:
        nblk = lg // (2 * c)
        a_ref = jnp.concatenate(
            [jnp.broadcast_to(a[2 * c * i + c - 1:2 * c * i + c, :], (2 * c, G_DK))
             for i in range(nblk)], axis=0)
        qt = (qh * jnp.exp(jnp.minimum(a - a_ref, 0.0))).astype(BF16)
        kt = (kh * jnp.exp(jnp.minimum(a_ref - a, 0.0))).astype(BF16)
        valid = (row // (2 * c) == col // (2 * c)) & (row % (2 * c) >= c) & (col % (2 * c) < c)
        att = att + jnp.where(valid, _dot_nt(qt, kt), 0.0)
        c *= 2
    rows = lax.broadcasted_iota(jnp.int32, (sub, lg), 0)
    cols = lax.broadcasted_iota(jnp.int32, (sub, lg), 1)
    blocks = []
    for blk in range(lg // sub):
        r0 = blk * sub
        a_b = a[r0:r0 + sub, :]
        q_b = qh[r0:r0 + sub, :]
        acc = jnp.zeros((sub, lg), F32)
        for s in range(sub):
            e = jnp.exp(jnp.minimum(a_b - a[r0 + s:r0 + s + 1, :], 0.0))
            p = jnp.sum(q_b * e * kh[r0 + s:r0 + s + 1, :], axis=-1, keepdims=True)
            acc = jnp.where((cols == r0 + s) & (rows >= s), p, acc)
        blocks.append(acc)
    return att + jnp.concatenate(blocks, axis=0)


def _gla_chunk_guarded(seqs, gh_ref, *, lg, sub):
    ks = [slice(h * G_DK, (h + 1) * G_DK) for h in range(G_HEADS)]
    vs = [slice(h * G_DV, (h + 1) * G_DV) for h in range(G_HEADS)]
    pre = []
    for sq in seqs:
        a_last = sq.a[lg - 1:lg, :]
        qf = sq.qg.astype(F32)
        kf = sq.kg.astype(F32)
        pre.append(dict(a=sq.a, a_last=a_last, qf=qf, kf=kf,
                        qs=(qf * jnp.exp(sq.a)).astype(BF16),
                        kl=(kf * jnp.exp(a_last - sq.a)).astype(BF16)))
    units = [(sq, p, h) for sq, p in zip(seqs, pre) for h in range(G_HEADS)]
    s_old = [sq.s_src[h] for sq, _, h in units]
    att = [_gla_scores_guarded(p["qf"][:, ks[h]], p["kf"][:, ks[h]], p["a"][:, ks[h]], lg=lg, sub=sub)
           for _, p, h in units]
    o = [_dot(p["qs"][:, ks[h]], s_old[u].astype(BF16)) + _dot(att[u].astype(BF16), sq.vg[:, vs[h]])
         for u, (sq, p, h) in enumerate(units)]
    for u, (sq, _, h) in enumerate(units):
        sq.og_store(h, _rms(o[u], gh_ref[:, vs[h]]))
    for u, (sq, p, h) in enumerate(units):
        sq.s_dst[h] = (jnp.exp(_col_of_row(p["a_last"][:, ks[h]])) * s_old[u]
                       + _dot_tn(p["kl"][:, ks[h]], sq.vg[:, vs[h]]))


def _gla_block_direct(seqs, gh_ref, *, tb, lg):
    ks = [slice(h * G_DK, (h + 1) * G_DK) for h in range(G_HEADS)]
    vs = [slice(h * G_DV, (h + 1) * G_DV) for h in range(G_HEADS)]
    chunks = [slice(c * lg, (c + 1) * lg) for c in range(tb // lg)]
    row = lax.broadcasted_iota(jnp.int32, (lg, lg), 0)
    col = lax.broadcasted_iota(jnp.int32, (lg, lg), 1)
    causal = col <= row
    pre = []
    for sq in seqs:
        a_last = [sq.a[r.stop - 1:r.stop, :] for r in chunks]
        a_last_rows = jnp.concatenate([jnp.broadcast_to(al, (lg, G_KW)) for al in a_last], axis=0)
        qf = sq.qg.astype(F32)
        kf = sq.kg.astype(F32)
        pre.append(dict(a_last=a_last,
                        qs=(qf * jnp.exp(sq.a)).astype(BF16),
                        kt=(kf * jnp.exp(-sq.a)).astype(BF16),
                        kl=(kf * jnp.exp(a_last_rows - sq.a)).astype(BF16)))
    units = [(sq, p, h) for sq, p in zip(seqs, pre) for h in range(G_HEADS)]
    att = [[jnp.where(causal, _dot_nt(p["qs"][r, ks[h]], p["kt"][r, ks[h]]), 0.0).astype(BF16)
            for r in chunks] for _, p, h in units]
    s_cur = [sq.s_src[h] for sq, _, h in units]
    for c, r in enumerate(chunks):
        o = [_dot(p["qs"][r, ks[h]], s_cur[u].astype(BF16)) + _dot(att[u][c], sq.vg[r, vs[h]])
             for u, (sq, p, h) in enumerate(units)]
        for u, (sq, _, h) in enumerate(units):
            sq.og_rows_store(r, h, _rms(o[u], gh_ref[:, vs[h]]))
        s_cur = [jnp.exp(_col_of_row(p["a_last"][c][:, ks[h]])) * s_cur[u]
                 + _dot_tn(p["kl"][r, ks[h]], sq.vg[r, vs[h]])
                 for u, (sq, p, h) in enumerate(units)]
    for u, (sq, _, h) in enumerate(units):
        sq.s_dst[h] = s_cur[u]


def _scan_kernel(*refs, bt, tb, lm, lg, sub, has_state, time_minor, single_step):
    n_in = 9 + (4 if has_state else 0)
    qm_ref, km_ref, vm_ref, gif_ref, qg_ref, kg_ref, la_ref, vg_ref, ggh_ref = refs[:9]
    state_in = refs[9:n_in]
    hm_ref, og_ref = refs[n_in:n_in + 2]
    state_out = refs[n_in + 2:n_in + 6]
    scratch = refs[n_in + 6:]
    t = pl.program_id(1)
    nt = pl.num_programs(1)

    streamed = has_state and single_step
    src = state_in if streamed else scratch
    dst = state_out if streamed else scratch
    if not streamed:
        @pl.when(t == 0)
        def _():
            for i, ref in enumerate(scratch):
                ref[...] = state_in[i][...] if has_state else jnp.zeros(ref.shape, F32)

    def seq_records(bi, m_rows, g_rows):
        def hm_store(h, val):
            if time_minor:
                hm_ref[bi, 0, h * M_HEAD_DIM:(h + 1) * M_HEAD_DIM, :] = val
            else:
                hm_ref[bi, m_rows, h * M_HEAD_DIM:(h + 1) * M_HEAD_DIM] = val

        def og_rows_store(rows, h, val):
            og_ref[bi, rows, h * G_DV:(h + 1) * G_DV] = val

        return types.SimpleNamespace(
            bi=bi, hm_store=hm_store, og_rows_store=og_rows_store,
            og_store=functools.partial(og_rows_store, g_rows),
            c_src=src[0].at[bi], n_src=src[1].at[bi], m_src=src[2].at[bi], s_src=src[3].at[bi],
            c_dst=dst[0].at[bi], n_dst=dst[1].at[bi], m_dst=dst[2].at[bi], s_dst=dst[3].at[bi])

    def m_body(ci, carry):
        rows = pl.ds(pl.multiple_of(ci * lm, lm), lm)
        seqs = [seq_records(bi, rows, None) for bi in range(bt)]
        if time_minor:
            for sq in seqs:
                sq.qt, sq.k, sq.vt = qm_ref[sq.bi, 0], km_ref[sq.bi, rows, :], vm_ref[sq.bi, 0]
                sq.gift = gif_ref[sq.bi, 0]
            _mlstm_chunk_tm(seqs, lm=lm)
        else:
            for sq in seqs:
                sq.q, sq.k, sq.v = qm_ref[sq.bi, rows, :], km_ref[sq.bi, rows, :], vm_ref[sq.bi, rows, :]
                sq.gif = gif_ref[sq.bi, rows, :]
            _mlstm_chunk(seqs, lm=lm)
        return carry

    def gla_records(rows):
        seqs = [seq_records(bi, None, rows) for bi in range(bt)]
        for sq in seqs:
            sq.qg, sq.kg, sq.vg = qg_ref[sq.bi, rows, :], kg_ref[sq.bi, rows, :], vg_ref[sq.bi, rows, :]
            sq.a = la_ref[sq.bi, rows, :]
        return seqs

    def g_body(ci, carry):
        rows = pl.ds(pl.multiple_of(ci * lg, lg), lg)
        _gla_chunk_guarded(gla_records(rows), ggh_ref, lg=lg, sub=sub)
        return carry

    def loop(body, n):
        if n == 1:
            body(0, 0)
        else:
            lax.fori_loop(0, n, body, 0)

    loop(m_body, tb // lm)
    if lg <= sub:
        loop(g_body, tb // lg)
    else:
        chunk_decay = [la_ref[:, c * lg + lg - 1:(c + 1) * lg, :] for c in range(tb // lg)]
        span_ok = jnp.min(functools.reduce(jnp.minimum, chunk_decay)) >= -GLA_SAFE_SPAN

        @pl.when(span_ok)
        def _():
            _gla_block_direct(gla_records(slice(0, tb)), ggh_ref, tb=tb, lg=lg)

        @pl.when(jnp.logical_not(span_ok))
        def _():
            loop(g_body, tb // lg)

    if not streamed:
        @pl.when(t == nt - 1)
        def _():
            for out, ref in zip(state_out, scratch):
                out[...] = ref[...]


def _scan(qm, km, vm, gif, qg, kg, la, vg, ggh, state, *, bt, tb, lm, lg, sub, time_minor):
    b, t, _ = km.shape
    assert b % bt == 0 and t % tb == 0 and tb % lm == 0 and tb % lg == 0 and lg % sub == 0
    has_state = state is not None
    assert not (time_minor and has_state) and (not time_minor or (lm % LANES == 0 and tb == lm))
    tok = lambda w: pl.BlockSpec((bt, tb, w), lambda i, j: (i, j, 0))
    tok_t = lambda w: pl.BlockSpec((bt, 1, w, tb), lambda i, j: (i, j, 0, 0))
    c_shape = (M_HEADS, M_HEAD_DIM, M_HEAD_DIM)
    n_shape = (M_HEADS, 1, M_HEAD_DIM)
    m_shape = (SUBLANES, LANES) if time_minor else (1, LANES)
    s_shape = (G_HEADS, G_DK, G_DV)
    st_shapes = [(bt,) + shp for shp in (c_shape, n_shape, m_shape, s_shape)]
    st_specs = [pl.BlockSpec(shp, lambda i, j, nd=len(shp): (i,) + (0,) * (nd - 1)) for shp in st_shapes]
    single_step = t == tb
    carried = not (has_state and single_step)
    if time_minor:
        qv_spec, gif_spec = tok_t(M_WIDTH), tok_t(2 * SUBLANES)
        hm_spec, hm_shape = tok_t(M_WIDTH), (b, t // tb, M_WIDTH, tb)
    else:
        qv_spec, gif_spec = tok(M_WIDTH), tok(LANES)
        hm_spec, hm_shape = tok(M_WIDTH), (b, t, M_WIDTH)
    in_specs = [qv_spec, tok(M_WIDTH), qv_spec, gif_spec, tok(G_KW), tok(G_KW),
                tok(G_KW), tok(G_VW), _const_spec((1, G_VW))]
    args = [qm, km, vm, gif, qg, kg, la, vg, ggh]
    if has_state:
        in_specs += st_specs
        args += list(state)
    return pl.pallas_call(
        functools.partial(_scan_kernel, bt=bt, tb=tb, lm=lm, lg=lg, sub=sub, has_state=has_state,
                          time_minor=time_minor, single_step=single_step),
        grid=(b // bt, t // tb),
        in_specs=in_specs,
        out_specs=[hm_spec, tok(G_VW)] + st_specs,
        out_shape=[jax.ShapeDtypeStruct(hm_shape, F32),
                   jax.ShapeDtypeStruct((b, t, G_VW), F32)]
                  + [jax.ShapeDtypeStruct((b,) + shp[1:], F32) for shp in st_shapes],
        scratch_shapes=[pltpu.VMEM(shp, F32) for shp in st_shapes] if carried else [],
        compiler_params=pltpu.CompilerParams(
            dimension_semantics=("parallel", "arbitrary"), vmem_limit_bytes=VMEM_LIMIT),
        name="mixer_scan_state" if has_state else "mixer_scan",
    )(*args)


POST_O, POST_R, POST_A, POST_B = 0, 1024, 2048, 3072


def _post_kernel(xa_ref, hma_ref, oga_ref, xb_ref, hmb_ref, ogb_ref, g_ref, gmh_ref, w_ref, wpa_ref, wpb_ref,
                 wo_ref, oa_ref, ob_ref, *, tiles_a):
    def merge(x_ref, hm_ref, og_ref, o_ref, time_minor):
        x = x_ref[...]
        xn = _rms(x, g_ref[...]).astype(BF16)
        gate = lambda off: _dot(xn, w_ref[:, off:off + D_MODEL])
        if time_minor:
            hm = jnp.concatenate(
                [jnp.concatenate([jnp.transpose(hm_ref[0, j, h * M_HEAD_DIM:(h + 1) * M_HEAD_DIM, :])
                                  for h in range(M_HEADS)], axis=1)
                 for j in range(hm_ref.shape[1])], axis=0)
        else:
            hm = hm_ref[...]
        hm = (_sigmoid(gate(POST_O)) * (hm * gmh_ref[...])).astype(BF16)
        og = (_silu(gate(POST_R)) * og_ref[...]).astype(BF16)
        y = (_sigmoid(gate(POST_A)) * _dot(hm, wpa_ref[...])
             + _sigmoid(gate(POST_B)) * _dot(og, wpb_ref[...]))
        o_ref[...] = x + _dot(y.astype(BF16), wo_ref[...])

    on_a = pl.program_id(0) < tiles_a
    pl.when(on_a)(lambda: merge(xa_ref, hma_ref, oga_ref, oa_ref, True))
    pl.when(jnp.logical_not(on_a))(lambda: merge(xb_ref, hmb_ref, ogb_ref, ob_ref, False))


def _post(xa, hma, oga, xb, hmb, ogb, g, gmh, w_post, wpa, wpb, wo, *, tm):
    na, nb = xa.shape[0], xb.shape[0]
    assert na % tm == 0 and nb % tm == 0
    tiles_a = na // tm
    idx_a = lambda i: jnp.minimum(i, tiles_a - 1)
    idx_b = lambda i: jnp.maximum(i - tiles_a, 0)
    tok_a = pl.BlockSpec((tm, D_MODEL), lambda i: (idx_a(i), 0))
    tok_b = pl.BlockSpec((tm, D_MODEL), lambda i: (idx_b(i), 0))
    nblk, blk = hma.shape[1], hma.shape[3]
    assert tm % blk == 0 and (nblk * blk) % tm == 0
    per_seq = nblk * blk // tm
    hma_spec = pl.BlockSpec((1, tm // blk, M_WIDTH, blk),
                            lambda i: (idx_a(i) // per_seq, idx_a(i) % per_seq, 0, 0))
    return pl.pallas_call(
        functools.partial(_post_kernel, tiles_a=tiles_a),
        grid=(tiles_a + nb // tm,),
        in_specs=[tok_a, hma_spec, tok_a, tok_b, tok_b, tok_b, _const_spec((1, D_MODEL)),
                  _const_spec((1, M_WIDTH)), _const_spec((D_MODEL, 4 * D_MODEL)),
                  _const_spec((M_WIDTH, D_MODEL)), _const_spec((G_VW, D_MODEL)),
                  _const_spec((D_MODEL, D_MODEL))],
        out_specs=[tok_a, tok_b],
        out_shape=[jax.ShapeDtypeStruct((na, D_MODEL), F32), jax.ShapeDtypeStruct((nb, D_MODEL), F32)],
        compiler_params=pltpu.CompilerParams(
            dimension_semantics=("arbitrary",), vmem_limit_bytes=VMEM_LIMIT),
        name="mixer_post",
    )(xa, hma, oga, xb, hmb, ogb, g, gmh, w_post, wpa, wpb, wo)


def _pack_weights(g_ffn1, w_ffn1_up, w_ffn1_down, g_mix, w_in, conv_w, conv_b, w_mq, w_mk, b_if,
                  g_mhead, w_a2, b_a, g_ghead, w_pa, w_pb, w_o, g_ffn2, w_ffn2_up, w_ffn2_down,
                  g_final):
    l = 0
    row = lambda v: v.reshape(1, -1).astype(F32)
    edges = [0]
    for wd in (M_WIDTH, M_WIDTH, M_WIDTH, 2 * M_HEADS, G_KW, G_KW, G_VW, G_VW, G_RANK, D_MODEL, D_MODEL):
        edges.append(edges[-1] + wd)
    seg = lambda i: w_in[l][:, edges[i]:edges[i + 1]]
    u_m, v_m, o_m, if_m, q_g, k_g, v_g, r_g, a_g, g_a, g_b = (seg(i) for i in range(11))
    w_pre = jnp.concatenate([u_m, v_m, q_g, k_g, v_g], axis=1).astype(BF16)
    w_small = jnp.concatenate(
        [if_m, a_g, jnp.zeros((D_MODEL, LANES - 2 * M_HEADS - G_RANK), F32)], axis=1).astype(BF16)
    w_post = jnp.concatenate([o_m, r_g, g_a, g_b], axis=1).astype(BF16)
    bif = jnp.concatenate([b_if[l].reshape(-1), jnp.zeros((LANES - 2 * M_HEADS,), F32)]).reshape(1, LANES)
    wa2 = jnp.zeros((LANES, G_KW), F32).at[A_LANE0:A_LANE0 + G_RANK].set(w_a2[l]).astype(BF16)
    return dict(
        g_ffn1=row(g_ffn1[l]), w_ffn1_up=w_ffn1_up[l].astype(BF16), w_ffn1_down=w_ffn1_down[l].astype(BF16),
        g_mix=row(g_mix[l]), w_pre=w_pre, w_small=w_small, w_post=w_post,
        conv_w=conv_w[l].astype(F32), conv_b=row(conv_b[l]),
        w_mq=w_mq[l].astype(BF16), w_mk=w_mk[l].astype(BF16), bif=bif,
        g_mhead=row(g_mhead[l]), wa2=wa2, b_a=row(b_a[l]), g_ghead=row(g_ghead[l]),
        w_pa=w_pa[l].astype(BF16), w_pb=w_pb[l].astype(BF16), w_o=w_o[l].astype(BF16),
        g_ffn2=row(g_ffn2[l]), w_ffn2_up=w_ffn2_up[l].astype(BF16), w_ffn2_down=w_ffn2_down[l].astype(BF16),
        g_final=row(g_final))


def _mixer(x1, b, t, state, w, *, pre_tile, scan_tile, lm, lg, sub, time_minor, **_):
    n = b * t
    if state is None:
        cst = jnp.zeros((b, SUBLANES, M_WIDTH), F32)
        scan_state = None
    else:
        conv0, c0, n0, m0, s0 = state
        cst = jnp.concatenate(
            [jnp.zeros((b, SUBLANES - (CONV_W - 1), M_WIDTH), F32), conv0.astype(F32)], axis=1)
        m0p = jnp.concatenate([m0, jnp.zeros((b, LANES - M_HEADS), F32)], axis=1).reshape(b, 1, LANES)
        scan_state = (c0, n0.reshape(b, M_HEADS, 1, M_HEAD_DIM), m0p, s0)
    qm, km, vm, gif, qg, kg, la, vg, cout = _pre(
        x1.reshape(b, t, D_MODEL), cst, w["g_mix"], w["w_pre"], w["w_small"], w["conv_w"], w["conv_b"],
        w["w_mq"], w["w_mk"], w["bif"], w["wa2"], w["b_a"], bt=pre_tile[0], tb=pre_tile[1], sub=lm, lg=lg,
        time_minor=time_minor)
    hm, og, c_new, n_new, m_new, s_new = _scan(
        qm, km, vm, gif, qg, kg, la, vg, w["g_ghead"], scan_state,
        bt=scan_tile[0], tb=scan_tile[1], lm=lm, lg=lg, sub=sub, time_minor=time_minor)
    m_heads = m_new[:, :M_HEADS, 0] if time_minor else m_new[:, 0, :M_HEADS]
    return (hm if time_minor else hm.reshape(n, M_WIDTH), og.reshape(n, G_VW),
            cout[:, SUBLANES - (CONV_W - 1):, :][None],
            c_new[None],
            n_new.reshape(b, M_HEADS, M_HEAD_DIM)[None],
            m_heads[None],
            s_new[None])


FFN_ROWS = 1024
DECODE_SEQS_PER_STEP = 4
PREFILL_SEQS_PER_STEP = 2


def _tiles(b, t):
    n = b * t
    tm = math.gcd(n, 512)
    if t % 256 == 0:
        return dict(tm=math.gcd(t, 512), pre_tile=(1, math.gcd(t, 512)), scan_tile=(math.gcd(b, PREFILL_SEQS_PER_STEP), 256),
                    lm=256, lg=128, sub=16, time_minor=True)
    assert t <= 64 and t % SUBLANES == 0
    bt = max(1, min(b, 512 // t))
    while b % bt:
        bt -= 1
    return dict(tm=tm, pre_tile=(bt, t), scan_tile=(math.gcd(b, DECODE_SEQS_PER_STEP), t), lm=t, lg=t,
                sub=min(16, t), time_minor=False)


def kernel(x_prompt, x_sample, state_conv, state_mlstm_C, state_mlstm_n, state_mlstm_m, state_gla_S,
           g_ffn1, w_ffn1_up, w_ffn1_down, g_mix, w_in, conv_w, conv_b, w_mq, w_mk, b_if, g_mhead,
           w_a2, b_a, g_ghead, w_pa, w_pb, w_o, g_ffn2, w_ffn2_up, w_ffn2_down, g_final):
    w = _pack_weights(g_ffn1, w_ffn1_up, w_ffn1_down, g_mix, w_in, conv_w, conv_b, w_mq, w_mk, b_if,
                      g_mhead, w_a2, b_a, g_ghead, w_pa, w_pb, w_o, g_ffn2, w_ffn2_up, w_ffn2_down,
                      g_final)
    bp, tp, _ = x_prompt.shape
    bs, ts, _ = x_sample.shape
    tm_ffn = math.gcd(math.gcd(bp * tp, bs * ts), FFN_ROWS)
    x1_p, x1_s = _ffn(x_prompt.reshape(bp * tp, D_MODEL), x_sample.reshape(bs * ts, D_MODEL),
                      w["g_ffn1"], w["w_ffn1_up"], w["w_ffn1_down"], w["g_final"], final_norm=False, tm=tm_ffn)
    tiles_p, tiles_s = _tiles(bp, tp), _tiles(bs, ts)
    assert tiles_p["time_minor"] and not tiles_s["time_minor"]
    out_p = _mixer(x1_p, bp, tp, None, w, **tiles_p)
    state = (state_conv[0], state_mlstm_C[0], state_mlstm_n[0], state_mlstm_m[0], state_gla_S[0])
    out_s = _mixer(x1_s, bs, ts, state, w, **tiles_s)
    x2_p, x2_s = _post(x1_p, out_p[0], out_p[1], x1_s, out_s[0], out_s[1], w["g_mix"], w["g_mhead"], w["w_post"],
                       w["w_pa"], w["w_pb"], w["w_o"], tm=math.gcd(tiles_p["tm"], tiles_s["tm"]))
    y_p, y_s = _ffn(x2_p, x2_s, w["g_ffn2"], w["w_ffn2_up"], w["w_ffn2_down"], w["g_final"],
                    final_norm=True, tm=tm_ffn)
    return (y_p.reshape(bp, tp, D_MODEL), y_s.reshape(bs, ts, D_MODEL)) + out_p[2:] + out_s[2:]
```

```python
import functools
import math
import types

import jax
import jax.numpy as jnp
from jax import lax
from jax.experimental import pallas as pl
from jax.experimental.pallas import tpu as pltpu

D_MODEL = 1024
M_HEADS = 4
M_HEAD_DIM = 256
M_WIDTH = 1024
CONV_W = 4
G_HEADS = 4
G_DK = 128
G_DV = 256
G_KW = 512
G_VW = 1024
G_RANK = 16
G_TAU = 16.0
D_FF = 2816
EPS = 1e-6

LANES = 128
SUBLANES = 8
BF16_ROWS = 16
VMEM_LIMIT = 56 * 1024 * 1024

F32 = jnp.float32
BF16 = jnp.bfloat16
NEG_INF = float("-inf")


def _sigmoid(x):
    return 1.0 / (1.0 + jnp.exp(-x))


def _silu(x):
    return x * _sigmoid(x)


def _log_sigmoid(x):
    return jnp.minimum(x, 0.0) - jnp.log(1.0 + jnp.exp(-jnp.abs(x)))


def _rms(x, g):
    return x * lax.rsqrt(jnp.mean(x * x, axis=-1, keepdims=True) + EPS) * g


def _dot(a, b):
    return jnp.dot(a, b, preferred_element_type=F32)


def _dot_nt(a, b):
    return lax.dot_general(a, b, (((1,), (1,)), ((), ())), preferred_element_type=F32)


def _dot_tn(a, b):
    return lax.dot_general(a, b, (((0,), (0,)), ((), ())), preferred_element_type=F32)


def _const_spec(shape):
    nd = len(shape)
    return pl.BlockSpec(shape, lambda *_: (0,) * nd, pipeline_mode=pl.Buffered(1))


FF_CHUNK = 256


def _ffn_kernel(xa_ref, xb_ref, g_ref, wup_ref, wdown_ref, gfin_ref, oa_ref, ob_ref, h_ref, *,
                final_norm, tiles_a):
    def half_step(x_ref, o_ref):
        x = x_ref[...]
        xn = _rms(x, g_ref[...]).astype(BF16)
        for c in range(D_FF // FF_CHUNK):
            a = _dot(xn, wup_ref[:, c * FF_CHUNK:(c + 1) * FF_CHUNK])
            g = _dot(xn, wup_ref[:, D_FF + c * FF_CHUNK:D_FF + (c + 1) * FF_CHUNK])
            h_ref[:, c * FF_CHUNK:(c + 1) * FF_CHUNK] = (_silu(g) * a).astype(BF16)
        y = x + 0.5 * _dot(h_ref[...], wdown_ref[...])
        o_ref[...] = _rms(y, gfin_ref[...]) if final_norm else y

    on_a = pl.program_id(0) < tiles_a
    pl.when(on_a)(lambda: half_step(xa_ref, oa_ref))
    pl.when(jnp.logical_not(on_a))(lambda: half_step(xb_ref, ob_ref))


def _ffn(xa, xb, g, wup, wdown, gfin, *, final_norm, tm):
    na, nb = xa.shape[0], xb.shape[0]
    assert na % tm == 0 and nb % tm == 0
    tiles_a, tiles_b = na // tm, nb // tm
    spec_a = pl.BlockSpec((tm, D_MODEL), lambda i: (jnp.minimum(i, tiles_a - 1), 0))
    spec_b = pl.BlockSpec((tm, D_MODEL), lambda i: (jnp.maximum(i - tiles_a, 0), 0),
                          **(dict(pipeline_mode=pl.Buffered(1)) if tiles_b == 1 else {}))
    return pl.pallas_call(
        functools.partial(_ffn_kernel, final_norm=final_norm, tiles_a=tiles_a),
        grid=(tiles_a + tiles_b,),
        in_specs=[
            spec_a, spec_b,
            _const_spec((1, D_MODEL)),
            _const_spec((D_MODEL, 2 * D_FF)),
            _const_spec((D_FF, D_MODEL)),
            _const_spec((1, D_MODEL)),
        ],
        out_specs=[spec_a, spec_b],
        out_shape=[jax.ShapeDtypeStruct((na, D_MODEL), F32), jax.ShapeDtypeStruct((nb, D_MODEL), F32)],
        scratch_shapes=[pltpu.VMEM((tm, D_FF), BF16)],
        compiler_params=pltpu.CompilerParams(
            dimension_semantics=("arbitrary",), vmem_limit_bytes=VMEM_LIMIT),
        name="ffn_final" if final_norm else "ffn",
    )(xa, xb, g, wup, wdown, gfin)


PRE_U, PRE_V, PRE_QG, PRE_KG, PRE_VG = 0, 1024, 2048, 2560, 3072
PRE_WIDTH = 4096
A_LANE0 = 8


def _pre_kernel(x_ref, cst_ref, g_ref, w_ref, ws_ref, cw_ref, cb_ref, wq_ref, wk_ref, bif_ref,
                wa2_ref, ba_ref,
                qm_ref, km_ref, vm_ref, gif_ref, qg_ref, kg_ref, la_ref, vg_ref, cout_ref,
                cbuf_ref, *, bt, tb, sub, lg, time_minor):
    t = pl.program_id(1)
    m = bt * sub

    @pl.when(t == 0)
    def _():
        cbuf_ref[:, 0:SUBLANES, :] = cst_ref[...]

    @pl.when(t > 0)
    def _():
        cbuf_ref[:, 0:SUBLANES, :] = cbuf_ref[:, tb:tb + SUBLANES, :]

    for j in range(tb // sub):
        r0 = j * sub
        rows = slice(r0, r0 + sub)
        to_rows = lambda y, w: y.reshape(bt, sub, w)
        xn = _rms(x_ref[:, rows, :].reshape(m, D_MODEL), g_ref[...]).astype(BF16)
        small = _dot_nt(xn, ws_ref[...])

        u = _dot_nt(xn, w_ref[PRE_U:PRE_U + M_WIDTH, :])
        cbuf_ref[:, SUBLANES + r0:SUBLANES + r0 + sub, :] = to_rows(u, M_WIDTH)
        c = cb_ref[...].reshape(1, 1, M_WIDTH)
        for tap in range(CONV_W):
            off = SUBLANES - (CONV_W - 1) + tap + r0
            c = c + cbuf_ref[:, off:off + sub, :] * cw_ref[tap:tap + 1, :].reshape(1, 1, M_WIDTH)
        ch = _silu(c).reshape(m, M_WIDTH).astype(BF16)

        vm = _dot_nt(xn, w_ref[PRE_V:PRE_V + M_WIDTH, :])
        gates = small + bif_ref[...]
        lane = lax.broadcasted_iota(jnp.int32, gates.shape, 1)
        gif = jnp.where(lane < M_HEADS, gates, jnp.where(lane < 2 * M_HEADS, _log_sigmoid(gates), 0.0))
        if time_minor:
            b = pltpu.roll(_scan_rows(gif, jnp.add, 0.0), LANES - M_HEADS, axis=1)
            g = gif - b
            gmax = _scan_rows(g, jnp.maximum, NEG_INF)
            packed = jnp.where(lane < M_HEADS, g,
                               jnp.where(lane < 2 * M_HEADS, pltpu.roll(gmax, M_HEADS, axis=1),
                                         jnp.where(lane < 3 * M_HEADS, pltpu.roll(b, 2 * M_HEADS, axis=1), 0.0)))
            gif_ref[0, j] = jnp.transpose(packed)[0:2 * SUBLANES, :]
        else:
            gif_ref[:, rows, :] = to_rows(gif, LANES)
            vm_ref[:, rows, :] = to_rows(vm.astype(BF16), M_WIDTH)

        def decay_columns(h):
            cols = slice(h * G_DK, (h + 1) * G_DK)
            a2 = _dot(small.astype(BF16), wa2_ref[:, cols]) + ba_ref[:, cols]
            la_ref[:, rows, cols] = to_rows(
                _scan_rows(_log_sigmoid(a2) * (1.0 / G_TAU), jnp.add, 0.0, segment=lg), G_DK)

        qg_ref[:, rows, :] = to_rows(
            (_dot_nt(xn, w_ref[PRE_QG:PRE_QG + G_KW, :]) * (G_DK ** -0.5)).astype(BF16), G_KW)
        decay_columns(0)
        kg_ref[:, rows, :] = to_rows(_dot_nt(xn, w_ref[PRE_KG:PRE_KG + G_KW, :]).astype(BF16), G_KW)
        decay_columns(1)
        half = G_VW // 2
        for p in range(2):
            vg_ref[:, rows, p * half:(p + 1) * half] = to_rows(
                _dot_nt(xn, w_ref[PRE_VG + p * half:PRE_VG + (p + 1) * half, :]).astype(BF16), half)
            decay_columns(2 + p)

        for h in range(M_HEADS):
            sl = slice(h * M_HEAD_DIM, (h + 1) * M_HEAD_DIM)
            qh = _dot(ch[:, sl], wq_ref[h])
            if time_minor:
                qm_ref[0, j, sl, :] = jnp.transpose(qh).astype(BF16)
                vm_ref[0, j, sl, :] = jnp.transpose(vm[:, sl]).astype(BF16)
            else:
                qm_ref[:, rows, sl] = to_rows(qh.astype(BF16), M_HEAD_DIM)
            km_ref[:, rows, sl] = to_rows(
                (_dot(ch[:, sl], wk_ref[h]) * (M_HEAD_DIM ** -0.5)).astype(BF16), M_HEAD_DIM)
    cout_ref[...] = cbuf_ref[:, tb:tb + SUBLANES, :]


def _pre(x, cst, g, w_pre, w_small, conv_w, conv_b, wq, wk, bif, wa2, ba, *, bt, tb, sub, lg, time_minor):
    b, t, _ = x.shape
    assert b % bt == 0 and t % tb == 0 and tb % sub == 0 and sub % SUBLANES == 0 and sub % lg == 0
    assert not time_minor or (bt == 1 and sub % LANES == 0)
    tok = lambda w: pl.BlockSpec((bt, tb, w), lambda i, j: (i, j, 0))
    per_seq = pl.BlockSpec((bt, SUBLANES, M_WIDTH), lambda i, j: (i, 0, 0))
    sds = lambda w, dt: jax.ShapeDtypeStruct((b, t, w), dt)
    if time_minor:
        tok_t = lambda w: pl.BlockSpec((1, tb // sub, w, sub), lambda i, j: (i, j, 0, 0))
        sds_t = lambda w, dt: jax.ShapeDtypeStruct((b, t // sub, w, sub), dt)
        qv_spec, qv_shape = tok_t(M_WIDTH), sds_t(M_WIDTH, BF16)
        gif_spec, gif_shape = tok_t(2 * SUBLANES), sds_t(2 * SUBLANES, F32)
    else:
        qv_spec, qv_shape = tok(M_WIDTH), sds(M_WIDTH, BF16)
        gif_spec, gif_shape = tok(LANES), sds(LANES, F32)
    return pl.pallas_call(
        functools.partial(_pre_kernel, bt=bt, tb=tb, sub=sub, lg=lg, time_minor=time_minor),
        grid=(b // bt, t // tb),
        in_specs=[
            tok(D_MODEL), per_seq,
            _const_spec((1, D_MODEL)),
            _const_spec((PRE_WIDTH, D_MODEL)),
            _const_spec((LANES, D_MODEL)),
            _const_spec((CONV_W, M_WIDTH)),
            _const_spec((1, M_WIDTH)),
            _const_spec((M_HEADS, M_HEAD_DIM, M_HEAD_DIM)),
            _const_spec((M_HEADS, M_HEAD_DIM, M_HEAD_DIM)),
            _const_spec((1, LANES)),
            _const_spec((LANES, G_KW)),
            _const_spec((1, G_KW)),
        ],
        out_specs=[qv_spec, tok(M_WIDTH), qv_spec, gif_spec, tok(G_KW), tok(G_KW),
                   tok(G_KW), tok(G_VW), per_seq],
        out_shape=[qv_shape, sds(M_WIDTH, BF16), qv_shape, gif_shape,
                   sds(G_KW, BF16), sds(G_KW, BF16), sds(G_KW, F32), sds(G_VW, BF16),
                   jax.ShapeDtypeStruct((b, SUBLANES, M_WIDTH), F32)],
        scratch_shapes=[pltpu.VMEM((bt, tb + SUBLANES, M_WIDTH), F32)],
        compiler_params=pltpu.CompilerParams(
            dimension_semantics=("parallel", "arbitrary"), vmem_limit_bytes=VMEM_LIMIT),
        name="mixer_pre",
    )(x, cst, g, w_pre, w_small, conv_w, conv_b, wq, wk, bif, wa2, ba)


def _scan_rows(x, op, fill, segment=None):
    n = segment or x.shape[0]
    row = lax.broadcasted_iota(jnp.int32, x.shape, 0) % n
    s = 1
    while s < n:
        x = op(x, jnp.where(row >= s, pltpu.roll(x, s, axis=0), fill))
        s *= 2
    return x


def _col_of_row(row):
    return jnp.transpose(jnp.broadcast_to(row, (SUBLANES, LANES)))[:, 0:1]


def _mlstm_chunk_tm(seqs, *, lm):
    reps = lm // LANES
    head_row = lax.broadcasted_iota(jnp.int32, (SUBLANES, LANES), 0) < M_HEADS
    src = lax.broadcasted_iota(jnp.int32, (lm, lm), 0)
    dst = lax.broadcasted_iota(jnp.int32, (lm, lm), 1)
    causal = src <= dst
    sl = [slice(h * M_HEAD_DIM, (h + 1) * M_HEAD_DIM) for h in range(M_HEADS)]
    row = lambda x, h: x[h:h + 1, :]
    gates = []
    for sq in seqs:
        g = sq.gift[0:SUBLANES]
        gmax = pltpu.roll(g, M_HEADS, axis=0)
        b = sq.gift[SUBLANES:2 * SUBLANES]
        m_prev = jnp.concatenate([sq.m_src[...]] * reps, axis=1)
        mx = jnp.maximum(m_prev, gmax)
        mx_last = jnp.broadcast_to(mx[:, lm - 1:lm], (SUBLANES, lm))
        gates.append(dict(
            mx=mx,
            inter=jnp.exp(m_prev - mx),
            nfloor=jnp.exp(-(b + mx)),
            w_in=jnp.exp(g - mx_last),
            decay=jnp.exp(m_prev - mx_last),
            g_col=jnp.transpose(g)))
        m_new = jnp.broadcast_to(b[:, lm - 1:lm], (SUBLANES, lm)) + mx_last
        sq.m_dst[...] = jnp.where(head_row, m_new[:, :LANES], 0.0)
    units = [(sq, gt, h) for sq, gt in zip(seqs, gates) for h in range(M_HEADS)]
    c_old = [sq.c_src[h] for sq, _, h in units]
    n_old = [sq.n_src[h] for sq, _, h in units]
    cn = [jnp.concatenate([c_old[u].astype(BF16),
                           jnp.broadcast_to(n_old[u].astype(BF16), (BF16_ROWS, M_HEAD_DIM))], axis=0)
          for u in range(len(units))]
    cq = [_dot(cn[u], sq.qt[sl[h], :]) for u, (sq, _, h) in enumerate(units)]
    dm = [jnp.exp(jnp.where(causal, gt["g_col"][:, h:h + 1] - row(gt["mx"], h), NEG_INF))
          for _, gt, h in units]
    st = [_dot(sq.k[:, sl[h]], sq.qt[sl[h], :]) * dm[u]
          for u, (sq, _, h) in enumerate(units)]
    num = [row(gt["inter"], h) * cq[u][:M_HEAD_DIM] + _dot(sq.vt[sl[h], :], st[u].astype(BF16))
           for u, (sq, gt, h) in enumerate(units)]
    for u, (sq, gt, h) in enumerate(units):
        den = (row(gt["inter"], h) * cq[u][M_HEAD_DIM:M_HEAD_DIM + 1]
               + jnp.sum(st[u], axis=0, keepdims=True))
        rcp = 1.0 / jnp.maximum(jnp.abs(den), row(gt["nfloor"], h))
        ms = jnp.mean(num[u] * num[u], axis=0, keepdims=True)
        sq.hm_store(h, num[u] * (rcp * lax.rsqrt(rcp * rcp * ms + EPS)))
    for u, (sq, gt, h) in enumerate(units):
        w_h = row(gt["w_in"], h)
        dec_h = jnp.concatenate([row(gt["decay"], h)[:, :LANES]] * (M_HEAD_DIM // LANES), axis=1)
        wv = jnp.concatenate([(sq.vt[sl[h], :].astype(F32) * w_h).astype(BF16),
                              jnp.broadcast_to(w_h.astype(BF16), (BF16_ROWS, lm))], axis=0)
        upd = _dot(wv, sq.k[:, sl[h]])
        sq.c_dst[h] = dec_h * c_old[u] + upd[:M_HEAD_DIM]
        sq.n_dst[h] = dec_h * n_old[u] + upd[M_HEAD_DIM:M_HEAD_DIM + 1]


def _mlstm_chunk(seqs, *, lm):
    row = lax.broadcasted_iota(jnp.int32, (lm, lm), 0)
    col = lax.broadcasted_iota(jnp.int32, (lm, lm), 1)
    causal = col <= row
    sl = [slice(h * M_HEAD_DIM, (h + 1) * M_HEAD_DIM) for h in range(M_HEADS)]
    gates = []
    for sq in seqs:
        cs = _scan_rows(sq.gif, jnp.add, 0.0)
        b = pltpu.roll(cs, LANES - M_HEADS, axis=1)
        g = sq.gif - b
        m_prev = sq.m_src[...]
        mx = jnp.maximum(m_prev, _scan_rows(g, jnp.maximum, NEG_INF))
        mx_last = mx[lm - 1:lm, :]
        gates.append(dict(
            mx=mx,
            inter=jnp.exp(m_prev - mx),
            nfloor=jnp.exp(-(b + mx)),
            g_t=jnp.transpose(g),
            w_in=jnp.exp(g - mx_last),
            decay=jnp.exp(m_prev - mx_last)))
        sq.m_dst[...] = b[lm - 1:lm, :] + mx_last
    units = [(sq, gt, h) for sq, gt in zip(seqs, gates) for h in range(M_HEADS)]
    col_of = lambda x, h: x[:, h:h + 1]
    c_old = [sq.c_src[h] for sq, _, h in units]
    n_old = [sq.n_src[h] for sq, _, h in units]
    dm = [jnp.exp(jnp.where(causal, gt["g_t"][h:h + 1, :] - col_of(gt["mx"], h), NEG_INF))
          for _, gt, h in units]
    qn = [jnp.sum(sq.q[:, sl[h]].astype(F32) * n_old[u], axis=-1, keepdims=True)
          for u, (sq, _, h) in enumerate(units)]
    s = [_dot_nt(sq.q[:, sl[h]], sq.k[:, sl[h]]) * dm[u] for u, (sq, _, h) in enumerate(units)]
    num = [col_of(gt["inter"], h) * _dot_nt(sq.q[:, sl[h]], c_old[u].astype(BF16))
           + _dot(s[u].astype(BF16), sq.v[:, sl[h]]) for u, (sq, gt, h) in enumerate(units)]
    for u, (sq, gt, h) in enumerate(units):
        den = col_of(gt["inter"], h) * qn[u] + jnp.sum(s[u], axis=-1, keepdims=True)
        rcp = 1.0 / jnp.maximum(jnp.abs(den), col_of(gt["nfloor"], h))
        ms = jnp.mean(num[u] * num[u], axis=-1, keepdims=True)
        sq.hm_store(h, num[u] * (rcp * lax.rsqrt(rcp * rcp * ms + EPS)))
    for u, (sq, gt, h) in enumerate(units):
        w_h = col_of(gt["w_in"], h)
        dec_h = col_of(gt["decay"], h)
        wv = (w_h * sq.v[:, sl[h]].astype(F32)).astype(BF16)
        sq.c_dst[h] = dec_h * c_old[u] + _dot_tn(wv, sq.k[:, sl[h]])
        sq.n_dst[h] = dec_h * n_old[u] + jnp.sum(w_h * sq.k[:, sl[h]].astype(F32), axis=0, keepdims=True)


GLA_SAFE_SPAN = 40.0


def _gla_scores_guarded(qh, kh, a, *, lg, sub):
    row = lax.broadcasted_iota(jnp.int32, (lg, lg), 0)
    col = lax.broadcasted_iota(jnp.int32, (lg, lg), 1)
    att = jnp.zeros((lg, lg), F32)
    c = sub
    while c < lg:
        nblk = lg // (2 * c)
        a_ref = jnp.concatenate(
            [jnp.broadcast_to(a[2 * c * i + c - 1:2 * c * i + c, :], (2 * c, G_DK))
             for i in range(nblk)], axis=0)
        qt = (qh * jnp.exp(jnp.minimum(a - a_ref, 0.0))).astype(BF16)
        kt = (kh * jnp.exp(jnp.minimum(a_ref - a, 0.0))).astype(BF16)
        valid = (row // (2 * c) == col // (2 * c)) & (row % (2 * c) >= c) & (col % (2 * c) < c)
        att = att + jnp.where(valid, _dot_nt(qt, kt), 0.0)
        c *= 2
    rows = lax.broadcasted_iota(jnp.int32, (sub, lg), 0)
    cols = lax.broadcasted_iota(jnp.int32, (sub, lg), 1)
    blocks = []
    for blk in range(lg // sub):
        r0 = blk * sub
        a_b = a[r0:r0 + sub, :]
        q_b = qh[r0:r0 + sub, :]
        acc = jnp.zeros((sub, lg), F32)
        for s in range(sub):
            e = jnp.exp(jnp.minimum(a_b - a[r0 + s:r0 + s + 1, :], 0.0))
            p = jnp.sum(q_b * e * kh[r0 + s:r0 + s + 1, :], axis=-1, keepdims=True)
            acc = jnp.where((cols == r0 + s) & (rows >= s), p, acc)
        blocks.append(acc)
    return att + jnp.concatenate(blocks, axis=0)


def _gla_chunk_guarded(seqs, gh_ref, *, lg, sub):
    ks = [slice(h * G_DK, (h + 1) * G_DK) for h in range(G_HEADS)]
    vs = [slice(h * G_DV, (h + 1) * G_DV) for h in range(G_HEADS)]
    pre = []
    for sq in seqs:
        a_last = sq.a[lg - 1:lg, :]
        qf = sq.qg.astype(F32)
        kf = sq.kg.astype(F32)
        pre.append(dict(a=sq.a, a_last=a_last, qf=qf, kf=kf,
                        qs=(qf * jnp.exp(sq.a)).astype(BF16),
                        kl=(kf * jnp.exp(a_last - sq.a)).astype(BF16)))
    units = [(sq, p, h) for sq, p in zip(seqs, pre) for h in range(G_HEADS)]
    s_old = [sq.s_src[h] for sq, _, h in units]
    att = [_gla_scores_guarded(p["qf"][:, ks[h]], p["kf"][:, ks[h]], p["a"][:, ks[h]], lg=lg, sub=sub)
           for _, p, h in units]
    o = [_dot(p["qs"][:, ks[h]], s_old[u].astype(BF16)) + _dot(att[u].astype(BF16), sq.vg[:, vs[h]])
         for u, (sq, p, h) in enumerate(units)]
    for u, (sq, _, h) in enumerate(units):
        sq.og_store(h, _rms(o[u], gh_ref[:, vs[h]]))
    for u, (sq, p, h) in enumerate(units):
        sq.s_dst[h] = (jnp.exp(_col_of_row(p["a_last"][:, ks[h]])) * s_old[u]
                       + _dot_tn(p["kl"][:, ks[h]], sq.vg[:, vs[h]]))


def _gla_block_direct(seqs, gh_ref, *, tb, lg):
    ks = [slice(h * G_DK, (h + 1) * G_DK) for h in range(G_HEADS)]
    vs = [slice(h * G_DV, (h + 1) * G_DV) for h in range(G_HEADS)]
    chunks = [slice(c * lg, (c + 1) * lg) for c in range(tb // lg)]
    row = lax.broadcasted_iota(jnp.int32, (lg, lg), 0)
    col = lax.broadcasted_iota(jnp.int32, (lg, lg), 1)
    causal = col <= row
    pre = []
    for sq in seqs:
        a_last = [sq.a[r.stop - 1:r.stop, :] for r in chunks]
        a_last_rows = jnp.concatenate([jnp.broadcast_to(al, (lg, G_KW)) for al in a_last], axis=0)
        qf = sq.qg.astype(F32)
        kf = sq.kg.astype(F32)
        pre.append(dict(a_last=a_last,
                        qs=(qf * jnp.exp(sq.a)).astype(BF16),
                        kt=(kf * jnp.exp(-sq.a)).astype(BF16),
                        kl=(kf * jnp.exp(a_last_rows - sq.a)).astype(BF16)))
    units = [(sq, p, h) for sq, p in zip(seqs, pre) for h in range(G_HEADS)]
    att = [[jnp.where(causal, _dot_nt(p["qs"][r, ks[h]], p["kt"][r, ks[h]]), 0.0).astype(BF16)
            for r in chunks] for _, p, h in units]
    s_cur = [sq.s_src[h] for sq, _, h in units]
    for c, r in enumerate(chunks):
        o = [_dot(p["qs"][r, ks[h]], s_cur[u].astype(BF16)) + _dot(att[u][c], sq.vg[r, vs[h]])
             for u, (sq, p, h) in enumerate(units)]
        for u, (sq, _, h) in enumerate(units):
            sq.og_rows_store(r, h, _rms(o[u], gh_ref[:, vs[h]]))
        s_cur = [jnp.exp(_col_of_row(p["a_last"][c][:, ks[h]])) * s_cur[u]
                 + _dot_tn(p["kl"][r, ks[h]], sq.vg[r, vs[h]])
                 for u, (sq, p, h) in enumerate(units)]
    for u, (sq, _, h) in enumerate(units):
        sq.s_dst[h] = s_cur[u]


def _scan_kernel(*refs, bt, tb, lm, lg, sub, has_state, time_minor, single_step):
    n_in = 9 + (4 if has_state else 0)
    qm_ref, km_ref, vm_ref, gif_ref, qg_ref, kg_ref, la_ref, vg_ref, ggh_ref = refs[:9]
    state_in = refs[9:n_in]
    hm_ref, og_ref = refs[n_in:n_in + 2]
    state_out = refs[n_in + 2:n_in + 6]
    scratch = refs[n_in + 6:]
    t = pl.program_id(1)
    nt = pl.num_programs(1)

    streamed = has_state and single_step
    src = state_in if streamed else scratch
    dst = state_out if streamed else scratch
    if not streamed:
        @pl.when(t == 0)
        def _():
            for i, ref in enumerate(scratch):
                ref[...] = state_in[i][...] if has_state else jnp.zeros(ref.shape, F32)

    def seq_records(bi, m_rows, g_rows):
        def hm_store(h, val):
            if time_minor:
                hm_ref[bi, 0, h * M_HEAD_DIM:(h + 1) * M_HEAD_DIM, :] = val
            else:
                hm_ref[bi, m_rows, h * M_HEAD_DIM:(h + 1) * M_HEAD_DIM] = val

        def og_rows_store(rows, h, val):
            og_ref[bi, rows, h * G_DV:(h + 1) * G_DV] = val

        return types.SimpleNamespace(
            bi=bi, hm_store=hm_store, og_rows_store=og_rows_store,
            og_store=functools.partial(og_rows_store, g_rows),
            c_src=src[0].at[bi], n_src=src[1].at[bi], m_src=src[2].at[bi], s_src=src[3].at[bi],
            c_dst=dst[0].at[bi], n_dst=dst[1].at[bi], m_dst=dst[2].at[bi], s_dst=dst[3].at[bi])

    def m_body(ci, carry):
        rows = pl.ds(pl.multiple_of(ci * lm, lm), lm)
        seqs = [seq_records(bi, rows, None) for bi in range(bt)]
        if time_minor:
            for sq in seqs:
                sq.qt, sq.k, sq.vt = qm_ref[sq.bi, 0], km_ref[sq.bi, rows, :], vm_ref[sq.bi, 0]
                sq.gift = gif_ref[sq.bi, 0]
            _mlstm_chunk_tm(seqs, lm=lm)
        else:
            for sq in seqs:
                sq.q, sq.k, sq.v = qm_ref[sq.bi, rows, :], km_ref[sq.bi, rows, :], vm_ref[sq.bi, rows, :]
                sq.gif = gif_ref[sq.bi, rows, :]
            _mlstm_chunk(seqs, lm=lm)
        return carry

    def gla_records(rows):
        seqs = [seq_records(bi, None, rows) for bi in range(bt)]
        for sq in seqs:
            sq.qg, sq.kg, sq.vg = qg_ref[sq.bi, rows, :], kg_ref[sq.bi, rows, :], vg_ref[sq.bi, rows, :]
            sq.a = la_ref[sq.bi, rows, :]
        return seqs

    def g_body(ci, carry):
        rows = pl.ds(pl.multiple_of(ci * lg, lg), lg)
        _gla_chunk_guarded(gla_records(rows), ggh_ref, lg=lg, sub=sub)
        return carry

    def loop(body, n):
        if n == 1:
            body(0, 0)
        else:
            lax.fori_loop(0, n, body, 0)

    loop(m_body, tb // lm)
    if lg <= sub:
        loop(g_body, tb // lg)
    else:
        chunk_decay = [la_ref[:, c * lg + lg - 1:(c + 1) * lg, :] for c in range(tb // lg)]
        span_ok = jnp.min(functools.reduce(jnp.minimum, chunk_decay)) >= -GLA_SAFE_SPAN

        @pl.when(span_ok)
        def _():
            _gla_block_direct(gla_records(slice(0, tb)), ggh_ref, tb=tb, lg=lg)

        @pl.when(jnp.logical_not(span_ok))
        def _():
            loop(g_body, tb // lg)

    if not streamed:
        @pl.when(t == nt - 1)
        def _():
            for out, ref in zip(state_out, scratch):
                out[...] = ref[...]


def _scan(qm, km, vm, gif, qg, kg, la, vg, ggh, state, *, bt, tb, lm, lg, sub, time_minor):
    b, t, _ = km.shape
    assert b % bt == 0 and t % tb == 0 and tb % lm == 0 and tb % lg == 0 and lg % sub == 0
    has_state = state is not None
    assert not (time_minor and has_state) and (not time_minor or (lm % LANES == 0 and tb == lm))
    tok = lambda w: pl.BlockSpec((bt, tb, w), lambda i, j: (i, j, 0))
    tok_t = lambda w: pl.BlockSpec((bt, 1, w, tb), lambda i, j: (i, j, 0, 0))
    c_shape = (M_HEADS, M_HEAD_DIM, M_HEAD_DIM)
    n_shape = (M_HEADS, 1, M_HEAD_DIM)
    m_shape = (SUBLANES, LANES) if time_minor else (1, LANES)
    s_shape = (G_HEADS, G_DK, G_DV)
    st_shapes = [(bt,) + shp for shp in (c_shape, n_shape, m_shape, s_shape)]
    st_specs = [pl.BlockSpec(shp, lambda i, j, nd=len(shp): (i,) + (0,) * (nd - 1)) for shp in st_shapes]
    single_step = t == tb
    carried = not (has_state and single_step)
    if time_minor:
        qv_spec, gif_spec = tok_t(M_WIDTH), tok_t(2 * SUBLANES)
        hm_spec, hm_shape = tok_t(M_WIDTH), (b, t // tb, M_WIDTH, tb)
    else:
        qv_spec, gif_spec = tok(M_WIDTH), tok(LANES)
        hm_spec, hm_shape = tok(M_WIDTH), (b, t, M_WIDTH)
    in_specs = [qv_spec, tok(M_WIDTH), qv_spec, gif_spec, tok(G_KW), tok(G_KW),
                tok(G_KW), tok(G_VW), _const_spec((1, G_VW))]
    args = [qm, km, vm, gif, qg, kg, la, vg, ggh]
    if has_state:
        in_specs += st_specs
        args += list(state)
    return pl.pallas_call(
        functools.partial(_scan_kernel, bt=bt, tb=tb, lm=lm, lg=lg, sub=sub, has_state=has_state,
                          time_minor=time_minor, single_step=single_step),
        grid=(b // bt, t // tb),
        in_specs=in_specs,
        out_specs=[hm_spec, tok(G_VW)] + st_specs,
        out_shape=[jax.ShapeDtypeStruct(hm_shape, F32),
                   jax.ShapeDtypeStruct((b, t, G_VW), F32)]
                  + [jax.ShapeDtypeStruct((b,) + shp[1:], F32) for shp in st_shapes],
        scratch_shapes=[pltpu.VMEM(shp, F32) for shp in st_shapes] if carried else [],
        compiler_params=pltpu.CompilerParams(
            dimension_semantics=("parallel", "arbitrary"), vmem_limit_bytes=VMEM_LIMIT),
        name="mixer_scan_state" if has_state else "mixer_scan",
    )(*args)


POST_O, POST_R, POST_A, POST_B = 0, 1024, 2048, 3072


def _post_kernel(x_ref, hm_ref, og_ref, g_ref, gmh_ref, w_ref, wpa_ref, wpb_ref, wo_ref, o_ref, *,
                 time_minor):
    x = x_ref[...]
    xn = _rms(x, g_ref[...]).astype(BF16)
    gate = lambda off: _dot_nt(xn, w_ref[off:off + D_MODEL, :])
    if time_minor:
        hm = jnp.concatenate(
            [jnp.concatenate([jnp.transpose(hm_ref[0, j, h * M_HEAD_DIM:(h + 1) * M_HEAD_DIM, :])
                              for h in range(M_HEADS)], axis=1)
             for j in range(hm_ref.shape[1])], axis=0)
    else:
        hm = hm_ref[...]
    hm = (_sigmoid(gate(POST_O)) * (hm * gmh_ref[...])).astype(BF16)
    og = (_silu(gate(POST_R)) * og_ref[...]).astype(BF16)
    y = (_sigmoid(gate(POST_A)) * _dot(hm, wpa_ref[...])
         + _sigmoid(gate(POST_B)) * _dot(og, wpb_ref[...]))
    o_ref[...] = x + _dot(y.astype(BF16), wo_ref[...])


def _post(x2d, hm, og2d, g, gmh, w_post, wpa, wpb, wo, *, tm, time_minor):
    n = x2d.shape[0]
    assert n % tm == 0
    tok = pl.BlockSpec((tm, D_MODEL), lambda i: (i, 0))
    if time_minor:
        nblk, blk = hm.shape[1], hm.shape[3]
        assert tm % blk == 0 and (nblk * blk) % tm == 0
        per_seq = nblk * blk // tm
        hm_spec = pl.BlockSpec((1, tm // blk, M_WIDTH, blk), lambda i: (i // per_seq, i % per_seq, 0, 0))
    else:
        hm_spec = tok
    return pl.pallas_call(
        functools.partial(_post_kernel, time_minor=time_minor),
        grid=(n // tm,),
        in_specs=[tok, hm_spec, tok, _const_spec((1, D_MODEL)), _const_spec((1, M_WIDTH)),
                  _const_spec((4 * D_MODEL, D_MODEL)),
                  _const_spec((M_WIDTH, D_MODEL)), _const_spec((G_VW, D_MODEL)),
                  _const_spec((D_MODEL, D_MODEL))],
        out_specs=tok,
        out_shape=jax.ShapeDtypeStruct((n, D_MODEL), F32),
        compiler_params=pltpu.CompilerParams(
            dimension_semantics=("parallel",), vmem_limit_bytes=VMEM_LIMIT),
        name="mixer_post",
    )(x2d, hm, og2d, g, gmh, w_post, wpa, wpb, wo)


def _pack_weights(g_ffn1, w_ffn1_up, w_ffn1_down, g_mix, w_in, conv_w, conv_b, w_mq, w_mk, b_if,
                  g_mhead, w_a2, b_a, g_ghead, w_pa, w_pb, w_o, g_ffn2, w_ffn2_up, w_ffn2_down,
                  g_final):
    l = 0
    row = lambda v: v.reshape(1, -1).astype(F32)
    edges = [0]
    for wd in (M_WIDTH, M_WIDTH, M_WIDTH, 2 * M_HEADS, G_KW, G_KW, G_VW, G_VW, G_RANK, D_MODEL, D_MODEL):
        edges.append(edges[-1] + wd)
    w_in_t = jnp.swapaxes(w_in[l], 0, 1)
    seg = lambda i: w_in_t[edges[i]:edges[i + 1], :]
    u_m, v_m, o_m, if_m, q_g, k_g, v_g, r_g, a_g, g_a, g_b = (seg(i) for i in range(11))
    w_pre = jnp.concatenate([u_m, v_m, q_g, k_g, v_g], axis=0).astype(BF16)
    w_small = jnp.concatenate(
        [if_m, a_g, jnp.zeros((LANES - 2 * M_HEADS - G_RANK, D_MODEL), F32)], axis=0).astype(BF16)
    w_post = jnp.concatenate([o_m, r_g, g_a, g_b], axis=0).astype(BF16)
    bif = jnp.concatenate([b_if[l].reshape(-1), jnp.zeros((LANES - 2 * M_HEADS,), F32)]).reshape(1, LANES)
    wa2 = jnp.zeros((LANES, G_KW), F32).at[A_LANE0:A_LANE0 + G_RANK].set(w_a2[l]).astype(BF16)
    return dict(
        g_ffn1=row(g_ffn1[l]), w_ffn1_up=w_ffn1_up[l].astype(BF16), w_ffn1_down=w_ffn1_down[l].astype(BF16),
        g_mix=row(g_mix[l]), w_pre=w_pre, w_small=w_small, w_post=w_post,
        conv_w=conv_w[l].astype(F32), conv_b=row(conv_b[l]),
        w_mq=w_mq[l].astype(BF16), w_mk=w_mk[l].astype(BF16), bif=bif,
        g_mhead=row(g_mhead[l]), wa2=wa2, b_a=row(b_a[l]), g_ghead=row(g_ghead[l]),
        w_pa=w_pa[l].astype(BF16), w_pb=w_pb[l].astype(BF16), w_o=w_o[l].astype(BF16),
        g_ffn2=row(g_ffn2[l]), w_ffn2_up=w_ffn2_up[l].astype(BF16), w_ffn2_down=w_ffn2_down[l].astype(BF16),
        g_final=row(g_final))


def _mixer(x1, b, t, state, w, *, tm, pre_tile, scan_tile, lm, lg, sub, time_minor):
    n = b * t
    if state is None:
        cst = jnp.zeros((b, SUBLANES, M_WIDTH), F32)
        scan_state = None
    else:
        conv0, c0, n0, m0, s0 = state
        cst = jnp.concatenate(
            [jnp.zeros((b, SUBLANES - (CONV_W - 1), M_WIDTH), F32), conv0.astype(F32)], axis=1)
        m0p = jnp.concatenate([m0, jnp.zeros((b, LANES - M_HEADS), F32)], axis=1).reshape(b, 1, LANES)
        scan_state = (c0, n0.reshape(b, M_HEADS, 1, M_HEAD_DIM), m0p, s0)
    qm, km, vm, gif, qg, kg, la, vg, cout = _pre(
        x1.reshape(b, t, D_MODEL), cst, w["g_mix"], w["w_pre"], w["w_small"], w["conv_w"], w["conv_b"],
        w["w_mq"], w["w_mk"], w["bif"], w["wa2"], w["b_a"], bt=pre_tile[0], tb=pre_tile[1], sub=lm, lg=lg,
        time_minor=time_minor)
    hm, og, c_new, n_new, m_new, s_new = _scan(
        qm, km, vm, gif, qg, kg, la, vg, w["g_ghead"], scan_state,
        bt=scan_tile[0], tb=scan_tile[1], lm=lm, lg=lg, sub=sub, time_minor=time_minor)
    x2 = _post(x1, hm if time_minor else hm.reshape(n, M_WIDTH), og.reshape(n, G_VW), w["g_mix"],
               w["g_mhead"], w["w_post"], w["w_pa"], w["w_pb"], w["w_o"], tm=tm, time_minor=time_minor)
    m_heads = m_new[:, :M_HEADS, 0] if time_minor else m_new[:, 0, :M_HEADS]
    return (x2,
            cout[:, SUBLANES - (CONV_W - 1):, :][None],
            c_new[None],
            n_new.reshape(b, M_HEADS, M_HEAD_DIM)[None],
            m_heads[None],
            s_new[None])


FFN_ROWS = 1024
DECODE_SEQS_PER_STEP = 4
PREFILL_SEQS_PER_STEP = 2


def _tiles(b, t):
    n = b * t
    tm = math.gcd(n, 512)
    if t % 256 == 0:
        return dict(tm=math.gcd(t, 512), pre_tile=(1, math.gcd(t, 512)), scan_tile=(math.gcd(b, PREFILL_SEQS_PER_STEP), 256),
                    lm=256, lg=128, sub=16, time_minor=True)
    assert t <= 64 and t % SUBLANES == 0
    bt = max(1, min(b, 512 // t))
    while b % bt:
        bt -= 1
    return dict(tm=tm, pre_tile=(bt, t), scan_tile=(math.gcd(b, DECODE_SEQS_PER_STEP), t), lm=t, lg=t,
                sub=min(16, t), time_minor=False)


def kernel(x_prompt, x_sample, state_conv, state_mlstm_C, state_mlstm_n, state_mlstm_m, state_gla_S,
           g_ffn1, w_ffn1_up, w_ffn1_down, g_mix, w_in, conv_w, conv_b, w_mq, w_mk, b_if, g_mhead,
           w_a2, b_a, g_ghead, w_pa, w_pb, w_o, g_ffn2, w_ffn2_up, w_ffn2_down, g_final):
    w = _pack_weights(g_ffn1, w_ffn1_up, w_ffn1_down, g_mix, w_in, conv_w, conv_b, w_mq, w_mk, b_if,
                      g_mhead, w_a2, b_a, g_ghead, w_pa, w_pb, w_o, g_ffn2, w_ffn2_up, w_ffn2_down,
                      g_final)
    bp, tp, _ = x_prompt.shape
    bs, ts, _ = x_sample.shape
    tm_ffn = math.gcd(math.gcd(bp * tp, bs * ts), FFN_ROWS)
    x1_p, x1_s = _ffn(x_prompt.reshape(bp * tp, D_MODEL), x_sample.reshape(bs * ts, D_MODEL),
                      w["g_ffn1"], w["w_ffn1_up"], w["w_ffn1_down"], w["g_final"], final_norm=False, tm=tm_ffn)
    out_p = _mixer(x1_p, bp, tp, None, w, **_tiles(bp, tp))
    state = (state_conv[0], state_mlstm_C[0], state_mlstm_n[0], state_mlstm_m[0], state_gla_S[0])
    out_s = _mixer(x1_s, bs, ts, state, w, **_tiles(bs, ts))
    y_p, y_s = _ffn(out_p[0], out_s[0], w["g_ffn2"], w["w_ffn2_up"], w["w_ffn2_down"], w["g_final"],
                    final_norm=True, tm=tm_ffn)
    return (y_p.reshape(bp, tp, D_MODEL), y_s.reshape(bs, ts, D_MODEL)) + out_p[1:] + out_s[1:]
```

```python
import functools
import math
import types

import jax
import jax.numpy as jnp
from jax import lax
from jax.experimental import pallas as pl
from jax.experimental.pallas import tpu as pltpu

D_MODEL = 1024
M_HEADS = 4
M_HEAD_DIM = 256
M_WIDTH = 1024
CONV_W = 4
G_HEADS = 4
G_DK = 128
G_DV = 256
G_KW = 512
G_VW = 1024
G_RANK = 16
G_TAU = 16.0
D_FF = 2816
EPS = 1e-6

LANES = 128
SUBLANES = 8
BF16_ROWS = 16
VMEM_LIMIT = 56 * 1024 * 1024

F32 = jnp.float32
BF16 = jnp.bfloat16
NEG_INF = float("-inf")


def _sigmoid(x):
    return 1.0 / (1.0 + jnp.exp(-x))


def _silu(x):
    return x * _sigmoid(x)


def _log_sigmoid(x):
    return jnp.minimum(x, 0.0) - jnp.log(1.0 + jnp.exp(-jnp.abs(x)))


def _rms(x, g):
    return x * lax.rsqrt(jnp.mean(x * x, axis=-1, keepdims=True) + EPS) * g


def _dot(a, b):
    return jnp.dot(a, b, preferred_element_type=F32)


def _dot_nt(a, b):
    return lax.dot_general(a, b, (((1,), (1,)), ((), ())), preferred_element_type=F32)


def _dot_tn(a, b):
    return lax.dot_general(a, b, (((0,), (0,)), ((), ())), preferred_element_type=F32)


def _const_spec(shape):
    nd = len(shape)
    return pl.BlockSpec(shape, lambda *_: (0,) * nd, pipeline_mode=pl.Buffered(1))


FF_CHUNK = 256


def _ffn_kernel(xa_ref, xb_ref, g_ref, wup_ref, wdown_ref, gfin_ref, oa_ref, ob_ref, h_ref, *,
                final_norm, tiles_a):
    def half_step(x_ref, o_ref):
        x = x_ref[...]
        xn = _rms(x, g_ref[...]).astype(BF16)
        for c in range(D_FF // FF_CHUNK):
            a = _dot(xn, wup_ref[:, c * FF_CHUNK:(c + 1) * FF_CHUNK])
            g = _dot(xn, wup_ref[:, D_FF + c * FF_CHUNK:D_FF + (c + 1) * FF_CHUNK])
            h_ref[:, c * FF_CHUNK:(c + 1) * FF_CHUNK] = (_silu(g) * a).astype(BF16)
        y = x + 0.5 * _dot(h_ref[...], wdown_ref[...])
        o_ref[...] = _rms(y, gfin_ref[...]) if final_norm else y

    on_a = pl.program_id(0) < tiles_a
    pl.when(on_a)(lambda: half_step(xa_ref, oa_ref))
    pl.when(jnp.logical_not(on_a))(lambda: half_step(xb_ref, ob_ref))


def _ffn(xa, xb, g, wup, wdown, gfin, *, final_norm, tm):
    na, nb = xa.shape[0], xb.shape[0]
    assert na % tm == 0 and nb % tm == 0
    tiles_a, tiles_b = na // tm, nb // tm
    spec_a = pl.BlockSpec((tm, D_MODEL), lambda i: (jnp.minimum(i, tiles_a - 1), 0))
    spec_b = pl.BlockSpec((tm, D_MODEL), lambda i: (jnp.maximum(i - tiles_a, 0), 0),
                          **(dict(pipeline_mode=pl.Buffered(1)) if tiles_b == 1 else {}))
    return pl.pallas_call(
        functools.partial(_ffn_kernel, final_norm=final_norm, tiles_a=tiles_a),
        grid=(tiles_a + tiles_b,),
        in_specs=[
            spec_a, spec_b,
            _const_spec((1, D_MODEL)),
            _const_spec((D_MODEL, 2 * D_FF)),
            _const_spec((D_FF, D_MODEL)),
            _const_spec((1, D_MODEL)),
        ],
        out_specs=[spec_a, spec_b],
        out_shape=[jax.ShapeDtypeStruct((na, D_MODEL), F32), jax.ShapeDtypeStruct((nb, D_MODEL), F32)],
        scratch_shapes=[pltpu.VMEM((tm, D_FF), BF16)],
        compiler_params=pltpu.CompilerParams(
            dimension_semantics=("arbitrary",), vmem_limit_bytes=VMEM_LIMIT),
        name="ffn_final" if final_norm else "ffn",
    )(xa, xb, g, wup, wdown, gfin)


PRE_U, PRE_V, PRE_QG, PRE_KG, PRE_VG = 0, 1024, 2048, 2560, 3072
PRE_WIDTH = 4096
A_LANE0 = 8


def _pre_kernel(x_ref, cst_ref, g_ref, w_ref, ws_ref, cw_ref, cb_ref, wq_ref, wk_ref, bif_ref,
                wa2_ref, ba_ref,
                qm_ref, km_ref, vm_ref, gif_ref, qg_ref, kg_ref, la_ref, vg_ref, cout_ref,
                cbuf_ref, *, bt, tb, sub, lg, time_minor):
    t = pl.program_id(1)
    m = bt * sub

    @pl.when(t == 0)
    def _():
        cbuf_ref[:, 0:SUBLANES, :] = cst_ref[...]

    @pl.when(t > 0)
    def _():
        cbuf_ref[:, 0:SUBLANES, :] = cbuf_ref[:, tb:tb + SUBLANES, :]

    for j in range(tb // sub):
        r0 = j * sub
        rows = slice(r0, r0 + sub)
        to_rows = lambda y, w: y.reshape(bt, sub, w)
        xn = _rms(x_ref[:, rows, :].reshape(m, D_MODEL), g_ref[...]).astype(BF16)
        small = _dot(xn, ws_ref[...])

        u = _dot(xn, w_ref[:, PRE_U:PRE_U + M_WIDTH])
        cbuf_ref[:, SUBLANES + r0:SUBLANES + r0 + sub, :] = to_rows(u, M_WIDTH)
        c = cb_ref[...].reshape(1, 1, M_WIDTH)
        for tap in range(CONV_W):
            off = SUBLANES - (CONV_W - 1) + tap + r0
            c = c + cbuf_ref[:, off:off + sub, :] * cw_ref[tap:tap + 1, :].reshape(1, 1, M_WIDTH)
        ch = _silu(c).reshape(m, M_WIDTH).astype(BF16)

        vm = _dot(xn, w_ref[:, PRE_V:PRE_V + M_WIDTH])
        gates = small + bif_ref[...]
        lane = lax.broadcasted_iota(jnp.int32, gates.shape, 1)
        gif = jnp.where(lane < M_HEADS, gates, jnp.where(lane < 2 * M_HEADS, _log_sigmoid(gates), 0.0))
        if time_minor:
            b = pltpu.roll(_scan_rows(gif, jnp.add, 0.0), LANES - M_HEADS, axis=1)
            g = gif - b
            gmax = _scan_rows(g, jnp.maximum, NEG_INF)
            packed = jnp.where(lane < M_HEADS, g,
                               jnp.where(lane < 2 * M_HEADS, pltpu.roll(gmax, M_HEADS, axis=1),
                                         jnp.where(lane < 3 * M_HEADS, pltpu.roll(b, 2 * M_HEADS, axis=1), 0.0)))
            gif_ref[0, j] = jnp.transpose(packed)[0:2 * SUBLANES, :]
        else:
            gif_ref[:, rows, :] = to_rows(gif, LANES)
            vm_ref[:, rows, :] = to_rows(vm.astype(BF16), M_WIDTH)

        def decay_columns(h):
            cols = slice(h * G_DK, (h + 1) * G_DK)
            a2 = _dot(small.astype(BF16), wa2_ref[:, cols]) + ba_ref[:, cols]
            la_ref[:, rows, cols] = to_rows(
                _scan_rows(_log_sigmoid(a2) * (1.0 / G_TAU), jnp.add, 0.0, segment=lg), G_DK)

        qg_ref[:, rows, :] = to_rows(
            (_dot(xn, w_ref[:, PRE_QG:PRE_QG + G_KW]) * (G_DK ** -0.5)).astype(BF16), G_KW)
        decay_columns(0)
        kg_ref[:, rows, :] = to_rows(_dot(xn, w_ref[:, PRE_KG:PRE_KG + G_KW]).astype(BF16), G_KW)
        decay_columns(1)
        half = G_VW // 2
        for p in range(2):
            vg_ref[:, rows, p * half:(p + 1) * half] = to_rows(
                _dot(xn, w_ref[:, PRE_VG + p * half:PRE_VG + (p + 1) * half]).astype(BF16), half)
            decay_columns(2 + p)

        for h in range(M_HEADS):
            sl = slice(h * M_HEAD_DIM, (h + 1) * M_HEAD_DIM)
            qh = _dot(ch[:, sl], wq_ref[h])
            if time_minor:
                qm_ref[0, j, sl, :] = jnp.transpose(qh).astype(BF16)
                vm_ref[0, j, sl, :] = jnp.transpose(vm[:, sl]).astype(BF16)
            else:
                qm_ref[:, rows, sl] = to_rows(qh.astype(BF16), M_HEAD_DIM)
            km_ref[:, rows, sl] = to_rows(
                (_dot(ch[:, sl], wk_ref[h]) * (M_HEAD_DIM ** -0.5)).astype(BF16), M_HEAD_DIM)
    cout_ref[...] = cbuf_ref[:, tb:tb + SUBLANES, :]


def _pre(x, cst, g, w_pre, w_small, conv_w, conv_b, wq, wk, bif, wa2, ba, *, bt, tb, sub, lg, time_minor):
    b, t, _ = x.shape
    assert b % bt == 0 and t % tb == 0 and tb % sub == 0 and sub % SUBLANES == 0 and sub % lg == 0
    assert not time_minor or (bt == 1 and sub % LANES == 0)
    tok = lambda w: pl.BlockSpec((bt, tb, w), lambda i, j: (i, j, 0))
    per_seq = pl.BlockSpec((bt, SUBLANES, M_WIDTH), lambda i, j: (i, 0, 0))
    sds = lambda w, dt: jax.ShapeDtypeStruct((b, t, w), dt)
    if time_minor:
        tok_t = lambda w: pl.BlockSpec((1, tb // sub, w, sub), lambda i, j: (i, j, 0, 0))
        sds_t = lambda w, dt: jax.ShapeDtypeStruct((b, t // sub, w, sub), dt)
        qv_spec, qv_shape = tok_t(M_WIDTH), sds_t(M_WIDTH, BF16)
        gif_spec, gif_shape = tok_t(2 * SUBLANES), sds_t(2 * SUBLANES, F32)
    else:
        qv_spec, qv_shape = tok(M_WIDTH), sds(M_WIDTH, BF16)
        gif_spec, gif_shape = tok(LANES), sds(LANES, F32)
    return pl.pallas_call(
        functools.partial(_pre_kernel, bt=bt, tb=tb, sub=sub, lg=lg, time_minor=time_minor),
        grid=(b // bt, t // tb),
        in_specs=[
            tok(D_MODEL), per_seq,
            _const_spec((1, D_MODEL)),
            _const_spec((D_MODEL, PRE_WIDTH)),
            _const_spec((D_MODEL, LANES)),
            _const_spec((CONV_W, M_WIDTH)),
            _const_spec((1, M_WIDTH)),
            _const_spec((M_HEADS, M_HEAD_DIM, M_HEAD_DIM)),
            _const_spec((M_HEADS, M_HEAD_DIM, M_HEAD_DIM)),
            _const_spec((1, LANES)),
            _const_spec((LANES, G_KW)),
            _const_spec((1, G_KW)),
        ],
        out_specs=[qv_spec, tok(M_WIDTH), qv_spec, gif_spec, tok(G_KW), tok(G_KW),
                   tok(G_KW), tok(G_VW), per_seq],
        out_shape=[qv_shape, sds(M_WIDTH, BF16), qv_shape, gif_shape,
                   sds(G_KW, BF16), sds(G_KW, BF16), sds(G_KW, F32), sds(G_VW, BF16),
                   jax.ShapeDtypeStruct((b, SUBLANES, M_WIDTH), F32)],
        scratch_shapes=[pltpu.VMEM((bt, tb + SUBLANES, M_WIDTH), F32)],
        compiler_params=pltpu.CompilerParams(
            dimension_semantics=("parallel", "arbitrary"), vmem_limit_bytes=VMEM_LIMIT),
        name="mixer_pre",
    )(x, cst, g, w_pre, w_small, conv_w, conv_b, wq, wk, bif, wa2, ba)


def _scan_rows(x, op, fill, segment=None):
    n = segment or x.shape[0]
    row = lax.broadcasted_iota(jnp.int32, x.shape, 0) % n
    s = 1
    while s < n:
        x = op(x, jnp.where(row >= s, pltpu.roll(x, s, axis=0), fill))
        s *= 2
    return x


def _col_of_row(row):
    return jnp.transpose(jnp.broadcast_to(row, (SUBLANES, LANES)))[:, 0:1]


def _mlstm_chunk_tm(seqs, *, lm):
    reps = lm // LANES
    head_row = lax.broadcasted_iota(jnp.int32, (SUBLANES, LANES), 0) < M_HEADS
    src = lax.broadcasted_iota(jnp.int32, (lm, lm), 0)
    dst = lax.broadcasted_iota(jnp.int32, (lm, lm), 1)
    causal = src <= dst
    sl = [slice(h * M_HEAD_DIM, (h + 1) * M_HEAD_DIM) for h in range(M_HEADS)]
    row = lambda x, h: x[h:h + 1, :]
    gates = []
    for sq in seqs:
        g = sq.gift[0:SUBLANES]
        gmax = pltpu.roll(g, M_HEADS, axis=0)
        b = sq.gift[SUBLANES:2 * SUBLANES]
        m_prev = jnp.concatenate([sq.m_src[...]] * reps, axis=1)
        mx = jnp.maximum(m_prev, gmax)
        mx_last = jnp.broadcast_to(mx[:, lm - 1:lm], (SUBLANES, lm))
        gates.append(dict(
            mx=mx,
            inter=jnp.exp(m_prev - mx),
            nfloor=jnp.exp(-(b + mx)),
            w_in=jnp.exp(g - mx_last),
            decay=jnp.exp(m_prev - mx_last),
            g_col=jnp.transpose(g)))
        m_new = jnp.broadcast_to(b[:, lm - 1:lm], (SUBLANES, lm)) + mx_last
        sq.m_dst[...] = jnp.where(head_row, m_new[:, :LANES], 0.0)
    units = [(sq, gt, h) for sq, gt in zip(seqs, gates) for h in range(M_HEADS)]
    c_old = [sq.c_src[h] for sq, _, h in units]
    n_old = [sq.n_src[h] for sq, _, h in units]
    cn = [jnp.concatenate([c_old[u].astype(BF16),
                           jnp.broadcast_to(n_old[u].astype(BF16), (BF16_ROWS, M_HEAD_DIM))], axis=0)
          for u in range(len(units))]
    cq = [_dot(cn[u], sq.qt[sl[h], :]) for u, (sq, _, h) in enumerate(units)]
    dm = [jnp.exp(jnp.where(causal, gt["g_col"][:, h:h + 1] - row(gt["mx"], h), NEG_INF))
          for _, gt, h in units]
    st = [_dot(sq.k[:, sl[h]], sq.qt[sl[h], :]) * dm[u]
          for u, (sq, _, h) in enumerate(units)]
    num = [row(gt["inter"], h) * cq[u][:M_HEAD_DIM] + _dot(sq.vt[sl[h], :], st[u].astype(BF16))
           for u, (sq, gt, h) in enumerate(units)]
    for u, (sq, gt, h) in enumerate(units):
        den = (row(gt["inter"], h) * cq[u][M_HEAD_DIM:M_HEAD_DIM + 1]
               + jnp.sum(st[u], axis=0, keepdims=True))
        rcp = 1.0 / jnp.maximum(jnp.abs(den), row(gt["nfloor"], h))
        ms = jnp.mean(num[u] * num[u], axis=0, keepdims=True)
        sq.hm_store(h, num[u] * (rcp * lax.rsqrt(rcp * rcp * ms + EPS)))
    for u, (sq, gt, h) in enumerate(units):
        w_h = row(gt["w_in"], h)
        dec_h = jnp.concatenate([row(gt["decay"], h)[:, :LANES]] * (M_HEAD_DIM // LANES), axis=1)
        wv = jnp.concatenate([(sq.vt[sl[h], :].astype(F32) * w_h).astype(BF16),
                              jnp.broadcast_to(w_h.astype(BF16), (BF16_ROWS, lm))], axis=0)
        upd = _dot(wv, sq.k[:, sl[h]])
        sq.c_dst[h] = dec_h * c_old[u] + upd[:M_HEAD_DIM]
        sq.n_dst[h] = dec_h * n_old[u] + upd[M_HEAD_DIM:M_HEAD_DIM + 1]


def _mlstm_chunk(seqs, *, lm):
    row = lax.broadcasted_iota(jnp.int32, (lm, lm), 0)
    col = lax.broadcasted_iota(jnp.int32, (lm, lm), 1)
    causal = col <= row
    sl = [slice(h * M_HEAD_DIM, (h + 1) * M_HEAD_DIM) for h in range(M_HEADS)]
    gates = []
    for sq in seqs:
        cs = _scan_rows(sq.gif, jnp.add, 0.0)
        b = pltpu.roll(cs, LANES - M_HEADS, axis=1)
        g = sq.gif - b
        m_prev = sq.m_src[...]
        mx = jnp.maximum(m_prev, _scan_rows(g, jnp.maximum, NEG_INF))
        mx_last = mx[lm - 1:lm, :]
        gates.append(dict(
            mx=mx,
            inter=jnp.exp(m_prev - mx),
            nfloor=jnp.exp(-(b + mx)),
            g_t=jnp.transpose(g),
            w_in=jnp.exp(g - mx_last),
            decay=jnp.exp(m_prev - mx_last)))
        sq.m_dst[...] = b[lm - 1:lm, :] + mx_last
    units = [(sq, gt, h) for sq, gt in zip(seqs, gates) for h in range(M_HEADS)]
    col_of = lambda x, h: x[:, h:h + 1]
    c_old = [sq.c_src[h] for sq, _, h in units]
    n_old = [sq.n_src[h] for sq, _, h in units]
    dm = [jnp.exp(jnp.where(causal, gt["g_t"][h:h + 1, :] - col_of(gt["mx"], h), NEG_INF))
          for _, gt, h in units]
    qn = [jnp.sum(sq.q[:, sl[h]].astype(F32) * n_old[u], axis=-1, keepdims=True)
          for u, (sq, _, h) in enumerate(units)]
    s = [_dot_nt(sq.q[:, sl[h]], sq.k[:, sl[h]]) * dm[u] for u, (sq, _, h) in enumerate(units)]
    num = [col_of(gt["inter"], h) * _dot_nt(sq.q[:, sl[h]], c_old[u].astype(BF16))
           + _dot(s[u].astype(BF16), sq.v[:, sl[h]]) for u, (sq, gt, h) in enumerate(units)]
    for u, (sq, gt, h) in enumerate(units):
        den = col_of(gt["inter"], h) * qn[u] + jnp.sum(s[u], axis=-1, keepdims=True)
        rcp = 1.0 / jnp.maximum(jnp.abs(den), col_of(gt["nfloor"], h))
        ms = jnp.mean(num[u] * num[u], axis=-1, keepdims=True)
        sq.hm_store(h, num[u] * (rcp * lax.rsqrt(rcp * rcp * ms + EPS)))
    for u, (sq, gt, h) in enumerate(units):
        w_h = col_of(gt["w_in"], h)
        dec_h = col_of(gt["decay"], h)
        wv = (w_h * sq.v[:, sl[h]].astype(F32)).astype(BF16)
        sq.c_dst[h] = dec_h * c_old[u] + _dot_tn(wv, sq.k[:, sl[h]])
        sq.n_dst[h] = dec_h * n_old[u] + jnp.sum(w_h * sq.k[:, sl[h]].astype(F32), axis=0, keepdims=True)


GLA_SAFE_SPAN = 40.0


def _gla_scores_guarded(qh, kh, a, *, lg, sub):
    row = lax.broadcasted_iota(jnp.int32, (lg, lg), 0)
    col = lax.broadcasted_iota(jnp.int32, (lg, lg), 1)
    att = jnp.zeros((lg, lg), F32)
    c = sub
    while c < lg:
        nblk = lg // (2 * c)
        a_ref = jnp.concatenate(
            [jnp.broadcast_to(a[2 * c * i + c - 1:2 * c * i + c, :], (2 * c, G_DK))
             for i in range(nblk)], axis=0)
        qt = (qh * jnp.exp(jnp.minimum(a - a_ref, 0.0))).astype(BF16)
        kt = (kh * jnp.exp(jnp.minimum(a_ref - a, 0.0))).astype(BF16)
        valid = (row // (2 * c) == col // (2 * c)) & (row % (2 * c) >= c) & (col % (2 * c) < c)
        att = att + jnp.where(valid, _dot_nt(qt, kt), 0.0)
        c *= 2
    rows = lax.broadcasted_iota(jnp.int32, (sub, lg), 0)
    cols = lax.broadcasted_iota(jnp.int32, (sub, lg), 1)
    blocks = []
    for blk in range(lg // sub):
        r0 = blk * sub
        a_b = a[r0:r0 + sub, :]
        q_b = qh[r0:r0 + sub, :]
        acc = jnp.zeros((sub, lg), F32)
        for s in range(sub):
            e = jnp.exp(jnp.minimum(a_b - a[r0 + s:r0 + s + 1, :], 0.0))
            p = jnp.sum(q_b * e * kh[r0 + s:r0 + s + 1, :], axis=-1, keepdims=True)
            acc = jnp.where((cols == r0 + s) & (rows >= s), p, acc)
        blocks.append(acc)
    return att + jnp.concatenate(blocks, axis=0)


def _gla_chunk_guarded(seqs, gh_ref, *, lg, sub):
    ks = [slice(h * G_DK, (h + 1) * G_DK) for h in range(G_HEADS)]
    vs = [slice(h * G_DV, (h + 1) * G_DV) for h in range(G_HEADS)]
    pre = []
    for sq in seqs:
        a_last = sq.a[lg - 1:lg, :]
        qf = sq.qg.astype(F32)
        kf = sq.kg.astype(F32)
        pre.append(dict(a=sq.a, a_last=a_last, qf=qf, kf=kf,
                        qs=(qf * jnp.exp(sq.a)).astype(BF16),
                        kl=(kf * jnp.exp(a_last - sq.a)).astype(BF16)))
    units = [(sq, p, h) for sq, p in zip(seqs, pre) for h in range(G_HEADS)]
    s_old = [sq.s_src[h] for sq, _, h in units]
    att = [_gla_scores_guarded(p["qf"][:, ks[h]], p["kf"][:, ks[h]], p["a"][:, ks[h]], lg=lg, sub=sub)
           for _, p, h in units]
    o = [_dot(p["qs"][:, ks[h]], s_old[u].astype(BF16)) + _dot(att[u].astype(BF16), sq.vg[:, vs[h]])
         for u, (sq, p, h) in enumerate(units)]
    for u, (sq, _, h) in enumerate(units):
        sq.og_store(h, _rms(o[u], gh_ref[:, vs[h]]))
    for u, (sq, p, h) in enumerate(units):
        sq.s_dst[h] = (jnp.exp(_col_of_row(p["a_last"][:, ks[h]])) * s_old[u]
                       + _dot_tn(p["kl"][:, ks[h]], sq.vg[:, vs[h]]))


def _gla_block_direct(seqs, gh_ref, *, tb, lg):
    ks = [slice(h * G_DK, (h + 1) * G_DK) for h in range(G_HEADS)]
    vs = [slice(h * G_DV, (h + 1) * G_DV) for h in range(G_HEADS)]
    chunks = [slice(c * lg, (c + 1) * lg) for c in range(tb // lg)]
    row = lax.broadcasted_iota(jnp.int32, (lg, lg), 0)
    col = lax.broadcasted_iota(jnp.int32, (lg, lg), 1)
    causal = col <= row
    pre = []
    for sq in seqs:
        a_last = [sq.a[r.stop - 1:r.stop, :] for r in chunks]
        a_last_rows = jnp.concatenate([jnp.broadcast_to(al, (lg, G_KW)) for al in a_last], axis=0)
        qf = sq.qg.astype(F32)
        kf = sq.kg.astype(F32)
        pre.append(dict(a_last=a_last,
                        qs=(qf * jnp.exp(sq.a)).astype(BF16),
                        kt=(kf * jnp.exp(-sq.a)).astype(BF16),
                        kl=(kf * jnp.exp(a_last_rows - sq.a)).astype(BF16)))
    units = [(sq, p, h) for sq, p in zip(seqs, pre) for h in range(G_HEADS)]
    att = [[jnp.where(causal, _dot_nt(p["qs"][r, ks[h]], p["kt"][r, ks[h]]), 0.0).astype(BF16)
            for r in chunks] for _, p, h in units]
    s_cur = [sq.s_src[h] for sq, _, h in units]
    for c, r in enumerate(chunks):
        o = [_dot(p["qs"][r, ks[h]], s_cur[u].astype(BF16)) + _dot(att[u][c], sq.vg[r, vs[h]])
             for u, (sq, p, h) in enumerate(units)]
        for u, (sq, _, h) in enumerate(units):
            sq.og_rows_store(r, h, _rms(o[u], gh_ref[:, vs[h]]))
        s_cur = [jnp.exp(_col_of_row(p["a_last"][c][:, ks[h]])) * s_cur[u]
                 + _dot_tn(p["kl"][r, ks[h]], sq.vg[r, vs[h]])
                 for u, (sq, p, h) in enumerate(units)]
    for u, (sq, _, h) in enumerate(units):
        sq.s_dst[h] = s_cur[u]


def _scan_kernel(*refs, bt, tb, lm, lg, sub, has_state, time_minor, single_step):
    n_in = 9 + (4 if has_state else 0)
    qm_ref, km_ref, vm_ref, gif_ref, qg_ref, kg_ref, la_ref, vg_ref, ggh_ref = refs[:9]
    state_in = refs[9:n_in]
    hm_ref, og_ref = refs[n_in:n_in + 2]
    state_out = refs[n_in + 2:n_in + 6]
    scratch = refs[n_in + 6:]
    t = pl.program_id(1)
    nt = pl.num_programs(1)

    streamed = has_state and single_step
    src = state_in if streamed else scratch
    dst = state_out if streamed else scratch
    if not streamed:
        @pl.when(t == 0)
        def _():
            for i, ref in enumerate(scratch):
                ref[...] = state_in[i][...] if has_state else jnp.zeros(ref.shape, F32)

    def seq_records(bi, m_rows, g_rows):
        def hm_store(h, val):
            if time_minor:
                hm_ref[bi, 0, h * M_HEAD_DIM:(h + 1) * M_HEAD_DIM, :] = val.astype(hm_ref.dtype)
            else:
                hm_ref[bi, m_rows, h * M_HEAD_DIM:(h + 1) * M_HEAD_DIM] = val.astype(hm_ref.dtype)

        def og_rows_store(rows, h, val):
            og_ref[bi, rows, h * G_DV:(h + 1) * G_DV] = val.astype(og_ref.dtype)

        return types.SimpleNamespace(
            bi=bi, hm_store=hm_store, og_rows_store=og_rows_store,
            og_store=functools.partial(og_rows_store, g_rows),
            c_src=src[0].at[bi], n_src=src[1].at[bi], m_src=src[2].at[bi], s_src=src[3].at[bi],
            c_dst=dst[0].at[bi], n_dst=dst[1].at[bi], m_dst=dst[2].at[bi], s_dst=dst[3].at[bi])

    def m_body(ci, carry):
        rows = pl.ds(pl.multiple_of(ci * lm, lm), lm)
        seqs = [seq_records(bi, rows, None) for bi in range(bt)]
        if time_minor:
            for sq in seqs:
                sq.qt, sq.k, sq.vt = qm_ref[sq.bi, 0], km_ref[sq.bi, rows, :], vm_ref[sq.bi, 0]
                sq.gift = gif_ref[sq.bi, 0]
            _mlstm_chunk_tm(seqs, lm=lm)
        else:
            for sq in seqs:
                sq.q, sq.k, sq.v = qm_ref[sq.bi, rows, :], km_ref[sq.bi, rows, :], vm_ref[sq.bi, rows, :]
                sq.gif = gif_ref[sq.bi, rows, :]
            _mlstm_chunk(seqs, lm=lm)
        return carry

    def gla_records(rows):
        seqs = [seq_records(bi, None, rows) for bi in range(bt)]
        for sq in seqs:
            sq.qg, sq.kg, sq.vg = qg_ref[sq.bi, rows, :], kg_ref[sq.bi, rows, :], vg_ref[sq.bi, rows, :]
            sq.a = la_ref[sq.bi, rows, :]
        return seqs

    def g_body(ci, carry):
        rows = pl.ds(pl.multiple_of(ci * lg, lg), lg)
        _gla_chunk_guarded(gla_records(rows), ggh_ref, lg=lg, sub=sub)
        return carry

    def loop(body, n):
        if n == 1:
            body(0, 0)
        else:
            lax.fori_loop(0, n, body, 0)

    loop(m_body, tb // lm)
    if lg <= sub:
        loop(g_body, tb // lg)
    else:
        chunk_decay = [la_ref[:, c * lg + lg - 1:(c + 1) * lg, :] for c in range(tb // lg)]
        span_ok = jnp.min(functools.reduce(jnp.minimum, chunk_decay)) >= -GLA_SAFE_SPAN

        @pl.when(span_ok)
        def _():
            _gla_block_direct(gla_records(slice(0, tb)), ggh_ref, tb=tb, lg=lg)

        @pl.when(jnp.logical_not(span_ok))
        def _():
            loop(g_body, tb // lg)

    if not streamed:
        @pl.when(t == nt - 1)
        def _():
            for out, ref in zip(state_out, scratch):
                out[...] = ref[...]


def _scan(qm, km, vm, gif, qg, kg, la, vg, ggh, state, *, bt, tb, lm, lg, sub, time_minor):
    b, t, _ = km.shape
    assert b % bt == 0 and t % tb == 0 and tb % lm == 0 and tb % lg == 0 and lg % sub == 0
    has_state = state is not None
    assert not (time_minor and has_state) and (not time_minor or (lm % LANES == 0 and tb == lm))
    tok = lambda w: pl.BlockSpec((bt, tb, w), lambda i, j: (i, j, 0))
    tok_t = lambda w: pl.BlockSpec((bt, 1, w, tb), lambda i, j: (i, j, 0, 0))
    c_shape = (M_HEADS, M_HEAD_DIM, M_HEAD_DIM)
    n_shape = (M_HEADS, 1, M_HEAD_DIM)
    m_shape = (SUBLANES, LANES) if time_minor else (1, LANES)
    s_shape = (G_HEADS, G_DK, G_DV)
    st_shapes = [(bt,) + shp for shp in (c_shape, n_shape, m_shape, s_shape)]
    st_specs = [pl.BlockSpec(shp, lambda i, j, nd=len(shp): (i,) + (0,) * (nd - 1)) for shp in st_shapes]
    single_step = t == tb
    carried = not (has_state and single_step)
    if time_minor:
        qv_spec, gif_spec = tok_t(M_WIDTH), tok_t(2 * SUBLANES)
        hm_spec, hm_shape = tok_t(M_WIDTH), (b, t // tb, M_WIDTH, tb)
    else:
        qv_spec, gif_spec = tok(M_WIDTH), tok(LANES)
        hm_spec, hm_shape = tok(M_WIDTH), (b, t, M_WIDTH)
    in_specs = [qv_spec, tok(M_WIDTH), qv_spec, gif_spec, tok(G_KW), tok(G_KW),
                tok(G_KW), tok(G_VW), _const_spec((1, G_VW))]
    args = [qm, km, vm, gif, qg, kg, la, vg, ggh]
    if has_state:
        in_specs += st_specs
        args += list(state)
    return pl.pallas_call(
        functools.partial(_scan_kernel, bt=bt, tb=tb, lm=lm, lg=lg, sub=sub, has_state=has_state,
                          time_minor=time_minor, single_step=single_step),
        grid=(b // bt, t // tb),
        in_specs=in_specs,
        out_specs=[hm_spec, tok(G_VW)] + st_specs,
        out_shape=[jax.ShapeDtypeStruct(hm_shape, BF16),
                   jax.ShapeDtypeStruct((b, t, G_VW), BF16)]
                  + [jax.ShapeDtypeStruct((b,) + shp[1:], F32) for shp in st_shapes],
        scratch_shapes=[pltpu.VMEM(shp, F32) for shp in st_shapes] if carried else [],
        compiler_params=pltpu.CompilerParams(
            dimension_semantics=("parallel", "arbitrary"), vmem_limit_bytes=VMEM_LIMIT),
        name="mixer_scan_state" if has_state else "mixer_scan",
    )(*args)


POST_O, POST_R, POST_A, POST_B = 0, 1024, 2048, 3072


def _post_kernel(x_ref, hm_ref, og_ref, g_ref, gmh_ref, w_ref, wpa_ref, wpb_ref, wo_ref, o_ref, *,
                 time_minor):
    x = x_ref[...]
    xn = _rms(x, g_ref[...]).astype(BF16)
    gate = lambda off: _dot(xn, w_ref[:, off:off + D_MODEL])
    if time_minor:
        hm = jnp.concatenate(
            [jnp.concatenate([jnp.transpose(hm_ref[0, j, h * M_HEAD_DIM:(h + 1) * M_HEAD_DIM, :].astype(F32))
                              for h in range(M_HEADS)], axis=1)
             for j in range(hm_ref.shape[1])], axis=0)
    else:
        hm = hm_ref[...].astype(F32)
    hm = (_sigmoid(gate(POST_O)) * (hm * gmh_ref[...])).astype(BF16)
    og = (_silu(gate(POST_R)) * og_ref[...].astype(F32)).astype(BF16)
    y = (_sigmoid(gate(POST_A)) * _dot(hm, wpa_ref[...])
         + _sigmoid(gate(POST_B)) * _dot(og, wpb_ref[...]))
    o_ref[...] = x + _dot(y.astype(BF16), wo_ref[...])


def _post(x2d, hm, og2d, g, gmh, w_post, wpa, wpb, wo, *, tm, time_minor):
    n = x2d.shape[0]
    assert n % tm == 0
    tok = pl.BlockSpec((tm, D_MODEL), lambda i: (i, 0))
    if time_minor:
        nblk, blk = hm.shape[1], hm.shape[3]
        assert tm % blk == 0 and (nblk * blk) % tm == 0
        per_seq = nblk * blk // tm
        hm_spec = pl.BlockSpec((1, tm // blk, M_WIDTH, blk), lambda i: (i // per_seq, i % per_seq, 0, 0))
    else:
        hm_spec = tok
    return pl.pallas_call(
        functools.partial(_post_kernel, time_minor=time_minor),
        grid=(n // tm,),
        in_specs=[tok, hm_spec, tok, _const_spec((1, D_MODEL)), _const_spec((1, M_WIDTH)),
                  _const_spec((D_MODEL, 4 * D_MODEL)),
                  _const_spec((M_WIDTH, D_MODEL)), _const_spec((G_VW, D_MODEL)),
                  _const_spec((D_MODEL, D_MODEL))],
        out_specs=tok,
        out_shape=jax.ShapeDtypeStruct((n, D_MODEL), F32),
        compiler_params=pltpu.CompilerParams(
            dimension_semantics=("parallel",), vmem_limit_bytes=VMEM_LIMIT),
        name="mixer_post",
    )(x2d, hm, og2d, g, gmh, w_post, wpa, wpb, wo)


def _pack_weights(g_ffn1, w_ffn1_up, w_ffn1_down, g_mix, w_in, conv_w, conv_b, w_mq, w_mk, b_if,
                  g_mhead, w_a2, b_a, g_ghead, w_pa, w_pb, w_o, g_ffn2, w_ffn2_up, w_ffn2_down,
                  g_final):
    l = 0
    row = lambda v: v.reshape(1, -1).astype(F32)
    edges = [0]
    for wd in (M_WIDTH, M_WIDTH, M_WIDTH, 2 * M_HEADS, G_KW, G_KW, G_VW, G_VW, G_RANK, D_MODEL, D_MODEL):
        edges.append(edges[-1] + wd)
    seg = lambda i: w_in[l][:, edges[i]:edges[i + 1]]
    u_m, v_m, o_m, if_m, q_g, k_g, v_g, r_g, a_g, g_a, g_b = (seg(i) for i in range(11))
    w_pre = jnp.concatenate([u_m, v_m, q_g, k_g, v_g], axis=1).astype(BF16)
    w_small = jnp.concatenate(
        [if_m, a_g, jnp.zeros((D_MODEL, LANES - 2 * M_HEADS - G_RANK), F32)], axis=1).astype(BF16)
    w_post = jnp.concatenate([o_m, r_g, g_a, g_b], axis=1).astype(BF16)
    bif = jnp.concatenate([b_if[l].reshape(-1), jnp.zeros((LANES - 2 * M_HEADS,), F32)]).reshape(1, LANES)
    wa2 = jnp.zeros((LANES, G_KW), F32).at[A_LANE0:A_LANE0 + G_RANK].set(w_a2[l]).astype(BF16)
    return dict(
        g_ffn1=row(g_ffn1[l]), w_ffn1_up=w_ffn1_up[l].astype(BF16), w_ffn1_down=w_ffn1_down[l].astype(BF16),
        g_mix=row(g_mix[l]), w_pre=w_pre, w_small=w_small, w_post=w_post,
        conv_w=conv_w[l].astype(F32), conv_b=row(conv_b[l]),
        w_mq=w_mq[l].astype(BF16), w_mk=w_mk[l].astype(BF16), bif=bif,
        g_mhead=row(g_mhead[l]), wa2=wa2, b_a=row(b_a[l]), g_ghead=row(g_ghead[l]),
        w_pa=w_pa[l].astype(BF16), w_pb=w_pb[l].astype(BF16), w_o=w_o[l].astype(BF16),
        g_ffn2=row(g_ffn2[l]), w_ffn2_up=w_ffn2_up[l].astype(BF16), w_ffn2_down=w_ffn2_down[l].astype(BF16),
        g_final=row(g_final))


def _mixer(x1, b, t, state, w, *, tm, pre_tile, scan_tile, lm, lg, sub, time_minor):
    n = b * t
    if state is None:
        cst = jnp.zeros((b, SUBLANES, M_WIDTH), F32)
        scan_state = None
    else:
        conv0, c0, n0, m0, s0 = state
        cst = jnp.concatenate(
            [jnp.zeros((b, SUBLANES - (CONV_W - 1), M_WIDTH), F32), conv0.astype(F32)], axis=1)
        m0p = jnp.concatenate([m0, jnp.zeros((b, LANES - M_HEADS), F32)], axis=1).reshape(b, 1, LANES)
        scan_state = (c0, n0.reshape(b, M_HEADS, 1, M_HEAD_DIM), m0p, s0)
    qm, km, vm, gif, qg, kg, la, vg, cout = _pre(
        x1.reshape(b, t, D_MODEL), cst, w["g_mix"], w["w_pre"], w["w_small"], w["conv_w"], w["conv_b"],
        w["w_mq"], w["w_mk"], w["bif"], w["wa2"], w["b_a"], bt=pre_tile[0], tb=pre_tile[1], sub=lm, lg=lg,
        time_minor=time_minor)
    hm, og, c_new, n_new, m_new, s_new = _scan(
        qm, km, vm, gif, qg, kg, la, vg, w["g_ghead"], scan_state,
        bt=scan_tile[0], tb=scan_tile[1], lm=lm, lg=lg, sub=sub, time_minor=time_minor)
    x2 = _post(x1, hm if time_minor else hm.reshape(n, M_WIDTH), og.reshape(n, G_VW), w["g_mix"],
               w["g_mhead"], w["w_post"], w["w_pa"], w["w_pb"], w["w_o"], tm=tm, time_minor=time_minor)
    m_heads = m_new[:, :M_HEADS, 0] if time_minor else m_new[:, 0, :M_HEADS]
    return (x2,
            cout[:, SUBLANES - (CONV_W - 1):, :][None],
            c_new[None],
            n_new.reshape(b, M_HEADS, M_HEAD_DIM)[None],
            m_heads[None],
            s_new[None])


FFN_ROWS = 1024
DECODE_SEQS_PER_STEP = 4
PREFILL_SEQS_PER_STEP = 2


def _tiles(b, t):
    n = b * t
    tm = math.gcd(n, 512)
    if t % 256 == 0:
        return dict(tm=math.gcd(t, 512), pre_tile=(1, math.gcd(t, 512)), scan_tile=(math.gcd(b, PREFILL_SEQS_PER_STEP), 256),
                    lm=256, lg=128, sub=16, time_minor=True)
    assert t <= 64 and t % SUBLANES == 0
    bt = max(1, min(b, 512 // t))
    while b % bt:
        bt -= 1
    return dict(tm=tm, pre_tile=(bt, t), scan_tile=(math.gcd(b, DECODE_SEQS_PER_STEP), t), lm=t, lg=t,
                sub=min(16, t), time_minor=False)


def kernel(x_prompt, x_sample, state_conv, state_mlstm_C, state_mlstm_n, state_mlstm_m, state_gla_S,
           g_ffn1, w_ffn1_up, w_ffn1_down, g_mix, w_in, conv_w, conv_b, w_mq, w_mk, b_if, g_mhead,
           w_a2, b_a, g_ghead, w_pa, w_pb, w_o, g_ffn2, w_ffn2_up, w_ffn2_down, g_final):
    w = _pack_weights(g_ffn1, w_ffn1_up, w_ffn1_down, g_mix, w_in, conv_w, conv_b, w_mq, w_mk, b_if,
                      g_mhead, w_a2, b_a, g_ghead, w_pa, w_pb, w_o, g_ffn2, w_ffn2_up, w_ffn2_down,
                      g_final)
    bp, tp, _ = x_prompt.shape
    bs, ts, _ = x_sample.shape
    tm_ffn = math.gcd(math.gcd(bp * tp, bs * ts), FFN_ROWS)
    x1_p, x1_s = _ffn(x_prompt.reshape(bp * tp, D_MODEL), x_sample.reshape(bs * ts, D_MODEL),
                      w["g_ffn1"], w["w_ffn1_up"], w["w_ffn1_down"], w["g_final"], final_norm=False, tm=tm_ffn)
    out_p = _mixer(x1_p, bp, tp, None, w, **_tiles(bp, tp))
    state = (state_conv[0], state_mlstm_C[0], state_mlstm_n[0], state_mlstm_m[0], state_gla_S[0])
    out_s = _mixer(x1_s, bs, ts, state, w, **_tiles(bs, ts))
    y_p, y_s = _ffn(out_p[0], out_s[0], w["g_ffn2"], w["w_ffn2_up"], w["w_ffn2_down"], w["g_final"],
                    final_norm=True, tm=tm_ffn)
    return (y_p.reshape(bp, tp, D_MODEL), y_s.reshape(bs, ts, D_MODEL)) + out_p[1:] + out_s[1:]
```

```python
import functools
import math
import types

import jax
import jax.numpy as jnp
from jax import lax
from jax.experimental import pallas as pl
from jax.experimental.pallas import tpu as pltpu

D_MODEL = 1024
M_HEADS = 4
M_HEAD_DIM = 256
M_WIDTH = 1024
CONV_W = 4
G_HEADS = 4
G_DK = 128
G_DV = 256
G_KW = 512
G_VW = 1024
G_RANK = 16
G_TAU = 16.0
D_FF = 2816
EPS = 1e-6

LANES = 128
SUBLANES = 8
BF16_ROWS = 16
VMEM_LIMIT = 56 * 1024 * 1024

F32 = jnp.float32
BF16 = jnp.bfloat16
NEG_INF = float("-inf")


def _sigmoid(x):
    return 1.0 / (1.0 + jnp.exp(-x))


def _silu(x):
    return x * _sigmoid(x)


def _log_sigmoid(x):
    return jnp.minimum(x, 0.0) - jnp.log(1.0 + jnp.exp(-jnp.abs(x)))


def _rms(x, g):
    return x * lax.rsqrt(jnp.mean(x * x, axis=-1, keepdims=True) + EPS) * g


def _dot(a, b):
    return jnp.dot(a, b, preferred_element_type=F32)


def _dot_nt(a, b):
    return lax.dot_general(a, b, (((1,), (1,)), ((), ())), preferred_element_type=F32)


def _dot_tn(a, b):
    return lax.dot_general(a, b, (((0,), (0,)), ((), ())), preferred_element_type=F32)


def _const_spec(shape):
    nd = len(shape)
    return pl.BlockSpec(shape, lambda *_: (0,) * nd, pipeline_mode=pl.Buffered(1))


FF_CHUNK = 256


UP_CHUNK_ROWS = 32
DOWN_CHUNK_ROWS = 176
W_SLOTS = 3


def _load_rounded(src_hbm, dst_ref, stage_ref, sem_ref, rows):
    n = src_hbm.shape[0] // rows

    def copy(k):
        return pltpu.make_async_copy(src_hbm.at[pl.ds(k * rows, rows), :],
                                     stage_ref.at[k % W_SLOTS], sem_ref.at[k % W_SLOTS])

    for k in range(min(W_SLOTS - 1, n)):
        copy(k).start()
    for k in range(n):
        if k + W_SLOTS - 1 < n:
            copy(k + W_SLOTS - 1).start()
        copy(k).wait()
        dst_ref[k * rows:(k + 1) * rows, :] = stage_ref[k % W_SLOTS].astype(BF16)


def _ffn_kernel(xa_ref, xb_ref, g_ref, wup_hbm, wdown_hbm, gfin_ref, oa_ref, ob_ref, h_ref,
                wup_ref, wdown_ref, sup_ref, sdown_ref, sem_up, sem_down, *, final_norm, tiles_a):
    @pl.when(pl.program_id(0) == 0)
    def _():
        _load_rounded(wup_hbm, wup_ref, sup_ref, sem_up, UP_CHUNK_ROWS)
        _load_rounded(wdown_hbm, wdown_ref, sdown_ref, sem_down, DOWN_CHUNK_ROWS)

    def half_step(x_ref, o_ref):
        x = x_ref[...]
        xn = _rms(x, g_ref[...]).astype(BF16)
        for c in range(D_FF // FF_CHUNK):
            a = _dot(xn, wup_ref[:, c * FF_CHUNK:(c + 1) * FF_CHUNK])
            g = _dot(xn, wup_ref[:, D_FF + c * FF_CHUNK:D_FF + (c + 1) * FF_CHUNK])
            h_ref[:, c * FF_CHUNK:(c + 1) * FF_CHUNK] = (_silu(g) * a).astype(BF16)
        y = x + 0.5 * _dot(h_ref[...], wdown_ref[...])
        o_ref[...] = _rms(y, gfin_ref[...]) if final_norm else y

    on_a = pl.program_id(0) < tiles_a
    pl.when(on_a)(lambda: half_step(xa_ref, oa_ref))
    pl.when(jnp.logical_not(on_a))(lambda: half_step(xb_ref, ob_ref))


def _ffn(xa, xb, g, wup, wdown, gfin, *, final_norm, tm):
    na, nb = xa.shape[0], xb.shape[0]
    assert na % tm == 0 and nb % tm == 0
    tiles_a, tiles_b = na // tm, nb // tm
    spec_a = pl.BlockSpec((tm, D_MODEL), lambda i: (jnp.minimum(i, tiles_a - 1), 0))
    spec_b = pl.BlockSpec((tm, D_MODEL), lambda i: (jnp.maximum(i - tiles_a, 0), 0),
                          **(dict(pipeline_mode=pl.Buffered(1)) if tiles_b == 1 else {}))
    return pl.pallas_call(
        functools.partial(_ffn_kernel, final_norm=final_norm, tiles_a=tiles_a),
        grid=(tiles_a + tiles_b,),
        in_specs=[
            spec_a, spec_b,
            _const_spec((1, D_MODEL)),
            pl.BlockSpec(memory_space=pl.ANY),
            pl.BlockSpec(memory_space=pl.ANY),
            _const_spec((1, D_MODEL)),
        ],
        out_specs=[spec_a, spec_b],
        out_shape=[jax.ShapeDtypeStruct((na, D_MODEL), F32), jax.ShapeDtypeStruct((nb, D_MODEL), F32)],
        scratch_shapes=[
            pltpu.VMEM((tm, D_FF), BF16),
            pltpu.VMEM((D_MODEL, 2 * D_FF), BF16),
            pltpu.VMEM((D_FF, D_MODEL), BF16),
            pltpu.VMEM((W_SLOTS, UP_CHUNK_ROWS, 2 * D_FF), F32),
            pltpu.VMEM((W_SLOTS, DOWN_CHUNK_ROWS, D_MODEL), F32),
            pltpu.SemaphoreType.DMA((W_SLOTS,)),
            pltpu.SemaphoreType.DMA((W_SLOTS,)),
        ],
        compiler_params=pltpu.CompilerParams(
            dimension_semantics=("arbitrary",), vmem_limit_bytes=VMEM_LIMIT),
        name="ffn_final" if final_norm else "ffn",
    )(xa, xb, g, wup.astype(F32), wdown.astype(F32), gfin)


PRE_U, PRE_V, PRE_QG, PRE_KG, PRE_VG = 0, 1024, 2048, 2560, 3072
PRE_WIDTH = 4096
A_LANE0 = 8


def _pre_kernel(x_ref, cst_ref, g_ref, w_ref, ws_ref, cw_ref, cb_ref, wq_ref, wk_ref, bif_ref,
                wa2_ref, ba_ref,
                qm_ref, km_ref, vm_ref, gif_ref, qg_ref, kg_ref, la_ref, vg_ref, cout_ref,
                cbuf_ref, *, bt, tb, sub, lg, time_minor):
    t = pl.program_id(1)
    m = bt * sub

    @pl.when(t == 0)
    def _():
        cbuf_ref[:, 0:SUBLANES, :] = cst_ref[...]

    @pl.when(t > 0)
    def _():
        cbuf_ref[:, 0:SUBLANES, :] = cbuf_ref[:, tb:tb + SUBLANES, :]

    for j in range(tb // sub):
        r0 = j * sub
        rows = slice(r0, r0 + sub)
        to_rows = lambda y, w: y.reshape(bt, sub, w)
        xn = _rms(x_ref[:, rows, :].reshape(m, D_MODEL), g_ref[...]).astype(BF16)
        small = _dot(xn, ws_ref[...])

        u = _dot(xn, w_ref[:, PRE_U:PRE_U + M_WIDTH])
        cbuf_ref[:, SUBLANES + r0:SUBLANES + r0 + sub, :] = to_rows(u, M_WIDTH)
        c = cb_ref[...].reshape(1, 1, M_WIDTH)
        for tap in range(CONV_W):
            off = SUBLANES - (CONV_W - 1) + tap + r0
            c = c + cbuf_ref[:, off:off + sub, :] * cw_ref[tap:tap + 1, :].reshape(1, 1, M_WIDTH)
        ch = _silu(c).reshape(m, M_WIDTH).astype(BF16)

        vm = _dot(xn, w_ref[:, PRE_V:PRE_V + M_WIDTH])
        gates = small + bif_ref[...]
        lane = lax.broadcasted_iota(jnp.int32, gates.shape, 1)
        gif = jnp.where(lane < M_HEADS, gates, jnp.where(lane < 2 * M_HEADS, _log_sigmoid(gates), 0.0))
        if time_minor:
            b = pltpu.roll(_scan_rows(gif, jnp.add, 0.0), LANES - M_HEADS, axis=1)
            g = gif - b
            gmax = _scan_rows(g, jnp.maximum, NEG_INF)
            packed = jnp.where(lane < M_HEADS, g,
                               jnp.where(lane < 2 * M_HEADS, pltpu.roll(gmax, M_HEADS, axis=1),
                                         jnp.where(lane < 3 * M_HEADS, pltpu.roll(b, 2 * M_HEADS, axis=1), 0.0)))
            gif_ref[0, j] = jnp.transpose(packed)[0:2 * SUBLANES, :]
        else:
            gif_ref[:, rows, :] = to_rows(gif, LANES)
            vm_ref[:, rows, :] = to_rows(vm.astype(BF16), M_WIDTH)

        def decay_columns(h):
            cols = slice(h * G_DK, (h + 1) * G_DK)
            a2 = _dot(small.astype(BF16), wa2_ref[:, cols]) + ba_ref[:, cols]
            la_ref[:, rows, cols] = to_rows(
                _scan_rows(_log_sigmoid(a2) * (1.0 / G_TAU), jnp.add, 0.0, segment=lg), G_DK)

        qg_ref[:, rows, :] = to_rows(
            (_dot(xn, w_ref[:, PRE_QG:PRE_QG + G_KW]) * (G_DK ** -0.5)).astype(BF16), G_KW)
        decay_columns(0)
        kg_ref[:, rows, :] = to_rows(_dot(xn, w_ref[:, PRE_KG:PRE_KG + G_KW]).astype(BF16), G_KW)
        decay_columns(1)
        half = G_VW // 2
        for p in range(2):
            vg_ref[:, rows, p * half:(p + 1) * half] = to_rows(
                _dot(xn, w_ref[:, PRE_VG + p * half:PRE_VG + (p + 1) * half]).astype(BF16), half)
            decay_columns(2 + p)

        for h in range(M_HEADS):
            sl = slice(h * M_HEAD_DIM, (h + 1) * M_HEAD_DIM)
            qh = _dot(ch[:, sl], wq_ref[h])
            if time_minor:
                qm_ref[0, j, sl, :] = jnp.transpose(qh).astype(BF16)
                vm_ref[0, j, sl, :] = jnp.transpose(vm[:, sl]).astype(BF16)
            else:
                qm_ref[:, rows, sl] = to_rows(qh.astype(BF16), M_HEAD_DIM)
            km_ref[:, rows, sl] = to_rows(
                (_dot(ch[:, sl], wk_ref[h]) * (M_HEAD_DIM ** -0.5)).astype(BF16), M_HEAD_DIM)
    cout_ref[...] = cbuf_ref[:, tb:tb + SUBLANES, :]


def _pre(x, cst, g, w_pre, w_small, conv_w, conv_b, wq, wk, bif, wa2, ba, *, bt, tb, sub, lg, time_minor):
    b, t, _ = x.shape
    assert b % bt == 0 and t % tb == 0 and tb % sub == 0 and sub % SUBLANES == 0 and sub % lg == 0
    assert not time_minor or (bt == 1 and sub % LANES == 0)
    tok = lambda w: pl.BlockSpec((bt, tb, w), lambda i, j: (i, j, 0))
    per_seq = pl.BlockSpec((bt, SUBLANES, M_WIDTH), lambda i, j: (i, 0, 0))
    sds = lambda w, dt: jax.ShapeDtypeStruct((b, t, w), dt)
    if time_minor:
        tok_t = lambda w: pl.BlockSpec((1, tb // sub, w, sub), lambda i, j: (i, j, 0, 0))
        sds_t = lambda w, dt: jax.ShapeDtypeStruct((b, t // sub, w, sub), dt)
        qv_spec, qv_shape = tok_t(M_WIDTH), sds_t(M_WIDTH, BF16)
        gif_spec, gif_shape = tok_t(2 * SUBLANES), sds_t(2 * SUBLANES, F32)
    else:
        qv_spec, qv_shape = tok(M_WIDTH), sds(M_WIDTH, BF16)
        gif_spec, gif_shape = tok(LANES), sds(LANES, F32)
    return pl.pallas_call(
        functools.partial(_pre_kernel, bt=bt, tb=tb, sub=sub, lg=lg, time_minor=time_minor),
        grid=(b // bt, t // tb),
        in_specs=[
            tok(D_MODEL), per_seq,
            _const_spec((1, D_MODEL)),
            _const_spec((D_MODEL, PRE_WIDTH)),
            _const_spec((D_MODEL, LANES)),
            _const_spec((CONV_W, M_WIDTH)),
            _const_spec((1, M_WIDTH)),
            _const_spec((M_HEADS, M_HEAD_DIM, M_HEAD_DIM)),
            _const_spec((M_HEADS, M_HEAD_DIM, M_HEAD_DIM)),
            _const_spec((1, LANES)),
            _const_spec((LANES, G_KW)),
            _const_spec((1, G_KW)),
        ],
        out_specs=[qv_spec, tok(M_WIDTH), qv_spec, gif_spec, tok(G_KW), tok(G_KW),
                   tok(G_KW), tok(G_VW), per_seq],
        out_shape=[qv_shape, sds(M_WIDTH, BF16), qv_shape, gif_shape,
                   sds(G_KW, BF16), sds(G_KW, BF16), sds(G_KW, F32), sds(G_VW, BF16),
                   jax.ShapeDtypeStruct((b, SUBLANES, M_WIDTH), F32)],
        scratch_shapes=[pltpu.VMEM((bt, tb + SUBLANES, M_WIDTH), F32)],
        compiler_params=pltpu.CompilerParams(
            dimension_semantics=("parallel", "arbitrary"), vmem_limit_bytes=VMEM_LIMIT),
        name="mixer_pre",
    )(x, cst, g, w_pre, w_small, conv_w, conv_b, wq, wk, bif, wa2, ba)


def _scan_rows(x, op, fill, segment=None):
    n = segment or x.shape[0]
    row = lax.broadcasted_iota(jnp.int32, x.shape, 0) % n
    s = 1
    while s < n:
        x = op(x, jnp.where(row >= s, pltpu.roll(x, s, axis=0), fill))
        s *= 2
    return x


def _col_of_row(row):
    return jnp.transpose(jnp.broadcast_to(row, (SUBLANES, LANES)))[:, 0:1]


def _mlstm_chunk_tm(seqs, *, lm):
    reps = lm // LANES
    head_row = lax.broadcasted_iota(jnp.int32, (SUBLANES, LANES), 0) < M_HEADS
    src = lax.broadcasted_iota(jnp.int32, (lm, lm), 0)
    dst = lax.broadcasted_iota(jnp.int32, (lm, lm), 1)
    causal = src <= dst
    sl = [slice(h * M_HEAD_DIM, (h + 1) * M_HEAD_DIM) for h in range(M_HEADS)]
    row = lambda x, h: x[h:h + 1, :]
    gates = []
    for sq in seqs:
        g = sq.gift[0:SUBLANES]
        gmax = pltpu.roll(g, M_HEADS, axis=0)
        b = sq.gift[SUBLANES:2 * SUBLANES]
        m_prev = jnp.concatenate([sq.m_src[...]] * reps, axis=1)
        mx = jnp.maximum(m_prev, gmax)
        mx_last = jnp.broadcast_to(mx[:, lm - 1:lm], (SUBLANES, lm))
        gates.append(dict(
            mx=mx,
            inter=jnp.exp(m_prev - mx),
            nfloor=jnp.exp(-(b + mx)),
            w_in=jnp.exp(g - mx_last),
            decay=jnp.exp(m_prev - mx_last),
            g_col=jnp.transpose(g)))
        m_new = jnp.broadcast_to(b[:, lm - 1:lm], (SUBLANES, lm)) + mx_last
        sq.m_dst[...] = jnp.where(head_row, m_new[:, :LANES], 0.0)
    units = [(sq, gt, h) for sq, gt in zip(seqs, gates) for h in range(M_HEADS)]
    c_old = [sq.c_src[h] for sq, _, h in units]
    n_old = [sq.n_src[h] for sq, _, h in units]
    cn = [jnp.concatenate([c_old[u].astype(BF16),
                           jnp.broadcast_to(n_old[u].astype(BF16), (BF16_ROWS, M_HEAD_DIM))], axis=0)
          for u in range(len(units))]
    cq = [_dot(cn[u], sq.qt[sl[h], :]) for u, (sq, _, h) in enumerate(units)]
    dm = [jnp.exp(jnp.where(causal, gt["g_col"][:, h:h + 1] - row(gt["mx"], h), NEG_INF))
          for _, gt, h in units]
    st = [_dot(sq.k[:, sl[h]], sq.qt[sl[h], :]) * dm[u]
          for u, (sq, _, h) in enumerate(units)]
    num = [row(gt["inter"], h) * cq[u][:M_HEAD_DIM] + _dot(sq.vt[sl[h], :], st[u].astype(BF16))
           for u, (sq, gt, h) in enumerate(units)]
    for u, (sq, gt, h) in enumerate(units):
        den = (row(gt["inter"], h) * cq[u][M_HEAD_DIM:M_HEAD_DIM + 1]
               + jnp.sum(st[u], axis=0, keepdims=True))
        rcp = 1.0 / jnp.maximum(jnp.abs(den), row(gt["nfloor"], h))
        ms = jnp.mean(num[u] * num[u], axis=0, keepdims=True)
        sq.hm_store(h, num[u] * (rcp * lax.rsqrt(rcp * rcp * ms + EPS)))
    for u, (sq, gt, h) in enumerate(units):
        w_h = row(gt["w_in"], h)
        dec_h = jnp.concatenate([row(gt["decay"], h)[:, :LANES]] * (M_HEAD_DIM // LANES), axis=1)
        wv = jnp.concatenate([(sq.vt[sl[h], :].astype(F32) * w_h).astype(BF16),
                              jnp.broadcast_to(w_h.astype(BF16), (BF16_ROWS, lm))], axis=0)
        upd = _dot(wv, sq.k[:, sl[h]])
        sq.c_dst[h] = dec_h * c_old[u] + upd[:M_HEAD_DIM]
        sq.n_dst[h] = dec_h * n_old[u] + upd[M_HEAD_DIM:M_HEAD_DIM + 1]


def _mlstm_chunk(seqs, *, lm):
    row = lax.broadcasted_iota(jnp.int32, (lm, lm), 0)
    col = lax.broadcasted_iota(jnp.int32, (lm, lm), 1)
    causal = col <= row
    sl = [slice(h * M_HEAD_DIM, (h + 1) * M_HEAD_DIM) for h in range(M_HEADS)]
    gates = []
    for sq in seqs:
        cs = _scan_rows(sq.gif, jnp.add, 0.0)
        b = pltpu.roll(cs, LANES - M_HEADS, axis=1)
        g = sq.gif - b
        m_prev = sq.m_src[...]
        mx = jnp.maximum(m_prev, _scan_rows(g, jnp.maximum, NEG_INF))
        mx_last = mx[lm - 1:lm, :]
        gates.append(dict(
            mx=mx,
            inter=jnp.exp(m_prev - mx),
            nfloor=jnp.exp(-(b + mx)),
            g_t=jnp.transpose(g),
            w_in=jnp.exp(g - mx_last),
            decay=jnp.exp(m_prev - mx_last)))
        sq.m_dst[...] = b[lm - 1:lm, :] + mx_last
    units = [(sq, gt, h) for sq, gt in zip(seqs, gates) for h in range(M_HEADS)]
    col_of = lambda x, h: x[:, h:h + 1]
    c_old = [sq.c_src[h] for sq, _, h in units]
    n_old = [sq.n_src[h] for sq, _, h in units]
    dm = [jnp.exp(jnp.where(causal, gt["g_t"][h:h + 1, :] - col_of(gt["mx"], h), NEG_INF))
          for _, gt, h in units]
    qn = [jnp.sum(sq.q[:, sl[h]].astype(F32) * n_old[u], axis=-1, keepdims=True)
          for u, (sq, _, h) in enumerate(units)]
    s = [_dot_nt(sq.q[:, sl[h]], sq.k[:, sl[h]]) * dm[u] for u, (sq, _, h) in enumerate(units)]
    num = [col_of(gt["inter"], h) * _dot_nt(sq.q[:, sl[h]], c_old[u].astype(BF16))
           + _dot(s[u].astype(BF16), sq.v[:, sl[h]]) for u, (sq, gt, h) in enumerate(units)]
    for u, (sq, gt, h) in enumerate(units):
        den = col_of(gt["inter"], h) * qn[u] + jnp.sum(s[u], axis=-1, keepdims=True)
        rcp = 1.0 / jnp.maximum(jnp.abs(den), col_of(gt["nfloor"], h))
        ms = jnp.mean(num[u] * num[u], axis=-1, keepdims=True)
        sq.hm_store(h, num[u] * (rcp * lax.rsqrt(rcp * rcp * ms + EPS)))
    for u, (sq, gt, h) in enumerate(units):
        w_h = col_of(gt["w_in"], h)
        dec_h = col_of(gt["decay"], h)
        wv = (w_h * sq.v[:, sl[h]].astype(F32)).astype(BF16)
        sq.c_dst[h] = dec_h * c_old[u] + _dot_tn(wv, sq.k[:, sl[h]])
        sq.n_dst[h] = dec_h * n_old[u] + jnp.sum(w_h * sq.k[:, sl[h]].astype(F32), axis=0, keepdims=True)


GLA_SAFE_SPAN = 40.0


def _gla_scores_guarded(qh, kh, a, *, lg, sub):
    row = lax.broadcasted_iota(jnp.int32, (lg, lg), 0)
    col = lax.broadcasted_iota(jnp.int32, (lg, lg), 1)
    att = jnp.zeros((lg, lg), F32)
    c = sub
    while c < lg:
        nblk = lg // (2 * c)
        a_ref = jnp.concatenate(
            [jnp.broadcast_to(a[2 * c * i + c - 1:2 * c * i + c, :], (2 * c, G_DK))
             for i in range(nblk)], axis=0)
        qt = (qh * jnp.exp(jnp.minimum(a - a_ref, 0.0))).astype(BF16)
        kt = (kh * jnp.exp(jnp.minimum(a_ref - a, 0.0))).astype(BF16)
        valid = (row // (2 * c) == col // (2 * c)) & (row % (2 * c) >= c) & (col % (2 * c) < c)
        att = att + jnp.where(valid, _dot_nt(qt, kt), 0.0)
        c *= 2
    rows = lax.broadcasted_iota(jnp.int32, (sub, lg), 0)
    cols = lax.broadcasted_iota(jnp.int32, (sub, lg), 1)
    blocks = []
    for blk in range(lg // sub):
        r0 = blk * sub
        a_b = a[r0:r0 + sub, :]
        q_b = qh[r0:r0 + sub, :]
        acc = jnp.zeros((sub, lg), F32)
        for s in range(sub):
            e = jnp.exp(jnp.minimum(a_b - a[r0 + s:r0 + s + 1, :], 0.0))
            p = jnp.sum(q_b * e * kh[r0 + s:r0 + s + 1, :], axis=-1, keepdims=True)
            acc = jnp.where((cols == r0 + s) & (rows >= s), p, acc)
        blocks.append(acc)
    return att + jnp.concatenate(blocks, axis=0)


def _gla_chunk_guarded(seqs, gh_ref, *, lg, sub):
    ks = [slice(h * G_DK, (h + 1) * G_DK) for h in range(G_HEADS)]
    vs = [slice(h * G_DV, (h + 1) * G_DV) for h in range(G_HEADS)]
    pre = []
    for sq in seqs:
        a_last = sq.a[lg - 1:lg, :]
        qf = sq.qg.astype(F32)
        kf = sq.kg.astype(F32)
        pre.append(dict(a=sq.a, a_last=a_last, qf=qf, kf=kf,
                        qs=(qf * jnp.exp(sq.a)).astype(BF16),
                        kl=(kf * jnp.exp(a_last - sq.a)).astype(BF16)))
    units = [(sq, p, h) for sq, p in zip(seqs, pre) for h in range(G_HEADS)]
    s_old = [sq.s_src[h] for sq, _, h in units]
    att = [_gla_scores_guarded(p["qf"][:, ks[h]], p["kf"][:, ks[h]], p["a"][:, ks[h]], lg=lg, sub=sub)
           for _, p, h in units]
    o = [_dot(p["qs"][:, ks[h]], s_old[u].astype(BF16)) + _dot(att[u].astype(BF16), sq.vg[:, vs[h]])
         for u, (sq, p, h) in enumerate(units)]
    for u, (sq, _, h) in enumerate(units):
        sq.og_store(h, _rms(o[u], gh_ref[:, vs[h]]))
    for u, (sq, p, h) in enumerate(units):
        sq.s_dst[h] = (jnp.exp(_col_of_row(p["a_last"][:, ks[h]])) * s_old[u]
                       + _dot_tn(p["kl"][:, ks[h]], sq.vg[:, vs[h]]))


def _gla_block_direct(seqs, gh_ref, *, tb, lg):
    ks = [slice(h * G_DK, (h + 1) * G_DK) for h in range(G_HEADS)]
    vs = [slice(h * G_DV, (h + 1) * G_DV) for h in range(G_HEADS)]
    chunks = [slice(c * lg, (c + 1) * lg) for c in range(tb // lg)]
    row = lax.broadcasted_iota(jnp.int32, (lg, lg), 0)
    col = lax.broadcasted_iota(jnp.int32, (lg, lg), 1)
    causal = col <= row
    pre = []
    for sq in seqs:
        a_last = [sq.a[r.stop - 1:r.stop, :] for r in chunks]
        a_last_rows = jnp.concatenate([jnp.broadcast_to(al, (lg, G_KW)) for al in a_last], axis=0)
        qf = sq.qg.astype(F32)
        kf = sq.kg.astype(F32)
        pre.append(dict(a_last=a_last,
                        qs=(qf * jnp.exp(sq.a)).astype(BF16),
                        kt=(kf * jnp.exp(-sq.a)).astype(BF16),
                        kl=(kf * jnp.exp(a_last_rows - sq.a)).astype(BF16)))
    units = [(sq, p, h) for sq, p in zip(seqs, pre) for h in range(G_HEADS)]
    att = [[jnp.where(causal, _dot_nt(p["qs"][r, ks[h]], p["kt"][r, ks[h]]), 0.0).astype(BF16)
            for r in chunks] for _, p, h in units]
    s_cur = [sq.s_src[h] for sq, _, h in units]
    for c, r in enumerate(chunks):
        o = [_dot(p["qs"][r, ks[h]], s_cur[u].astype(BF16)) + _dot(att[u][c], sq.vg[r, vs[h]])
             for u, (sq, p, h) in enumerate(units)]
        for u, (sq, _, h) in enumerate(units):
            sq.og_rows_store(r, h, _rms(o[u], gh_ref[:, vs[h]]))
        s_cur = [jnp.exp(_col_of_row(p["a_last"][c][:, ks[h]])) * s_cur[u]
                 + _dot_tn(p["kl"][r, ks[h]], sq.vg[r, vs[h]])
                 for u, (sq, p, h) in enumerate(units)]
    for u, (sq, _, h) in enumerate(units):
        sq.s_dst[h] = s_cur[u]


def _scan_kernel(*refs, bt, tb, lm, lg, sub, has_state, time_minor, single_step):
    n_in = 9 + (4 if has_state else 0)
    qm_ref, km_ref, vm_ref, gif_ref, qg_ref, kg_ref, la_ref, vg_ref, ggh_ref = refs[:9]
    state_in = refs[9:n_in]
    hm_ref, og_ref = refs[n_in:n_in + 2]
    state_out = refs[n_in + 2:n_in + 6]
    scratch = refs[n_in + 6:]
    t = pl.program_id(1)
    nt = pl.num_programs(1)

    streamed = has_state and single_step
    src = state_in if streamed else scratch
    dst = state_out if streamed else scratch
    if not streamed:
        @pl.when(t == 0)
        def _():
            for i, ref in enumerate(scratch):
                ref[...] = state_in[i][...] if has_state else jnp.zeros(ref.shape, F32)

    def seq_records(bi, m_rows, g_rows):
        def hm_store(h, val):
            if time_minor:
                hm_ref[bi, 0, h * M_HEAD_DIM:(h + 1) * M_HEAD_DIM, :] = val
            else:
                hm_ref[bi, m_rows, h * M_HEAD_DIM:(h + 1) * M_HEAD_DIM] = val

        def og_rows_store(rows, h, val):
            og_ref[bi, rows, h * G_DV:(h + 1) * G_DV] = val

        return types.SimpleNamespace(
            bi=bi, hm_store=hm_store, og_rows_store=og_rows_store,
            og_store=functools.partial(og_rows_store, g_rows),
            c_src=src[0].at[bi], n_src=src[1].at[bi], m_src=src[2].at[bi], s_src=src[3].at[bi],
            c_dst=dst[0].at[bi], n_dst=dst[1].at[bi], m_dst=dst[2].at[bi], s_dst=dst[3].at[bi])

    def m_body(ci, carry):
        rows = pl.ds(pl.multiple_of(ci * lm, lm), lm)
        seqs = [seq_records(bi, rows, None) for bi in range(bt)]
        if time_minor:
            for sq in seqs:
                sq.qt, sq.k, sq.vt = qm_ref[sq.bi, 0], km_ref[sq.bi, rows, :], vm_ref[sq.bi, 0]
                sq.gift = gif_ref[sq.bi, 0]
            _mlstm_chunk_tm(seqs, lm=lm)
        else:
            for sq in seqs:
                sq.q, sq.k, sq.v = qm_ref[sq.bi, rows, :], km_ref[sq.bi, rows, :], vm_ref[sq.bi, rows, :]
                sq.gif = gif_ref[sq.bi, rows, :]
            _mlstm_chunk(seqs, lm=lm)
        return carry

    def gla_records(rows):
        seqs = [seq_records(bi, None, rows) for bi in range(bt)]
        for sq in seqs:
            sq.qg, sq.kg, sq.vg = qg_ref[sq.bi, rows, :], kg_ref[sq.bi, rows, :], vg_ref[sq.bi, rows, :]
            sq.a = la_ref[sq.bi, rows, :]
        return seqs

    def g_body(ci, carry):
        rows = pl.ds(pl.multiple_of(ci * lg, lg), lg)
        _gla_chunk_guarded(gla_records(rows), ggh_ref, lg=lg, sub=sub)
        return carry

    def loop(body, n):
        if n == 1:
            body(0, 0)
        else:
            lax.fori_loop(0, n, body, 0)

    loop(m_body, tb // lm)
    if lg <= sub:
        loop(g_body, tb // lg)
    else:
        chunk_decay = [la_ref[:, c * lg + lg - 1:(c + 1) * lg, :] for c in range(tb // lg)]
        span_ok = jnp.min(functools.reduce(jnp.minimum, chunk_decay)) >= -GLA_SAFE_SPAN

        @pl.when(span_ok)
        def _():
            _gla_block_direct(gla_records(slice(0, tb)), ggh_ref, tb=tb, lg=lg)

        @pl.when(jnp.logical_not(span_ok))
        def _():
            loop(g_body, tb // lg)

    if not streamed:
        @pl.when(t == nt - 1)
        def _():
            for out, ref in zip(state_out, scratch):
                out[...] = ref[...]


def _scan(qm, km, vm, gif, qg, kg, la, vg, ggh, state, *, bt, tb, lm, lg, sub, time_minor):
    b, t, _ = km.shape
    assert b % bt == 0 and t % tb == 0 and tb % lm == 0 and tb % lg == 0 and lg % sub == 0
    has_state = state is not None
    assert not (time_minor and has_state) and (not time_minor or (lm % LANES == 0 and tb == lm))
    tok = lambda w: pl.BlockSpec((bt, tb, w), lambda i, j: (i, j, 0))
    tok_t = lambda w: pl.BlockSpec((bt, 1, w, tb), lambda i, j: (i, j, 0, 0))
    c_shape = (M_HEADS, M_HEAD_DIM, M_HEAD_DIM)
    n_shape = (M_HEADS, 1, M_HEAD_DIM)
    m_shape = (SUBLANES, LANES) if time_minor else (1, LANES)
    s_shape = (G_HEADS, G_DK, G_DV)
    st_shapes = [(bt,) + shp for shp in (c_shape, n_shape, m_shape, s_shape)]
    st_specs = [pl.BlockSpec(shp, lambda i, j, nd=len(shp): (i,) + (0,) * (nd - 1)) for shp in st_shapes]
    single_step = t == tb
    carried = not (has_state and single_step)
    if time_minor:
        qv_spec, gif_spec = tok_t(M_WIDTH), tok_t(2 * SUBLANES)
        hm_spec, hm_shape = tok_t(M_WIDTH), (b, t // tb, M_WIDTH, tb)
    else:
        qv_spec, gif_spec = tok(M_WIDTH), tok(LANES)
        hm_spec, hm_shape = tok(M_WIDTH), (b, t, M_WIDTH)
    in_specs = [qv_spec, tok(M_WIDTH), qv_spec, gif_spec, tok(G_KW), tok(G_KW),
                tok(G_KW), tok(G_VW), _const_spec((1, G_VW))]
    args = [qm, km, vm, gif, qg, kg, la, vg, ggh]
    if has_state:
        in_specs += st_specs
        args += list(state)
    return pl.pallas_call(
        functools.partial(_scan_kernel, bt=bt, tb=tb, lm=lm, lg=lg, sub=sub, has_state=has_state,
                          time_minor=time_minor, single_step=single_step),
        grid=(b // bt, t // tb),
        in_specs=in_specs,
        out_specs=[hm_spec, tok(G_VW)] + st_specs,
        out_shape=[jax.ShapeDtypeStruct(hm_shape, F32),
                   jax.ShapeDtypeStruct((b, t, G_VW), F32)]
                  + [jax.ShapeDtypeStruct((b,) + shp[1:], F32) for shp in st_shapes],
        scratch_shapes=[pltpu.VMEM(shp, F32) for shp in st_shapes] if carried else [],
        compiler_params=pltpu.CompilerParams(
            dimension_semantics=("parallel", "arbitrary"), vmem_limit_bytes=VMEM_LIMIT),
        name="mixer_scan_state" if has_state else "mixer_scan",
    )(*args)


POST_O, POST_R, POST_A, POST_B = 0, 1024, 2048, 3072


def _post_kernel(x_ref, hm_ref, og_ref, g_ref, gmh_ref, w_ref, wpa_ref, wpb_ref, wo_ref, o_ref, *,
                 time_minor):
    x = x_ref[...]
    xn = _rms(x, g_ref[...]).astype(BF16)
    gate = lambda off: _dot(xn, w_ref[:, off:off + D_MODEL])
    if time_minor:
        hm = jnp.concatenate(
            [jnp.concatenate([jnp.transpose(hm_ref[0, j, h * M_HEAD_DIM:(h + 1) * M_HEAD_DIM, :])
                              for h in range(M_HEADS)], axis=1)
             for j in range(hm_ref.shape[1])], axis=0)
    else:
        hm = hm_ref[...]
    hm = (_sigmoid(gate(POST_O)) * (hm * gmh_ref[...])).astype(BF16)
    og = (_silu(gate(POST_R)) * og_ref[...]).astype(BF16)
    y = (_sigmoid(gate(POST_A)) * _dot(hm, wpa_ref[...])
         + _sigmoid(gate(POST_B)) * _dot(og, wpb_ref[...]))
    o_ref[...] = x + _dot(y.astype(BF16), wo_ref[...])


def _post(x2d, hm, og2d, g, gmh, w_post, wpa, wpb, wo, *, tm, time_minor):
    n = x2d.shape[0]
    assert n % tm == 0
    tok = pl.BlockSpec((tm, D_MODEL), lambda i: (i, 0))
    if time_minor:
        nblk, blk = hm.shape[1], hm.shape[3]
        assert tm % blk == 0 and (nblk * blk) % tm == 0
        per_seq = nblk * blk // tm
        hm_spec = pl.BlockSpec((1, tm // blk, M_WIDTH, blk), lambda i: (i // per_seq, i % per_seq, 0, 0))
    else:
        hm_spec = tok
    return pl.pallas_call(
        functools.partial(_post_kernel, time_minor=time_minor),
        grid=(n // tm,),
        in_specs=[tok, hm_spec, tok, _const_spec((1, D_MODEL)), _const_spec((1, M_WIDTH)),
                  _const_spec((D_MODEL, 4 * D_MODEL)),
                  _const_spec((M_WIDTH, D_MODEL)), _const_spec((G_VW, D_MODEL)),
                  _const_spec((D_MODEL, D_MODEL))],
        out_specs=tok,
        out_shape=jax.ShapeDtypeStruct((n, D_MODEL), F32),
        compiler_params=pltpu.CompilerParams(
            dimension_semantics=("parallel",), vmem_limit_bytes=VMEM_LIMIT),
        name="mixer_post",
    )(x2d, hm, og2d, g, gmh, w_post, wpa, wpb, wo)


def _pack_weights(g_ffn1, w_ffn1_up, w_ffn1_down, g_mix, w_in, conv_w, conv_b, w_mq, w_mk, b_if,
                  g_mhead, w_a2, b_a, g_ghead, w_pa, w_pb, w_o, g_ffn2, w_ffn2_up, w_ffn2_down,
                  g_final):
    l = 0
    row = lambda v: v.reshape(1, -1).astype(F32)
    edges = [0]
    for wd in (M_WIDTH, M_WIDTH, M_WIDTH, 2 * M_HEADS, G_KW, G_KW, G_VW, G_VW, G_RANK, D_MODEL, D_MODEL):
        edges.append(edges[-1] + wd)
    seg = lambda i: w_in[l][:, edges[i]:edges[i + 1]]
    u_m, v_m, o_m, if_m, q_g, k_g, v_g, r_g, a_g, g_a, g_b = (seg(i) for i in range(11))
    w_pre = jnp.concatenate([u_m, v_m, q_g, k_g, v_g], axis=1).astype(BF16)
    w_small = jnp.concatenate(
        [if_m, a_g, jnp.zeros((D_MODEL, LANES - 2 * M_HEADS - G_RANK), F32)], axis=1).astype(BF16)
    w_post = jnp.concatenate([o_m, r_g, g_a, g_b], axis=1).astype(BF16)
    bif = jnp.concatenate([b_if[l].reshape(-1), jnp.zeros((LANES - 2 * M_HEADS,), F32)]).reshape(1, LANES)
    wa2 = jnp.zeros((LANES, G_KW), F32).at[A_LANE0:A_LANE0 + G_RANK].set(w_a2[l]).astype(BF16)
    return dict(
        g_ffn1=row(g_ffn1[l]), w_ffn1_up=w_ffn1_up[l], w_ffn1_down=w_ffn1_down[l],
        g_mix=row(g_mix[l]), w_pre=w_pre, w_small=w_small, w_post=w_post,
        conv_w=conv_w[l].astype(F32), conv_b=row(conv_b[l]),
        w_mq=w_mq[l].astype(BF16), w_mk=w_mk[l].astype(BF16), bif=bif,
        g_mhead=row(g_mhead[l]), wa2=wa2, b_a=row(b_a[l]), g_ghead=row(g_ghead[l]),
        w_pa=w_pa[l].astype(BF16), w_pb=w_pb[l].astype(BF16), w_o=w_o[l].astype(BF16),
        g_ffn2=row(g_ffn2[l]), w_ffn2_up=w_ffn2_up[l], w_ffn2_down=w_ffn2_down[l],
        g_final=row(g_final))


def _mixer(x1, b, t, state, w, *, tm, pre_tile, scan_tile, lm, lg, sub, time_minor):
    n = b * t
    if state is None:
        cst = jnp.zeros((b, SUBLANES, M_WIDTH), F32)
        scan_state = None
    else:
        conv0, c0, n0, m0, s0 = state
        cst = jnp.concatenate(
            [jnp.zeros((b, SUBLANES - (CONV_W - 1), M_WIDTH), F32), conv0.astype(F32)], axis=1)
        m0p = jnp.concatenate([m0, jnp.zeros((b, LANES - M_HEADS), F32)], axis=1).reshape(b, 1, LANES)
        scan_state = (c0, n0.reshape(b, M_HEADS, 1, M_HEAD_DIM), m0p, s0)
    qm, km, vm, gif, qg, kg, la, vg, cout = _pre(
        x1.reshape(b, t, D_MODEL), cst, w["g_mix"], w["w_pre"], w["w_small"], w["conv_w"], w["conv_b"],
        w["w_mq"], w["w_mk"], w["bif"], w["wa2"], w["b_a"], bt=pre_tile[0], tb=pre_tile[1], sub=lm, lg=lg,
        time_minor=time_minor)
    hm, og, c_new, n_new, m_new, s_new = _scan(
        qm, km, vm, gif, qg, kg, la, vg, w["g_ghead"], scan_state,
        bt=scan_tile[0], tb=scan_tile[1], lm=lm, lg=lg, sub=sub, time_minor=time_minor)
    x2 = _post(x1, hm if time_minor else hm.reshape(n, M_WIDTH), og.reshape(n, G_VW), w["g_mix"],
               w["g_mhead"], w["w_post"], w["w_pa"], w["w_pb"], w["w_o"], tm=tm, time_minor=time_minor)
    m_heads = m_new[:, :M_HEADS, 0] if time_minor else m_new[:, 0, :M_HEADS]
    return (x2,
            cout[:, SUBLANES - (CONV_W - 1):, :][None],
            c_new[None],
            n_new.reshape(b, M_HEADS, M_HEAD_DIM)[None],
            m_heads[None],
            s_new[None])


FFN_ROWS = 1024
DECODE_SEQS_PER_STEP = 4
PREFILL_SEQS_PER_STEP = 2


def _tiles(b, t):
    n = b * t
    tm = math.gcd(n, 512)
    if t % 256 == 0:
        return dict(tm=math.gcd(t, 512), pre_tile=(1, math.gcd(t, 512)), scan_tile=(math.gcd(b, PREFILL_SEQS_PER_STEP), 256),
                    lm=256, lg=128, sub=16, time_minor=True)
    assert t <= 64 and t % SUBLANES == 0
    bt = max(1, min(b, 512 // t))
    while b % bt:
        bt -= 1
    return dict(tm=tm, pre_tile=(bt, t), scan_tile=(math.gcd(b, DECODE_SEQS_PER_STEP), t), lm=t, lg=t,
                sub=min(16, t), time_minor=False)


def kernel(x_prompt, x_sample, state_conv, state_mlstm_C, state_mlstm_n, state_mlstm_m, state_gla_S,
           g_ffn1, w_ffn1_up, w_ffn1_down, g_mix, w_in, conv_w, conv_b, w_mq, w_mk, b_if, g_mhead,
           w_a2, b_a, g_ghead, w_pa, w_pb, w_o, g_ffn2, w_ffn2_up, w_ffn2_down, g_final):
    w = _pack_weights(g_ffn1, w_ffn1_up, w_ffn1_down, g_mix, w_in, conv_w, conv_b, w_mq, w_mk, b_if,
                      g_mhead, w_a2, b_a, g_ghead, w_pa, w_pb, w_o, g_ffn2, w_ffn2_up, w_ffn2_down,
                      g_final)
    bp, tp, _ = x_prompt.shape
    bs, ts, _ = x_sample.shape
    tm_ffn = math.gcd(math.gcd(bp * tp, bs * ts), FFN_ROWS)
    x1_p, x1_s = _ffn(x_prompt.reshape(bp * tp, D_MODEL), x_sample.reshape(bs * ts, D_MODEL),
                      w["g_ffn1"], w["w_ffn1_up"], w["w_ffn1_down"], w["g_final"], final_norm=False, tm=tm_ffn)
    out_p = _mixer(x1_p, bp, tp, None, w, **_tiles(bp, tp))
    state = (state_conv[0], state_mlstm_C[0], state_mlstm_n[0], state_mlstm_m[0], state_gla_S[0])
    out_s = _mixer(x1_s, bs, ts, state, w, **_tiles(bs, ts))
    y_p, y_s = _ffn(out_p[0], out_s[0], w["g_ffn2"], w["w_ffn2_up"], w["w_ffn2_down"], w["g_final"],
                    final_norm=True, tm=tm_ffn)
    return (y_p.reshape(bp, tp, D_MODEL), y_s.reshape(bs, ts, D_MODEL)) + out_p[1:] + out_s[1:]
```

```python
import functools
import math
import types

import jax
import jax.numpy as jnp
from jax import lax
from jax.experimental import pallas as pl
from jax.experimental.pallas import tpu as pltpu

D_MODEL = 1024
M_HEADS = 4
M_HEAD_DIM = 256
M_WIDTH = 1024
CONV_W = 4
G_HEADS = 4
G_DK = 128
G_DV = 256
G_KW = 512
G_VW = 1024
G_RANK = 16
G_TAU = 16.0
D_FF = 2816
EPS = 1e-6

LANES = 128
SUBLANES = 8
BF16_ROWS = 16
VMEM_LIMIT = 56 * 1024 * 1024

F32 = jnp.float32
BF16 = jnp.bfloat16
NEG_INF = float("-inf")


def _sigmoid(x):
    return 1.0 / (1.0 + jnp.exp(-x))


def _silu(x):
    return x * _sigmoid(x)


def _log_sigmoid(x):
    return jnp.minimum(x, 0.0) - jnp.log(1.0 + jnp.exp(-jnp.abs(x)))


def _rms(x, g):
    return x * lax.rsqrt(jnp.mean(x * x, axis=-1, keepdims=True) + EPS) * g


def _dot(a, b):
    return jnp.dot(a, b, preferred_element_type=F32)


def _dot_nt(a, b):
    return lax.dot_general(a, b, (((1,), (1,)), ((), ())), preferred_element_type=F32)


def _dot_tn(a, b):
    return lax.dot_general(a, b, (((0,), (0,)), ((), ())), preferred_element_type=F32)


def _const_spec(shape):
    nd = len(shape)
    return pl.BlockSpec(shape, lambda *_: (0,) * nd, pipeline_mode=pl.Buffered(1))


FF_CHUNK = 256


W_CHUNK = (256, 512)
W_SLOTS = 8


def _load_rounded(pairs, stage_ref, sem_ref):
    r, c = W_CHUNK
    jobs = [(src, dst, i * r, j * c) for src, dst in pairs
            for i in range(src.shape[0] // r) for j in range(src.shape[1] // c)]
    n = len(jobs)

    def copy(k):
        src, _, r0, c0 = jobs[k]
        return pltpu.make_async_copy(src.at[pl.ds(r0, r), pl.ds(c0, c)],
                                     stage_ref.at[k % W_SLOTS], sem_ref.at[k % W_SLOTS])

    for k in range(min(W_SLOTS - 1, n)):
        copy(k).start()
    for k in range(n):
        if k + W_SLOTS - 1 < n:
            copy(k + W_SLOTS - 1).start()
        copy(k).wait()
        _, dst, r0, c0 = jobs[k]
        dst[r0:r0 + r, c0:c0 + c] = stage_ref[k % W_SLOTS].astype(BF16)


def _ffn_kernel(xa_ref, xb_ref, g_ref, wup_hbm, wdown_hbm, gfin_ref, oa_ref, ob_ref, h_ref,
                wup_ref, wdown_ref, stage_ref, sem_ref, *, final_norm, tiles_a):
    @pl.when(pl.program_id(0) == 0)
    def _():
        _load_rounded([(wup_hbm, wup_ref), (wdown_hbm, wdown_ref)], stage_ref, sem_ref)

    def half_step(x_ref, o_ref):
        x = x_ref[...]
        xn = _rms(x, g_ref[...]).astype(BF16)
        for c in range(D_FF // FF_CHUNK):
            a = _dot(xn, wup_ref[:, c * FF_CHUNK:(c + 1) * FF_CHUNK])
            g = _dot(xn, wup_ref[:, D_FF + c * FF_CHUNK:D_FF + (c + 1) * FF_CHUNK])
            h_ref[:, c * FF_CHUNK:(c + 1) * FF_CHUNK] = (_silu(g) * a).astype(BF16)
        y = x + 0.5 * _dot(h_ref[...], wdown_ref[...])
        o_ref[...] = _rms(y, gfin_ref[...]) if final_norm else y

    on_a = pl.program_id(0) < tiles_a
    pl.when(on_a)(lambda: half_step(xa_ref, oa_ref))
    pl.when(jnp.logical_not(on_a))(lambda: half_step(xb_ref, ob_ref))


def _ffn(xa, xb, g, wup, wdown, gfin, *, final_norm, tm):
    na, nb = xa.shape[0], xb.shape[0]
    assert na % tm == 0 and nb % tm == 0
    tiles_a, tiles_b = na // tm, nb // tm
    spec_a = pl.BlockSpec((tm, D_MODEL), lambda i: (jnp.minimum(i, tiles_a - 1), 0))
    spec_b = pl.BlockSpec((tm, D_MODEL), lambda i: (jnp.maximum(i - tiles_a, 0), 0),
                          **(dict(pipeline_mode=pl.Buffered(1)) if tiles_b == 1 else {}))
    return pl.pallas_call(
        functools.partial(_ffn_kernel, final_norm=final_norm, tiles_a=tiles_a),
        grid=(tiles_a + tiles_b,),
        in_specs=[
            spec_a, spec_b,
            _const_spec((1, D_MODEL)),
            pl.BlockSpec(memory_space=pl.ANY),
            pl.BlockSpec(memory_space=pl.ANY),
            _const_spec((1, D_MODEL)),
        ],
        out_specs=[spec_a, spec_b],
        out_shape=[jax.ShapeDtypeStruct((na, D_MODEL), F32), jax.ShapeDtypeStruct((nb, D_MODEL), F32)],
        scratch_shapes=[
            pltpu.VMEM((tm, D_FF), BF16),
            pltpu.VMEM((D_MODEL, 2 * D_FF), BF16),
            pltpu.VMEM((D_FF, D_MODEL), BF16),
            pltpu.VMEM((W_SLOTS,) + W_CHUNK, F32),
            pltpu.SemaphoreType.DMA((W_SLOTS,)),
        ],
        compiler_params=pltpu.CompilerParams(
            dimension_semantics=("arbitrary",), vmem_limit_bytes=VMEM_LIMIT),
        name="ffn_final" if final_norm else "ffn",
    )(xa, xb, g, wup.astype(F32), wdown.astype(F32), gfin)


PRE_U, PRE_V, PRE_QG, PRE_KG, PRE_VG = 0, 1024, 2048, 2560, 3072
PRE_WIDTH = 4096
A_LANE0 = 8


def _pre_kernel(x_ref, cst_ref, g_ref, w_ref, ws_ref, cw_ref, cb_ref, wq_ref, wk_ref, bif_ref,
                wa2_ref, ba_ref,
                qm_ref, km_ref, vm_ref, gif_ref, qg_ref, kg_ref, la_ref, vg_ref, cout_ref,
                cbuf_ref, *, bt, tb, sub, lg, time_minor):
    t = pl.program_id(1)
    m = bt * sub

    @pl.when(t == 0)
    def _():
        cbuf_ref[:, 0:SUBLANES, :] = cst_ref[...]

    @pl.when(t > 0)
    def _():
        cbuf_ref[:, 0:SUBLANES, :] = cbuf_ref[:, tb:tb + SUBLANES, :]

    for j in range(tb // sub):
        r0 = j * sub
        rows = slice(r0, r0 + sub)
        to_rows = lambda y, w: y.reshape(bt, sub, w)
        xn = _rms(x_ref[:, rows, :].reshape(m, D_MODEL), g_ref[...]).astype(BF16)
        small = _dot(xn, ws_ref[...])

        u = _dot(xn, w_ref[:, PRE_U:PRE_U + M_WIDTH])
        cbuf_ref[:, SUBLANES + r0:SUBLANES + r0 + sub, :] = to_rows(u, M_WIDTH)
        c = cb_ref[...].reshape(1, 1, M_WIDTH)
        for tap in range(CONV_W):
            off = SUBLANES - (CONV_W - 1) + tap + r0
            c = c + cbuf_ref[:, off:off + sub, :] * cw_ref[tap:tap + 1, :].reshape(1, 1, M_WIDTH)
        ch = _silu(c).reshape(m, M_WIDTH).astype(BF16)

        vm = _dot(xn, w_ref[:, PRE_V:PRE_V + M_WIDTH])
        gates = small + bif_ref[...]
        lane = lax.broadcasted_iota(jnp.int32, gates.shape, 1)
        gif = jnp.where(lane < M_HEADS, gates, jnp.where(lane < 2 * M_HEADS, _log_sigmoid(gates), 0.0))
        if time_minor:
            b = pltpu.roll(_scan_rows(gif, jnp.add, 0.0), LANES - M_HEADS, axis=1)
            g = gif - b
            gmax = _scan_rows(g, jnp.maximum, NEG_INF)
            packed = jnp.where(lane < M_HEADS, g,
                               jnp.where(lane < 2 * M_HEADS, pltpu.roll(gmax, M_HEADS, axis=1),
                                         jnp.where(lane < 3 * M_HEADS, pltpu.roll(b, 2 * M_HEADS, axis=1), 0.0)))
            gif_ref[0, j] = jnp.transpose(packed)[0:2 * SUBLANES, :]
        else:
            gif_ref[:, rows, :] = to_rows(gif, LANES)
            vm_ref[:, rows, :] = to_rows(vm.astype(BF16), M_WIDTH)

        def decay_columns(h):
            cols = slice(h * G_DK, (h + 1) * G_DK)
            a2 = _dot(small.astype(BF16), wa2_ref[:, cols]) + ba_ref[:, cols]
            la_ref[:, rows, cols] = to_rows(
                _scan_rows(_log_sigmoid(a2) * (1.0 / G_TAU), jnp.add, 0.0, segment=lg), G_DK)

        qg_ref[:, rows, :] = to_rows(
            (_dot(xn, w_ref[:, PRE_QG:PRE_QG + G_KW]) * (G_DK ** -0.5)).astype(BF16), G_KW)
        decay_columns(0)
        kg_ref[:, rows, :] = to_rows(_dot(xn, w_ref[:, PRE_KG:PRE_KG + G_KW]).astype(BF16), G_KW)
        decay_columns(1)
        half = G_VW // 2
        for p in range(2):
            vg_ref[:, rows, p * half:(p + 1) * half] = to_rows(
                _dot(xn, w_ref[:, PRE_VG + p * half:PRE_VG + (p + 1) * half]).astype(BF16), half)
            decay_columns(2 + p)

        for h in range(M_HEADS):
            sl = slice(h * M_HEAD_DIM, (h + 1) * M_HEAD_DIM)
            qh = _dot(ch[:, sl], wq_ref[h])
            if time_minor:
                qm_ref[0, j, sl, :] = jnp.transpose(qh).astype(BF16)
                vm_ref[0, j, sl, :] = jnp.transpose(vm[:, sl]).astype(BF16)
            else:
                qm_ref[:, rows, sl] = to_rows(qh.astype(BF16), M_HEAD_DIM)
            km_ref[:, rows, sl] = to_rows(
                (_dot(ch[:, sl], wk_ref[h]) * (M_HEAD_DIM ** -0.5)).astype(BF16), M_HEAD_DIM)
    cout_ref[...] = cbuf_ref[:, tb:tb + SUBLANES, :]


def _pre(x, cst, g, w_pre, w_small, conv_w, conv_b, wq, wk, bif, wa2, ba, *, bt, tb, sub, lg, time_minor):
    b, t, _ = x.shape
    assert b % bt == 0 and t % tb == 0 and tb % sub == 0 and sub % SUBLANES == 0 and sub % lg == 0
    assert not time_minor or (bt == 1 and sub % LANES == 0)
    tok = lambda w: pl.BlockSpec((bt, tb, w), lambda i, j: (i, j, 0))
    per_seq = pl.BlockSpec((bt, SUBLANES, M_WIDTH), lambda i, j: (i, 0, 0))
    sds = lambda w, dt: jax.ShapeDtypeStruct((b, t, w), dt)
    if time_minor:
        tok_t = lambda w: pl.BlockSpec((1, tb // sub, w, sub), lambda i, j: (i, j, 0, 0))
        sds_t = lambda w, dt: jax.ShapeDtypeStruct((b, t // sub, w, sub), dt)
        qv_spec, qv_shape = tok_t(M_WIDTH), sds_t(M_WIDTH, BF16)
        gif_spec, gif_shape = tok_t(2 * SUBLANES), sds_t(2 * SUBLANES, F32)
    else:
        qv_spec, qv_shape = tok(M_WIDTH), sds(M_WIDTH, BF16)
        gif_spec, gif_shape = tok(LANES), sds(LANES, F32)
    return pl.pallas_call(
        functools.partial(_pre_kernel, bt=bt, tb=tb, sub=sub, lg=lg, time_minor=time_minor),
        grid=(b // bt, t // tb),
        in_specs=[
            tok(D_MODEL), per_seq,
            _const_spec((1, D_MODEL)),
            _const_spec((D_MODEL, PRE_WIDTH)),
            _const_spec((D_MODEL, LANES)),
            _const_spec((CONV_W, M_WIDTH)),
            _const_spec((1, M_WIDTH)),
            _const_spec((M_HEADS, M_HEAD_DIM, M_HEAD_DIM)),
            _const_spec((M_HEADS, M_HEAD_DIM, M_HEAD_DIM)),
            _const_spec((1, LANES)),
            _const_spec((LANES, G_KW)),
            _const_spec((1, G_KW)),
        ],
        out_specs=[qv_spec, tok(M_WIDTH), qv_spec, gif_spec, tok(G_KW), tok(G_KW),
                   tok(G_KW), tok(G_VW), per_seq],
        out_shape=[qv_shape, sds(M_WIDTH, BF16), qv_shape, gif_shape,
                   sds(G_KW, BF16), sds(G_KW, BF16), sds(G_KW, F32), sds(G_VW, BF16),
                   jax.ShapeDtypeStruct((b, SUBLANES, M_WIDTH), F32)],
        scratch_shapes=[pltpu.VMEM((bt, tb + SUBLANES, M_WIDTH), F32)],
        compiler_params=pltpu.CompilerParams(
            dimension_semantics=("parallel", "arbitrary"), vmem_limit_bytes=VMEM_LIMIT),
        name="mixer_pre",
    )(x, cst, g, w_pre, w_small, conv_w, conv_b, wq, wk, bif, wa2, ba)


def _scan_rows(x, op, fill, segment=None):
    n = segment or x.shape[0]
    row = lax.broadcasted_iota(jnp.int32, x.shape, 0) % n
    s = 1
    while s < n:
        x = op(x, jnp.where(row >= s, pltpu.roll(x, s, axis=0), fill))
        s *= 2
    return x


def _col_of_row(row):
    return jnp.transpose(jnp.broadcast_to(row, (SUBLANES, LANES)))[:, 0:1]


def _mlstm_chunk_tm(seqs, *, lm):
    reps = lm // LANES
    head_row = lax.broadcasted_iota(jnp.int32, (SUBLANES, LANES), 0) < M_HEADS
    src = lax.broadcasted_iota(jnp.int32, (lm, lm), 0)
    dst = lax.broadcasted_iota(jnp.int32, (lm, lm), 1)
    causal = src <= dst
    sl = [slice(h * M_HEAD_DIM, (h + 1) * M_HEAD_DIM) for h in range(M_HEADS)]
    row = lambda x, h: x[h:h + 1, :]
    gates = []
    for sq in seqs:
        g = sq.gift[0:SUBLANES]
        gmax = pltpu.roll(g, M_HEADS, axis=0)
        b = sq.gift[SUBLANES:2 * SUBLANES]
        m_prev = jnp.concatenate([sq.m_src[...]] * reps, axis=1)
        mx = jnp.maximum(m_prev, gmax)
        mx_last = jnp.broadcast_to(mx[:, lm - 1:lm], (SUBLANES, lm))
        gates.append(dict(
            mx=mx,
            inter=jnp.exp(m_prev - mx),
            nfloor=jnp.exp(-(b + mx)),
            w_in=jnp.exp(g - mx_last),
            decay=jnp.exp(m_prev - mx_last),
            g_col=jnp.transpose(g)))
        m_new = jnp.broadcast_to(b[:, lm - 1:lm], (SUBLANES, lm)) + mx_last
        sq.m_dst[...] = jnp.where(head_row, m_new[:, :LANES], 0.0)
    units = [(sq, gt, h) for sq, gt in zip(seqs, gates) for h in range(M_HEADS)]
    c_old = [sq.c_src[h] for sq, _, h in units]
    n_old = [sq.n_src[h] for sq, _, h in units]
    cn = [jnp.concatenate([c_old[u].astype(BF16),
                           jnp.broadcast_to(n_old[u].astype(BF16), (BF16_ROWS, M_HEAD_DIM))], axis=0)
          for u in range(len(units))]
    cq = [_dot(cn[u], sq.qt[sl[h], :]) for u, (sq, _, h) in enumerate(units)]
    dm = [jnp.exp(jnp.where(causal, gt["g_col"][:, h:h + 1] - row(gt["mx"], h), NEG_INF))
          for _, gt, h in units]
    st = [_dot(sq.k[:, sl[h]], sq.qt[sl[h], :]) * dm[u]
          for u, (sq, _, h) in enumerate(units)]
    num = [row(gt["inter"], h) * cq[u][:M_HEAD_DIM] + _dot(sq.vt[sl[h], :], st[u].astype(BF16))
           for u, (sq, gt, h) in enumerate(units)]
    for u, (sq, gt, h) in enumerate(units):
        den = (row(gt["inter"], h) * cq[u][M_HEAD_DIM:M_HEAD_DIM + 1]
               + jnp.sum(st[u], axis=0, keepdims=True))
        rcp = 1.0 / jnp.maximum(jnp.abs(den), row(gt["nfloor"], h))
        ms = jnp.mean(num[u] * num[u], axis=0, keepdims=True)
        sq.hm_store(h, num[u] * (rcp * lax.rsqrt(rcp * rcp * ms + EPS)))
    for u, (sq, gt, h) in enumerate(units):
        w_h = row(gt["w_in"], h)
        dec_h = jnp.concatenate([row(gt["decay"], h)[:, :LANES]] * (M_HEAD_DIM // LANES), axis=1)
        wv = jnp.concatenate([(sq.vt[sl[h], :].astype(F32) * w_h).astype(BF16),
                              jnp.broadcast_to(w_h.astype(BF16), (BF16_ROWS, lm))], axis=0)
        upd = _dot(wv, sq.k[:, sl[h]])
        sq.c_dst[h] = dec_h * c_old[u] + upd[:M_HEAD_DIM]
        sq.n_dst[h] = dec_h * n_old[u] + upd[M_HEAD_DIM:M_HEAD_DIM + 1]


def _mlstm_chunk(seqs, *, lm):
    row = lax.broadcasted_iota(jnp.int32, (lm, lm), 0)
    col = lax.broadcasted_iota(jnp.int32, (lm, lm), 1)
    causal = col <= row
    sl = [slice(h * M_HEAD_DIM, (h + 1) * M_HEAD_DIM) for h in range(M_HEADS)]
    gates = []
    for sq in seqs:
        cs = _scan_rows(sq.gif, jnp.add, 0.0)
        b = pltpu.roll(cs, LANES - M_HEADS, axis=1)
        g = sq.gif - b
        m_prev = sq.m_src[...]
        mx = jnp.maximum(m_prev, _scan_rows(g, jnp.maximum, NEG_INF))
        mx_last = mx[lm - 1:lm, :]
        gates.append(dict(
            mx=mx,
            inter=jnp.exp(m_prev - mx),
            nfloor=jnp.exp(-(b + mx)),
            g_t=jnp.transpose(g),
            w_in=jnp.exp(g - mx_last),
            decay=jnp.exp(m_prev - mx_last)))
        sq.m_dst[...] = b[lm - 1:lm, :] + mx_last
    units = [(sq, gt, h) for sq, gt in zip(seqs, gates) for h in range(M_HEADS)]
    col_of = lambda x, h: x[:, h:h + 1]
    c_old = [sq.c_src[h] for sq, _, h in units]
    n_old = [sq.n_src[h] for sq, _, h in units]
    dm = [jnp.exp(jnp.where(causal, gt["g_t"][h:h + 1, :] - col_of(gt["mx"], h), NEG_INF))
          for _, gt, h in units]
    qn = [jnp.sum(sq.q[:, sl[h]].astype(F32) * n_old[u], axis=-1, keepdims=True)
          for u, (sq, _, h) in enumerate(units)]
    s = [_dot_nt(sq.q[:, sl[h]], sq.k[:, sl[h]]) * dm[u] for u, (sq, _, h) in enumerate(units)]
    num = [col_of(gt["inter"], h) * _dot_nt(sq.q[:, sl[h]], c_old[u].astype(BF16))
           + _dot(s[u].astype(BF16), sq.v[:, sl[h]]) for u, (sq, gt, h) in enumerate(units)]
    for u, (sq, gt, h) in enumerate(units):
        den = col_of(gt["inter"], h) * qn[u] + jnp.sum(s[u], axis=-1, keepdims=True)
        rcp = 1.0 / jnp.maximum(jnp.abs(den), col_of(gt["nfloor"], h))
        ms = jnp.mean(num[u] * num[u], axis=-1, keepdims=True)
        sq.hm_store(h, num[u] * (rcp * lax.rsqrt(rcp * rcp * ms + EPS)))
    for u, (sq, gt, h) in enumerate(units):
        w_h = col_of(gt["w_in"], h)
        dec_h = col_of(gt["decay"], h)
        wv = (w_h * sq.v[:, sl[h]].astype(F32)).astype(BF16)
        sq.c_dst[h] = dec_h * c_old[u] + _dot_tn(wv, sq.k[:, sl[h]])
        sq.n_dst[h] = dec_h * n_old[u] + jnp.sum(w_h * sq.k[:, sl[h]].astype(F32), axis=0, keepdims=True)


GLA_SAFE_SPAN = 40.0


def _gla_scores_guarded(qh, kh, a, *, lg, sub):
    row = lax.broadcasted_iota(jnp.int32, (lg, lg), 0)
    col = lax.broadcasted_iota(jnp.int32, (lg, lg), 1)
    att = jnp.zeros((lg, lg), F32)
    c = sub
    while c < lg:
        nblk = lg // (2 * c)
        a_ref = jnp.concatenate(
            [jnp.broadcast_to(a[2 * c * i + c - 1:2 * c * i + c, :], (2 * c, G_DK))
             for i in range(nblk)], axis=0)
        qt = (qh * jnp.exp(jnp.minimum(a - a_ref, 0.0))).astype(BF16)
        kt = (kh * jnp.exp(jnp.minimum(a_ref - a, 0.0))).astype(BF16)
        valid = (row // (2 * c) == col // (2 * c)) & (row % (2 * c) >= c) & (col % (2 * c) < c)
        att = att + jnp.where(valid, _dot_nt(qt, kt), 0.0)
        c *= 2
    rows = lax.broadcasted_iota(jnp.int32, (sub, lg), 0)
    cols = lax.broadcasted_iota(jnp.int32, (sub, lg), 1)
    blocks = []
    for blk in range(lg // sub):
        r0 = blk * sub
        a_b = a[r0:r0 + sub, :]
        q_b = qh[r0:r0 + sub, :]
        acc = jnp.zeros((sub, lg), F32)
        for s in range(sub):
            e = jnp.exp(jnp.minimum(a_b - a[r0 + s:r0 + s + 1, :], 0.0))
            p = jnp.sum(q_b * e * kh[r0 + s:r0 + s + 1, :], axis=-1, keepdims=True)
            acc = jnp.where((cols == r0 + s) & (rows >= s), p, acc)
        blocks.append(acc)
    return att + jnp.concatenate(blocks, axis=0)


def _gla_chunk_guarded(seqs, gh_ref, *, lg, sub):
    ks = [slice(h * G_DK, (h + 1) * G_DK) for h in range(G_HEADS)]
    vs = [slice(h * G_DV, (h + 1) * G_DV) for h in range(G_HEADS)]
    pre = []
    for sq in seqs:
        a_last = sq.a[lg - 1:lg, :]
        qf = sq.qg.astype(F32)
        kf = sq.kg.astype(F32)
        pre.append(dict(a=sq.a, a_last=a_last, qf=qf, kf=kf,
                        qs=(qf * jnp.exp(sq.a)).astype(BF16),
                        kl=(kf * jnp.exp(a_last - sq.a)).astype(BF16)))
    units = [(sq, p, h) for sq, p in zip(seqs, pre) for h in range(G_HEADS)]
    s_old = [sq.s_src[h] for sq, _, h in units]
    att = [_gla_scores_guarded(p["qf"][:, ks[h]], p["kf"][:, ks[h]], p["a"][:, ks[h]], lg=lg, sub=sub)
           for _, p, h in units]
    o = [_dot(p["qs"][:, ks[h]], s_old[u].astype(BF16)) + _dot(att[u].astype(BF16), sq.vg[:, vs[h]])
         for u, (sq, p, h) in enumerate(units)]
    for u, (sq, _, h) in enumerate(units):
        sq.og_store(h, _rms(o[u], gh_ref[:, vs[h]]))
    for u, (sq, p, h) in enumerate(units):
        sq.s_dst[h] = (jnp.exp(_col_of_row(p["a_last"][:, ks[h]])) * s_old[u]
                       + _dot_tn(p["kl"][:, ks[h]], sq.vg[:, vs[h]]))


def _gla_block_direct(seqs, gh_ref, *, tb, lg):
    ks = [slice(h * G_DK, (h + 1) * G_DK) for h in range(G_HEADS)]
    vs = [slice(h * G_DV, (h + 1) * G_DV) for h in range(G_HEADS)]
    chunks = [slice(c * lg, (c + 1) * lg) for c in range(tb // lg)]
    row = lax.broadcasted_iota(jnp.int32, (lg, lg), 0)
    col = lax.broadcasted_iota(jnp.int32, (lg, lg), 1)
    causal = col <= row
    pre = []
    for sq in seqs:
        a_last = [sq.a[r.stop - 1:r.stop, :] for r in chunks]
        a_last_rows = jnp.concatenate([jnp.broadcast_to(al, (lg, G_KW)) for al in a_last], axis=0)
        qf = sq.qg.astype(F32)
        kf = sq.kg.astype(F32)
        pre.append(dict(a_last=a_last,
                        qs=(qf * jnp.exp(sq.a)).astype(BF16),
                        kt=(kf * jnp.exp(-sq.a)).astype(BF16),
                        kl=(kf * jnp.exp(a_last_rows - sq.a)).astype(BF16)))
    units = [(sq, p, h) for sq, p in zip(seqs, pre) for h in range(G_HEADS)]
    att = [[jnp.where(causal, _dot_nt(p["qs"][r, ks[h]], p["kt"][r, ks[h]]), 0.0).astype(BF16)
            for r in chunks] for _, p, h in units]
    s_cur = [sq.s_src[h] for sq, _, h in units]
    for c, r in enumerate(chunks):
        o = [_dot(p["qs"][r, ks[h]], s_cur[u].astype(BF16)) + _dot(att[u][c], sq.vg[r, vs[h]])
             for u, (sq, p, h) in enumerate(units)]
        for u, (sq, _, h) in enumerate(units):
            sq.og_rows_store(r, h, _rms(o[u], gh_ref[:, vs[h]]))
        s_cur = [jnp.exp(_col_of_row(p["a_last"][c][:, ks[h]])) * s_cur[u]
                 + _dot_tn(p["kl"][r, ks[h]], sq.vg[r, vs[h]])
                 for u, (sq, p, h) in enumerate(units)]
    for u, (sq, _, h) in enumerate(units):
        sq.s_dst[h] = s_cur[u]


def _scan_kernel(*refs, bt, tb, lm, lg, sub, has_state, time_minor, single_step):
    n_in = 9 + (4 if has_state else 0)
    qm_ref, km_ref, vm_ref, gif_ref, qg_ref, kg_ref, la_ref, vg_ref, ggh_ref = refs[:9]
    state_in = refs[9:n_in]
    hm_ref, og_ref = refs[n_in:n_in + 2]
    state_out = refs[n_in + 2:n_in + 6]
    scratch = refs[n_in + 6:]
    t = pl.program_id(1)
    nt = pl.num_programs(1)

    streamed = has_state and single_step
    src = state_in if streamed else scratch
    dst = state_out if streamed else scratch
    if not streamed:
        @pl.when(t == 0)
        def _():
            for i, ref in enumerate(scratch):
                ref[...] = state_in[i][...] if has_state else jnp.zeros(ref.shape, F32)

    def seq_records(bi, m_rows, g_rows):
        def hm_store(h, val):
            if time_minor:
                hm_ref[bi, 0, h * M_HEAD_DIM:(h + 1) * M_HEAD_DIM, :] = val
            else:
                hm_ref[bi, m_rows, h * M_HEAD_DIM:(h + 1) * M_HEAD_DIM] = val

        def og_rows_store(rows, h, val):
            og_ref[bi, rows, h * G_DV:(h + 1) * G_DV] = val

        return types.SimpleNamespace(
            bi=bi, hm_store=hm_store, og_rows_store=og_rows_store,
            og_store=functools.partial(og_rows_store, g_rows),
            c_src=src[0].at[bi], n_src=src[1].at[bi], m_src=src[2].at[bi], s_src=src[3].at[bi],
            c_dst=dst[0].at[bi], n_dst=dst[1].at[bi], m_dst=dst[2].at[bi], s_dst=dst[3].at[bi])

    def m_body(ci, carry):
        rows = pl.ds(pl.multiple_of(ci * lm, lm), lm)
        seqs = [seq_records(bi, rows, None) for bi in range(bt)]
        if time_minor:
            for sq in seqs:
                sq.qt, sq.k, sq.vt = qm_ref[sq.bi, 0], km_ref[sq.bi, rows, :], vm_ref[sq.bi, 0]
                sq.gift = gif_ref[sq.bi, 0]
            _mlstm_chunk_tm(seqs, lm=lm)
        else:
            for sq in seqs:
                sq.q, sq.k, sq.v = qm_ref[sq.bi, rows, :], km_ref[sq.bi, rows, :], vm_ref[sq.bi, rows, :]
                sq.gif = gif_ref[sq.bi, rows, :]
            _mlstm_chunk(seqs, lm=lm)
        return carry

    def gla_records(rows):
        seqs = [seq_records(bi, None, rows) for bi in range(bt)]
        for sq in seqs:
            sq.qg, sq.kg, sq.vg = qg_ref[sq.bi, rows, :], kg_ref[sq.bi, rows, :], vg_ref[sq.bi, rows, :]
            sq.a = la_ref[sq.bi, rows, :]
        return seqs

    def g_body(ci, carry):
        rows = pl.ds(pl.multiple_of(ci * lg, lg), lg)
        _gla_chunk_guarded(gla_records(rows), ggh_ref, lg=lg, sub=sub)
        return carry

    def loop(body, n):
        if n == 1:
            body(0, 0)
        else:
            lax.fori_loop(0, n, body, 0)

    loop(m_body, tb // lm)
    if lg <= sub:
        loop(g_body, tb // lg)
    else:
        chunk_decay = [la_ref[:, c * lg + lg - 1:(c + 1) * lg, :] for c in range(tb // lg)]
        span_ok = jnp.min(functools.reduce(jnp.minimum, chunk_decay)) >= -GLA_SAFE_SPAN

        @pl.when(span_ok)
        def _():
            _gla_block_direct(gla_records(slice(0, tb)), ggh_ref, tb=tb, lg=lg)

        @pl.when(jnp.logical_not(span_ok))
        def _():
            loop(g_body, tb // lg)

    if not streamed:
        @pl.when(t == nt - 1)
        def _():
            for out, ref in zip(state_out, scratch):
                out[...] = ref[...]


def _scan(qm, km, vm, gif, qg, kg, la, vg, ggh, state, *, bt, tb, lm, lg, sub, time_minor):
    b, t, _ = km.shape
    assert b % bt == 0 and t % tb == 0 and tb % lm == 0 and tb % lg == 0 and lg % sub == 0
    has_state = state is not None
    assert not (time_minor and has_state) and (not time_minor or (lm % LANES == 0 and tb == lm))
    tok = lambda w: pl.BlockSpec((bt, tb, w), lambda i, j: (i, j, 0))
    tok_t = lambda w: pl.BlockSpec((bt, 1, w, tb), lambda i, j: (i, j, 0, 0))
    c_shape = (M_HEADS, M_HEAD_DIM, M_HEAD_DIM)
    n_shape = (M_HEADS, 1, M_HEAD_DIM)
    m_shape = (SUBLANES, LANES) if time_minor else (1, LANES)
    s_shape = (G_HEADS, G_DK, G_DV)
    st_shapes = [(bt,) + shp for shp in (c_shape, n_shape, m_shape, s_shape)]
    st_specs = [pl.BlockSpec(shp, lambda i, j, nd=len(shp): (i,) + (0,) * (nd - 1)) for shp in st_shapes]
    single_step = t == tb
    carried = not (has_state and single_step)
    if time_minor:
        qv_spec, gif_spec = tok_t(M_WIDTH), tok_t(2 * SUBLANES)
        hm_spec, hm_shape = tok_t(M_WIDTH), (b, t // tb, M_WIDTH, tb)
    else:
        qv_spec, gif_spec = tok(M_WIDTH), tok(LANES)
        hm_spec, hm_shape = tok(M_WIDTH), (b, t, M_WIDTH)
    in_specs = [qv_spec, tok(M_WIDTH), qv_spec, gif_spec, tok(G_KW), tok(G_KW),
                tok(G_KW), tok(G_VW), _const_spec((1, G_VW))]
    args = [qm, km, vm, gif, qg, kg, la, vg, ggh]
    if has_state:
        in_specs += st_specs
        args += list(state)
    return pl.pallas_call(
        functools.partial(_scan_kernel, bt=bt, tb=tb, lm=lm, lg=lg, sub=sub, has_state=has_state,
                          time_minor=time_minor, single_step=single_step),
        grid=(b // bt, t // tb),
        in_specs=in_specs,
        out_specs=[hm_spec, tok(G_VW)] + st_specs,
        out_shape=[jax.ShapeDtypeStruct(hm_shape, F32),
                   jax.ShapeDtypeStruct((b, t, G_VW), F32)]
                  + [jax.ShapeDtypeStruct((b,) + shp[1:], F32) for shp in st_shapes],
        scratch_shapes=[pltpu.VMEM(shp, F32) for shp in st_shapes] if carried else [],
        compiler_params=pltpu.CompilerParams(
            dimension_semantics=("parallel", "arbitrary"), vmem_limit_bytes=VMEM_LIMIT),
        name="mixer_scan_state" if has_state else "mixer_scan",
    )(*args)


POST_O, POST_R, POST_A, POST_B = 0, 1024, 2048, 3072


def _post_kernel(x_ref, hm_ref, og_ref, g_ref, gmh_ref, w_ref, wpa_ref, wpb_ref, wo_ref, o_ref, *,
                 time_minor):
    x = x_ref[...]
    xn = _rms(x, g_ref[...]).astype(BF16)
    gate = lambda off: _dot(xn, w_ref[:, off:off + D_MODEL])
    if time_minor:
        hm = jnp.concatenate(
            [jnp.concatenate([jnp.transpose(hm_ref[0, j, h * M_HEAD_DIM:(h + 1) * M_HEAD_DIM, :])
                              for h in range(M_HEADS)], axis=1)
             for j in range(hm_ref.shape[1])], axis=0)
    else:
        hm = hm_ref[...]
    hm = (_sigmoid(gate(POST_O)) * (hm * gmh_ref[...])).astype(BF16)
    og = (_silu(gate(POST_R)) * og_ref[...]).astype(BF16)
    y = (_sigmoid(gate(POST_A)) * _dot(hm, wpa_ref[...])
         + _sigmoid(gate(POST_B)) * _dot(og, wpb_ref[...]))
    o_ref[...] = x + _dot(y.astype(BF16), wo_ref[...])


def _post(x2d, hm, og2d, g, gmh, w_post, wpa, wpb, wo, *, tm, time_minor):
    n = x2d.shape[0]
    assert n % tm == 0
    tok = pl.BlockSpec((tm, D_MODEL), lambda i: (i, 0))
    if time_minor:
        nblk, blk = hm.shape[1], hm.shape[3]
        assert tm % blk == 0 and (nblk * blk) % tm == 0
        per_seq = nblk * blk // tm
        hm_spec = pl.BlockSpec((1, tm // blk, M_WIDTH, blk), lambda i: (i // per_seq, i % per_seq, 0, 0))
    else:
        hm_spec = tok
    return pl.pallas_call(
        functools.partial(_post_kernel, time_minor=time_minor),
        grid=(n // tm,),
        in_specs=[tok, hm_spec, tok, _const_spec((1, D_MODEL)), _const_spec((1, M_WIDTH)),
                  _const_spec((D_MODEL, 4 * D_MODEL)),
                  _const_spec((M_WIDTH, D_MODEL)), _const_spec((G_VW, D_MODEL)),
                  _const_spec((D_MODEL, D_MODEL))],
        out_specs=tok,
        out_shape=jax.ShapeDtypeStruct((n, D_MODEL), F32),
        compiler_params=pltpu.CompilerParams(
            dimension_semantics=("parallel",), vmem_limit_bytes=VMEM_LIMIT),
        name="mixer_post",
    )(x2d, hm, og2d, g, gmh, w_post, wpa, wpb, wo)


def _pack_weights(g_ffn1, w_ffn1_up, w_ffn1_down, g_mix, w_in, conv_w, conv_b, w_mq, w_mk, b_if,
                  g_mhead, w_a2, b_a, g_ghead, w_pa, w_pb, w_o, g_ffn2, w_ffn2_up, w_ffn2_down,
                  g_final):
    l = 0
    row = lambda v: v.reshape(1, -1).astype(F32)
    edges = [0]
    for wd in (M_WIDTH, M_WIDTH, M_WIDTH, 2 * M_HEADS, G_KW, G_KW, G_VW, G_VW, G_RANK, D_MODEL, D_MODEL):
        edges.append(edges[-1] + wd)
    seg = lambda i: w_in[l][:, edges[i]:edges[i + 1]]
    u_m, v_m, o_m, if_m, q_g, k_g, v_g, r_g, a_g, g_a, g_b = (seg(i) for i in range(11))
    w_pre = jnp.concatenate([u_m, v_m, q_g, k_g, v_g], axis=1).astype(BF16)
    w_small = jnp.concatenate(
        [if_m, a_g, jnp.zeros((D_MODEL, LANES - 2 * M_HEADS - G_RANK), F32)], axis=1).astype(BF16)
    w_post = jnp.concatenate([o_m, r_g, g_a, g_b], axis=1).astype(BF16)
    bif = jnp.concatenate([b_if[l].reshape(-1), jnp.zeros((LANES - 2 * M_HEADS,), F32)]).reshape(1, LANES)
    wa2 = jnp.zeros((LANES, G_KW), F32).at[A_LANE0:A_LANE0 + G_RANK].set(w_a2[l]).astype(BF16)
    return dict(
        g_ffn1=row(g_ffn1[l]), w_ffn1_up=w_ffn1_up[l], w_ffn1_down=w_ffn1_down[l],
        g_mix=row(g_mix[l]), w_pre=w_pre, w_small=w_small, w_post=w_post,
        conv_w=conv_w[l].astype(F32), conv_b=row(conv_b[l]),
        w_mq=w_mq[l].astype(BF16), w_mk=w_mk[l].astype(BF16), bif=bif,
        g_mhead=row(g_mhead[l]), wa2=wa2, b_a=row(b_a[l]), g_ghead=row(g_ghead[l]),
        w_pa=w_pa[l].astype(BF16), w_pb=w_pb[l].astype(BF16), w_o=w_o[l].astype(BF16),
        g_ffn2=row(g_ffn2[l]), w_ffn2_up=w_ffn2_up[l], w_ffn2_down=w_ffn2_down[l],
        g_final=row(g_final))


def _mixer(x1, b, t, state, w, *, tm, pre_tile, scan_tile, lm, lg, sub, time_minor):
    n = b * t
    if state is None:
        cst = jnp.zeros((b, SUBLANES, M_WIDTH), F32)
        scan_state = None
    else:
        conv0, c0, n0, m0, s0 = state
        cst = jnp.concatenate(
            [jnp.zeros((b, SUBLANES - (CONV_W - 1), M_WIDTH), F32), conv0.astype(F32)], axis=1)
        m0p = jnp.concatenate([m0, jnp.zeros((b, LANES - M_HEADS), F32)], axis=1).reshape(b, 1, LANES)
        scan_state = (c0, n0.reshape(b, M_HEADS, 1, M_HEAD_DIM), m0p, s0)
    qm, km, vm, gif, qg, kg, la, vg, cout = _pre(
        x1.reshape(b, t, D_MODEL), cst, w["g_mix"], w["w_pre"], w["w_small"], w["conv_w"], w["conv_b"],
        w["w_mq"], w["w_mk"], w["bif"], w["wa2"], w["b_a"], bt=pre_tile[0], tb=pre_tile[1], sub=lm, lg=lg,
        time_minor=time_minor)
    hm, og, c_new, n_new, m_new, s_new = _scan(
        qm, km, vm, gif, qg, kg, la, vg, w["g_ghead"], scan_state,
        bt=scan_tile[0], tb=scan_tile[1], lm=lm, lg=lg, sub=sub, time_minor=time_minor)
    x2 = _post(x1, hm if time_minor else hm.reshape(n, M_WIDTH), og.reshape(n, G_VW), w["g_mix"],
               w["g_mhead"], w["w_post"], w["w_pa"], w["w_pb"], w["w_o"], tm=tm, time_minor=time_minor)
    m_heads = m_new[:, :M_HEADS, 0] if time_minor else m_new[:, 0, :M_HEADS]
    return (x2,
            cout[:, SUBLANES - (CONV_W - 1):, :][None],
            c_new[None],
            n_new.reshape(b, M_HEADS, M_HEAD_DIM)[None],
            m_heads[None],
            s_new[None])


FFN_ROWS = 1024
DECODE_SEQS_PER_STEP = 4
PREFILL_SEQS_PER_STEP = 2


def _tiles(b, t):
    n = b * t
    tm = math.gcd(n, 512)
    if t % 256 == 0:
        return dict(tm=math.gcd(t, 512), pre_tile=(1, math.gcd(t, 512)), scan_tile=(math.gcd(b, PREFILL_SEQS_PER_STEP), 256),
                    lm=256, lg=128, sub=16, time_minor=True)
    assert t <= 64 and t % SUBLANES == 0
    bt = max(1, min(b, 512 // t))
    while b % bt:
        bt -= 1
    return dict(tm=tm, pre_tile=(bt, t), scan_tile=(math.gcd(b, DECODE_SEQS_PER_STEP), t), lm=t, lg=t,
                sub=min(16, t), time_minor=False)


def kernel(x_prompt, x_sample, state_conv, state_mlstm_C, state_mlstm_n, state_mlstm_m, state_gla_S,
           g_ffn1, w_ffn1_up, w_ffn1_down, g_mix, w_in, conv_w, conv_b, w_mq, w_mk, b_if, g_mhead,
           w_a2, b_a, g_ghead, w_pa, w_pb, w_o, g_ffn2, w_ffn2_up, w_ffn2_down, g_final):
    w = _pack_weights(g_ffn1, w_ffn1_up, w_ffn1_down, g_mix, w_in, conv_w, conv_b, w_mq, w_mk, b_if,
                      g_mhead, w_a2, b_a, g_ghead, w_pa, w_pb, w_o, g_ffn2, w_ffn2_up, w_ffn2_down,
                      g_final)
    bp, tp, _ = x_prompt.shape
    bs, ts, _ = x_sample.shape
    tm_ffn = math.gcd(math.gcd(bp * tp, bs * ts), FFN_ROWS)
    x1_p, x1_s = _ffn(x_prompt.reshape(bp * tp, D_MODEL), x_sample.reshape(bs * ts, D_MODEL),
                      w["g_ffn1"], w["w_ffn1_up"], w["w_ffn1_down"], w["g_final"], final_norm=False, tm=tm_ffn)
    out_p = _mixer(x1_p, bp, tp, None, w, **_tiles(bp, tp))
    state = (state_conv[0], state_mlstm_C[0], state_mlstm_n[0], state_mlstm_m[0], state_gla_S[0])
    out_s = _mixer(x1_s, bs, ts, state, w, **_tiles(bs, ts))
    y_p, y_s = _ffn(out_p[0], out_s[0], w["g_ffn2"], w["w_ffn2_up"], w["w_ffn2_down"], w["g_final"],
                    final_norm=True, tm=tm_ffn)
    return (y_p.reshape(bp, tp, D_MODEL), y_s.reshape(bs, ts, D_MODEL)) + out_p[1:] + out_s[1:]
```
